```python
import math
import jax
import jax.numpy as jnp
from jax import lax
import numpy as np

D_MODEL = 1024
BATCH = 4
SEQ = 4096
DEPTH = 2

GRID_W = 64
CTX_LEN = 256

A_HEADS = 4
A_QK_DIM = 64
A_V_DIM = 2 * A_QK_DIM
A_WIDTH = A_HEADS * A_V_DIM
A_QK_COLS = A_HEADS * 2 * A_QK_DIM
ROPE_THETA = 10000.0
Q_BLOCK = 128
B_CHUNK = 128
B_GROUPS = 4
B_WIDTH = 512
B_GROUP_DIM = B_WIDTH // B_GROUPS
C_WIDTH = 512
C_BLOCKS = 8
C_BLOCK_DIM = C_WIDTH // C_BLOCKS
C_CONV = 4
CONV_PAD = (C_CONV // 2, (C_CONV - 1) // 2)
C_POW = 8.0
N_BRANCH = 3
BRANCH_WIDTH = 512
IN_SIZES = (A_QK_COLS, A_QK_COLS, A_WIDTH, B_WIDTH, B_WIDTH, C_WIDTH, C_WIDTH, N_BRANCH * D_MODEL)
IN_COLS = sum(IN_SIZES)
IN_SPLITS = tuple(int(v) for v in np.cumsum(IN_SIZES)[:-1])
N_EXPERTS = 64
TOP_K = 8
N_GROUPS = 8
TOPK_GROUPS = 4
D_EXPERT = 256
D_SHARED = 256
ROUTED_SCALE = 2.5
MOE_BLOCK = 128
DN_ALPHA = (2 * DEPTH) ** 0.25
DN_BETA = (8 * DEPTH) ** -0.25
LN_EPS = 1e-6
RMS_EPS = 1e-5
F32 = jnp.float32

kernel_name = "hybrid_diffusion_block"


def layer_norm(x, gain=None, bias=None):
    xf = x.astype(F32)
    mu = jnp.mean(xf, -1, keepdims=True)
    var = jnp.mean(jnp.square(xf - mu), -1, keepdims=True)
    y = (xf - mu) * lax.rsqrt(var + LN_EPS)
    if gain is not None:
        y = y * gain.astype(F32) + bias.astype(F32)
    return y.astype(x.dtype)


def rms_norm(x, gain):
    xf = x.astype(F32)
    y = xf * lax.rsqrt(jnp.mean(jnp.square(xf), -1, keepdims=True) + RMS_EPS)
    return (y * gain.astype(F32)).astype(x.dtype)


def modulate(h, shift, scale):
    return h * (1.0 + scale) + shift


def axial_rope_tables(n):
    rows = n // GRID_W
    pos_row = jnp.repeat(jnp.arange(rows), GRID_W).astype(F32)
    pos_col = jnp.tile(jnp.arange(GRID_W), rows).astype(F32)
    quarter = A_QK_DIM // 4
    inv = ROPE_THETA ** (-jnp.arange(quarter, dtype=F32) / quarter)
    ang_r = pos_row[:, None] * inv
    ang_c = pos_col[:, None] * inv
    ang = jnp.concatenate([ang_r, ang_r, ang_c, ang_c], axis=-1)
    return jnp.cos(ang)[:, None, None, :], jnp.sin(ang)[:, None, None, :]


def apply_rope(t, cos, sin):
    tr = t.reshape(*t.shape[:-1], 2, 2, A_QK_DIM // 4)
    rot = jnp.stack([-tr[..., 1, :], tr[..., 0, :]], axis=-2).reshape(t.shape)
    return t * cos + rot * sin


def _qk_heads(t):
    return t.reshape(*t.shape[:-1], A_HEADS, 2, A_QK_DIM)


def _v_heads(t):
    return t.reshape(*t.shape[:-1], A_HEADS, A_V_DIM)


def diff_attn_core(q, k, v, lam):
    s = jnp.einsum("bqhcd,bkhcd->bhcqk", q, k, preferred_element_type=F32) * (A_QK_DIM ** -0.5)
    p = jax.nn.softmax(s, axis=-1)
    w = p[:, :, 0] - lam * p[:, :, 1]
    return jnp.einsum("bhqk,bkhe->bqhe", w.astype(v.dtype), v)


def diff_attn_latent(q, k, v, k_ctx, v_ctx, lam):
    b, n = q.shape[:2]
    k_all = jnp.concatenate([k_ctx, k.astype(k_ctx.dtype)], axis=1)
    v_all = jnp.concatenate([v_ctx, v], axis=1)
    qb = jnp.moveaxis(q.reshape(b, n // Q_BLOCK, Q_BLOCK, *q.shape[2:]), 1, 0)
    out = lax.map(lambda qi: diff_attn_core(qi, k_all, v_all, lam), qb)
    return jnp.moveaxis(out, 0, 1).reshape(b, n, A_HEADS, A_V_DIM)


def _diff_post(o, gain, lam_init):
    return (rms_norm(o, gain) * (1.0 - lam_init)).reshape(*o.shape[:2], A_WIDTH)


def spatial_gating(u, v, ln_g, ln_b, w_s, b_s):
    b, n, _ = v.shape
    vg = v.reshape(b, n // B_CHUNK, B_CHUNK, B_GROUPS, B_GROUP_DIM)
    vg = layer_norm(vg, ln_g.reshape(B_GROUPS, B_GROUP_DIM), ln_b.reshape(B_GROUPS, B_GROUP_DIM))
    mixed = jnp.einsum("gpq,bnqgc->bnpgc", w_s, vg) + b_s.T[:, :, None]
    return u * mixed.reshape(b, n, B_WIDTH)


def depthwise_conv(x, w, bias):
    y = lax.conv_general_dilated(
        x, w.astype(x.dtype)[:, None, :], window_strides=(1,), padding=[CONV_PAD],
        dimension_numbers=("NWC", "WIO", "NWC"), feature_group_count=x.shape[-1])
    return y + bias


def block_diag(x, w, bias):
    xb = x.reshape(*x.shape[:-1], C_BLOCKS, C_BLOCK_DIM)
    return jnp.einsum("bnhi,hij->bnhj", xb, w).reshape(x.shape) + bias


def rglru_coeffs(x, w_a, b_a, w_x, b_x, lam):
    xf = x.astype(F32)
    r = jax.nn.sigmoid(block_diag(xf, w_a, b_a).astype(F32))
    i = jax.nn.sigmoid(block_diag(xf, w_x, b_x).astype(F32))
    log_a = -C_POW * r * jax.nn.softplus(-lam.astype(F32))
    a = jnp.exp(log_a)
    return a, jnp.sqrt(-jnp.expm1(2.0 * log_a)) * (i * xf)


def _lin_combine(left, right):
    a_l, h_l = left
    a_r, h_r = right
    return a_l * a_r, a_r * h_l + h_r


def rglru_direction(x_ctx, x_lat, w_a, b_a, w_x, b_x, lam, reverse):
    if reverse:
        x_ctx, x_lat = x_ctx[:, ::-1], x_lat[:, ::-1]
    a_c, u_c = rglru_coeffs(x_ctx, w_a, b_a, w_x, b_x, lam)
    _, h_c = lax.associative_scan(_lin_combine, (a_c, u_c), axis=1)
    a_l, u_l = rglru_coeffs(x_lat, w_a, b_a, w_x, b_x, lam)
    a_cum, h_l = lax.associative_scan(_lin_combine, (a_l, u_l), axis=1)
    h_l = h_l + a_cum * h_c[:, -1:, :]
    if reverse:
        h_c, h_l = h_c[:, ::-1], h_l[:, ::-1]
    return h_c.astype(x_ctx.dtype), h_l.astype(x_lat.dtype)


def merge_branches(o_a, o_b, o_c, g, w_branch, w_out):
    o = jnp.stack([o_a, o_b, o_c], axis=-2)
    proj = jnp.einsum("bnrc,rcd->bnrd", o, w_branch)
    gates = jax.nn.sigmoid(g.reshape(*g.shape[:-1], N_BRANCH, D_MODEL).astype(F32)).astype(proj.dtype)
    return jnp.sum(gates * proj, axis=-2) @ w_out


def token_mixer(h_lat, h_ctx, cos, sin, layer_idx, need_ctx_out,
                w_in, b_in, lam_q1, lam_k1, lam_q2, lam_k2, attn_norm_g,
                sg_ln_g, sg_ln_b, sg_w, sg_b,
                conv_w, conv_b, lru_wa, lru_ba, lru_wx, lru_bx, lru_lam,
                w_branch, w_out):
    q_l, k_l, v_l, u_l, s_l, x_l, y_l, g_l = jnp.split(h_lat @ w_in + b_in, IN_SPLITS, axis=-1)
    q_c, k_c, v_c, u_c, s_c, x_c, y_c, g_c = jnp.split(h_ctx @ w_in + b_in, IN_SPLITS, axis=-1)

    lam_init = 0.8 - 0.6 * math.exp(-0.3 * layer_idx)
    lam = (jnp.exp(jnp.sum(lam_q1.astype(F32) * lam_k1.astype(F32)))
           - jnp.exp(jnp.sum(lam_q2.astype(F32) * lam_k2.astype(F32))) + lam_init)
    kc = _qk_heads(k_c)
    vc = _v_heads(v_c)
    attn_l = diff_attn_latent(apply_rope(_qk_heads(q_l), cos, sin), apply_rope(_qk_heads(k_l), cos, sin),
                              _v_heads(v_l), kc, vc, lam)
    o_a_l = _diff_post(attn_l, attn_norm_g, lam_init)

    o_b_l = spatial_gating(jax.nn.gelu(u_l), jax.nn.gelu(s_l), sg_ln_g, sg_ln_b, sg_w, sg_b)

    xs_l = depthwise_conv(x_l, conv_w, conv_b)
    xs_c = depthwise_conv(x_c, conv_w, conv_b)
    hc_f, hl_f = rglru_direction(xs_c, xs_l, lru_wa[0], lru_ba[0], lru_wx[0], lru_bx[0], lru_lam[0], False)
    hc_b, hl_b = rglru_direction(xs_c, xs_l, lru_wa[1], lru_ba[1], lru_wx[1], lru_bx[1], lru_lam[1], True)
    o_c_l = jax.nn.gelu(y_l) * (hl_f + hl_b)

    mix_lat = merge_branches(o_a_l, o_b_l, o_c_l, g_l, w_branch, w_out)
    if not need_ctx_out:
        return mix_lat, None
    o_a_c = _diff_post(diff_attn_core(_qk_heads(q_c), kc, vc, lam), attn_norm_g, lam_init)
    o_b_c = spatial_gating(jax.nn.gelu(u_c), jax.nn.gelu(s_c), sg_ln_g, sg_ln_b, sg_w, sg_b)
    o_c_c = jax.nn.gelu(y_c) * (hc_f + hc_b)
    mix_ctx = merge_branches(o_a_c, o_b_c, o_c_c, g_c, w_branch, w_out)
    return mix_lat, mix_ctx


def moe_ffn(h, w_router, router_bias, w_gate, w_up, w_down, sh_gate, sh_up, sh_down):
    n_tok = h.shape[0]
    per_group = N_EXPERTS // N_GROUPS
    scores = jax.nn.sigmoid(jnp.matmul(h, w_router, preferred_element_type=F32))
    biased = scores + router_bias.astype(F32)
    group_score = jnp.sum(lax.top_k(biased.reshape(n_tok, N_GROUPS, per_group), 2)[0], axis=-1)
    _, top_groups = lax.top_k(group_score, TOPK_GROUPS)
    tok_rows = jnp.arange(n_tok)[:, None]
    group_ok = jnp.zeros((n_tok, N_GROUPS), bool).at[tok_rows, top_groups].set(True)
    expert_ok = jnp.repeat(group_ok, per_group, axis=1)
    _, top_e = lax.top_k(jnp.where(expert_ok, biased, -jnp.inf), TOP_K)
    s_sel = jnp.take_along_axis(scores, top_e, axis=1)
    weights = s_sel / jnp.sum(s_sel, -1, keepdims=True) * ROUTED_SCALE

    n_assign = n_tok * TOP_K
    flat_e = top_e.reshape(n_assign)
    flat_tok = jnp.repeat(jnp.arange(n_tok, dtype=jnp.int32), TOP_K)
    flat_w = weights.reshape(n_assign)
    order = jnp.argsort(flat_e)
    e_sorted = flat_e[order]
    counts = jnp.bincount(flat_e, length=N_EXPERTS)
    padded = (counts + MOE_BLOCK - 1) // MOE_BLOCK * MOE_BLOCK
    pad_end = jnp.cumsum(padded)
    pad_start = pad_end - padded
    start = jnp.cumsum(counts) - counts
    dest = pad_start[e_sorted] + (jnp.arange(n_assign) - start[e_sorted])
    n_slots = n_assign + N_EXPERTS * MOE_BLOCK
    n_blocks = n_slots // MOE_BLOCK
    slot_tok = jnp.zeros((n_slots,), jnp.int32).at[dest].set(flat_tok[order])
    slot_w = jnp.zeros((n_slots,), h.dtype).at[dest].set(flat_w[order].astype(h.dtype))
    block_e = jnp.minimum(jnp.searchsorted(pad_end, jnp.arange(n_blocks) * MOE_BLOCK, side="right"),
                          N_EXPERTS - 1)

    def expert_block(args):
        tok, wts, e = args
        xb = h[tok]
        act = jax.nn.silu(xb @ w_gate[e]) * (xb @ w_up[e])
        return (act @ w_down[e]) * wts[:, None]

    routed_slots = lax.map(expert_block, (slot_tok.reshape(n_blocks, MOE_BLOCK),
                                          slot_w.reshape(n_blocks, MOE_BLOCK), block_e))
    routed = jnp.zeros_like(h).at[slot_tok].add(routed_slots.reshape(n_slots, D_MODEL).astype(h.dtype))
    shared = (jax.nn.silu(h @ sh_gate) * (h @ sh_up)) @ sh_down
    return routed + shared


def setup_inputs(seed: int = 0) -> dict:
    key = jax.random.key(seed)
    ks = iter(jax.random.split(key, 48))

    def nrm(shape, scale):
        return jax.random.normal(next(ks), shape, F32) * scale

    def gain(shape):
        return 1.0 + nrm(shape, 0.02)

    L = DEPTH
    u_lam = jax.random.uniform(next(ks), (L, 2, C_WIDTH), F32, minval=0.9, maxval=0.999)
    s_lam = u_lam ** (1.0 / C_POW)
    return {
        "x": nrm((BATCH, SEQ, D_MODEL), 1.0),
        "c": nrm((BATCH, D_MODEL), 1.0),
        "ctx": nrm((BATCH, CTX_LEN, D_MODEL), 1.0),
        "c_ctx": nrm((D_MODEL,), 1.0),
        "w_mod": nrm((L, D_MODEL, 6 * D_MODEL), 0.5 * D_MODEL ** -0.5),
        "b_mod": nrm((L, 6 * D_MODEL), 0.02),
        "w_in": nrm((L, D_MODEL, IN_COLS), D_MODEL ** -0.5),
        "b_in": nrm((L, IN_COLS), 0.02),
        "lam_q1": nrm((L, A_QK_DIM), 0.1),
        "lam_k1": nrm((L, A_QK_DIM), 0.1),
        "lam_q2": nrm((L, A_QK_DIM), 0.1),
        "lam_k2": nrm((L, A_QK_DIM), 0.1),
        "attn_norm_g": gain((L, A_HEADS, A_V_DIM)),
        "sg_ln_g": gain((L, B_WIDTH)),
        "sg_ln_b": nrm((L, B_WIDTH), 0.02),
        "sg_w": nrm((L, B_GROUPS, B_CHUNK, B_CHUNK), B_CHUNK ** -0.5),
        "sg_b": 1.0 + nrm((L, B_GROUPS, B_CHUNK), 0.01),
        "conv_w": nrm((L, C_CONV, C_WIDTH), C_CONV ** -0.5),
        "conv_b": nrm((L, C_WIDTH), 0.02),
        "lru_wa": nrm((L, 2, C_BLOCKS, C_BLOCK_DIM, C_BLOCK_DIM), C_BLOCK_DIM ** -0.5),
        "lru_ba": nrm((L, 2, C_WIDTH), 0.02),
        "lru_wx": nrm((L, 2, C_BLOCKS, C_BLOCK_DIM, C_BLOCK_DIM), C_BLOCK_DIM ** -0.5),
        "lru_bx": nrm((L, 2, C_WIDTH), 0.02),
        "lru_lam": jnp.log(s_lam) - jnp.log1p(-s_lam),
        "w_branch": nrm((L, N_BRANCH, BRANCH_WIDTH, D_MODEL), BRANCH_WIDTH ** -0.5),
        "w_out": nrm((L, D_MODEL, D_MODEL), DN_BETA * D_MODEL ** -0.5),
        "ln1_g": gain((L, D_MODEL)),
        "ln1_b": nrm((L, D_MODEL), 0.02),
        "w_router": nrm((L, D_MODEL, N_EXPERTS), D_MODEL ** -0.5),
        "router_bias": nrm((L, N_EXPERTS), 0.01),
        "moe_w_gate": nrm((L, N_EXPERTS, D_MODEL, D_EXPERT), D_MODEL ** -0.5),
        "moe_w_up": nrm((L, N_EXPERTS, D_MODEL, D_EXPERT), D_MODEL ** -0.5),
        "moe_w_down": nrm((L, N_EXPERTS, D_EXPERT, D_MODEL), DN_BETA * D_EXPERT ** -0.5),
        "sh_w_gate": nrm((L, D_MODEL, D_SHARED), D_MODEL ** -0.5),
        "sh_w_up": nrm((L, D_MODEL, D_SHARED), D_MODEL ** -0.5),
        "sh_w_down": nrm((L, D_SHARED, D_MODEL), DN_BETA * D_SHARED ** -0.5),
        "ln2_g": gain((L, D_MODEL)),
        "ln2_b": nrm((L, D_MODEL), 0.02),
    }


def reference(x, c, ctx, c_ctx, w_mod, b_mod, w_in, b_in, lam_q1, lam_k1, lam_q2, lam_k2, attn_norm_g,
              sg_ln_g, sg_ln_b, sg_w, sg_b, conv_w, conv_b, lru_wa, lru_ba, lru_wx, lru_bx, lru_lam,
              w_branch, w_out, ln1_g, ln1_b, w_router, router_bias, moe_w_gate, moe_w_up, moe_w_down,
              sh_w_gate, sh_w_up, sh_w_down, ln2_g, ln2_b):
    b, n, _ = x.shape
    n_ctx = ctx.shape[1]
    cos, sin = axial_rope_tables(n)
    for i in range(DEPTH):
        last = i == DEPTH - 1
        mod = jax.nn.silu(c) @ w_mod[i] + b_mod[i]
        mod_c = jax.nn.silu(c_ctx) @ w_mod[i] + b_mod[i]
        sh1, sc1, g1, sh2, sc2, g2 = jnp.split(mod[:, None, :], 6, axis=-1)
        csh1, csc1, cg1, csh2, csc2, cg2 = jnp.split(mod_c, 6, axis=-1)

        h_lat = modulate(layer_norm(x), sh1, sc1)
        h_ctx = modulate(layer_norm(ctx), csh1, csc1)
        mix_lat, mix_ctx = token_mixer(
            h_lat, h_ctx, cos, sin, i, not last,
            w_in[i], b_in[i], lam_q1[i], lam_k1[i], lam_q2[i], lam_k2[i], attn_norm_g[i],
            sg_ln_g[i], sg_ln_b[i], sg_w[i], sg_b[i],
            conv_w[i], conv_b[i], lru_wa[i], lru_ba[i], lru_wx[i], lru_bx[i], lru_lam[i],
            w_branch[i], w_out[i])
        x = layer_norm(DN_ALPHA * x + g1 * mix_lat, ln1_g[i], ln1_b[i])
        h2 = modulate(layer_norm(x), sh2, sc2).reshape(b * n, D_MODEL)
        moe_args = (w_router[i], router_bias[i], moe_w_gate[i], moe_w_up[i], moe_w_down[i],
                    sh_w_gate[i], sh_w_up[i], sh_w_down[i])
        if last:
            y = moe_ffn(h2, *moe_args)
            x = layer_norm(DN_ALPHA * x + g2 * y.reshape(b, n, D_MODEL), ln2_g[i], ln2_b[i])
        else:
            ctx = layer_norm(DN_ALPHA * ctx + cg1 * mix_ctx, ln1_g[i], ln1_b[i])
            h2c = modulate(layer_norm(ctx), csh2, csc2).reshape(b * n_ctx, D_MODEL)
            y = moe_ffn(jnp.concatenate([h2, h2c], axis=0), *moe_args)
            x = layer_norm(DN_ALPHA * x + g2 * y[: b * n].reshape(b, n, D_MODEL), ln2_g[i], ln2_b[i])
            ctx = layer_norm(DN_ALPHA * ctx + cg2 * y[b * n:].reshape(b, n_ctx, D_MODEL), ln2_g[i], ln2_b[i])
    return x
```

```python
import functools
import math

import jax
import jax.numpy as jnp
from jax import lax
from jax.experimental import pallas as pl
from jax.experimental.pallas import tpu as pltpu

F32 = jnp.float32
BF16 = jnp.bfloat16
I32 = jnp.int32

A_HEADS = 4
A_QK_DIM = 64
A_V_DIM = 2 * A_QK_DIM
GRID_W = 64
ROPE_THETA = 10000.0
B_CHUNK = 128
B_GROUPS = 4
C_BLOCKS = 8
C_POW = 8.0
TOP_K = 8
N_GROUPS = 8
TOPK_GROUPS = 4
ROUTED_SCALE = 2.5
LN_EPS = 1e-6
RMS_EPS = 1e-5

TM = 512
MOE_BLK = 256
TD = 256
MOD_ROWS = 16
VMEM_LIMIT = 56 * 1024 * 1024


def _cparams(sem):
    return pltpu.CompilerParams(dimension_semantics=sem, vmem_limit_bytes=VMEM_LIMIT)


def _ln(x):
    mu = jnp.mean(x, axis=-1, keepdims=True)
    xc = x - mu
    var = jnp.mean(xc * xc, axis=-1, keepdims=True)
    return xc * lax.rsqrt(var + LN_EPS)


def _gelu(x):
    cdf = 0.5 * (1.0 + jnp.tanh(math.sqrt(2.0 / math.pi) * (x + 0.044715 * (x * x * x))))
    return x * cdf


def _sigmoid(x):
    return 1.0 / (1.0 + jnp.exp(-x))


def _silu(x):
    return x * _sigmoid(x)


def _dot(a, b):
    return jnp.dot(a, b, preferred_element_type=F32)


def _dot_nt(a, b):
    return lax.dot_general(a, b, (((1,), (1,)), ((), ())), preferred_element_type=F32)


def _mod_kernel(c_ref, w_ref, b_ref, o_ref):
    o_ref[...] = _dot(_silu(c_ref[...]).astype(BF16), w_ref[...]) + b_ref[...]


def _mod_rows(c_all, w, b):
    m, d = c_all.shape
    n = w.shape[1]
    tn = 1536
    return pl.pallas_call(
        _mod_kernel,
        grid=(n // tn,),
        in_specs=[pl.BlockSpec((m, d), lambda j: (0, 0)),
                  pl.BlockSpec((d, tn), lambda j: (0, j)),
                  pl.BlockSpec((1, tn), lambda j: (0, j))],
        out_specs=pl.BlockSpec((m, tn), lambda j: (0, j)),
        out_shape=jax.ShapeDtypeStruct((m, n), F32),
        compiler_params=_cparams(("parallel",)),
        name="adaln_rows",
    )(c_all, w, b)


def _in_proj_kernel(x_ref, sh_ref, sc_ref, w_ref, b_ref, o_ref, h_scr):
    @pl.when(pl.program_id(1) == 0)
    def _():
        h = _ln(x_ref[...]) * (1.0 + sc_ref[...]) + sh_ref[...]
        h_scr[...] = h.astype(BF16)

    o_ref[...] = _dot(h_scr[...], w_ref[...]) + b_ref[...]


def _in_proj(x_all, modp, w, b, seg):
    t, d = x_all.shape
    n = w.shape[1]
    tn = 512
    return pl.pallas_call(
        _in_proj_kernel,
        grid=(t // TM, n // tn),
        in_specs=[pl.BlockSpec((TM, d), lambda i, j: (i, 0)),
                  pl.BlockSpec((None, None, 1, d), lambda i, j: (0, seg(i), 0, 0)),
                  pl.BlockSpec((None, None, 1, d), lambda i, j: (1, seg(i), 0, 0)),
                  pl.BlockSpec((d, tn), lambda i, j: (0, j)),
                  pl.BlockSpec((1, tn), lambda i, j: (0, j))],
        out_specs=pl.BlockSpec((TM, tn), lambda i, j: (i, j)),
        out_shape=jax.ShapeDtypeStruct((t, n), F32),
        scratch_shapes=[pltpu.VMEM((TM, d), BF16)],
        compiler_params=_cparams(("parallel", "arbitrary")),
        name="in_proj",
    )(x_all, modp, modp, w, b)


def _rope_kernel(q_ref, k_ref, v_ref, cos_ref, sa_ref, sb_ref, qo_ref, ko_ref, vo_ref):
    cos = cos_ref[...]
    sa = sa_ref[...]
    sb = sb_ref[...]

    def rope(x):
        w = x.shape[-1]
        return x * cos + pltpu.roll(x, w - 16, 1) * sa + pltpu.roll(x, 16, 1) * sb

    qo_ref[...] = (rope(q_ref[...]) * (A_QK_DIM ** -0.5)).astype(BF16)
    ko_ref[...] = rope(k_ref[...]).astype(BF16)
    vo_ref[...] = v_ref[...].astype(BF16)


def _rope_tables(n):
    rows = n // GRID_W
    pos_row = jnp.repeat(jnp.arange(rows), GRID_W).astype(F32)
    pos_col = jnp.tile(jnp.arange(GRID_W), rows).astype(F32)
    quarter = A_QK_DIM // 4
    inv = ROPE_THETA ** (-jnp.arange(quarter, dtype=F32) / quarter)
    ang_r = pos_row[:, None] * inv
    ang_c = pos_col[:, None] * inv
    ang = jnp.concatenate([ang_r, ang_r, ang_c, ang_c], axis=-1)
    cos = jnp.cos(ang)
    sin = jnp.sin(ang)
    first_half = (jnp.arange(A_QK_DIM) % (2 * quarter)) < quarter
    sa = jnp.where(first_half, -sin, 0.0)
    sb = jnp.where(first_half, 0.0, sin)
    reps = 2 * A_HEADS

    def widen(tbl, fill):
        tbl = jnp.tile(tbl, (1, reps))
        return jnp.concatenate([tbl, jnp.full((TM, tbl.shape[1]), fill, F32)], axis=0)

    return widen(cos, 1.0), widen(sa, 0.0), widen(sb, 0.0)


def _rope(proj, tables, tbl_idx):
    t = proj.shape[0]
    w = 2 * A_HEADS * A_QK_DIM
    cos, sa, sb = tables
    col = lambda c: pl.BlockSpec((TM, w), lambda i, c=c: (i, c))
    tbl = pl.BlockSpec((TM, w), lambda i: (tbl_idx(i), 0))
    out = pl.BlockSpec((TM, w), lambda i: (i, 0))
    shp = jax.ShapeDtypeStruct((t, w), BF16)
    return pl.pallas_call(
        _rope_kernel,
        grid=(t // TM,),
        in_specs=[col(0), col(1), col(2), tbl, tbl, tbl],
        out_specs=[out, out, out],
        out_shape=[shp, shp, shp],
        compiler_params=_cparams(("parallel",)),
        name="rope_cast",
    )(proj, proj, proj, cos, sa, sb)


def _attn_kernel(*refs, lam_init, with_lat):
    if with_lat:
        lam_ref, gain_ref, q_ref, kc_ref, vc_ref, kl_ref, vl_ref, o_ref = refs
    else:
        lam_ref, gain_ref, q_ref, kc_ref, vc_ref, o_ref = refs
    lv = lam_ref[...]
    lam = (jnp.exp(jnp.sum(lv[0:1] * lv[1:2], axis=-1, keepdims=True))
           - jnp.exp(jnp.sum(lv[2:3] * lv[3:4], axis=-1, keepdims=True)) + lam_init)
    outs = []
    for c in range(2):
        cols = slice(c * A_QK_DIM, (c + 1) * A_QK_DIM)
        qc = q_ref[:, cols]
        s_c = _dot_nt(qc, kc_ref[:, cols])
        m = jnp.max(s_c, axis=-1, keepdims=True)
        if with_lat:
            s_l = _dot_nt(qc, kl_ref[:, cols])
            m = jnp.maximum(m, jnp.max(s_l, axis=-1, keepdims=True))
        p_c = jnp.exp(s_c - m)
        den = jnp.sum(p_c, axis=-1, keepdims=True)
        pv = _dot(p_c.astype(BF16), vc_ref[...])
        if with_lat:
            p_l = jnp.exp(s_l - m)
            den = den + jnp.sum(p_l, axis=-1, keepdims=True)
            pv = pv + _dot(p_l.astype(BF16), vl_ref[...])
        outs.append(pv / den)
    o = outs[0] - lam * outs[1]
    o = o * lax.rsqrt(jnp.mean(o * o, axis=-1, keepdims=True) + RMS_EPS)
    o_ref[...] = (o * gain_ref[...]) * (1.0 - lam_init)


def _attention(qr, kr, vb, lam_vecs, gain, *, lam_init, batch, n_lat, n_ctx, latent_queries):
    hw = A_V_DIM
    ctx_blk0 = batch * n_lat // n_ctx
    lam_spec = pl.BlockSpec(lam_vecs.shape, lambda b, h, i: (0, 0))
    gain_spec = pl.BlockSpec((None, 1, hw), lambda b, h, i: (h, 0, 0))
    kv_ctx = pl.BlockSpec((n_ctx, hw), lambda b, h, i: (ctx_blk0 + b, h))
    if latent_queries:
        tq = 256
        nq = n_lat // tq
        q_spec = pl.BlockSpec((tq, hw), lambda b, h, i: (b * nq + i, h))
        kv_lat = pl.BlockSpec((n_lat, hw), lambda b, h, i: (b, h))
        in_specs = [lam_spec, gain_spec, q_spec, kv_ctx, kv_ctx, kv_lat, kv_lat]
        args = (lam_vecs, gain, qr, kr, vb, kr, vb)
        rows = batch * n_lat
    else:
        tq = n_ctx
        nq = 1
        q_spec = pl.BlockSpec((tq, hw), lambda b, h, i: (ctx_blk0 + b, h))
        in_specs = [lam_spec, gain_spec, q_spec, kv_ctx, kv_ctx]
        args = (lam_vecs, gain, qr, kr, vb)
        rows = batch * n_ctx
    return pl.pallas_call(
        functools.partial(_attn_kernel, lam_init=lam_init, with_lat=latent_queries),
        grid=(batch, A_HEADS, nq),
        in_specs=in_specs,
        out_specs=pl.BlockSpec((tq, hw), lambda b, h, i: (b * nq + i, h)),
        out_shape=jax.ShapeDtypeStruct((rows, A_HEADS * hw), F32),
        compiler_params=_cparams(("parallel", "parallel", "arbitrary")),
        name="diff_attn_lat" if latent_queries else "diff_attn_ctx",
    )(*args)


def _sg_kernel(u_ref, s_ref, g_ref, b_ref, w_ref, bs_ref, o_ref):
    gw = B_CHUNK
    for ci in range(TM // B_CHUNK):
        rows = slice(ci * B_CHUNK, (ci + 1) * B_CHUNK)
        for g in range(B_GROUPS):
            cols = slice(g * gw, (g + 1) * gw)
            vn = _ln(_gelu(s_ref[rows, cols])) * g_ref[:, cols] + b_ref[:, cols]
            mixed = _dot(w_ref[g], vn.astype(BF16)) + bs_ref[g]
            o_ref[rows, cols] = _gelu(u_ref[rows, cols]) * mixed


def _spatial_gating(proj, ln_g, ln_b, w_s, b_s):
    t = proj.shape[0]
    w = B_GROUPS * B_CHUNK
    full = lambda a: pl.BlockSpec(a.shape, lambda i: (0,) * a.ndim)
    return pl.pallas_call(
        _sg_kernel,
        grid=(t // TM,),
        in_specs=[pl.BlockSpec((TM, w), lambda i: (i, 3)),
                  pl.BlockSpec((TM, w), lambda i: (i, 4)),
                  full(ln_g), full(ln_b), full(w_s), full(b_s)],
        out_specs=pl.BlockSpec((TM, w), lambda i: (i, 0)),
        out_shape=jax.ShapeDtypeStruct((t, w), F32),
        compiler_params=_cparams(("parallel",)),
        name="spatial_gating",
    )(proj, proj, ln_g, ln_b, w_s, b_s)


def _lru_kernel(*refs, reverse, final, n_tiles, tl):
    if final:
        (x_ref, xp_ref, xn_ref, cw_ref, cb_ref, w_ref, ba_ref, bx_ref, lam_ref, h0_ref,
         hf_ref, y_ref, o_ref, hl_ref, carry) = refs
    else:
        (x_ref, xp_ref, xn_ref, cw_ref, cb_ref, w_ref, ba_ref, bx_ref, lam_ref, h0_ref,
         o_ref, hl_ref, carry) = refs
    step = pl.program_id(1)

    @pl.when(step == 0)
    def _():
        carry[...] = h0_ref[...]

    ti = (n_tiles - 1 - step) if reverse else step
    x = x_ref[...]
    cw = x.shape[-1]
    row = lax.broadcasted_iota(I32, x.shape, 0)
    has_prev = ti > 0
    has_next = ti < n_tiles - 1
    p6 = jnp.where(has_prev, xp_ref[6:7, :], 0.0)
    p7 = jnp.where(has_prev, xp_ref[7:8, :], 0.0)
    n0 = jnp.where(has_next, xn_ref[0:1, :], 0.0)
    xm1 = jnp.where(row == 0, p7, pltpu.roll(x, 1, 0))
    xm2 = jnp.where(row == 0, p6, jnp.where(row == 1, p7, pltpu.roll(x, 2, 0)))
    xp1 = jnp.where(row == tl - 1, n0, pltpu.roll(x, tl - 1, 0))
    taps = cw_ref[...]
    xs = taps[0:1] * xm2 + taps[1:2] * xm1 + taps[2:3] * x + taps[3:4] * xp1 + cb_ref[...]

    z = _dot(xs.astype(BF16), w_ref[...])
    r = _sigmoid(z[:, :cw] + ba_ref[...])
    ig = _sigmoid(z[:, cw:] + bx_ref[...])
    nl = -lam_ref[...]
    softplus = jnp.maximum(nl, 0.0) + jnp.log(1.0 + jnp.exp(-jnp.abs(nl)))
    log_a = -C_POW * r * softplus
    a = jnp.exp(log_a)
    u = jnp.sqrt(1.0 - jnp.exp(2.0 * log_a)) * (ig * xs)

    d = 1
    while d < tl:
        if reverse:
            ok = row < tl - d
            shift = tl - d
        else:
            ok = row >= d
            shift = d
        a_sh = jnp.where(ok, pltpu.roll(a, shift, 0), 1.0)
        u_sh = jnp.where(ok, pltpu.roll(u, shift, 0), 0.0)
        u = u + a * u_sh
        a = a * a_sh
        d *= 2
    h = u + a * carry[0:1, :]
    edge = h[0:1, :] if reverse else h[tl - 1:tl, :]
    carry[...] = jnp.broadcast_to(edge, carry.shape)
    hl_ref[...] = jnp.broadcast_to(edge, hl_ref.shape)
    if final:
        o_ref[...] = _gelu(y_ref[...]) * (hf_ref[...] + h)
    else:
        o_ref[...] = h


def _lru_pass(proj, hf, h0, conv_w, conv_b, w_dense, ba, bx, lam, *, batch, seq, tl, row0, reverse, final):
    t_all = proj.shape[0]
    cw = conv_w.shape[1]
    n_tiles = seq // tl
    base = row0 // tl
    sub = tl // 8
    last8 = t_all // 8 - 1

    def tile(b, s):
        ti = (n_tiles - 1 - s) if reverse else s
        return base + b * n_tiles + ti

    def loc(b, s):
        ti = (n_tiles - 1 - s) if reverse else s
        return b * n_tiles + ti

    full = lambda a: pl.BlockSpec(a.shape, lambda b, s: (0,) * a.ndim)
    in_specs = [pl.BlockSpec((tl, cw), lambda b, s: (tile(b, s), 5)),
                pl.BlockSpec((8, cw), lambda b, s: (jnp.maximum(tile(b, s) * sub - 1, 0), 5)),
                pl.BlockSpec((8, cw), lambda b, s: (jnp.minimum((tile(b, s) + 1) * sub, last8), 5)),
                full(conv_w), full(conv_b), full(w_dense), full(ba), full(bx), full(lam),
                pl.BlockSpec((None, 8, cw), lambda b, s: (b, 0, 0))]
    args = [proj, proj, proj, conv_w, conv_b, w_dense, ba, bx, lam, h0]
    if final:
        in_specs += [pl.BlockSpec((tl, cw), lambda b, s: (loc(b, s), 0)),
                     pl.BlockSpec((tl, cw), lambda b, s: (tile(b, s), 6))]
        args += [hf, proj]
    return pl.pallas_call(
        functools.partial(_lru_kernel, reverse=reverse, final=final, n_tiles=n_tiles, tl=tl),
        grid=(batch, n_tiles),
        in_specs=in_specs,
        out_specs=[pl.BlockSpec((tl, cw), lambda b, s: (loc(b, s), 0)),
                   pl.BlockSpec((None, 8, cw), lambda b, s: (b, 0, 0))],
        out_shape=[jax.ShapeDtypeStruct((batch * seq, cw), F32),
                   jax.ShapeDtypeStruct((batch, 8, cw), F32)],
        scratch_shapes=[pltpu.VMEM((8, cw), F32)],
        compiler_params=_cparams(("parallel", "arbitrary")),
        name="rglru_%s_%s" % ("bwd" if reverse else "fwd", "lat" if row0 == 0 else "ctx"),
    )(*args)


def _merge_kernel(x_ref, oa_ref, ob_ref, oc_ref, g0, g1, g2, g3, g4, g5, wb_ref, wo_ref,
                  gate1_ref, sh2_ref, sc2_ref, lng_ref, lnb_ref, wrh_ref, wrl_ref,
                  xn_ref, h2_ref, sc_ref, *, alpha):
    gates = ((g0, g1), (g2, g3), (g4, g5))
    half = g0.shape[-1]
    mix = None
    for r, o_ref in enumerate((oa_ref, ob_ref, oc_ref)):
        proj = _dot(o_ref[...].astype(BF16), wb_ref[r])
        gate = jnp.concatenate([_sigmoid(gates[r][0][...]), _sigmoid(gates[r][1][...])], axis=-1)
        term = gate * proj
        mix = term if mix is None else mix + term
    out = _dot(mix.astype(BF16), wo_ref[...])
    xn = _ln(alpha * x_ref[...] + gate1_ref[...] * out) * lng_ref[...] + lnb_ref[...]
    xn_ref[...] = xn
    h2 = _ln(xn) * (1.0 + sc2_ref[...]) + sh2_ref[...]
    h2_ref[...] = h2
    h_hi = h2.astype(BF16)
    h_lo = (h2 - h_hi.astype(F32)).astype(BF16)
    logits = _dot_nt(wrh_ref[...], h_hi) + (_dot_nt(wrh_ref[...], h_lo) + _dot_nt(wrl_ref[...], h_hi))
    sc_ref[...] = _sigmoid(logits)


def _merge(x_all, o_a, o_b, o_c, proj, modp, w_branch, w_out, ln_g, ln_b, wr_hi, wr_lo, seg, *, n_rows, alpha):
    d = x_all.shape[1]
    bw = o_a.shape[1]
    n_exp = wr_hi.shape[0]
    tok = lambda w: pl.BlockSpec((TM, w), lambda i: (i, 0))
    gate = lambda c: pl.BlockSpec((TM, bw), lambda i, c=c: (i, c))
    mod = lambda k: pl.BlockSpec((None, None, 1, d), lambda i, k=k: (k, seg(i), 0, 0))
    full = lambda a: pl.BlockSpec(a.shape, lambda i: (0,) * a.ndim)
    return pl.pallas_call(
        functools.partial(_merge_kernel, alpha=alpha),
        grid=(n_rows // TM,),
        in_specs=[tok(d), tok(bw), tok(bw), tok(bw)] + [gate(7 + c) for c in range(6)]
                 + [full(w_branch), full(w_out), mod(2), mod(3), mod(4), full(ln_g), full(ln_b),
                    full(wr_hi), full(wr_lo)],
        out_specs=[tok(d), tok(d), pl.BlockSpec((n_exp, TM), lambda i: (0, i))],
        out_shape=[jax.ShapeDtypeStruct((n_rows, d), F32),
                   jax.ShapeDtypeStruct((n_rows, d), F32),
                   jax.ShapeDtypeStruct((n_exp, n_rows), F32)],
        compiler_params=_cparams(("parallel",)),
        name="merge_residual_router",
    )(x_all, o_a, o_b, o_c, proj, proj, proj, proj, proj, proj, w_branch, w_out,
      modp, modp, modp, ln_g, ln_b, wr_hi, wr_lo)


def _route_kernel(s_ref, bias_ref, tri_ref, idx_ref, w_ref, rank_ref, cnt_ref, carry):
    @pl.when(pl.program_id(0) == 0)
    def _():
        carry[...] = jnp.zeros_like(carry)

    s = s_ref[...]
    n_exp, tn = s.shape
    per = n_exp // N_GROUPS
    neg = -jnp.inf
    biased = s + bias_ref[...]
    sub = lax.broadcasted_iota(I32, (per, tn), 0)
    gs_rows = []
    for g in range(N_GROUPS):
        blk = biased[g * per:(g + 1) * per, :]
        m1 = jnp.max(blk, axis=0, keepdims=True)
        first = jnp.min(jnp.where(blk == m1, sub, per), axis=0, keepdims=True)
        m2 = jnp.max(jnp.where(sub == first, neg, blk), axis=0, keepdims=True)
        gs_rows.append(m1 + m2)
    gs = jnp.concatenate(gs_rows, axis=0)
    gi = lax.broadcasted_iota(I32, gs.shape, 0)
    g_ok = jnp.zeros(gs.shape, F32)
    cur = gs
    for _ in range(TOPK_GROUPS):
        m = jnp.max(cur, axis=0, keepdims=True)
        pick = jnp.min(jnp.where(cur == m, gi, N_GROUPS), axis=0, keepdims=True)
        hit = gi == pick
        g_ok = jnp.where(hit, 1.0, g_ok)
        cur = jnp.where(hit, neg, cur)
    ok_rows = [jnp.broadcast_to(g_ok[g:g + 1, :], (per, tn)) for g in range(N_GROUPS)]
    expert_ok = jnp.concatenate(ok_rows, axis=0)
    masked = jnp.where(expert_ok > 0.0, biased, neg)
    ei = lax.broadcasted_iota(I32, s.shape, 0)
    picks, pick_s, hits = [], [], []
    sel = jnp.zeros(s.shape, F32)
    for _ in range(TOP_K):
        m = jnp.max(masked, axis=0, keepdims=True)
        pick = jnp.min(jnp.where(masked == m, ei, n_exp), axis=0, keepdims=True)
        hit = ei == pick
        picks.append(pick)
        pick_s.append(jnp.sum(jnp.where(hit, s, 0.0), axis=0, keepdims=True))
        hits.append(hit)
        sel = jnp.where(hit, 1.0, sel)
        masked = jnp.where(hit, neg, masked)
    tot = pick_s[0]
    for k in range(1, TOP_K):
        tot = tot + pick_s[k]
    idx_ref[...] = jnp.concatenate(picks, axis=0)
    w_ref[...] = jnp.concatenate([p / tot * ROUTED_SCALE for p in pick_s], axis=0)
    incl = _dot(sel.astype(BF16), tri_ref[...])
    before = incl - sel + carry[:, 0:1]
    ranks = [jnp.sum(jnp.where(hit, before, 0.0), axis=0, keepdims=True) for hit in hits]
    rank_ref[...] = jnp.concatenate(ranks, axis=0).astype(I32)
    total = carry[:, 0:1] + incl[:, tn - 1:tn]
    carry[...] = jnp.broadcast_to(total, carry.shape)
    cnt_ref[...] = jnp.broadcast_to(total, cnt_ref.shape)


def _route(scores_t, bias_col, tri):
    n_exp, t = scores_t.shape
    return pl.pallas_call(
        _route_kernel,
        grid=(t // TM,),
        in_specs=[pl.BlockSpec((n_exp, TM), lambda i: (0, i)),
                  pl.BlockSpec((n_exp, 1), lambda i: (0, 0)),
                  pl.BlockSpec((TM, TM), lambda i: (0, 0))],
        out_specs=[pl.BlockSpec((TOP_K, TM), lambda i: (0, i)),
                   pl.BlockSpec((TOP_K, TM), lambda i: (0, i)),
                   pl.BlockSpec((TOP_K, TM), lambda i: (0, i)),
                   pl.BlockSpec((n_exp, 128), lambda i: (0, 0))],
        out_shape=[jax.ShapeDtypeStruct((TOP_K, t), I32),
                   jax.ShapeDtypeStruct((TOP_K, t), F32),
                   jax.ShapeDtypeStruct((TOP_K, t), I32),
                   jax.ShapeDtypeStruct((n_exp, 128), F32)],
        scratch_shapes=[pltpu.VMEM((n_exp, 128), F32)],
        compiler_params=_cparams(("arbitrary",)),
        name="route_topk",
    )(scores_t, bias_col, tri)


def _dest_kernel(start_ref, idx_ref, rank_ref, o_ref, *, n_exp):
    idx = idx_ref[...]
    acc = rank_ref[...]
    for e in range(n_exp):
        acc = acc + jnp.where(idx == e, start_ref[e], 0)
    o_ref[...] = acc


def _dest_slots(pad_start, idx, rank):
    k, t = idx.shape
    blk = lambda: pl.BlockSpec((k, TM), lambda i, s: (0, i))
    return pl.pallas_call(
        functools.partial(_dest_kernel, n_exp=pad_start.shape[0]),
        grid_spec=pltpu.PrefetchScalarGridSpec(
            num_scalar_prefetch=1, grid=(t // TM,), in_specs=[blk(), blk()], out_specs=blk()),
        out_shape=jax.ShapeDtypeStruct((k, t), I32),
        compiler_params=_cparams(("parallel",)),
        name="route_slots",
    )(pad_start, idx, rank)


def _dispatch_kernel(pad_lo_ref, pad_hi_ref, dest_ref, h_ref, xs_ref, zero_scr, sem, *, n_exp):
    k_top, td = dest_ref.shape

    def row_copy(t, slot):
        return pltpu.make_async_copy(h_ref.at[pl.ds(t, 1), :], xs_ref.at[pl.ds(slot, 1), :], sem.at[0])

    def zero_copy(slot):
        return pltpu.make_async_copy(zero_scr.at[pl.ds(0, 1), :], xs_ref.at[pl.ds(slot, 1), :], sem.at[1])

    @pl.when(pl.program_id(0) == 0)
    def _():
        zero_scr[...] = jnp.zeros_like(zero_scr)
        for e in range(n_exp):
            lo = pad_lo_ref[e]
            hi = pad_hi_ref[e]

            def start(r, c):
                zero_copy(r).start()
                return c

            def wait(r, c):
                zero_copy(r).wait()
                return c

            lax.fori_loop(lo, hi, start, 0)
            lax.fori_loop(lo, hi, wait, 0)

    def start_tok(t, c):
        for k in range(k_top):
            row_copy(t, dest_ref[k, t]).start()
        return c

    def wait_tok(t, c):
        for k in range(k_top):
            row_copy(t, dest_ref[k, t]).wait()
        return c

    lax.fori_loop(0, td, start_tok, 0)
    lax.fori_loop(0, td, wait_tok, 0)


def _dispatch(pad_lo, pad_hi, dest, h2, n_slots):
    t, d = h2.shape
    k = dest.shape[0]
    return pl.pallas_call(
        functools.partial(_dispatch_kernel, n_exp=pad_lo.shape[0]),
        grid_spec=pltpu.PrefetchScalarGridSpec(
            num_scalar_prefetch=2, grid=(t // TD,),
            in_specs=[pl.BlockSpec((k, TD), lambda i, a, b: (0, i), memory_space=pltpu.SMEM),
                      pl.BlockSpec((TD, d), lambda i, a, b: (i, 0))],
            out_specs=pl.BlockSpec(memory_space=pl.ANY),
            scratch_shapes=[pltpu.VMEM((8, d), F32), pltpu.SemaphoreType.DMA((2,))]),
        out_shape=jax.ShapeDtypeStruct((n_slots, d), F32),
        compiler_params=_cparams(("arbitrary",)),
        name="moe_dispatch",
    )(pad_lo, pad_hi, dest, h2)


def _expert_kernel(be_ref, nu_ref, x_ref, wg_ref, wu_ref, wd_ref, o_ref):
    @pl.when(pl.program_id(0) < nu_ref[0])
    def _():
        x = x_ref[...].astype(BF16)
        act = _silu(_dot(x, wg_ref[...])) * _dot(x, wu_ref[...])
        o_ref[...] = _dot(act.astype(BF16), wd_ref[...])


def _experts(block_e, n_used, xs, w_gate, w_up, w_down):
    n_slots, d = xs.shape
    de = w_gate.shape[2]
    blk = lambda i, be, nu: jnp.minimum(i, nu[0] - 1)
    return pl.pallas_call(
        _expert_kernel,
        grid_spec=pltpu.PrefetchScalarGridSpec(
            num_scalar_prefetch=2, grid=(n_slots // MOE_BLK,),
            in_specs=[pl.BlockSpec((MOE_BLK, d), lambda i, be, nu: (blk(i, be, nu), 0)),
                      pl.BlockSpec((None, d, de), lambda i, be, nu: (be[blk(i, be, nu)], 0, 0)),
                      pl.BlockSpec((None, d, de), lambda i, be, nu: (be[blk(i, be, nu)], 0, 0)),
                      pl.BlockSpec((None, de, d), lambda i, be, nu: (be[blk(i, be, nu)], 0, 0))],
            out_specs=pl.BlockSpec((MOE_BLK, d), lambda i, be, nu: (blk(i, be, nu), 0))),
        out_shape=jax.ShapeDtypeStruct((n_slots, d), F32),
        compiler_params=_cparams(("arbitrary",)),
        name="moe_experts",
    )(block_e, n_used, xs, w_gate, w_up, w_down)


def _combine_kernel(dest_ref, w_ref, x_ref, h_ref, gate2_ref, sg_ref, su_ref, sd_ref, lng_ref, lnb_ref,
                    ys_ref, o_ref, gbuf, sem, *, alpha):
    k_top, td = dest_ref.shape

    def row_copy(t, k):
        return pltpu.make_async_copy(ys_ref.at[pl.ds(dest_ref[k, t], 1), :], gbuf.at[k, pl.ds(t, 1), :], sem.at[0])

    def start_tok(t, c):
        for k in range(k_top):
            row_copy(t, k).start()
        return c

    def wait_tok(t, c):
        for k in range(k_top):
            row_copy(t, k).wait()
        return c

    lax.fori_loop(0, td, start_tok, 0)
    h = h_ref[...].astype(BF16)
    shared = _dot((_silu(_dot(h, sg_ref[...])) * _dot(h, su_ref[...])).astype(BF16), sd_ref[...])
    lax.fori_loop(0, td, wait_tok, 0)
    w = w_ref[...]
    routed = gbuf[0] * w[:, 0:1]
    for k in range(1, k_top):
        routed = routed + gbuf[k] * w[:, k:k + 1]
    y = routed + shared
    o_ref[...] = _ln(alpha * x_ref[...] + gate2_ref[...] * y) * lng_ref[...] + lnb_ref[...]


def _combine(dest, w_tok, xn, h2, modp, sh_gate, sh_up, sh_down, ln_g, ln_b, ys, seg_td, *, alpha):
    t, d = xn.shape
    k = dest.shape[0]
    tok = pl.BlockSpec((TD, d), lambda i: (i, 0))
    full = lambda a: pl.BlockSpec(a.shape, lambda i: (0,) * a.ndim)
    return pl.pallas_call(
        functools.partial(_combine_kernel, alpha=alpha),
        grid=(t // TD,),
        in_specs=[pl.BlockSpec((k, TD), lambda i: (0, i), memory_space=pltpu.SMEM),
                  pl.BlockSpec((TD, k), lambda i: (i, 0)),
                  tok, tok,
                  pl.BlockSpec((None, None, 1, d), lambda i: (5, seg_td(i), 0, 0)),
                  full(sh_gate), full(sh_up), full(sh_down), full(ln_g), full(ln_b),
                  pl.BlockSpec(memory_space=pl.ANY)],
        out_specs=tok,
        out_shape=jax.ShapeDtypeStruct((t, d), F32),
        scratch_shapes=[pltpu.VMEM((k, TD, d), F32), pltpu.SemaphoreType.DMA((1,))],
        compiler_params=_cparams(("arbitrary",)),
        name="moe_combine",
    )(dest, w_tok, xn, h2, modp, sh_gate, sh_up, sh_down, ln_g, ln_b, ys)


def _moe(xn, h2, scores_t, modp, seg_td, router_bias, w_gate, w_up, w_down, sh_gate, sh_up, sh_down,
         ln_g, ln_b, tri, *, alpha):
    t = xn.shape[0]
    n_exp = scores_t.shape[0]
    idx, w_sel, rank, cnt = _route(scores_t, router_bias.reshape(n_exp, 1), tri)
    counts = cnt[:, 0].astype(I32)
    padded = (counts + MOE_BLK - 1) // MOE_BLK * MOE_BLK
    pad_end = jnp.cumsum(padded)
    pad_start = pad_end - padded
    n_slots = t * TOP_K + n_exp * MOE_BLK
    n_blocks = n_slots // MOE_BLK
    block_e = jnp.minimum(jnp.searchsorted(pad_end, jnp.arange(n_blocks, dtype=I32) * MOE_BLK, side="right"),
                          n_exp - 1).astype(I32)
    n_used = (pad_end[-1:] // MOE_BLK).astype(I32)
    dest = _dest_slots(pad_start.astype(I32), idx, rank)
    xs = _dispatch((pad_start + counts).astype(I32), pad_end.astype(I32), dest, h2, n_slots)
    ys = _experts(block_e, n_used, xs, w_gate, w_up, w_down)
    return _combine(dest, w_sel.T, xn, h2, modp, sh_gate, sh_up, sh_down, ln_g, ln_b, ys, seg_td, alpha=alpha)


def kernel(x, c, ctx, c_ctx, w_mod, b_mod, w_in, b_in, lam_q1, lam_k1, lam_q2, lam_k2, attn_norm_g, sg_ln_g, sg_ln_b, sg_w, sg_b, conv_w, conv_b, lru_wa, lru_ba, lru_wx, lru_bx, lru_lam, w_branch, w_out, ln1_g, ln1_b, w_router, router_bias, moe_w_gate, moe_w_up, moe_w_down, sh_w_gate, sh_w_up, sh_w_down, ln2_g, ln2_b):
    batch, n_lat, d = x.shape
    n_ctx = ctx.shape[1]
    depth = w_mod.shape[0]
    t_lat = batch * n_lat
    t_ctx = batch * n_ctx
    t_all = t_lat + t_ctx
    assert n_lat % TM == 0 and t_ctx % TM == 0 and n_ctx % B_CHUNK == 0 and t_lat % n_ctx == 0
    assert batch + 1 <= MOD_ROWS and TM % TD == 0
    alpha = (2 * depth) ** 0.25
    cw = conv_w.shape[2]
    tiles_per_batch = n_lat // TM

    seg = lambda i: jnp.minimum(i // tiles_per_batch, batch)
    seg_td = lambda i: jnp.minimum(i // (n_lat // TD), batch)
    tbl_idx = lambda i: jnp.where(i < t_lat // TM, i % tiles_per_batch, tiles_per_batch)

    x_all = jnp.concatenate([x.reshape(t_lat, d), ctx.reshape(t_ctx, d)], axis=0)
    c_all = jnp.zeros((MOD_ROWS, d), F32).at[:batch].set(c).at[batch].set(c_ctx)
    tables = _rope_tables(n_lat)
    tri = (jnp.arange(TM)[:, None] <= jnp.arange(TM)[None, :]).astype(BF16)
    row = lambda v: v.reshape(1, -1)

    def dense_blocks(w):
        nb, bi, bj = w.shape
        eye = jnp.eye(nb, dtype=w.dtype)
        return (w[:, :, None, :] * eye[:, None, :, None]).reshape(nb * bi, nb * bj)

    for l in range(depth):
        last = l == depth - 1
        lam_init = 0.8 - 0.6 * math.exp(-0.3 * l)
        mod = _mod_rows(c_all, w_mod[l].astype(BF16), row(b_mod[l]))
        modp = mod.reshape(MOD_ROWS, 6, 1, d).transpose(1, 0, 2, 3)

        proj = _in_proj(x_all, modp, w_in[l].astype(BF16), row(b_in[l]), seg)
        qr, kr, vb = _rope(proj, tables, tbl_idx)

        lam_vecs = jnp.stack([lam_q1[l], lam_k1[l], lam_q2[l], lam_k2[l]])
        gain = attn_norm_g[l].reshape(A_HEADS, 1, A_V_DIM)
        attn = functools.partial(_attention, qr, kr, vb, lam_vecs, gain, lam_init=lam_init,
                                 batch=batch, n_lat=n_lat, n_ctx=n_ctx)
        o_a = attn(latent_queries=True)

        o_b = _spatial_gating(proj, row(sg_ln_g[l]), row(sg_ln_b[l]), sg_w[l].astype(BF16),
                              sg_b[l].reshape(B_GROUPS, B_CHUNK, 1))

        lru = functools.partial(_lru_pass, proj, conv_w=conv_w[l], conv_b=row(conv_b[l]), batch=batch)
        zeros_h = jnp.zeros((batch, 8, cw), F32)
        o_c_parts = None
        hf = {}
        for direction in range(2):
            wd = jnp.concatenate([dense_blocks(lru_wa[l, direction]), dense_blocks(lru_wx[l, direction])],
                                 axis=1).astype(BF16)
            par = dict(w_dense=wd, ba=row(lru_ba[l, direction]), bx=row(lru_bx[l, direction]),
                       lam=row(lru_lam[l, direction]), reverse=direction == 1, final=direction == 1)
            h_ctx, edge = lru(hf.get("ctx"), zeros_h, seq=n_ctx, tl=n_ctx, row0=t_lat, **par)
            h_lat, _ = lru(hf.get("lat"), edge, seq=n_lat, tl=TM, row0=0, **par)
            hf = {"ctx": h_ctx, "lat": h_lat}
        o_c_lat, o_c_ctx = hf["lat"], hf["ctx"]

        n_rows = t_lat if last else t_all
        if last:
            o_a_all, o_c = o_a, o_c_lat
        else:
            o_a_all = jnp.concatenate([o_a, attn(latent_queries=False)], axis=0)
            o_c = jnp.concatenate([o_c_lat, o_c_ctx], axis=0)
        wr_t = w_router[l].T
        wr_hi = wr_t.astype(BF16)
        wr_lo = (wr_t - wr_hi.astype(F32)).astype(BF16)
        xn, h2, scores_t = _merge(x_all, o_a_all, o_b, o_c, proj, modp, w_branch[l].astype(BF16),
                                  w_out[l].astype(BF16), row(ln1_g[l]), row(ln1_b[l]), wr_hi, wr_lo, seg,
                                  n_rows=n_rows, alpha=alpha)
        x_all = _moe(xn, h2, scores_t, modp, seg_td, router_bias[l], moe_w_gate[l].astype(BF16),
                     moe_w_up[l].astype(BF16), moe_w_down[l].astype(BF16), sh_w_gate[l].astype(BF16),
                     sh_w_up[l].astype(BF16), sh_w_down[l].astype(BF16), row(ln2_g[l]), row(ln2_b[l]), tri,
                     alpha=alpha)
    return x_all[:t_lat].reshape(batch, n_lat, d)
```

```python
import functools
import math

import jax
import jax.numpy as jnp
from jax import lax
from jax.experimental import pallas as pl
from jax.experimental.pallas import tpu as pltpu

F32 = jnp.float32
BF16 = jnp.bfloat16
I32 = jnp.int32

A_HEADS = 4
A_QK_DIM = 64
A_V_DIM = 2 * A_QK_DIM
GRID_W = 64
ROPE_THETA = 10000.0
B_CHUNK = 128
B_GROUPS = 4
C_BLOCKS = 8
C_POW = 8.0
TOP_K = 8
N_GROUPS = 8
TOPK_GROUPS = 4
ROUTED_SCALE = 2.5
LN_EPS = 1e-6
RMS_EPS = 1e-5

TM = 512
MOE_BLK = 256
TD = 256
MOD_ROWS = 16
VMEM_LIMIT = 56 * 1024 * 1024


def _cparams(sem):
    return pltpu.CompilerParams(dimension_semantics=sem, vmem_limit_bytes=VMEM_LIMIT)


def _ln(x):
    mu = jnp.mean(x, axis=-1, keepdims=True)
    xc = x - mu
    var = jnp.mean(xc * xc, axis=-1, keepdims=True)
    return xc * lax.rsqrt(var + LN_EPS)


def _gelu(x):
    cdf = 0.5 * (1.0 + jnp.tanh(math.sqrt(2.0 / math.pi) * (x + 0.044715 * (x * x * x))))
    return x * cdf


def _sigmoid(x):
    return 1.0 / (1.0 + jnp.exp(-x))


def _silu(x):
    return x * _sigmoid(x)


def _dot(a, b):
    return jnp.dot(a, b, preferred_element_type=F32)


def _dot_nt(a, b):
    return lax.dot_general(a, b, (((1,), (1,)), ((), ())), preferred_element_type=F32)


def _mod_kernel(c_ref, w_ref, b_ref, o_ref):
    o_ref[...] = _dot(_silu(c_ref[...]).astype(BF16), w_ref[...]) + b_ref[...]


def _mod_rows(c_all, w, b):
    m, d = c_all.shape
    n = w.shape[1]
    tn = 1536
    return pl.pallas_call(
        _mod_kernel,
        grid=(n // tn,),
        in_specs=[pl.BlockSpec((m, d), lambda j: (0, 0)),
                  pl.BlockSpec((d, tn), lambda j: (0, j)),
                  pl.BlockSpec((1, tn), lambda j: (0, j))],
        out_specs=pl.BlockSpec((m, tn), lambda j: (0, j)),
        out_shape=jax.ShapeDtypeStruct((m, n), F32),
        compiler_params=_cparams(("parallel",)),
        name="adaln_rows",
    )(c_all, w, b)


def _in_proj_kernel(x_ref, sh_ref, sc_ref, w_ref, b_ref, o_ref, h_scr):
    @pl.when(pl.program_id(1) == 0)
    def _():
        h = _ln(x_ref[...]) * (1.0 + sc_ref[...]) + sh_ref[...]
        h_scr[...] = h.astype(BF16)

    o_ref[...] = _dot(h_scr[...], w_ref[...]) + b_ref[...]


def _in_proj(x_all, modp, w, b, seg, tm):
    t, d = x_all.shape
    n = w.shape[1]
    tn = 512
    return pl.pallas_call(
        _in_proj_kernel,
        grid=(t // tm, n // tn),
        in_specs=[pl.BlockSpec((tm, d), lambda i, j: (i, 0)),
                  pl.BlockSpec((None, None, 1, d), lambda i, j: (0, seg(i), 0, 0)),
                  pl.BlockSpec((None, None, 1, d), lambda i, j: (1, seg(i), 0, 0)),
                  pl.BlockSpec((d, tn), lambda i, j: (0, j)),
                  pl.BlockSpec((1, tn), lambda i, j: (0, j))],
        out_specs=pl.BlockSpec((tm, tn), lambda i, j: (i, j)),
        out_shape=jax.ShapeDtypeStruct((t, n), F32),
        scratch_shapes=[pltpu.VMEM((tm, d), BF16)],
        compiler_params=_cparams(("parallel", "arbitrary")),
        name="in_proj",
    )(x_all, modp, modp, w, b)


def _rope_kernel(q_ref, k_ref, v_ref, cos_ref, sa_ref, sb_ref, qo_ref, ko_ref, vo_ref):
    cos = cos_ref[...]
    sa = sa_ref[...]
    sb = sb_ref[...]

    def rope(x):
        w = x.shape[-1]
        return x * cos + pltpu.roll(x, w - 16, 1) * sa + pltpu.roll(x, 16, 1) * sb

    qo_ref[...] = (rope(q_ref[...]) * (A_QK_DIM ** -0.5)).astype(BF16)
    ko_ref[...] = rope(k_ref[...]).astype(BF16)
    vo_ref[...] = v_ref[...].astype(BF16)


def _rope_tables(n):
    rows = n // GRID_W
    pos_row = jnp.repeat(jnp.arange(rows), GRID_W).astype(F32)
    pos_col = jnp.tile(jnp.arange(GRID_W), rows).astype(F32)
    quarter = A_QK_DIM // 4
    inv = ROPE_THETA ** (-jnp.arange(quarter, dtype=F32) / quarter)
    ang_r = pos_row[:, None] * inv
    ang_c = pos_col[:, None] * inv
    ang = jnp.concatenate([ang_r, ang_r, ang_c, ang_c], axis=-1)
    cos = jnp.cos(ang)
    sin = jnp.sin(ang)
    first_half = (jnp.arange(A_QK_DIM) % (2 * quarter)) < quarter
    sa = jnp.where(first_half, -sin, 0.0)
    sb = jnp.where(first_half, 0.0, sin)
    reps = 2 * A_HEADS

    def widen(tbl, fill):
        tbl = jnp.tile(tbl, (1, reps))
        return jnp.concatenate([tbl, jnp.full((TM, tbl.shape[1]), fill, F32)], axis=0)

    return widen(cos, 1.0), widen(sa, 0.0), widen(sb, 0.0)


def _rope(proj, tables, tbl_idx):
    t = proj.shape[0]
    w = 2 * A_HEADS * A_QK_DIM
    cos, sa, sb = tables
    col = lambda c: pl.BlockSpec((TM, w), lambda i, c=c: (i, c))
    tbl = pl.BlockSpec((TM, w), lambda i: (tbl_idx(i), 0))
    out = pl.BlockSpec((TM, w), lambda i: (i, 0))
    shp = jax.ShapeDtypeStruct((t, w), BF16)
    return pl.pallas_call(
        _rope_kernel,
        grid=(t // TM,),
        in_specs=[col(0), col(1), col(2), tbl, tbl, tbl],
        out_specs=[out, out, out],
        out_shape=[shp, shp, shp],
        compiler_params=_cparams(("parallel",)),
        name="rope_cast",
    )(proj, proj, proj, cos, sa, sb)


def _attn_kernel(*refs, lam_init, with_lat):
    if with_lat:
        lam_ref, gain_ref, q_ref, kc_ref, vc_ref, kl_ref, vl_ref, o_ref = refs
    else:
        lam_ref, gain_ref, q_ref, kc_ref, vc_ref, o_ref = refs
    lv = lam_ref[...]
    lam = (jnp.exp(jnp.sum(lv[0:1] * lv[1:2], axis=-1, keepdims=True))
           - jnp.exp(jnp.sum(lv[2:3] * lv[3:4], axis=-1, keepdims=True)) + lam_init)
    outs = []
    for c in range(2):
        cols = slice(c * A_QK_DIM, (c + 1) * A_QK_DIM)
        qc = q_ref[:, cols]
        s_c = _dot_nt(qc, kc_ref[:, cols])
        m = jnp.max(s_c, axis=-1, keepdims=True)
        if with_lat:
            s_l = _dot_nt(qc, kl_ref[:, cols])
            m = jnp.maximum(m, jnp.max(s_l, axis=-1, keepdims=True))
        p_c = jnp.exp(s_c - m)
        den = jnp.sum(p_c, axis=-1, keepdims=True)
        pv = _dot(p_c.astype(BF16), vc_ref[...])
        if with_lat:
            p_l = jnp.exp(s_l - m)
            den = den + jnp.sum(p_l, axis=-1, keepdims=True)
            pv = pv + _dot(p_l.astype(BF16), vl_ref[...])
        outs.append(pv / den)
    o = outs[0] - lam * outs[1]
    o = o * lax.rsqrt(jnp.mean(o * o, axis=-1, keepdims=True) + RMS_EPS)
    o_ref[...] = (o * gain_ref[...]) * (1.0 - lam_init)


def _attention(qr, kr, vb, lam_vecs, gain, *, lam_init, batch, n_lat, n_ctx, latent_queries):
    hw = A_V_DIM
    ctx_blk0 = batch * n_lat // n_ctx
    lam_spec = pl.BlockSpec(lam_vecs.shape, lambda b, h, i: (0, 0))
    gain_spec = pl.BlockSpec((None, 1, hw), lambda b, h, i: (h, 0, 0))
    kv_ctx = pl.BlockSpec((n_ctx, hw), lambda b, h, i: (ctx_blk0 + b, h))
    if latent_queries:
        tq = 256
        nq = n_lat // tq
        q_spec = pl.BlockSpec((tq, hw), lambda b, h, i: (b * nq + i, h))
        kv_lat = pl.BlockSpec((n_lat, hw), lambda b, h, i: (b, h))
        in_specs = [lam_spec, gain_spec, q_spec, kv_ctx, kv_ctx, kv_lat, kv_lat]
        args = (lam_vecs, gain, qr, kr, vb, kr, vb)
        rows = batch * n_lat
    else:
        tq = n_ctx
        nq = 1
        q_spec = pl.BlockSpec((tq, hw), lambda b, h, i: (ctx_blk0 + b, h))
        in_specs = [lam_spec, gain_spec, q_spec, kv_ctx, kv_ctx]
        args = (lam_vecs, gain, qr, kr, vb)
        rows = batch * n_ctx
    return pl.pallas_call(
        functools.partial(_attn_kernel, lam_init=lam_init, with_lat=latent_queries),
        grid=(batch, A_HEADS, nq),
        in_specs=in_specs,
        out_specs=pl.BlockSpec((tq, hw), lambda b, h, i: (b * nq + i, h)),
        out_shape=jax.ShapeDtypeStruct((rows, A_HEADS * hw), F32),
        compiler_params=_cparams(("parallel", "parallel", "arbitrary")),
        name="diff_attn_lat" if latent_queries else "diff_attn_ctx",
    )(*args)


def _sg_kernel(u_ref, s_ref, g_ref, b_ref, w_ref, bs_ref, o_ref):
    gw = B_CHUNK
    for ci in range(TM // B_CHUNK):
        rows = slice(ci * B_CHUNK, (ci + 1) * B_CHUNK)
        for g in range(B_GROUPS):
            cols = slice(g * gw, (g + 1) * gw)
            vn = _ln(_gelu(s_ref[rows, cols])) * g_ref[:, cols] + b_ref[:, cols]
            mixed = _dot(w_ref[g], vn.astype(BF16)) + bs_ref[g]
            o_ref[rows, cols] = _gelu(u_ref[rows, cols]) * mixed


def _spatial_gating(proj, ln_g, ln_b, w_s, b_s):
    t = proj.shape[0]
    w = B_GROUPS * B_CHUNK
    full = lambda a: pl.BlockSpec(a.shape, lambda i: (0,) * a.ndim)
    return pl.pallas_call(
        _sg_kernel,
        grid=(t // TM,),
        in_specs=[pl.BlockSpec((TM, w), lambda i: (i, 3)),
                  pl.BlockSpec((TM, w), lambda i: (i, 4)),
                  full(ln_g), full(ln_b), full(w_s), full(b_s)],
        out_specs=pl.BlockSpec((TM, w), lambda i: (i, 0)),
        out_shape=jax.ShapeDtypeStruct((t, w), F32),
        compiler_params=_cparams(("parallel",)),
        name="spatial_gating",
    )(proj, proj, ln_g, ln_b, w_s, b_s)


def _lru_kernel(*refs, reverse, final, n_tiles, tl):
    if final:
        (x_ref, xp_ref, xn_ref, cw_ref, cb_ref, w_ref, ba_ref, bx_ref, lam_ref, h0_ref,
         hf_ref, y_ref, o_ref, hl_ref, carry) = refs
    else:
        (x_ref, xp_ref, xn_ref, cw_ref, cb_ref, w_ref, ba_ref, bx_ref, lam_ref, h0_ref,
         o_ref, hl_ref, carry) = refs
    step = pl.program_id(1)

    @pl.when(step == 0)
    def _():
        carry[...] = h0_ref[...]

    ti = (n_tiles - 1 - step) if reverse else step
    x = x_ref[...]
    cw = x.shape[-1]
    row = lax.broadcasted_iota(I32, x.shape, 0)
    has_prev = ti > 0
    has_next = ti < n_tiles - 1
    p6 = jnp.where(has_prev, xp_ref[6:7, :], 0.0)
    p7 = jnp.where(has_prev, xp_ref[7:8, :], 0.0)
    n0 = jnp.where(has_next, xn_ref[0:1, :], 0.0)
    xm1 = jnp.where(row == 0, p7, pltpu.roll(x, 1, 0))
    xm2 = jnp.where(row == 0, p6, jnp.where(row == 1, p7, pltpu.roll(x, 2, 0)))
    xp1 = jnp.where(row == tl - 1, n0, pltpu.roll(x, tl - 1, 0))
    taps = cw_ref[...]
    xs = taps[0:1] * xm2 + taps[1:2] * xm1 + taps[2:3] * x + taps[3:4] * xp1 + cb_ref[...]

    z = _dot(xs.astype(BF16), w_ref[...])
    r = _sigmoid(z[:, :cw] + ba_ref[...])
    ig = _sigmoid(z[:, cw:] + bx_ref[...])
    nl = -lam_ref[...]
    softplus = jnp.maximum(nl, 0.0) + jnp.log(1.0 + jnp.exp(-jnp.abs(nl)))
    log_a = -C_POW * r * softplus
    a = jnp.exp(log_a)
    u = jnp.sqrt(1.0 - jnp.exp(2.0 * log_a)) * (ig * xs)

    d = 1
    while d < tl:
        if reverse:
            ok = row < tl - d
            shift = tl - d
        else:
            ok = row >= d
            shift = d
        a_sh = jnp.where(ok, pltpu.roll(a, shift, 0), 1.0)
        u_sh = jnp.where(ok, pltpu.roll(u, shift, 0), 0.0)
        u = u + a * u_sh
        a = a * a_sh
        d *= 2
    h = u + a * carry[0:1, :]
    edge = h[0:1, :] if reverse else h[tl - 1:tl, :]
    carry[...] = jnp.broadcast_to(edge, carry.shape)
    hl_ref[...] = jnp.broadcast_to(edge, hl_ref.shape)
    if final:
        o_ref[...] = _gelu(y_ref[...]) * (hf_ref[...] + h)
    else:
        o_ref[...] = h


def _lru_pass(proj, hf, h0, conv_w, conv_b, w_dense, ba, bx, lam, *, batch, seq, tl, row0, reverse, final):
    t_all = proj.shape[0]
    cw = conv_w.shape[1]
    n_tiles = seq // tl
    base = row0 // tl
    sub = tl // 8
    last8 = t_all // 8 - 1

    def tile(b, s):
        ti = (n_tiles - 1 - s) if reverse else s
        return base + b * n_tiles + ti

    def loc(b, s):
        ti = (n_tiles - 1 - s) if reverse else s
        return b * n_tiles + ti

    full = lambda a: pl.BlockSpec(a.shape, lambda b, s: (0,) * a.ndim)
    in_specs = [pl.BlockSpec((tl, cw), lambda b, s: (tile(b, s), 5)),
                pl.BlockSpec((8, cw), lambda b, s: (jnp.maximum(tile(b, s) * sub - 1, 0), 5)),
                pl.BlockSpec((8, cw), lambda b, s: (jnp.minimum((tile(b, s) + 1) * sub, last8), 5)),
                full(conv_w), full(conv_b), full(w_dense), full(ba), full(bx), full(lam),
                pl.BlockSpec((None, 8, cw), lambda b, s: (b, 0, 0))]
    args = [proj, proj, proj, conv_w, conv_b, w_dense, ba, bx, lam, h0]
    if final:
        in_specs += [pl.BlockSpec((tl, cw), lambda b, s: (loc(b, s), 0)),
                     pl.BlockSpec((tl, cw), lambda b, s: (tile(b, s), 6))]
        args += [hf, proj]
    return pl.pallas_call(
        functools.partial(_lru_kernel, reverse=reverse, final=final, n_tiles=n_tiles, tl=tl),
        grid=(batch, n_tiles),
        in_specs=in_specs,
        out_specs=[pl.BlockSpec((tl, cw), lambda b, s: (loc(b, s), 0)),
                   pl.BlockSpec((None, 8, cw), lambda b, s: (b, 0, 0))],
        out_shape=[jax.ShapeDtypeStruct((batch * seq, cw), F32),
                   jax.ShapeDtypeStruct((batch, 8, cw), F32)],
        scratch_shapes=[pltpu.VMEM((8, cw), F32)],
        compiler_params=_cparams(("parallel", "arbitrary")),
        name="rglru_%s_%s" % ("bwd" if reverse else "fwd", "lat" if row0 == 0 else "ctx"),
    )(*args)


def _merge_kernel(x_ref, oa_ref, ob_ref, oc_ref, g0, g1, g2, g3, g4, g5, wb_ref, wo_ref,
                  gate1_ref, sh2_ref, sc2_ref, lng_ref, lnb_ref, wrh_ref, wrl_ref,
                  xn_ref, h2_ref, sc_ref, *, alpha):
    gates = ((g0, g1), (g2, g3), (g4, g5))
    half = g0.shape[-1]
    mix = None
    for r, o_ref in enumerate((oa_ref, ob_ref, oc_ref)):
        proj = _dot(o_ref[...].astype(BF16), wb_ref[r])
        gate = jnp.concatenate([_sigmoid(gates[r][0][...]), _sigmoid(gates[r][1][...])], axis=-1)
        term = gate * proj
        mix = term if mix is None else mix + term
    out = _dot(mix.astype(BF16), wo_ref[...])
    xn = _ln(alpha * x_ref[...] + gate1_ref[...] * out) * lng_ref[...] + lnb_ref[...]
    xn_ref[...] = xn
    h2 = _ln(xn) * (1.0 + sc2_ref[...]) + sh2_ref[...]
    h2_ref[...] = h2
    h_hi = h2.astype(BF16)
    h_lo = (h2 - h_hi.astype(F32)).astype(BF16)
    logits = _dot_nt(wrh_ref[...], h_hi) + (_dot_nt(wrh_ref[...], h_lo) + _dot_nt(wrl_ref[...], h_hi))
    sc_ref[...] = _sigmoid(logits)


def _merge(x_all, o_a, o_b, o_c, proj, modp, w_branch, w_out, ln_g, ln_b, wr_hi, wr_lo, seg, *, n_rows, alpha):
    d = x_all.shape[1]
    bw = o_a.shape[1]
    n_exp = wr_hi.shape[0]
    tok = lambda w: pl.BlockSpec((TM, w), lambda i: (i, 0))
    gate = lambda c: pl.BlockSpec((TM, bw), lambda i, c=c: (i, c))
    mod = lambda k: pl.BlockSpec((None, None, 1, d), lambda i, k=k: (k, seg(i), 0, 0))
    full = lambda a: pl.BlockSpec(a.shape, lambda i: (0,) * a.ndim)
    return pl.pallas_call(
        functools.partial(_merge_kernel, alpha=alpha),
        grid=(n_rows // TM,),
        in_specs=[tok(d), tok(bw), tok(bw), tok(bw)] + [gate(7 + c) for c in range(6)]
                 + [full(w_branch), full(w_out), mod(2), mod(3), mod(4), full(ln_g), full(ln_b),
                    full(wr_hi), full(wr_lo)],
        out_specs=[tok(d), tok(d), pl.BlockSpec((n_exp, TM), lambda i: (0, i))],
        out_shape=[jax.ShapeDtypeStruct((n_rows, d), F32),
                   jax.ShapeDtypeStruct((n_rows, d), F32),
                   jax.ShapeDtypeStruct((n_exp, n_rows), F32)],
        compiler_params=_cparams(("parallel",)),
        name="merge_residual_router",
    )(x_all, o_a, o_b, o_c, proj, proj, proj, proj, proj, proj, w_branch, w_out,
      modp, modp, modp, ln_g, ln_b, wr_hi, wr_lo)


def _route_kernel(s_ref, bias_ref, tri_ref, idx_ref, w_ref, rank_ref, cnt_ref, carry):
    @pl.when(pl.program_id(0) == 0)
    def _():
        carry[...] = jnp.zeros_like(carry)

    s = s_ref[...]
    n_exp, tn = s.shape
    per = n_exp // N_GROUPS
    neg = -jnp.inf
    biased = s + bias_ref[...]
    sub = lax.broadcasted_iota(I32, (per, tn), 0)
    gs_rows = []
    for g in range(N_GROUPS):
        blk = biased[g * per:(g + 1) * per, :]
        m1 = jnp.max(blk, axis=0, keepdims=True)
        first = jnp.min(jnp.where(blk == m1, sub, per), axis=0, keepdims=True)
        m2 = jnp.max(jnp.where(sub == first, neg, blk), axis=0, keepdims=True)
        gs_rows.append(m1 + m2)
    gs = jnp.concatenate(gs_rows, axis=0)
    gi = lax.broadcasted_iota(I32, gs.shape, 0)
    g_ok = jnp.zeros(gs.shape, F32)
    cur = gs
    for _ in range(TOPK_GROUPS):
        m = jnp.max(cur, axis=0, keepdims=True)
        pick = jnp.min(jnp.where(cur == m, gi, N_GROUPS), axis=0, keepdims=True)
        hit = gi == pick
        g_ok = jnp.where(hit, 1.0, g_ok)
        cur = jnp.where(hit, neg, cur)
    ok_rows = [jnp.broadcast_to(g_ok[g:g + 1, :], (per, tn)) for g in range(N_GROUPS)]
    expert_ok = jnp.concatenate(ok_rows, axis=0)
    masked = jnp.where(expert_ok > 0.0, biased, neg)
    ei = lax.broadcasted_iota(I32, s.shape, 0)
    picks, pick_s, hits = [], [], []
    sel = jnp.zeros(s.shape, F32)
    for _ in range(TOP_K):
        m = jnp.max(masked, axis=0, keepdims=True)
        pick = jnp.min(jnp.where(masked == m, ei, n_exp), axis=0, keepdims=True)
        hit = ei == pick
        picks.append(pick)
        pick_s.append(jnp.sum(jnp.where(hit, s, 0.0), axis=0, keepdims=True))
        hits.append(hit)
        sel = jnp.where(hit, 1.0, sel)
        masked = jnp.where(hit, neg, masked)
    tot = pick_s[0]
    for k in range(1, TOP_K):
        tot = tot + pick_s[k]
    idx_ref[...] = jnp.concatenate(picks, axis=0)
    w_ref[...] = jnp.concatenate([p / tot * ROUTED_SCALE for p in pick_s], axis=0)
    incl = _dot(sel.astype(BF16), tri_ref[...])
    before = incl - sel + carry[:, 0:1]
    ranks = [jnp.sum(jnp.where(hit, before, 0.0), axis=0, keepdims=True) for hit in hits]
    rank_ref[...] = jnp.concatenate(ranks, axis=0).astype(I32)
    total = carry[:, 0:1] + incl[:, tn - 1:tn]
    carry[...] = jnp.broadcast_to(total, carry.shape)
    cnt_ref[...] = jnp.broadcast_to(total, cnt_ref.shape)


def _route(scores_t, bias_col, tri):
    n_exp, t = scores_t.shape
    return pl.pallas_call(
        _route_kernel,
        grid=(t // TM,),
        in_specs=[pl.BlockSpec((n_exp, TM), lambda i: (0, i)),
                  pl.BlockSpec((n_exp, 1), lambda i: (0, 0)),
                  pl.BlockSpec((TM, TM), lambda i: (0, 0))],
        out_specs=[pl.BlockSpec((TOP_K, TM), lambda i: (0, i)),
                   pl.BlockSpec((TOP_K, TM), lambda i: (0, i)),
                   pl.BlockSpec((TOP_K, TM), lambda i: (0, i)),
                   pl.BlockSpec((n_exp, 128), lambda i: (0, 0))],
        out_shape=[jax.ShapeDtypeStruct((TOP_K, t), I32),
                   jax.ShapeDtypeStruct((TOP_K, t), F32),
                   jax.ShapeDtypeStruct((TOP_K, t), I32),
                   jax.ShapeDtypeStruct((n_exp, 128), F32)],
        scratch_shapes=[pltpu.VMEM((n_exp, 128), F32)],
        compiler_params=_cparams(("arbitrary",)),
        name="route_topk",
    )(scores_t, bias_col, tri)


def _dest_kernel(start_ref, idx_ref, rank_ref, o_ref, *, n_exp):
    idx = idx_ref[...]
    acc = rank_ref[...]
    for e in range(n_exp):
        acc = acc + jnp.where(idx == e, start_ref[e], 0)
    o_ref[...] = acc


def _dest_slots(pad_start, idx, rank):
    k, t = idx.shape
    blk = lambda: pl.BlockSpec((k, TM), lambda i, s: (0, i))
    return pl.pallas_call(
        functools.partial(_dest_kernel, n_exp=pad_start.shape[0]),
        grid_spec=pltpu.PrefetchScalarGridSpec(
            num_scalar_prefetch=1, grid=(t // TM,), in_specs=[blk(), blk()], out_specs=blk()),
        out_shape=jax.ShapeDtypeStruct((k, t), I32),
        compiler_params=_cparams(("parallel",)),
        name="route_slots",
    )(pad_start, idx, rank)


def _dispatch_kernel(pad_lo_ref, pad_hi_ref, dest_ref, h_ref, xs_ref, zero_scr, sem, *, n_exp):
    k_top, td = dest_ref.shape

    def row_copy(t, slot):
        return pltpu.make_async_copy(h_ref.at[pl.ds(t, 1), :], xs_ref.at[pl.ds(slot, 1), :], sem.at[0])

    def zero_copy(slot):
        return pltpu.make_async_copy(zero_scr.at[pl.ds(0, 1), :], xs_ref.at[pl.ds(slot, 1), :], sem.at[1])

    @pl.when(pl.program_id(0) == 0)
    def _():
        zero_scr[...] = jnp.zeros_like(zero_scr)
        for e in range(n_exp):
            lo = pad_lo_ref[e]
            hi = pad_hi_ref[e]

            def start(r, c):
                zero_copy(r).start()
                return c

            def wait(r, c):
                zero_copy(r).wait()
                return c

            lax.fori_loop(lo, hi, start, 0)
            lax.fori_loop(lo, hi, wait, 0)

    def start_tok(t, c):
        for k in range(k_top):
            row_copy(t, dest_ref[k, t]).start()
        return c

    def wait_tok(t, c):
        for k in range(k_top):
            row_copy(t, dest_ref[k, t]).wait()
        return c

    lax.fori_loop(0, td, start_tok, 0)
    lax.fori_loop(0, td, wait_tok, 0)


def _dispatch(pad_lo, pad_hi, dest, h2, n_slots):
    t, d = h2.shape
    k = dest.shape[0]
    return pl.pallas_call(
        functools.partial(_dispatch_kernel, n_exp=pad_lo.shape[0]),
        grid_spec=pltpu.PrefetchScalarGridSpec(
            num_scalar_prefetch=2, grid=(t // TD,),
            in_specs=[pl.BlockSpec((k, TD), lambda i, a, b: (0, i), memory_space=pltpu.SMEM),
                      pl.BlockSpec((TD, d), lambda i, a, b: (i, 0))],
            out_specs=pl.BlockSpec(memory_space=pl.ANY),
            scratch_shapes=[pltpu.VMEM((8, d), F32), pltpu.SemaphoreType.DMA((2,))]),
        out_shape=jax.ShapeDtypeStruct((n_slots, d), F32),
        compiler_params=_cparams(("arbitrary",)),
        name="moe_dispatch",
    )(pad_lo, pad_hi, dest, h2)


def _expert_kernel(be_ref, nu_ref, x_ref, wg_ref, wu_ref, wd_ref, o_ref):
    @pl.when(pl.program_id(0) < nu_ref[0])
    def _():
        x = x_ref[...].astype(BF16)
        act = _silu(_dot(x, wg_ref[...])) * _dot(x, wu_ref[...])
        o_ref[...] = _dot(act.astype(BF16), wd_ref[...])


def _experts(block_e, n_used, xs, w_gate, w_up, w_down):
    n_slots, d = xs.shape
    de = w_gate.shape[2]
    blk = lambda i, be, nu: jnp.minimum(i, nu[0] - 1)
    return pl.pallas_call(
        _expert_kernel,
        grid_spec=pltpu.PrefetchScalarGridSpec(
            num_scalar_prefetch=2, grid=(n_slots // MOE_BLK,),
            in_specs=[pl.BlockSpec((MOE_BLK, d), lambda i, be, nu: (blk(i, be, nu), 0)),
                      pl.BlockSpec((None, d, de), lambda i, be, nu: (be[blk(i, be, nu)], 0, 0)),
                      pl.BlockSpec((None, d, de), lambda i, be, nu: (be[blk(i, be, nu)], 0, 0)),
                      pl.BlockSpec((None, de, d), lambda i, be, nu: (be[blk(i, be, nu)], 0, 0))],
            out_specs=pl.BlockSpec((MOE_BLK, d), lambda i, be, nu: (blk(i, be, nu), 0))),
        out_shape=jax.ShapeDtypeStruct((n_slots, d), F32),
        compiler_params=_cparams(("arbitrary",)),
        name="moe_experts",
    )(block_e, n_used, xs, w_gate, w_up, w_down)


def _combine_kernel(dest_ref, w_ref, x_ref, h_ref, gate2_ref, sg_ref, su_ref, sd_ref, lng_ref, lnb_ref,
                    ys_ref, o_ref, gbuf, sem, *, alpha):
    k_top, td = dest_ref.shape

    def row_copy(t, k):
        return pltpu.make_async_copy(ys_ref.at[pl.ds(dest_ref[k, t], 1), :], gbuf.at[k, pl.ds(t, 1), :], sem.at[0])

    def start_tok(t, c):
        for k in range(k_top):
            row_copy(t, k).start()
        return c

    def wait_tok(t, c):
        for k in range(k_top):
            row_copy(t, k).wait()
        return c

    lax.fori_loop(0, td, start_tok, 0)
    h = h_ref[...].astype(BF16)
    shared = _dot((_silu(_dot(h, sg_ref[...])) * _dot(h, su_ref[...])).astype(BF16), sd_ref[...])
    lax.fori_loop(0, td, wait_tok, 0)
    w = w_ref[...]
    routed = gbuf[0] * w[:, 0:1]
    for k in range(1, k_top):
        routed = routed + gbuf[k] * w[:, k:k + 1]
    y = routed + shared
    o_ref[...] = _ln(alpha * x_ref[...] + gate2_ref[...] * y) * lng_ref[...] + lnb_ref[...]


def _combine(dest, w_tok, xn, h2, modp, sh_gate, sh_up, sh_down, ln_g, ln_b, ys, seg_td, *, alpha):
    t, d = xn.shape
    k = dest.shape[0]
    tok = pl.BlockSpec((TD, d), lambda i: (i, 0))
    full = lambda a: pl.BlockSpec(a.shape, lambda i: (0,) * a.ndim)
    return pl.pallas_call(
        functools.partial(_combine_kernel, alpha=alpha),
        grid=(t // TD,),
        in_specs=[pl.BlockSpec((k, TD), lambda i: (0, i), memory_space=pltpu.SMEM),
                  pl.BlockSpec((TD, k), lambda i: (i, 0)),
                  tok, tok,
                  pl.BlockSpec((None, None, 1, d), lambda i: (5, seg_td(i), 0, 0)),
                  full(sh_gate), full(sh_up), full(sh_down), full(ln_g), full(ln_b),
                  pl.BlockSpec(memory_space=pl.ANY)],
        out_specs=tok,
        out_shape=jax.ShapeDtypeStruct((t, d), F32),
        scratch_shapes=[pltpu.VMEM((k, TD, d), F32), pltpu.SemaphoreType.DMA((1,))],
        compiler_params=_cparams(("arbitrary",)),
        name="moe_combine",
    )(dest, w_tok, xn, h2, modp, sh_gate, sh_up, sh_down, ln_g, ln_b, ys)


def _moe(xn, h2, scores_t, modp, seg_td, router_bias, w_gate, w_up, w_down, sh_gate, sh_up, sh_down,
         ln_g, ln_b, tri, *, alpha):
    t = xn.shape[0]
    n_exp = scores_t.shape[0]
    idx, w_sel, rank, cnt = _route(scores_t, router_bias.reshape(n_exp, 1), tri)
    counts = cnt[:, 0].astype(I32)
    padded = (counts + MOE_BLK - 1) // MOE_BLK * MOE_BLK
    pad_end = jnp.cumsum(padded)
    pad_start = pad_end - padded
    n_slots = t * TOP_K + n_exp * MOE_BLK
    n_blocks = n_slots // MOE_BLK
    block_row0 = jnp.arange(n_blocks, dtype=I32) * MOE_BLK
    block_e = jnp.minimum(jnp.sum(pad_end[None, :] <= block_row0[:, None], axis=1), n_exp - 1).astype(I32)
    n_used = (pad_end[-1:] // MOE_BLK).astype(I32)
    dest = _dest_slots(pad_start.astype(I32), idx, rank)
    xs = _dispatch((pad_start + counts).astype(I32), pad_end.astype(I32), dest, h2, n_slots)
    ys = _experts(block_e, n_used, xs, w_gate, w_up, w_down)
    return _combine(dest, w_sel.T, xn, h2, modp, sh_gate, sh_up, sh_down, ln_g, ln_b, ys, seg_td, alpha=alpha)


def kernel(x, c, ctx, c_ctx, w_mod, b_mod, w_in, b_in, lam_q1, lam_k1, lam_q2, lam_k2, attn_norm_g, sg_ln_g, sg_ln_b, sg_w, sg_b, conv_w, conv_b, lru_wa, lru_ba, lru_wx, lru_bx, lru_lam, w_branch, w_out, ln1_g, ln1_b, w_router, router_bias, moe_w_gate, moe_w_up, moe_w_down, sh_w_gate, sh_w_up, sh_w_down, ln2_g, ln2_b):
    batch, n_lat, d = x.shape
    n_ctx = ctx.shape[1]
    depth = w_mod.shape[0]
    t_lat = batch * n_lat
    t_ctx = batch * n_ctx
    t_all = t_lat + t_ctx
    assert n_lat % TM == 0 and t_ctx % TM == 0 and n_ctx % B_CHUNK == 0 and t_lat % n_ctx == 0
    assert batch + 1 <= MOD_ROWS and TM % TD == 0
    alpha = (2 * depth) ** 0.25
    cw = conv_w.shape[2]
    tiles_per_batch = n_lat // TM

    seg = lambda i: jnp.minimum(i // tiles_per_batch, batch)
    seg_td = lambda i: jnp.minimum(i // (n_lat // TD), batch)
    tbl_idx = lambda i: jnp.where(i < t_lat // TM, i % tiles_per_batch, tiles_per_batch)

    x_all = jnp.concatenate([x.reshape(t_lat, d), ctx.reshape(t_ctx, d)], axis=0)
    c_all = jnp.zeros((MOD_ROWS, d), F32).at[:batch].set(c).at[batch].set(c_ctx)
    tables = _rope_tables(n_lat)
    tri = (jnp.arange(TM)[:, None] <= jnp.arange(TM)[None, :]).astype(BF16)
    row = lambda v: v.reshape(1, -1)

    def dense_blocks(w):
        nb, bi, bj = w.shape
        eye = jnp.eye(nb, dtype=w.dtype)
        return (w[:, :, None, :] * eye[:, None, :, None]).reshape(nb * bi, nb * bj)

    for l in range(depth):
        last = l == depth - 1
        lam_init = 0.8 - 0.6 * math.exp(-0.3 * l)
        mod = _mod_rows(c_all, w_mod[l].astype(BF16), row(b_mod[l]))
        modp = mod.reshape(MOD_ROWS, 6, 1, d).transpose(1, 0, 2, 3)

        tm_in = 2 * TM if (n_lat % (2 * TM) == 0 and t_ctx % (2 * TM) == 0) else TM
        seg_in = lambda i: jnp.minimum(i // (n_lat // tm_in), batch)
        proj = _in_proj(x_all, modp, w_in[l].astype(BF16), row(b_in[l]), seg_in, tm_in)
        qr, kr, vb = _rope(proj, tables, tbl_idx)

        lam_vecs = jnp.stack([lam_q1[l], lam_k1[l], lam_q2[l], lam_k2[l]])
        gain = attn_norm_g[l].reshape(A_HEADS, 1, A_V_DIM)
        attn = functools.partial(_attention, qr, kr, vb, lam_vecs, gain, lam_init=lam_init,
                                 batch=batch, n_lat=n_lat, n_ctx=n_ctx)
        o_a = attn(latent_queries=True)

        o_b = _spatial_gating(proj, row(sg_ln_g[l]), row(sg_ln_b[l]), sg_w[l].astype(BF16),
                              sg_b[l].reshape(B_GROUPS, B_CHUNK, 1))

        lru = functools.partial(_lru_pass, proj, conv_w=conv_w[l], conv_b=row(conv_b[l]), batch=batch)
        zeros_h = jnp.zeros((batch, 8, cw), F32)
        o_c_parts = None
        hf = {}
        for direction in range(2):
            wd = jnp.concatenate([dense_blocks(lru_wa[l, direction]), dense_blocks(lru_wx[l, direction])],
                                 axis=1).astype(BF16)
            par = dict(w_dense=wd, ba=row(lru_ba[l, direction]), bx=row(lru_bx[l, direction]),
                       lam=row(lru_lam[l, direction]), reverse=direction == 1, final=direction == 1)
            h_ctx, edge = lru(hf.get("ctx"), zeros_h, seq=n_ctx, tl=n_ctx, row0=t_lat, **par)
            h_lat, _ = lru(hf.get("lat"), edge, seq=n_lat, tl=TM, row0=0, **par)
            hf = {"ctx": h_ctx, "lat": h_lat}
        o_c_lat, o_c_ctx = hf["lat"], hf["ctx"]

        n_rows = t_lat if last else t_all
        if last:
            o_a_all, o_c = o_a, o_c_lat
        else:
            o_a_all = jnp.concatenate([o_a, attn(latent_queries=False)], axis=0)
            o_c = jnp.concatenate([o_c_lat, o_c_ctx], axis=0)
        wr_t = w_router[l].T
        wr_hi = wr_t.astype(BF16)
        wr_lo = (wr_t - wr_hi.astype(F32)).astype(BF16)
        xn, h2, scores_t = _merge(x_all, o_a_all, o_b, o_c, proj, modp, w_branch[l].astype(BF16),
                                  w_out[l].astype(BF16), row(ln1_g[l]), row(ln1_b[l]), wr_hi, wr_lo, seg,
                                  n_rows=n_rows, alpha=alpha)
        x_all = _moe(xn, h2, scores_t, modp, seg_td, router_bias[l], moe_w_gate[l].astype(BF16),
                     moe_w_up[l].astype(BF16), moe_w_down[l].astype(BF16), sh_w_gate[l].astype(BF16),
                     sh_w_up[l].astype(BF16), sh_w_down[l].astype(BF16), row(ln2_g[l]), row(ln2_b[l]), tri,
                     alpha=alpha)
    return x_all[:t_lat].reshape(batch, n_lat, d)
```

```python
import functools
import math

import jax
import jax.numpy as jnp
from jax import lax
from jax.experimental import pallas as pl
from jax.experimental.pallas import tpu as pltpu

F32 = jnp.float32
BF16 = jnp.bfloat16
I32 = jnp.int32

A_HEADS = 4
A_QK_DIM = 64
A_V_DIM = 2 * A_QK_DIM
GRID_W = 64
ROPE_THETA = 10000.0
B_CHUNK = 128
B_GROUPS = 4
C_BLOCKS = 8
C_POW = 8.0
TOP_K = 8
N_GROUPS = 8
TOPK_GROUPS = 4
ROUTED_SCALE = 2.5
LN_EPS = 1e-6
RMS_EPS = 1e-5

TM = 512
MOE_BLK = 256
TD = 256
UNIT = 16
UNIT_LANES = 256
SORT_CHUNK = 512
MOD_ROWS = 16
VMEM_LIMIT = 56 * 1024 * 1024


def _cparams(sem):
    return pltpu.CompilerParams(dimension_semantics=sem, vmem_limit_bytes=VMEM_LIMIT)


def _ln(x):
    mu = jnp.mean(x, axis=-1, keepdims=True)
    xc = x - mu
    var = jnp.mean(xc * xc, axis=-1, keepdims=True)
    return xc * lax.rsqrt(var + LN_EPS)


def _gelu(x):
    cdf = 0.5 * (1.0 + jnp.tanh(math.sqrt(2.0 / math.pi) * (x + 0.044715 * (x * x * x))))
    return x * cdf


def _sigmoid(x):
    return 1.0 / (1.0 + jnp.exp(-x))


def _silu(x):
    return x * _sigmoid(x)


def _dot(a, b):
    return jnp.dot(a, b, preferred_element_type=F32)


def _dot_nt(a, b):
    return lax.dot_general(a, b, (((1,), (1,)), ((), ())), preferred_element_type=F32)


def _mod_kernel(c_ref, w_ref, b_ref, o_ref):
    o_ref[...] = _dot(_silu(c_ref[...]).astype(BF16), w_ref[...]) + b_ref[...]


def _mod_rows(c_all, w, b):
    m, d = c_all.shape
    n = w.shape[1]
    tn = 1536
    return pl.pallas_call(
        _mod_kernel,
        grid=(n // tn,),
        in_specs=[pl.BlockSpec((m, d), lambda j: (0, 0)),
                  pl.BlockSpec((d, tn), lambda j: (0, j)),
                  pl.BlockSpec((1, tn), lambda j: (0, j))],
        out_specs=pl.BlockSpec((m, tn), lambda j: (0, j)),
        out_shape=jax.ShapeDtypeStruct((m, n), F32),
        compiler_params=_cparams(("parallel",)),
        name="adaln_rows",
    )(c_all, w, b)


def _in_proj_kernel(x_ref, sh_ref, sc_ref, w_ref, b_ref, o_ref, h_scr):
    @pl.when(pl.program_id(1) == 0)
    def _():
        h = _ln(x_ref[...]) * (1.0 + sc_ref[...]) + sh_ref[...]
        h_scr[...] = h.astype(BF16)

    o_ref[...] = _dot(h_scr[...], w_ref[...]) + b_ref[...]


def _in_proj(x_all, modp, w, b, seg, tm):
    t, d = x_all.shape
    n = w.shape[1]
    tn = 512
    return pl.pallas_call(
        _in_proj_kernel,
        grid=(t // tm, n // tn),
        in_specs=[pl.BlockSpec((tm, d), lambda i, j: (i, 0)),
                  pl.BlockSpec((None, None, 1, d), lambda i, j: (0, seg(i), 0, 0)),
                  pl.BlockSpec((None, None, 1, d), lambda i, j: (1, seg(i), 0, 0)),
                  pl.BlockSpec((d, tn), lambda i, j: (0, j)),
                  pl.BlockSpec((1, tn), lambda i, j: (0, j))],
        out_specs=pl.BlockSpec((tm, tn), lambda i, j: (i, j)),
        out_shape=jax.ShapeDtypeStruct((t, n), F32),
        scratch_shapes=[pltpu.VMEM((tm, d), BF16)],
        compiler_params=_cparams(("parallel", "arbitrary")),
        name="in_proj",
    )(x_all, modp, modp, w, b)


def _rope_kernel(q_ref, k_ref, v_ref, cos_ref, sa_ref, sb_ref, qo_ref, ko_ref, vo_ref):
    cos = cos_ref[...]
    sa = sa_ref[...]
    sb = sb_ref[...]

    def rope(x):
        w = x.shape[-1]
        return x * cos + pltpu.roll(x, w - 16, 1) * sa + pltpu.roll(x, 16, 1) * sb

    qo_ref[...] = (rope(q_ref[...]) * (A_QK_DIM ** -0.5)).astype(BF16)
    ko_ref[...] = rope(k_ref[...]).astype(BF16)
    vo_ref[...] = v_ref[...].astype(BF16)


def _rope_tables(n):
    rows = n // GRID_W
    pos_row = jnp.repeat(jnp.arange(rows), GRID_W).astype(F32)
    pos_col = jnp.tile(jnp.arange(GRID_W), rows).astype(F32)
    quarter = A_QK_DIM // 4
    inv = ROPE_THETA ** (-jnp.arange(quarter, dtype=F32) / quarter)
    ang_r = pos_row[:, None] * inv
    ang_c = pos_col[:, None] * inv
    ang = jnp.concatenate([ang_r, ang_r, ang_c, ang_c], axis=-1)
    cos = jnp.cos(ang)
    sin = jnp.sin(ang)
    first_half = (jnp.arange(A_QK_DIM) % (2 * quarter)) < quarter
    sa = jnp.where(first_half, -sin, 0.0)
    sb = jnp.where(first_half, 0.0, sin)
    reps = 2 * A_HEADS

    def widen(tbl, fill):
        tbl = jnp.tile(tbl, (1, reps))
        return jnp.concatenate([tbl, jnp.full((TM, tbl.shape[1]), fill, F32)], axis=0)

    return widen(cos, 1.0), widen(sa, 0.0), widen(sb, 0.0)


def _rope(proj, tables, tbl_idx):
    t = proj.shape[0]
    w = 2 * A_HEADS * A_QK_DIM
    cos, sa, sb = tables
    col = lambda c: pl.BlockSpec((TM, w), lambda i, c=c: (i, c))
    tbl = pl.BlockSpec((TM, w), lambda i: (tbl_idx(i), 0))
    out = pl.BlockSpec((TM, w), lambda i: (i, 0))
    shp = jax.ShapeDtypeStruct((t, w), BF16)
    return pl.pallas_call(
        _rope_kernel,
        grid=(t // TM,),
        in_specs=[col(0), col(1), col(2), tbl, tbl, tbl],
        out_specs=[out, out, out],
        out_shape=[shp, shp, shp],
        compiler_params=_cparams(("parallel",)),
        name="rope_cast",
    )(proj, proj, proj, cos, sa, sb)


def _attn_kernel(*refs, lam_init, with_lat):
    if with_lat:
        lam_ref, gain_ref, q_ref, kc_ref, vc_ref, kl_ref, vl_ref, o_ref = refs
    else:
        lam_ref, gain_ref, q_ref, kc_ref, vc_ref, o_ref = refs
    lv = lam_ref[...]
    lam = (jnp.exp(jnp.sum(lv[0:1] * lv[1:2], axis=-1, keepdims=True))
           - jnp.exp(jnp.sum(lv[2:3] * lv[3:4], axis=-1, keepdims=True)) + lam_init)
    outs = []
    for c in range(2):
        cols = slice(c * A_QK_DIM, (c + 1) * A_QK_DIM)
        qc = q_ref[:, cols]
        s_c = _dot_nt(qc, kc_ref[:, cols])
        m = jnp.max(s_c, axis=-1, keepdims=True)
        if with_lat:
            s_l = _dot_nt(qc, kl_ref[:, cols])
            m = jnp.maximum(m, jnp.max(s_l, axis=-1, keepdims=True))
        p_c = jnp.exp(s_c - m)
        den = jnp.sum(p_c, axis=-1, keepdims=True)
        pv = _dot(p_c.astype(BF16), vc_ref[...])
        if with_lat:
            p_l = jnp.exp(s_l - m)
            den = den + jnp.sum(p_l, axis=-1, keepdims=True)
            pv = pv + _dot(p_l.astype(BF16), vl_ref[...])
        outs.append(pv / den)
    o = outs[0] - lam * outs[1]
    o = o * lax.rsqrt(jnp.mean(o * o, axis=-1, keepdims=True) + RMS_EPS)
    o_ref[...] = (o * gain_ref[...]) * (1.0 - lam_init)


def _attention(qr, kr, vb, lam_vecs, gain, *, lam_init, batch, n_lat, n_ctx, latent_queries):
    hw = A_V_DIM
    ctx_blk0 = batch * n_lat // n_ctx
    lam_spec = pl.BlockSpec(lam_vecs.shape, lambda b, h, i: (0, 0))
    gain_spec = pl.BlockSpec((None, 1, hw), lambda b, h, i: (h, 0, 0))
    kv_ctx = pl.BlockSpec((n_ctx, hw), lambda b, h, i: (ctx_blk0 + b, h))
    if latent_queries:
        tq = 256
        nq = n_lat // tq
        q_spec = pl.BlockSpec((tq, hw), lambda b, h, i: (b * nq + i, h))
        kv_lat = pl.BlockSpec((n_lat, hw), lambda b, h, i: (b, h))
        in_specs = [lam_spec, gain_spec, q_spec, kv_ctx, kv_ctx, kv_lat, kv_lat]
        args = (lam_vecs, gain, qr, kr, vb, kr, vb)
        rows = batch * n_lat
    else:
        tq = n_ctx
        nq = 1
        q_spec = pl.BlockSpec((tq, hw), lambda b, h, i: (ctx_blk0 + b, h))
        in_specs = [lam_spec, gain_spec, q_spec, kv_ctx, kv_ctx]
        args = (lam_vecs, gain, qr, kr, vb)
        rows = batch * n_ctx
    return pl.pallas_call(
        functools.partial(_attn_kernel, lam_init=lam_init, with_lat=latent_queries),
        grid=(batch, A_HEADS, nq),
        in_specs=in_specs,
        out_specs=pl.BlockSpec((tq, hw), lambda b, h, i: (b * nq + i, h)),
        out_shape=jax.ShapeDtypeStruct((rows, A_HEADS * hw), F32),
        compiler_params=_cparams(("parallel", "parallel", "arbitrary")),
        name="diff_attn_lat" if latent_queries else "diff_attn_ctx",
    )(*args)


def _sg_kernel(u_ref, s_ref, g_ref, b_ref, w_ref, bs_ref, o_ref):
    gw = B_CHUNK
    for ci in range(TM // B_CHUNK):
        rows = slice(ci * B_CHUNK, (ci + 1) * B_CHUNK)
        for g in range(B_GROUPS):
            cols = slice(g * gw, (g + 1) * gw)
            vn = _ln(_gelu(s_ref[rows, cols])) * g_ref[:, cols] + b_ref[:, cols]
            mixed = _dot(w_ref[g], vn.astype(BF16)) + bs_ref[g]
            o_ref[rows, cols] = _gelu(u_ref[rows, cols]) * mixed


def _spatial_gating(proj, ln_g, ln_b, w_s, b_s):
    t = proj.shape[0]
    w = B_GROUPS * B_CHUNK
    full = lambda a: pl.BlockSpec(a.shape, lambda i: (0,) * a.ndim)
    return pl.pallas_call(
        _sg_kernel,
        grid=(t // TM,),
        in_specs=[pl.BlockSpec((TM, w), lambda i: (i, 3)),
                  pl.BlockSpec((TM, w), lambda i: (i, 4)),
                  full(ln_g), full(ln_b), full(w_s), full(b_s)],
        out_specs=pl.BlockSpec((TM, w), lambda i: (i, 0)),
        out_shape=jax.ShapeDtypeStruct((t, w), F32),
        compiler_params=_cparams(("parallel",)),
        name="spatial_gating",
    )(proj, proj, ln_g, ln_b, w_s, b_s)


def _lru_kernel(*refs, reverse, final, n_tiles, tl):
    if final:
        (x_ref, xp_ref, xn_ref, cw_ref, cb_ref, w_ref, ba_ref, bx_ref, lam_ref, h0_ref,
         hf_ref, y_ref, o_ref, hl_ref, carry) = refs
    else:
        (x_ref, xp_ref, xn_ref, cw_ref, cb_ref, w_ref, ba_ref, bx_ref, lam_ref, h0_ref,
         o_ref, hl_ref, carry) = refs
    step = pl.program_id(1)

    @pl.when(step == 0)
    def _():
        carry[...] = h0_ref[...]

    ti = (n_tiles - 1 - step) if reverse else step
    x = x_ref[...]
    cw = x.shape[-1]
    row = lax.broadcasted_iota(I32, x.shape, 0)
    has_prev = ti > 0
    has_next = ti < n_tiles - 1
    p6 = jnp.where(has_prev, xp_ref[6:7, :], 0.0)
    p7 = jnp.where(has_prev, xp_ref[7:8, :], 0.0)
    n0 = jnp.where(has_next, xn_ref[0:1, :], 0.0)
    xm1 = jnp.where(row == 0, p7, pltpu.roll(x, 1, 0))
    xm2 = jnp.where(row == 0, p6, jnp.where(row == 1, p7, pltpu.roll(x, 2, 0)))
    xp1 = jnp.where(row == tl - 1, n0, pltpu.roll(x, tl - 1, 0))
    taps = cw_ref[...]
    xs = taps[0:1] * xm2 + taps[1:2] * xm1 + taps[2:3] * x + taps[3:4] * xp1 + cb_ref[...]

    z = _dot(xs.astype(BF16), w_ref[...])
    r = _sigmoid(z[:, :cw] + ba_ref[...])
    ig = _sigmoid(z[:, cw:] + bx_ref[...])
    nl = -lam_ref[...]
    softplus = jnp.maximum(nl, 0.0) + jnp.log(1.0 + jnp.exp(-jnp.abs(nl)))
    log_a = -C_POW * r * softplus
    a = jnp.exp(log_a)
    u = jnp.sqrt(1.0 - jnp.exp(2.0 * log_a)) * (ig * xs)

    d = 1
    while d < tl:
        if reverse:
            ok = row < tl - d
            shift = tl - d
        else:
            ok = row >= d
            shift = d
        a_sh = jnp.where(ok, pltpu.roll(a, shift, 0), 1.0)
        u_sh = jnp.where(ok, pltpu.roll(u, shift, 0), 0.0)
        u = u + a * u_sh
        a = a * a_sh
        d *= 2
    h = u + a * carry[0:1, :]
    edge = h[0:1, :] if reverse else h[tl - 1:tl, :]
    carry[...] = jnp.broadcast_to(edge, carry.shape)
    hl_ref[...] = jnp.broadcast_to(edge, hl_ref.shape)
    if final:
        o_ref[...] = _gelu(y_ref[...]) * (hf_ref[...] + h)
    else:
        o_ref[...] = h


def _lru_pass(proj, hf, h0, conv_w, conv_b, w_dense, ba, bx, lam, *, batch, seq, tl, row0, reverse, final):
    t_all = proj.shape[0]
    cw = conv_w.shape[1]
    n_tiles = seq // tl
    base = row0 // tl
    sub = tl // 8
    last8 = t_all // 8 - 1

    def tile(b, s):
        ti = (n_tiles - 1 - s) if reverse else s
        return base + b * n_tiles + ti

    def loc(b, s):
        ti = (n_tiles - 1 - s) if reverse else s
        return b * n_tiles + ti

    full = lambda a: pl.BlockSpec(a.shape, lambda b, s: (0,) * a.ndim)
    in_specs = [pl.BlockSpec((tl, cw), lambda b, s: (tile(b, s), 5)),
                pl.BlockSpec((8, cw), lambda b, s: (jnp.maximum(tile(b, s) * sub - 1, 0), 5)),
                pl.BlockSpec((8, cw), lambda b, s: (jnp.minimum((tile(b, s) + 1) * sub, last8), 5)),
                full(conv_w), full(conv_b), full(w_dense), full(ba), full(bx), full(lam),
                pl.BlockSpec((None, 8, cw), lambda b, s: (b, 0, 0))]
    args = [proj, proj, proj, conv_w, conv_b, w_dense, ba, bx, lam, h0]
    if final:
        in_specs += [pl.BlockSpec((tl, cw), lambda b, s: (loc(b, s), 0)),
                     pl.BlockSpec((tl, cw), lambda b, s: (tile(b, s), 6))]
        args += [hf, proj]
    return pl.pallas_call(
        functools.partial(_lru_kernel, reverse=reverse, final=final, n_tiles=n_tiles, tl=tl),
        grid=(batch, n_tiles),
        in_specs=in_specs,
        out_specs=[pl.BlockSpec((tl, cw), lambda b, s: (loc(b, s), 0)),
                   pl.BlockSpec((None, 8, cw), lambda b, s: (b, 0, 0))],
        out_shape=[jax.ShapeDtypeStruct((batch * seq, cw), F32),
                   jax.ShapeDtypeStruct((batch, 8, cw), F32)],
        scratch_shapes=[pltpu.VMEM((8, cw), F32)],
        compiler_params=_cparams(("parallel", "arbitrary")),
        name="rglru_%s_%s" % ("bwd" if reverse else "fwd", "lat" if row0 == 0 else "ctx"),
    )(*args)


def _merge_kernel(x_ref, oa_ref, ob_ref, oc_ref, g0, g1, g2, g3, g4, g5, wb_ref, wo_ref,
                  gate1_ref, sh2_ref, sc2_ref, lng_ref, lnb_ref, wrh_ref, wrl_ref,
                  xn_ref, h2_ref, sc_ref, *, alpha):
    gates = ((g0, g1), (g2, g3), (g4, g5))
    mix = None
    for r, o_ref in enumerate((oa_ref, ob_ref, oc_ref)):
        proj = _dot(o_ref[...].astype(BF16), wb_ref[r])
        gate = jnp.concatenate([_sigmoid(gates[r][0][...]), _sigmoid(gates[r][1][...])], axis=-1)
        term = gate * proj
        mix = term if mix is None else mix + term
    out = _dot(mix.astype(BF16), wo_ref[...])
    xn = _ln(alpha * x_ref[...] + gate1_ref[...] * out) * lng_ref[...] + lnb_ref[...]
    xn_ref[...] = xn
    h2 = _ln(xn) * (1.0 + sc2_ref[...]) + sh2_ref[...]
    h_hi = h2.astype(BF16)
    h2_ref[...] = h_hi
    h_lo = (h2 - h_hi.astype(F32)).astype(BF16)
    logits = _dot_nt(wrh_ref[...], h_hi) + (_dot_nt(wrh_ref[...], h_lo) + _dot_nt(wrl_ref[...], h_hi))
    sc_ref[...] = _sigmoid(logits)


def _merge(x_all, o_a, o_b, o_c, proj, modp, w_branch, w_out, ln_g, ln_b, wr_hi, wr_lo, seg, *, n_rows, alpha):
    d = x_all.shape[1]
    bw = o_a.shape[1]
    n_exp = wr_hi.shape[0]
    tok = lambda w: pl.BlockSpec((TM, w), lambda i: (i, 0))
    gate = lambda c: pl.BlockSpec((TM, bw), lambda i, c=c: (i, c))
    mod = lambda k: pl.BlockSpec((None, None, 1, d), lambda i, k=k: (k, seg(i), 0, 0))
    full = lambda a: pl.BlockSpec(a.shape, lambda i: (0,) * a.ndim)
    return pl.pallas_call(
        functools.partial(_merge_kernel, alpha=alpha),
        grid=(n_rows // TM,),
        in_specs=[tok(d), tok(bw), tok(bw), tok(bw)] + [gate(7 + c) for c in range(6)]
                 + [full(w_branch), full(w_out), mod(2), mod(3), mod(4), full(ln_g), full(ln_b),
                    full(wr_hi), full(wr_lo)],
        out_specs=[tok(d), tok(d), pl.BlockSpec((n_exp, TM), lambda i: (0, i))],
        out_shape=[jax.ShapeDtypeStruct((n_rows, d), F32),
                   jax.ShapeDtypeStruct((n_rows, d), BF16),
                   jax.ShapeDtypeStruct((n_exp, n_rows), F32)],
        compiler_params=_cparams(("parallel",)),
        name="merge_residual_router",
    )(x_all, o_a, o_b, o_c, proj, proj, proj, proj, proj, proj, w_branch, w_out,
      modp, modp, modp, ln_g, ln_b, wr_hi, wr_lo)


def _tile_rows(n_exp):
    return TOP_K * TD + n_exp * UNIT


def _route_kernel(s_ref, bias_ref, tri_ref, ltri_ref, pos_ref, w_ref, unit_ref, tot_ref, carry):
    @pl.when(pl.program_id(0) == 0)
    def _():
        carry[...] = jnp.zeros_like(carry)

    s = s_ref[...]
    n_exp, tn = s.shape
    per = n_exp // N_GROUPS
    neg = -jnp.inf
    biased = s + bias_ref[...]
    sub = lax.broadcasted_iota(I32, (per, tn), 0)
    gs_rows = []
    for g in range(N_GROUPS):
        blk = biased[g * per:(g + 1) * per, :]
        m1 = jnp.max(blk, axis=0, keepdims=True)
        first = jnp.min(jnp.where(blk == m1, sub, per), axis=0, keepdims=True)
        m2 = jnp.max(jnp.where(sub == first, neg, blk), axis=0, keepdims=True)
        gs_rows.append(m1 + m2)
    gs = jnp.concatenate(gs_rows, axis=0)
    gi = lax.broadcasted_iota(I32, gs.shape, 0)
    g_ok = jnp.zeros(gs.shape, F32)
    cur = gs
    for _ in range(TOPK_GROUPS):
        m = jnp.max(cur, axis=0, keepdims=True)
        pick = jnp.min(jnp.where(cur == m, gi, N_GROUPS), axis=0, keepdims=True)
        hit = gi == pick
        g_ok = jnp.where(hit, 1.0, g_ok)
        cur = jnp.where(hit, neg, cur)
    ok_rows = [jnp.broadcast_to(g_ok[g:g + 1, :], (per, tn)) for g in range(N_GROUPS)]
    expert_ok = jnp.concatenate(ok_rows, axis=0)
    masked = jnp.where(expert_ok > 0.0, biased, neg)
    ei = lax.broadcasted_iota(I32, s.shape, 0)
    pick_s, hits = [], []
    sel = jnp.zeros(s.shape, F32)
    for _ in range(TOP_K):
        m = jnp.max(masked, axis=0, keepdims=True)
        pick = jnp.min(jnp.where(masked == m, ei, n_exp), axis=0, keepdims=True)
        hit = ei == pick
        pick_s.append(jnp.sum(jnp.where(hit, s, 0.0), axis=0, keepdims=True))
        hits.append(hit)
        sel = jnp.where(hit, 1.0, sel)
        masked = jnp.where(hit, neg, masked)
    tot = pick_s[0]
    for k in range(1, TOP_K):
        tot = tot + pick_s[k]
    w_ref[...] = jnp.concatenate([p / tot * ROUTED_SCALE for p in pick_s], axis=0)

    incl = _dot(sel.astype(BF16), tri_ref[...])
    count = incl[:, tn - 1:tn]
    run = jnp.floor((count + (UNIT - 1)) * (1.0 / UNIT)) * UNIT
    run_start = _dot(ltri_ref[...], jnp.broadcast_to(run, (n_exp, 128)).astype(BF16))[:, 0:1]
    row_in_tile = incl - sel + run_start
    pos = [jnp.sum(jnp.where(hit, row_in_tile, 0.0), axis=0, keepdims=True) for hit in hits]
    pos_ref[...] = jnp.concatenate(pos, axis=0).astype(I32)

    region_used = carry[:, 0:1]
    u = lax.broadcasted_iota(I32, (n_exp, UNIT_LANES), 1).astype(F32)
    eu = lax.broadcasted_iota(I32, (n_exp, UNIT_LANES), 0).astype(F32)
    u0 = run_start * (1.0 / UNIT)
    nu = run * (1.0 / UNIT)
    inside = jnp.logical_and(u >= u0, u < u0 + nu)
    rel = jnp.sum(jnp.where(inside, u - u0 + region_used * (1.0 / UNIT), 0.0), axis=0, keepdims=True)
    exp_of = jnp.sum(jnp.where(inside, eu, 0.0), axis=0, keepdims=True)
    n_units = jnp.broadcast_to(jnp.sum(nu, axis=0, keepdims=True), (1, UNIT_LANES))
    pad = jnp.zeros((5, UNIT_LANES), F32)
    unit_ref[...] = jnp.concatenate([rel, exp_of, n_units, pad], axis=0).astype(I32)

    total = region_used + run
    carry[...] = jnp.broadcast_to(total, carry.shape)
    tot_ref[...] = jnp.broadcast_to(total, tot_ref.shape)


def _route(scores_t, bias_col, tri, ltri):
    n_exp, t = scores_t.shape
    nt = t // TD
    return pl.pallas_call(
        _route_kernel,
        grid=(nt,),
        in_specs=[pl.BlockSpec((n_exp, TD), lambda i: (0, i)),
                  pl.BlockSpec((n_exp, 1), lambda i: (0, 0)),
                  pl.BlockSpec((TD, TD), lambda i: (0, 0)),
                  pl.BlockSpec((n_exp, n_exp), lambda i: (0, 0))],
        out_specs=[pl.BlockSpec((TOP_K, TD), lambda i: (0, i)),
                   pl.BlockSpec((TOP_K, TD), lambda i: (0, i)),
                   pl.BlockSpec((8, UNIT_LANES), lambda i: (0, i)),
                   pl.BlockSpec((n_exp, 128), lambda i: (0, 0))],
        out_shape=[jax.ShapeDtypeStruct((TOP_K, t), I32),
                   jax.ShapeDtypeStruct((TOP_K, t), F32),
                   jax.ShapeDtypeStruct((8, nt * UNIT_LANES), I32),
                   jax.ShapeDtypeStruct((n_exp, 128), F32)],
        scratch_shapes=[pltpu.VMEM((n_exp, 128), F32)],
        compiler_params=_cparams(("arbitrary",)),
        name="route_topk",
    )(scores_t, bias_col, tri, ltri)


def _unit_dest_kernel(start_ref, unit_ref, o_ref, *, n_exp):
    tbl = unit_ref[...]
    exp_of = tbl[1:2, :]
    dst = tbl[0:1, :]
    for e in range(n_exp):
        dst = dst + jnp.where(exp_of == e, start_ref[e], 0)
    o_ref[...] = jnp.concatenate([dst, tbl[2:3, :], jnp.zeros((6, tbl.shape[1]), I32)], axis=0)


def _unit_dest(region_start_units, units):
    blk = lambda: pl.BlockSpec((8, UNIT_LANES), lambda i, s: (0, i))
    return pl.pallas_call(
        functools.partial(_unit_dest_kernel, n_exp=region_start_units.shape[0]),
        grid_spec=pltpu.PrefetchScalarGridSpec(
            num_scalar_prefetch=1, grid=(units.shape[1] // UNIT_LANES,), in_specs=[blk()], out_specs=blk()),
        out_shape=jax.ShapeDtypeStruct(units.shape, I32),
        compiler_params=_cparams(("parallel",)),
        name="route_units",
    )(region_start_units, units)


def _unit_rows(i):
    return pl.ds(pl.multiple_of(i * UNIT, UNIT), UNIT)


def _dispatch_kernel(tail_lo_ref, tail_hi_ref, unit_ref, pos_ref, h_ref, xs_ref, s_scr, zero_scr, sem, *, n_exp):
    rows = s_scr.shape[0]
    td = h_ref.shape[0]

    def zero_copy(g):
        return pltpu.make_async_copy(zero_scr, xs_ref.at[_unit_rows(g), :], sem.at[1])

    @pl.when(pl.program_id(0) == 0)
    def _():
        zero_scr[...] = jnp.zeros_like(zero_scr)

        def per_expert(e, c):
            lax.fori_loop(tail_lo_ref[e], tail_hi_ref[e], lambda g, cc: (zero_copy(g).start(), cc)[1], 0)
            lax.fori_loop(tail_lo_ref[e], tail_hi_ref[e], lambda g, cc: (zero_copy(g).wait(), cc)[1], 0)
            return c

        lax.fori_loop(0, n_exp, per_expert, 0)

    pos = pos_ref[...]
    h = h_ref[...]
    for r0 in range(0, rows, SORT_CHUNK):
        srow = lax.broadcasted_iota(I32, (SORT_CHUNK, td), 0) + r0
        onehot = jnp.zeros((SORT_CHUNK, td), F32)
        for k in range(pos.shape[0]):
            onehot = jnp.where(srow == pos[k:k + 1, :], 1.0, onehot)
        s_scr[r0:r0 + SORT_CHUNK, :] = _dot(onehot.astype(BF16), h).astype(BF16)

    def unit_copy(u):
        return pltpu.make_async_copy(s_scr.at[_unit_rows(u), :], xs_ref.at[_unit_rows(unit_ref[0, u]), :], sem.at[0])

    n_units = unit_ref[1, 0]
    lax.fori_loop(0, n_units, lambda u, c: (unit_copy(u).start(), c)[1], 0)
    lax.fori_loop(0, n_units, lambda u, c: (unit_copy(u).wait(), c)[1], 0)


def _dispatch(tail_lo, tail_hi, unit_dst, pos, h2, n_slots):
    t, d = h2.shape
    n_exp = tail_lo.shape[0]
    return pl.pallas_call(
        functools.partial(_dispatch_kernel, n_exp=n_exp),
        grid_spec=pltpu.PrefetchScalarGridSpec(
            num_scalar_prefetch=2, grid=(t // TD,),
            in_specs=[pl.BlockSpec((8, UNIT_LANES), lambda i, a, b: (0, i), memory_space=pltpu.SMEM),
                      pl.BlockSpec((TOP_K, TD), lambda i, a, b: (0, i)),
                      pl.BlockSpec((TD, d), lambda i, a, b: (i, 0))],
            out_specs=pl.BlockSpec(memory_space=pl.ANY),
            scratch_shapes=[pltpu.VMEM((_tile_rows(n_exp), d), BF16), pltpu.VMEM((UNIT, d), BF16),
                            pltpu.SemaphoreType.DMA((2,))]),
        out_shape=jax.ShapeDtypeStruct((n_slots, d), BF16),
        compiler_params=_cparams(("arbitrary",)),
        name="moe_dispatch",
    )(tail_lo, tail_hi, unit_dst, pos, h2)


def _expert_kernel(be_ref, nu_ref, x_ref, wg_ref, wu_ref, wd_ref, o_ref):
    @pl.when(pl.program_id(0) < nu_ref[0])
    def _():
        x = x_ref[...]
        act = _silu(_dot(x, wg_ref[...])) * _dot(x, wu_ref[...])
        o_ref[...] = _dot(act.astype(BF16), wd_ref[...]).astype(BF16)


def _experts(block_e, n_used, xs, w_gate, w_up, w_down):
    n_slots, d = xs.shape
    de = w_gate.shape[2]
    blk = lambda i, be, nu: jnp.minimum(i, nu[0] - 1)
    return pl.pallas_call(
        _expert_kernel,
        grid_spec=pltpu.PrefetchScalarGridSpec(
            num_scalar_prefetch=2, grid=(n_slots // MOE_BLK,),
            in_specs=[pl.BlockSpec((MOE_BLK, d), lambda i, be, nu: (blk(i, be, nu), 0)),
                      pl.BlockSpec((None, d, de), lambda i, be, nu: (be[blk(i, be, nu)], 0, 0)),
                      pl.BlockSpec((None, d, de), lambda i, be, nu: (be[blk(i, be, nu)], 0, 0)),
                      pl.BlockSpec((None, de, d), lambda i, be, nu: (be[blk(i, be, nu)], 0, 0))],
            out_specs=pl.BlockSpec((MOE_BLK, d), lambda i, be, nu: (blk(i, be, nu), 0))),
        out_shape=jax.ShapeDtypeStruct((n_slots, d), BF16),
        compiler_params=_cparams(("arbitrary",)),
        name="moe_experts",
    )(block_e, n_used, xs, w_gate, w_up, w_down)


def _combine_kernel(unit_ref, pos_ref, w_ref, x_ref, h_ref, gate2_ref, sg_ref, su_ref, sd_ref, lng_ref, lnb_ref,
                    ys_ref, o_ref, s_scr, sem, *, alpha):
    rows = s_scr.shape[0]
    td = x_ref.shape[0]

    @pl.when(pl.program_id(0) == 0)
    def _():
        s_scr[...] = jnp.zeros_like(s_scr)

    def unit_copy(u):
        return pltpu.make_async_copy(ys_ref.at[_unit_rows(unit_ref[0, u]), :], s_scr.at[_unit_rows(u), :], sem.at[0])

    n_units = unit_ref[1, 0]
    lax.fori_loop(0, n_units, lambda u, c: (unit_copy(u).start(), c)[1], 0)
    h = h_ref[...]
    acc = _dot((_silu(_dot(h, sg_ref[...])) * _dot(h, su_ref[...])).astype(BF16), sd_ref[...])
    pos = pos_ref[...]
    w = w_ref[...]
    lax.fori_loop(0, n_units, lambda u, c: (unit_copy(u).wait(), c)[1], 0)
    for r0 in range(0, rows, SORT_CHUNK):
        lane = lax.broadcasted_iota(I32, (td, SORT_CHUNK), 1) + r0
        wmat = jnp.zeros((td, SORT_CHUNK), F32)
        for k in range(pos.shape[1]):
            wmat = jnp.where(lane == pos[:, k:k + 1], w[:, k:k + 1], wmat)
        w_hi = wmat.astype(BF16)
        w_lo = (wmat - w_hi.astype(F32)).astype(BF16)
        blk = s_scr[r0:r0 + SORT_CHUNK, :]
        acc = acc + (_dot(w_hi, blk) + _dot(w_lo, blk))
    o_ref[...] = _ln(alpha * x_ref[...] + gate2_ref[...] * acc) * lng_ref[...] + lnb_ref[...]


def _combine(unit_dst, pos_tok, w_tok, xn, h2, modp, sh_gate, sh_up, sh_down, ln_g, ln_b, ys, seg_td, *,
             alpha, n_exp):
    t, d = xn.shape
    k = pos_tok.shape[1]
    tok = pl.BlockSpec((TD, d), lambda i: (i, 0))
    full = lambda a: pl.BlockSpec(a.shape, lambda i: (0,) * a.ndim)
    return pl.pallas_call(
        functools.partial(_combine_kernel, alpha=alpha),
        grid=(t // TD,),
        in_specs=[pl.BlockSpec((8, UNIT_LANES), lambda i: (0, i), memory_space=pltpu.SMEM),
                  pl.BlockSpec((TD, k), lambda i: (i, 0)),
                  pl.BlockSpec((TD, k), lambda i: (i, 0)),
                  tok, tok,
                  pl.BlockSpec((None, None, 1, d), lambda i: (5, seg_td(i), 0, 0)),
                  full(sh_gate), full(sh_up), full(sh_down), full(ln_g), full(ln_b),
                  pl.BlockSpec(memory_space=pl.ANY)],
        out_specs=tok,
        out_shape=jax.ShapeDtypeStruct((t, d), F32),
        scratch_shapes=[pltpu.VMEM((_tile_rows(n_exp), d), BF16), pltpu.SemaphoreType.DMA((1,))],
        compiler_params=_cparams(("arbitrary",)),
        name="moe_combine",
    )(unit_dst, pos_tok, w_tok, xn, h2, modp, sh_gate, sh_up, sh_down, ln_g, ln_b, ys)


def _moe(xn, h2, scores_t, modp, seg_td, router_bias, w_gate, w_up, w_down, sh_gate, sh_up, sh_down,
         ln_g, ln_b, tri, ltri, *, alpha):
    t = xn.shape[0]
    n_exp = scores_t.shape[0]
    assert _tile_rows(n_exp) % SORT_CHUNK == 0 and _tile_rows(n_exp) <= UNIT * UNIT_LANES
    pos, w_sel, units, tot = _route(scores_t, router_bias.reshape(n_exp, 1), tri, ltri)
    used = tot[:, 0].astype(I32)
    region = (used + MOE_BLK - 1) // MOE_BLK * MOE_BLK
    region_end = jnp.cumsum(region)
    region_start = region_end - region
    n_slots = TOP_K * t + n_exp * UNIT * (t // TD) + n_exp * MOE_BLK
    n_blocks = n_slots // MOE_BLK
    block_row0 = jnp.arange(n_blocks, dtype=I32) * MOE_BLK
    block_e = jnp.minimum(jnp.sum(region_end[None, :] <= block_row0[:, None], axis=1), n_exp - 1).astype(I32)
    n_used = (region_end[-1:] // MOE_BLK).astype(I32)
    unit_dst = _unit_dest((region_start // UNIT).astype(I32), units)
    xs = _dispatch(((region_start + used) // UNIT).astype(I32), (region_end // UNIT).astype(I32),
                   unit_dst, pos, h2, n_slots)
    ys = _experts(block_e, n_used, xs, w_gate, w_up, w_down)
    return _combine(unit_dst, pos.T, w_sel.T, xn, h2, modp, sh_gate, sh_up, sh_down, ln_g, ln_b, ys, seg_td,
                    alpha=alpha, n_exp=n_exp)


def kernel(x, c, ctx, c_ctx, w_mod, b_mod, w_in, b_in, lam_q1, lam_k1, lam_q2, lam_k2, attn_norm_g, sg_ln_g, sg_ln_b, sg_w, sg_b, conv_w, conv_b, lru_wa, lru_ba, lru_wx, lru_bx, lru_lam, w_branch, w_out, ln1_g, ln1_b, w_router, router_bias, moe_w_gate, moe_w_up, moe_w_down, sh_w_gate, sh_w_up, sh_w_down, ln2_g, ln2_b):
    batch, n_lat, d = x.shape
    n_ctx = ctx.shape[1]
    depth = w_mod.shape[0]
    n_exp = w_router.shape[2]
    t_lat = batch * n_lat
    t_ctx = batch * n_ctx
    t_all = t_lat + t_ctx
    assert n_lat % TM == 0 and t_ctx % TM == 0 and n_ctx % B_CHUNK == 0 and t_lat % n_ctx == 0
    assert batch + 1 <= MOD_ROWS and TM % TD == 0
    alpha = (2 * depth) ** 0.25
    cw = conv_w.shape[2]
    tiles_per_batch = n_lat // TM

    seg = lambda i: jnp.minimum(i // tiles_per_batch, batch)
    seg_td = lambda i: jnp.minimum(i // (n_lat // TD), batch)
    tbl_idx = lambda i: jnp.where(i < t_lat // TM, i % tiles_per_batch, tiles_per_batch)
    tm_in = 2 * TM if (n_lat % (2 * TM) == 0 and t_ctx % (2 * TM) == 0) else TM
    seg_in = lambda i: jnp.minimum(i // (n_lat // tm_in), batch)

    x_all = jnp.concatenate([x.reshape(t_lat, d), ctx.reshape(t_ctx, d)], axis=0)
    c_all = jnp.zeros((MOD_ROWS, d), F32).at[:batch].set(c).at[batch].set(c_ctx)
    tables = _rope_tables(n_lat)
    tri = (jnp.arange(TD)[:, None] <= jnp.arange(TD)[None, :]).astype(BF16)
    ltri = (jnp.arange(n_exp)[None, :] < jnp.arange(n_exp)[:, None]).astype(BF16)
    row = lambda v: v.reshape(1, -1)

    def dense_blocks(w):
        nb, bi, bj = w.shape
        eye = jnp.eye(nb, dtype=w.dtype)
        return (w[:, :, None, :] * eye[:, None, :, None]).reshape(nb * bi, nb * bj)

    for l in range(depth):
        last = l == depth - 1
        lam_init = 0.8 - 0.6 * math.exp(-0.3 * l)
        mod = _mod_rows(c_all, w_mod[l].astype(BF16), row(b_mod[l]))
        modp = mod.reshape(MOD_ROWS, 6, 1, d).transpose(1, 0, 2, 3)

        proj = _in_proj(x_all, modp, w_in[l].astype(BF16), row(b_in[l]), seg_in, tm_in)
        qr, kr, vb = _rope(proj, tables, tbl_idx)

        lam_vecs = jnp.stack([lam_q1[l], lam_k1[l], lam_q2[l], lam_k2[l]])
        gain = attn_norm_g[l].reshape(A_HEADS, 1, A_V_DIM)
        attn = functools.partial(_attention, qr, kr, vb, lam_vecs, gain, lam_init=lam_init,
                                 batch=batch, n_lat=n_lat, n_ctx=n_ctx)
        o_a = attn(latent_queries=True)

        o_b = _spatial_gating(proj, row(sg_ln_g[l]), row(sg_ln_b[l]), sg_w[l].astype(BF16),
                              sg_b[l].reshape(B_GROUPS, B_CHUNK, 1))

        lru = functools.partial(_lru_pass, proj, conv_w=conv_w[l], conv_b=row(conv_b[l]), batch=batch)
        zeros_h = jnp.zeros((batch, 8, cw), F32)
        hf = {}
        for direction in range(2):
            wd = jnp.concatenate([dense_blocks(lru_wa[l, direction]), dense_blocks(lru_wx[l, direction])],
                                 axis=1).astype(BF16)
            par = dict(w_dense=wd, ba=row(lru_ba[l, direction]), bx=row(lru_bx[l, direction]),
                       lam=row(lru_lam[l, direction]), reverse=direction == 1, final=direction == 1)
            h_ctx, edge = lru(hf.get("ctx"), zeros_h, seq=n_ctx, tl=n_ctx, row0=t_lat, **par)
            h_lat, _ = lru(hf.get("lat"), edge, seq=n_lat, tl=TM, row0=0, **par)
            hf = {"ctx": h_ctx, "lat": h_lat}
        o_c_lat, o_c_ctx = hf["lat"], hf["ctx"]

        n_rows = t_lat if last else t_all
        if last:
            o_a_all, o_c = o_a, o_c_lat
        else:
            o_a_all = jnp.concatenate([o_a, attn(latent_queries=False)], axis=0)
            o_c = jnp.concatenate([o_c_lat, o_c_ctx], axis=0)
        wr_t = w_router[l].T
        wr_hi = wr_t.astype(BF16)
        wr_lo = (wr_t - wr_hi.astype(F32)).astype(BF16)
        xn, h2, scores_t = _merge(x_all, o_a_all, o_b, o_c, proj, modp, w_branch[l].astype(BF16),
                                  w_out[l].astype(BF16), row(ln1_g[l]), row(ln1_b[l]), wr_hi, wr_lo, seg,
                                  n_rows=n_rows, alpha=alpha)
        x_all = _moe(xn, h2, scores_t, modp, seg_td, router_bias[l], moe_w_gate[l].astype(BF16),
                     moe_w_up[l].astype(BF16), moe_w_down[l].astype(BF16), sh_w_gate[l].astype(BF16),
                     sh_w_up[l].astype(BF16), sh_w_down[l].astype(BF16), row(ln2_g[l]), row(ln2_b[l]), tri, ltri,
                     alpha=alpha)
    return x_all[:t_lat].reshape(batch, n_lat, d)
```

```python
import functools
import math

import jax
import jax.numpy as jnp
from jax import lax
from jax.experimental import pallas as pl
from jax.experimental.pallas import tpu as pltpu

F32 = jnp.float32
BF16 = jnp.bfloat16
I32 = jnp.int32

A_HEADS = 4
A_QK_DIM = 64
A_V_DIM = 2 * A_QK_DIM
GRID_W = 64
ROPE_THETA = 10000.0
B_CHUNK = 128
B_GROUPS = 4
C_BLOCKS = 8
C_POW = 8.0
TOP_K = 8
N_GROUPS = 8
TOPK_GROUPS = 4
ROUTED_SCALE = 2.5
LN_EPS = 1e-6
RMS_EPS = 1e-5

TM = 512
MOE_BLK = 512
TD = 256
UNIT = 16
UNIT_LANES = 256
SORT_CHUNK = 512
MOD_ROWS = 16
VMEM_LIMIT = 56 * 1024 * 1024


def _cparams(sem):
    return pltpu.CompilerParams(dimension_semantics=sem, vmem_limit_bytes=VMEM_LIMIT)


def _ln(x):
    mu = jnp.mean(x, axis=-1, keepdims=True)
    xc = x - mu
    var = jnp.mean(xc * xc, axis=-1, keepdims=True)
    return xc * lax.rsqrt(var + LN_EPS)


def _gelu(x):
    cdf = 0.5 * (1.0 + jnp.tanh(math.sqrt(2.0 / math.pi) * (x + 0.044715 * (x * x * x))))
    return x * cdf


def _sigmoid(x):
    return 1.0 / (1.0 + jnp.exp(-x))


def _silu(x):
    return x * _sigmoid(x)


def _dot(a, b):
    return jnp.dot(a, b, preferred_element_type=F32)


def _dot_nt(a, b):
    return lax.dot_general(a, b, (((1,), (1,)), ((), ())), preferred_element_type=F32)


def _mod_kernel(c_ref, w_ref, b_ref, o_ref):
    o_ref[...] = _dot(_silu(c_ref[...]).astype(BF16), w_ref[...]) + b_ref[...]


def _mod_rows(c_all, w, b):
    m, d = c_all.shape
    n = w.shape[1]
    tn = 1536
    return pl.pallas_call(
        _mod_kernel,
        grid=(n // tn,),
        in_specs=[pl.BlockSpec((m, d), lambda j: (0, 0)),
                  pl.BlockSpec((d, tn), lambda j: (0, j)),
                  pl.BlockSpec((1, tn), lambda j: (0, j))],
        out_specs=pl.BlockSpec((m, tn), lambda j: (0, j)),
        out_shape=jax.ShapeDtypeStruct((m, n), F32),
        compiler_params=_cparams(("parallel",)),
        name="adaln_rows",
    )(c_all, w, b)


def _in_proj_kernel(x_ref, sh_ref, sc_ref, w_ref, b_ref, o_ref, h_scr):
    @pl.when(pl.program_id(1) == 0)
    def _():
        h = _ln(x_ref[...]) * (1.0 + sc_ref[...]) + sh_ref[...]
        h_scr[...] = h.astype(BF16)

    o_ref[...] = _dot(h_scr[...], w_ref[...]) + b_ref[...]


def _in_proj(x_all, modp, w, b, seg, tm):
    t, d = x_all.shape
    n = w.shape[1]
    tn = 512
    return pl.pallas_call(
        _in_proj_kernel,
        grid=(t // tm, n // tn),
        in_specs=[pl.BlockSpec((tm, d), lambda i, j: (i, 0)),
                  pl.BlockSpec((None, None, 1, d), lambda i, j: (0, seg(i), 0, 0)),
                  pl.BlockSpec((None, None, 1, d), lambda i, j: (1, seg(i), 0, 0)),
                  pl.BlockSpec((d, tn), lambda i, j: (0, j)),
                  pl.BlockSpec((1, tn), lambda i, j: (0, j))],
        out_specs=pl.BlockSpec((tm, tn), lambda i, j: (i, j)),
        out_shape=jax.ShapeDtypeStruct((t, n), F32),
        scratch_shapes=[pltpu.VMEM((tm, d), BF16)],
        compiler_params=_cparams(("parallel", "arbitrary")),
        name="in_proj",
    )(x_all, modp, modp, w, b)


def _rope_kernel(q_ref, k_ref, v_ref, cos_ref, sa_ref, sb_ref, qo_ref, ko_ref, vo_ref):
    cos = cos_ref[...]
    sa = sa_ref[...]
    sb = sb_ref[...]

    def rope(x):
        w = x.shape[-1]
        return x * cos + pltpu.roll(x, w - 16, 1) * sa + pltpu.roll(x, 16, 1) * sb

    qo_ref[...] = (rope(q_ref[...]) * (A_QK_DIM ** -0.5)).astype(BF16)
    ko_ref[...] = rope(k_ref[...]).astype(BF16)
    vo_ref[...] = v_ref[...].astype(BF16)


def _rope_tables(n):
    rows = n // GRID_W
    pos_row = jnp.repeat(jnp.arange(rows), GRID_W).astype(F32)
    pos_col = jnp.tile(jnp.arange(GRID_W), rows).astype(F32)
    quarter = A_QK_DIM // 4
    inv = ROPE_THETA ** (-jnp.arange(quarter, dtype=F32) / quarter)
    ang_r = pos_row[:, None] * inv
    ang_c = pos_col[:, None] * inv
    ang = jnp.concatenate([ang_r, ang_r, ang_c, ang_c], axis=-1)
    cos = jnp.cos(ang)
    sin = jnp.sin(ang)
    first_half = (jnp.arange(A_QK_DIM) % (2 * quarter)) < quarter
    sa = jnp.where(first_half, -sin, 0.0)
    sb = jnp.where(first_half, 0.0, sin)
    reps = 2 * A_HEADS

    def widen(tbl, fill):
        tbl = jnp.tile(tbl, (1, reps))
        return jnp.concatenate([tbl, jnp.full((TM, tbl.shape[1]), fill, F32)], axis=0)

    return widen(cos, 1.0), widen(sa, 0.0), widen(sb, 0.0)


def _rope(proj, tables, tbl_idx):
    t = proj.shape[0]
    w = 2 * A_HEADS * A_QK_DIM
    cos, sa, sb = tables
    col = lambda c: pl.BlockSpec((TM, w), lambda i, c=c: (i, c))
    tbl = pl.BlockSpec((TM, w), lambda i: (tbl_idx(i), 0))
    out = pl.BlockSpec((TM, w), lambda i: (i, 0))
    shp = jax.ShapeDtypeStruct((t, w), BF16)
    return pl.pallas_call(
        _rope_kernel,
        grid=(t // TM,),
        in_specs=[col(0), col(1), col(2), tbl, tbl, tbl],
        out_specs=[out, out, out],
        out_shape=[shp, shp, shp],
        compiler_params=_cparams(("parallel",)),
        name="rope_cast",
    )(proj, proj, proj, cos, sa, sb)


def _attn_kernel(*refs, lam_init, with_lat):
    if with_lat:
        lam_ref, gain_ref, q_ref, kc_ref, vc_ref, kl_ref, vl_ref, o_ref = refs
    else:
        lam_ref, gain_ref, q_ref, kc_ref, vc_ref, o_ref = refs
    lv = lam_ref[...]
    lam = (jnp.exp(jnp.sum(lv[0:1] * lv[1:2], axis=-1, keepdims=True))
           - jnp.exp(jnp.sum(lv[2:3] * lv[3:4], axis=-1, keepdims=True)) + lam_init)
    outs = []
    for c in range(2):
        cols = slice(c * A_QK_DIM, (c + 1) * A_QK_DIM)
        qc = q_ref[:, cols]
        s_c = _dot_nt(qc, kc_ref[:, cols])
        m = jnp.max(s_c, axis=-1, keepdims=True)
        if with_lat:
            s_l = _dot_nt(qc, kl_ref[:, cols])
            m = jnp.maximum(m, jnp.max(s_l, axis=-1, keepdims=True))
        p_c = jnp.exp(s_c - m)
        den = jnp.sum(p_c, axis=-1, keepdims=True)
        pv = _dot(p_c.astype(BF16), vc_ref[...])
        if with_lat:
            p_l = jnp.exp(s_l - m)
            den = den + jnp.sum(p_l, axis=-1, keepdims=True)
            pv = pv + _dot(p_l.astype(BF16), vl_ref[...])
        outs.append(pv / den)
    o = outs[0] - lam * outs[1]
    o = o * lax.rsqrt(jnp.mean(o * o, axis=-1, keepdims=True) + RMS_EPS)
    o_ref[...] = (o * gain_ref[...]) * (1.0 - lam_init)


def _attention(qr, kr, vb, lam_vecs, gain, *, lam_init, batch, n_lat, n_ctx, latent_queries):
    hw = A_V_DIM
    ctx_blk0 = batch * n_lat // n_ctx
    lam_spec = pl.BlockSpec(lam_vecs.shape, lambda b, h, i: (0, 0))
    gain_spec = pl.BlockSpec((None, 1, hw), lambda b, h, i: (h, 0, 0))
    kv_ctx = pl.BlockSpec((n_ctx, hw), lambda b, h, i: (ctx_blk0 + b, h))
    if latent_queries:
        tq = 256
        nq = n_lat // tq
        q_spec = pl.BlockSpec((tq, hw), lambda b, h, i: (b * nq + i, h))
        kv_lat = pl.BlockSpec((n_lat, hw), lambda b, h, i: (b, h))
        in_specs = [lam_spec, gain_spec, q_spec, kv_ctx, kv_ctx, kv_lat, kv_lat]
        args = (lam_vecs, gain, qr, kr, vb, kr, vb)
        rows = batch * n_lat
    else:
        tq = n_ctx
        nq = 1
        q_spec = pl.BlockSpec((tq, hw), lambda b, h, i: (ctx_blk0 + b, h))
        in_specs = [lam_spec, gain_spec, q_spec, kv_ctx, kv_ctx]
        args = (lam_vecs, gain, qr, kr, vb)
        rows = batch * n_ctx
    return pl.pallas_call(
        functools.partial(_attn_kernel, lam_init=lam_init, with_lat=latent_queries),
        grid=(batch, A_HEADS, nq),
        in_specs=in_specs,
        out_specs=pl.BlockSpec((tq, hw), lambda b, h, i: (b * nq + i, h)),
        out_shape=jax.ShapeDtypeStruct((rows, A_HEADS * hw), F32),
        compiler_params=_cparams(("parallel", "parallel", "arbitrary")),
        name="diff_attn_lat" if latent_queries else "diff_attn_ctx",
    )(*args)


def _sg_kernel(u_ref, s_ref, g_ref, b_ref, w_ref, bs_ref, o_ref):
    gw = B_CHUNK
    for ci in range(TM // B_CHUNK):
        rows = slice(ci * B_CHUNK, (ci + 1) * B_CHUNK)
        for g in range(B_GROUPS):
            cols = slice(g * gw, (g + 1) * gw)
            vn = _ln(_gelu(s_ref[rows, cols])) * g_ref[:, cols] + b_ref[:, cols]
            mixed = _dot(w_ref[g], vn.astype(BF16)) + bs_ref[g]
            o_ref[rows, cols] = _gelu(u_ref[rows, cols]) * mixed


def _spatial_gating(proj, ln_g, ln_b, w_s, b_s):
    t = proj.shape[0]
    w = B_GROUPS * B_CHUNK
    full = lambda a: pl.BlockSpec(a.shape, lambda i: (0,) * a.ndim)
    return pl.pallas_call(
        _sg_kernel,
        grid=(t // TM,),
        in_specs=[pl.BlockSpec((TM, w), lambda i: (i, 3)),
                  pl.BlockSpec((TM, w), lambda i: (i, 4)),
                  full(ln_g), full(ln_b), full(w_s), full(b_s)],
        out_specs=pl.BlockSpec((TM, w), lambda i: (i, 0)),
        out_shape=jax.ShapeDtypeStruct((t, w), F32),
        compiler_params=_cparams(("parallel",)),
        name="spatial_gating",
    )(proj, proj, ln_g, ln_b, w_s, b_s)


def _lru_kernel(*refs, reverse, final, n_tiles, tl):
    if final:
        (x_ref, xp_ref, xn_ref, cw_ref, cb_ref, w_ref, ba_ref, bx_ref, lam_ref, h0_ref,
         hf_ref, y_ref, o_ref, hl_ref, carry) = refs
    else:
        (x_ref, xp_ref, xn_ref, cw_ref, cb_ref, w_ref, ba_ref, bx_ref, lam_ref, h0_ref,
         o_ref, hl_ref, carry) = refs
    step = pl.program_id(1)

    @pl.when(step == 0)
    def _():
        carry[...] = h0_ref[...]

    ti = (n_tiles - 1 - step) if reverse else step
    x = x_ref[...]
    cw = x.shape[-1]
    row = lax.broadcasted_iota(I32, x.shape, 0)
    has_prev = ti > 0
    has_next = ti < n_tiles - 1
    p6 = jnp.where(has_prev, xp_ref[6:7, :], 0.0)
    p7 = jnp.where(has_prev, xp_ref[7:8, :], 0.0)
    n0 = jnp.where(has_next, xn_ref[0:1, :], 0.0)
    xm1 = jnp.where(row == 0, p7, pltpu.roll(x, 1, 0))
    xm2 = jnp.where(row == 0, p6, jnp.where(row == 1, p7, pltpu.roll(x, 2, 0)))
    xp1 = jnp.where(row == tl - 1, n0, pltpu.roll(x, tl - 1, 0))
    taps = cw_ref[...]
    xs = taps[0:1] * xm2 + taps[1:2] * xm1 + taps[2:3] * x + taps[3:4] * xp1 + cb_ref[...]

    z = _dot(xs.astype(BF16), w_ref[...])
    r = _sigmoid(z[:, :cw] + ba_ref[...])
    ig = _sigmoid(z[:, cw:] + bx_ref[...])
    nl = -lam_ref[...]
    softplus = jnp.maximum(nl, 0.0) + jnp.log(1.0 + jnp.exp(-jnp.abs(nl)))
    log_a = -C_POW * r * softplus
    a = jnp.exp(log_a)
    u = jnp.sqrt(1.0 - jnp.exp(2.0 * log_a)) * (ig * xs)

    d = 1
    while d < tl:
        if reverse:
            ok = row < tl - d
            shift = tl - d
        else:
            ok = row >= d
            shift = d
        a_sh = jnp.where(ok, pltpu.roll(a, shift, 0), 1.0)
        u_sh = jnp.where(ok, pltpu.roll(u, shift, 0), 0.0)
        u = u + a * u_sh
        a = a * a_sh
        d *= 2
    h = u + a * carry[0:1, :]
    edge = h[0:1, :] if reverse else h[tl - 1:tl, :]
    carry[...] = jnp.broadcast_to(edge, carry.shape)
    hl_ref[...] = jnp.broadcast_to(edge, hl_ref.shape)
    if final:
        o_ref[...] = _gelu(y_ref[...]) * (hf_ref[...] + h)
    else:
        o_ref[...] = h


def _lru_pass(proj, hf, h0, conv_w, conv_b, w_dense, ba, bx, lam, *, batch, seq, tl, row0, reverse, final):
    t_all = proj.shape[0]
    cw = conv_w.shape[1]
    n_tiles = seq // tl
    base = row0 // tl
    sub = tl // 8
    last8 = t_all // 8 - 1

    def tile(b, s):
        ti = (n_tiles - 1 - s) if reverse else s
        return base + b * n_tiles + ti

    def loc(b, s):
        ti = (n_tiles - 1 - s) if reverse else s
        return b * n_tiles + ti

    full = lambda a: pl.BlockSpec(a.shape, lambda b, s: (0,) * a.ndim)
    in_specs = [pl.BlockSpec((tl, cw), lambda b, s: (tile(b, s), 5)),
                pl.BlockSpec((8, cw), lambda b, s: (jnp.maximum(tile(b, s) * sub - 1, 0), 5)),
                pl.BlockSpec((8, cw), lambda b, s: (jnp.minimum((tile(b, s) + 1) * sub, last8), 5)),
                full(conv_w), full(conv_b), full(w_dense), full(ba), full(bx), full(lam),
                pl.BlockSpec((None, 8, cw), lambda b, s: (b, 0, 0))]
    args = [proj, proj, proj, conv_w, conv_b, w_dense, ba, bx, lam, h0]
    if final:
        in_specs += [pl.BlockSpec((tl, cw), lambda b, s: (loc(b, s), 0)),
                     pl.BlockSpec((tl, cw), lambda b, s: (tile(b, s), 6))]
        args += [hf, proj]
    return pl.pallas_call(
        functools.partial(_lru_kernel, reverse=reverse, final=final, n_tiles=n_tiles, tl=tl),
        grid=(batch, n_tiles),
        in_specs=in_specs,
        out_specs=[pl.BlockSpec((tl, cw), lambda b, s: (loc(b, s), 0)),
                   pl.BlockSpec((None, 8, cw), lambda b, s: (b, 0, 0))],
        out_shape=[jax.ShapeDtypeStruct((batch * seq, cw), F32),
                   jax.ShapeDtypeStruct((batch, 8, cw), F32)],
        scratch_shapes=[pltpu.VMEM((8, cw), F32)],
        compiler_params=_cparams(("parallel", "arbitrary")),
        name="rglru_%s_%s" % ("bwd" if reverse else "fwd", "lat" if row0 == 0 else "ctx"),
    )(*args)


def _merge_kernel(x_ref, oa_ref, ob_ref, oc_ref, g0, g1, g2, g3, g4, g5, wb_ref, wo_ref,
                  gate1_ref, sh2_ref, sc2_ref, lng_ref, lnb_ref, wrh_ref, wrl_ref,
                  xn_ref, h2_ref, sc_ref, *, alpha):
    gates = ((g0, g1), (g2, g3), (g4, g5))
    mix = None
    for r, o_ref in enumerate((oa_ref, ob_ref, oc_ref)):
        proj = _dot(o_ref[...].astype(BF16), wb_ref[r])
        gate = jnp.concatenate([_sigmoid(gates[r][0][...]), _sigmoid(gates[r][1][...])], axis=-1)
        term = gate * proj
        mix = term if mix is None else mix + term
    out = _dot(mix.astype(BF16), wo_ref[...])
    xn = _ln(alpha * x_ref[...] + gate1_ref[...] * out) * lng_ref[...] + lnb_ref[...]
    xn_ref[...] = xn
    h2 = _ln(xn) * (1.0 + sc2_ref[...]) + sh2_ref[...]
    h_hi = h2.astype(BF16)
    h2_ref[...] = h_hi
    h_lo = (h2 - h_hi.astype(F32)).astype(BF16)
    logits = _dot_nt(wrh_ref[...], h_hi) + (_dot_nt(wrh_ref[...], h_lo) + _dot_nt(wrl_ref[...], h_hi))
    sc_ref[...] = _sigmoid(logits)


def _merge(x_all, o_a, o_b, o_c, proj, modp, w_branch, w_out, ln_g, ln_b, wr_hi, wr_lo, seg, *, n_rows, alpha):
    d = x_all.shape[1]
    bw = o_a.shape[1]
    n_exp = wr_hi.shape[0]
    tok = lambda w: pl.BlockSpec((TM, w), lambda i: (i, 0))
    gate = lambda c: pl.BlockSpec((TM, bw), lambda i, c=c: (i, c))
    mod = lambda k: pl.BlockSpec((None, None, 1, d), lambda i, k=k: (k, seg(i), 0, 0))
    full = lambda a: pl.BlockSpec(a.shape, lambda i: (0,) * a.ndim)
    return pl.pallas_call(
        functools.partial(_merge_kernel, alpha=alpha),
        grid=(n_rows // TM,),
        in_specs=[tok(d), tok(bw), tok(bw), tok(bw)] + [gate(7 + c) for c in range(6)]
                 + [full(w_branch), full(w_out), mod(2), mod(3), mod(4), full(ln_g), full(ln_b),
                    full(wr_hi), full(wr_lo)],
        out_specs=[tok(d), tok(d), pl.BlockSpec((n_exp, TM), lambda i: (0, i))],
        out_shape=[jax.ShapeDtypeStruct((n_rows, d), F32),
                   jax.ShapeDtypeStruct((n_rows, d), BF16),
                   jax.ShapeDtypeStruct((n_exp, n_rows), F32)],
        compiler_params=_cparams(("parallel",)),
        name="merge_residual_router",
    )(x_all, o_a, o_b, o_c, proj, proj, proj, proj, proj, proj, w_branch, w_out,
      modp, modp, modp, ln_g, ln_b, wr_hi, wr_lo)


def _tile_rows(n_exp):
    return TOP_K * TD + n_exp * UNIT


def _route_kernel(s_ref, bias_ref, tri_ref, ltri_ref, pos_ref, w_ref, unit_ref, tot_ref, carry):
    @pl.when(pl.program_id(0) == 0)
    def _():
        carry[...] = jnp.zeros_like(carry)

    s = s_ref[...]
    n_exp, tn = s.shape
    per = n_exp // N_GROUPS
    neg = -jnp.inf
    biased = s + bias_ref[...]
    sub = lax.broadcasted_iota(I32, (per, tn), 0)
    gs_rows = []
    for g in range(N_GROUPS):
        blk = biased[g * per:(g + 1) * per, :]
        m1 = jnp.max(blk, axis=0, keepdims=True)
        first = jnp.min(jnp.where(blk == m1, sub, per), axis=0, keepdims=True)
        m2 = jnp.max(jnp.where(sub == first, neg, blk), axis=0, keepdims=True)
        gs_rows.append(m1 + m2)
    gs = jnp.concatenate(gs_rows, axis=0)
    gi = lax.broadcasted_iota(I32, gs.shape, 0)
    g_ok = jnp.zeros(gs.shape, F32)
    cur = gs
    for _ in range(TOPK_GROUPS):
        m = jnp.max(cur, axis=0, keepdims=True)
        pick = jnp.min(jnp.where(cur == m, gi, N_GROUPS), axis=0, keepdims=True)
        hit = gi == pick
        g_ok = jnp.where(hit, 1.0, g_ok)
        cur = jnp.where(hit, neg, cur)
    ok_rows = [jnp.broadcast_to(g_ok[g:g + 1, :], (per, tn)) for g in range(N_GROUPS)]
    expert_ok = jnp.concatenate(ok_rows, axis=0)
    masked = jnp.where(expert_ok > 0.0, biased, neg)
    ei = lax.broadcasted_iota(I32, s.shape, 0)
    pick_s, hits = [], []
    sel = jnp.zeros(s.shape, F32)
    for _ in range(TOP_K):
        m = jnp.max(masked, axis=0, keepdims=True)
        pick = jnp.min(jnp.where(masked == m, ei, n_exp), axis=0, keepdims=True)
        hit = ei == pick
        pick_s.append(jnp.sum(jnp.where(hit, s, 0.0), axis=0, keepdims=True))
        hits.append(hit)
        sel = jnp.where(hit, 1.0, sel)
        masked = jnp.where(hit, neg, masked)
    tot = pick_s[0]
    for k in range(1, TOP_K):
        tot = tot + pick_s[k]
    w_ref[...] = jnp.concatenate([p / tot * ROUTED_SCALE for p in pick_s], axis=0)

    incl = _dot(sel.astype(BF16), tri_ref[...])
    count = incl[:, tn - 1:tn]
    run = jnp.floor((count + (UNIT - 1)) * (1.0 / UNIT)) * UNIT
    run_start = _dot(ltri_ref[...], jnp.broadcast_to(run, (n_exp, 128)).astype(BF16))[:, 0:1]
    row_in_tile = incl - sel + run_start
    pos = [jnp.sum(jnp.where(hit, row_in_tile, 0.0), axis=0, keepdims=True) for hit in hits]
    pos_ref[...] = jnp.concatenate(pos, axis=0).astype(I32)

    region_used = carry[:, 0:1]
    u = lax.broadcasted_iota(I32, (n_exp, UNIT_LANES), 1).astype(F32)
    eu = lax.broadcasted_iota(I32, (n_exp, UNIT_LANES), 0).astype(F32)
    u0 = run_start * (1.0 / UNIT)
    nu = run * (1.0 / UNIT)
    inside = jnp.logical_and(u >= u0, u < u0 + nu)
    rel = jnp.sum(jnp.where(inside, u - u0 + region_used * (1.0 / UNIT), 0.0), axis=0, keepdims=True)
    exp_of = jnp.sum(jnp.where(inside, eu, 0.0), axis=0, keepdims=True)
    n_units = jnp.broadcast_to(jnp.sum(nu, axis=0, keepdims=True), (1, UNIT_LANES))
    pad = jnp.zeros((5, UNIT_LANES), F32)
    unit_ref[...] = jnp.concatenate([rel, exp_of, n_units, pad], axis=0).astype(I32)

    total = region_used + run
    carry[...] = jnp.broadcast_to(total, carry.shape)
    tot_ref[...] = jnp.broadcast_to(total, tot_ref.shape)


def _route(scores_t, bias_col, tri, ltri):
    n_exp, t = scores_t.shape
    nt = t // TD
    return pl.pallas_call(
        _route_kernel,
        grid=(nt,),
        in_specs=[pl.BlockSpec((n_exp, TD), lambda i: (0, i)),
                  pl.BlockSpec((n_exp, 1), lambda i: (0, 0)),
                  pl.BlockSpec((TD, TD), lambda i: (0, 0)),
                  pl.BlockSpec((n_exp, n_exp), lambda i: (0, 0))],
        out_specs=[pl.BlockSpec((TOP_K, TD), lambda i: (0, i)),
                   pl.BlockSpec((TOP_K, TD), lambda i: (0, i)),
                   pl.BlockSpec((8, UNIT_LANES), lambda i: (0, i)),
                   pl.BlockSpec((n_exp, 128), lambda i: (0, 0))],
        out_shape=[jax.ShapeDtypeStruct((TOP_K, t), I32),
                   jax.ShapeDtypeStruct((TOP_K, t), F32),
                   jax.ShapeDtypeStruct((8, nt * UNIT_LANES), I32),
                   jax.ShapeDtypeStruct((n_exp, 128), F32)],
        scratch_shapes=[pltpu.VMEM((n_exp, 128), F32)],
        compiler_params=_cparams(("arbitrary",)),
        name="route_topk",
    )(scores_t, bias_col, tri, ltri)


def _unit_dest_kernel(start_ref, unit_ref, o_ref, *, n_exp):
    tbl = unit_ref[...]
    exp_of = tbl[1:2, :]
    dst = tbl[0:1, :]
    for e in range(n_exp):
        dst = dst + jnp.where(exp_of == e, start_ref[e], 0)
    o_ref[...] = jnp.concatenate([dst, tbl[2:3, :], jnp.zeros((6, tbl.shape[1]), I32)], axis=0)


def _unit_dest(region_start_units, units):
    blk = lambda: pl.BlockSpec((8, UNIT_LANES), lambda i, s: (0, i))
    return pl.pallas_call(
        functools.partial(_unit_dest_kernel, n_exp=region_start_units.shape[0]),
        grid_spec=pltpu.PrefetchScalarGridSpec(
            num_scalar_prefetch=1, grid=(units.shape[1] // UNIT_LANES,), in_specs=[blk()], out_specs=blk()),
        out_shape=jax.ShapeDtypeStruct(units.shape, I32),
        compiler_params=_cparams(("parallel",)),
        name="route_units",
    )(region_start_units, units)


def _unit_rows(i):
    return pl.ds(pl.multiple_of(i * UNIT, UNIT), UNIT)


def _dispatch_kernel(tail_lo_ref, tail_hi_ref, nu_ref, unit_ref, pos_ref, h_ref, xs_ref, s_scr, zero_scr,
                     sem, zsem, *, n_exp):
    rows = s_scr.shape[1]
    td = h_ref.shape[0]
    step = pl.program_id(0)
    last = pl.num_programs(0) - 1
    slot = step % 2

    def zero_copy(g):
        return pltpu.make_async_copy(zero_scr, xs_ref.at[_unit_rows(g), :], zsem.at[0])

    def unit_copy(u, sl):
        return pltpu.make_async_copy(s_scr.at[sl, _unit_rows(u), :], xs_ref.at[_unit_rows(unit_ref[0, u]), :],
                                     sem.at[sl])

    def drain(count, sl):
        lax.fori_loop(0, count, lambda u, c: (unit_copy(0, sl).wait(), c)[1], 0)

    @pl.when(step >= 2)
    def _():
        drain(nu_ref[step - 2], slot)

    @pl.when(step == 0)
    def _():
        zero_scr[...] = jnp.zeros_like(zero_scr)

        def per_expert(e, c):
            lax.fori_loop(tail_lo_ref[e], tail_hi_ref[e], lambda g, cc: (zero_copy(g).start(), cc)[1], 0)
            lax.fori_loop(tail_lo_ref[e], tail_hi_ref[e], lambda g, cc: (zero_copy(g).wait(), cc)[1], 0)
            return c

        lax.fori_loop(0, n_exp, per_expert, 0)

    pos = pos_ref[...]
    h = h_ref[...]
    for r0 in range(0, rows, SORT_CHUNK):
        srow = lax.broadcasted_iota(I32, (SORT_CHUNK, td), 0) + r0
        onehot = jnp.zeros((SORT_CHUNK, td), F32)
        for k in range(pos.shape[0]):
            onehot = jnp.where(srow == pos[k:k + 1, :], 1.0, onehot)
        s_scr[slot, r0:r0 + SORT_CHUNK, :] = _dot(onehot.astype(BF16), h).astype(BF16)

    lax.fori_loop(0, nu_ref[step], lambda u, c: (unit_copy(u, slot).start(), c)[1], 0)

    @pl.when(step == last)
    def _():
        drain(nu_ref[step], slot)

        @pl.when(step >= 1)
        def _():
            drain(nu_ref[step - 1], 1 - slot)


def _dispatch(tail_lo, tail_hi, n_units, unit_dst, pos, h2, n_slots):
    t, d = h2.shape
    n_exp = tail_lo.shape[0]
    return pl.pallas_call(
        functools.partial(_dispatch_kernel, n_exp=n_exp),
        grid_spec=pltpu.PrefetchScalarGridSpec(
            num_scalar_prefetch=3, grid=(t // TD,),
            in_specs=[pl.BlockSpec((8, UNIT_LANES), lambda i, a, b, c: (0, i), memory_space=pltpu.SMEM),
                      pl.BlockSpec((TOP_K, TD), lambda i, a, b, c: (0, i)),
                      pl.BlockSpec((TD, d), lambda i, a, b, c: (i, 0))],
            out_specs=pl.BlockSpec(memory_space=pl.ANY),
            scratch_shapes=[pltpu.VMEM((2, _tile_rows(n_exp), d), BF16), pltpu.VMEM((UNIT, d), BF16),
                            pltpu.SemaphoreType.DMA((2,)), pltpu.SemaphoreType.DMA((1,))]),
        out_shape=jax.ShapeDtypeStruct((n_slots, d), BF16),
        compiler_params=_cparams(("arbitrary",)),
        name="moe_dispatch",
    )(tail_lo, tail_hi, n_units, unit_dst, pos, h2)


def _expert_kernel(blk0_ref, nblk_ref, wg_ref, wu_ref, wd_ref, xs_ref, ys_ref, xbuf, obuf, wgu_scr, wd_scr,
                   sem_in, sem_out):
    e = pl.program_id(0)
    n = nblk_ref[e]
    b0 = blk0_ref[e]
    de = wg_ref.shape[1]
    wgu_scr[:, :de] = wg_ref[...].astype(BF16)
    wgu_scr[:, de:] = wu_ref[...].astype(BF16)
    wd_scr[...] = wd_ref[...].astype(BF16)

    def rows(j):
        return pl.ds(pl.multiple_of((b0 + j) * MOE_BLK, MOE_BLK), MOE_BLK)

    def in_copy(j, sl):
        return pltpu.make_async_copy(xs_ref.at[rows(j), :], xbuf.at[sl], sem_in.at[sl])

    def out_copy(j, sl):
        return pltpu.make_async_copy(obuf.at[sl], ys_ref.at[rows(j), :], sem_out.at[sl])

    @pl.when(n > 0)
    def _():
        in_copy(0, 0).start()

    def block(j, c):
        sl = j % 2
        in_copy(j, sl).wait()

        @pl.when(j + 1 < n)
        def _():
            in_copy(j + 1, 1 - sl).start()

        @pl.when(j >= 2)
        def _():
            out_copy(j - 2, sl).wait()

        gu = _dot(xbuf[sl], wgu_scr[...])
        act = _silu(gu[:, :de]) * gu[:, de:]
        obuf[sl] = _dot(act.astype(BF16), wd_scr[...]).astype(BF16)
        out_copy(j, sl).start()
        return c

    lax.fori_loop(0, n, block, 0)

    @pl.when(n >= 2)
    def _():
        out_copy(n - 2, n % 2).wait()

    @pl.when(n >= 1)
    def _():
        out_copy(n - 1, (n - 1) % 2).wait()


def _experts(blk0, nblk, xs, w_gate, w_up, w_down, layer):
    n_slots, d = xs.shape
    _, n_exp, _, de = w_gate.shape
    return pl.pallas_call(
        _expert_kernel,
        grid_spec=pltpu.PrefetchScalarGridSpec(
            num_scalar_prefetch=2, grid=(n_exp,),
            in_specs=[pl.BlockSpec((None, None, d, de), lambda e, a, b: (layer, e, 0, 0)),
                      pl.BlockSpec((None, None, d, de), lambda e, a, b: (layer, e, 0, 0)),
                      pl.BlockSpec((None, None, de, d), lambda e, a, b: (layer, e, 0, 0)),
                      pl.BlockSpec(memory_space=pl.ANY)],
            out_specs=pl.BlockSpec(memory_space=pl.ANY),
            scratch_shapes=[pltpu.VMEM((2, MOE_BLK, d), BF16), pltpu.VMEM((2, MOE_BLK, d), BF16),
                            pltpu.VMEM((d, 2 * de), BF16), pltpu.VMEM((de, d), BF16),
                            pltpu.SemaphoreType.DMA((2,)), pltpu.SemaphoreType.DMA((2,))]),
        out_shape=jax.ShapeDtypeStruct((n_slots, d), BF16),
        compiler_params=_cparams(("arbitrary",)),
        name="moe_experts",
    )(blk0, nblk, w_gate, w_up, w_down, xs)


def _combine_kernel(unit_ref, next_ref, pos_ref, w_ref, x_ref, h_ref, gate2_ref, sg_ref, su_ref, sd_ref, lng_ref,
                    lnb_ref, ys_ref, o_ref, s_scr, sem, *, alpha):
    rows = s_scr.shape[1]
    td = x_ref.shape[0]
    step = pl.program_id(0)
    slot = step % 2

    def unit_copy(tbl, u, sl):
        return pltpu.make_async_copy(ys_ref.at[_unit_rows(tbl[0, u]), :], s_scr.at[sl, _unit_rows(u), :], sem.at[sl])

    def fetch(tbl, sl):
        lax.fori_loop(0, tbl[1, 0], lambda u, c: (unit_copy(tbl, u, sl).start(), c)[1], 0)

    @pl.when(step == 0)
    def _():
        s_scr[...] = jnp.zeros_like(s_scr)
        fetch(unit_ref, 0)

    @pl.when(step + 1 < pl.num_programs(0))
    def _():
        fetch(next_ref, 1 - slot)

    h = h_ref[...]
    acc = _dot((_silu(_dot(h, sg_ref[...])) * _dot(h, su_ref[...])).astype(BF16), sd_ref[...])
    pos = pos_ref[...]
    w = w_ref[...]
    lax.fori_loop(0, unit_ref[1, 0], lambda u, c: (unit_copy(unit_ref, u, slot).wait(), c)[1], 0)
    for r0 in range(0, rows, SORT_CHUNK):
        lane = lax.broadcasted_iota(I32, (td, SORT_CHUNK), 1) + r0
        wmat = jnp.zeros((td, SORT_CHUNK), F32)
        for k in range(pos.shape[1]):
            wmat = jnp.where(lane == pos[:, k:k + 1], w[:, k:k + 1], wmat)
        w_hi = wmat.astype(BF16)
        w_lo = (wmat - w_hi.astype(F32)).astype(BF16)
        blk = s_scr[slot, r0:r0 + SORT_CHUNK, :]
        acc = acc + (_dot(w_hi, blk) + _dot(w_lo, blk))
    o_ref[...] = _ln(alpha * x_ref[...] + gate2_ref[...] * acc) * lng_ref[...] + lnb_ref[...]


def _combine(unit_dst, pos_tok, w_tok, xn, h2, modp, sh_gate, sh_up, sh_down, ln_g, ln_b, ys, seg_td, *,
             alpha, n_exp):
    t, d = xn.shape
    k = pos_tok.shape[1]
    nt = t // TD
    tok = pl.BlockSpec((TD, d), lambda i: (i, 0))
    full = lambda a: pl.BlockSpec(a.shape, lambda i: (0,) * a.ndim)
    return pl.pallas_call(
        functools.partial(_combine_kernel, alpha=alpha),
        grid=(nt,),
        in_specs=[pl.BlockSpec((8, UNIT_LANES), lambda i: (0, i), memory_space=pltpu.SMEM),
                  pl.BlockSpec((8, UNIT_LANES), lambda i: (0, jnp.minimum(i + 1, nt - 1)), memory_space=pltpu.SMEM),
                  pl.BlockSpec((TD, k), lambda i: (i, 0)),
                  pl.BlockSpec((TD, k), lambda i: (i, 0)),
                  tok, tok,
                  pl.BlockSpec((None, None, 1, d), lambda i: (5, seg_td(i), 0, 0)),
                  full(sh_gate), full(sh_up), full(sh_down), full(ln_g), full(ln_b),
                  pl.BlockSpec(memory_space=pl.ANY)],
        out_specs=tok,
        out_shape=jax.ShapeDtypeStruct((t, d), F32),
        scratch_shapes=[pltpu.VMEM((2, _tile_rows(n_exp), d), BF16), pltpu.SemaphoreType.DMA((2,))],
        compiler_params=_cparams(("arbitrary",)),
        name="moe_combine",
    )(unit_dst, unit_dst, pos_tok, w_tok, xn, h2, modp, sh_gate, sh_up, sh_down, ln_g, ln_b, ys)


def _moe(xn, h2, scores_t, modp, seg_td, router_bias, w_gate, w_up, w_down, sh_gate, sh_up, sh_down,
         ln_g, ln_b, tri, ltri, *, alpha, layer):
    t = xn.shape[0]
    n_exp = scores_t.shape[0]
    assert _tile_rows(n_exp) % SORT_CHUNK == 0 and _tile_rows(n_exp) <= UNIT * UNIT_LANES
    pos, w_sel, units, tot = _route(scores_t, router_bias.reshape(n_exp, 1), tri, ltri)
    used = tot[:, 0].astype(I32)
    region = (used + MOE_BLK - 1) // MOE_BLK * MOE_BLK
    region_end = jnp.cumsum(region)
    region_start = region_end - region
    n_slots = TOP_K * t + n_exp * UNIT * (t // TD) + n_exp * MOE_BLK
    unit_dst = _unit_dest((region_start // UNIT).astype(I32), units)
    xs = _dispatch(((region_start + used) // UNIT).astype(I32), (region_end // UNIT).astype(I32),
                   unit_dst[1, ::UNIT_LANES], unit_dst, pos, h2, n_slots)
    ys = _experts((region_start // MOE_BLK).astype(I32), (region // MOE_BLK).astype(I32), xs, w_gate, w_up, w_down, layer)
    return _combine(unit_dst, pos.T, w_sel.T, xn, h2, modp, sh_gate, sh_up, sh_down, ln_g, ln_b, ys, seg_td,
                    alpha=alpha, n_exp=n_exp)


def kernel(x, c, ctx, c_ctx, w_mod, b_mod, w_in, b_in, lam_q1, lam_k1, lam_q2, lam_k2, attn_norm_g, sg_ln_g, sg_ln_b, sg_w, sg_b, conv_w, conv_b, lru_wa, lru_ba, lru_wx, lru_bx, lru_lam, w_branch, w_out, ln1_g, ln1_b, w_router, router_bias, moe_w_gate, moe_w_up, moe_w_down, sh_w_gate, sh_w_up, sh_w_down, ln2_g, ln2_b):
    batch, n_lat, d = x.shape
    n_ctx = ctx.shape[1]
    depth = w_mod.shape[0]
    n_exp = w_router.shape[2]
    t_lat = batch * n_lat
    t_ctx = batch * n_ctx
    t_all = t_lat + t_ctx
    assert n_lat % TM == 0 and t_ctx % TM == 0 and n_ctx % B_CHUNK == 0 and t_lat % n_ctx == 0
    assert batch + 1 <= MOD_ROWS and TM % TD == 0
    alpha = (2 * depth) ** 0.25
    cw = conv_w.shape[2]
    tiles_per_batch = n_lat // TM

    seg = lambda i: jnp.minimum(i // tiles_per_batch, batch)
    seg_td = lambda i: jnp.minimum(i // (n_lat // TD), batch)
    tbl_idx = lambda i: jnp.where(i < t_lat // TM, i % tiles_per_batch, tiles_per_batch)
    tm_in = 2 * TM if (n_lat % (2 * TM) == 0 and t_ctx % (2 * TM) == 0) else TM
    seg_in = lambda i: jnp.minimum(i // (n_lat // tm_in), batch)

    x_all = jnp.concatenate([x.reshape(t_lat, d), ctx.reshape(t_ctx, d)], axis=0)
    c_all = jnp.zeros((MOD_ROWS, d), F32).at[:batch].set(c).at[batch].set(c_ctx)
    tables = _rope_tables(n_lat)
    tri = (jnp.arange(TD)[:, None] <= jnp.arange(TD)[None, :]).astype(BF16)
    ltri = (jnp.arange(n_exp)[None, :] < jnp.arange(n_exp)[:, None]).astype(BF16)
    row = lambda v: v.reshape(1, -1)

    def dense_blocks(w):
        nb, bi, bj = w.shape
        eye = jnp.eye(nb, dtype=w.dtype)
        return (w[:, :, None, :] * eye[:, None, :, None]).reshape(nb * bi, nb * bj)

    for l in range(depth):
        last = l == depth - 1
        lam_init = 0.8 - 0.6 * math.exp(-0.3 * l)
        mod = _mod_rows(c_all, w_mod[l].astype(BF16), row(b_mod[l]))
        modp = mod.reshape(MOD_ROWS, 6, 1, d).transpose(1, 0, 2, 3)

        proj = _in_proj(x_all, modp, w_in[l].astype(BF16), row(b_in[l]), seg_in, tm_in)
        qr, kr, vb = _rope(proj, tables, tbl_idx)

        lam_vecs = jnp.stack([lam_q1[l], lam_k1[l], lam_q2[l], lam_k2[l]])
        gain = attn_norm_g[l].reshape(A_HEADS, 1, A_V_DIM)
        attn = functools.partial(_attention, qr, kr, vb, lam_vecs, gain, lam_init=lam_init,
                                 batch=batch, n_lat=n_lat, n_ctx=n_ctx)
        o_a = attn(latent_queries=True)

        o_b = _spatial_gating(proj, row(sg_ln_g[l]), row(sg_ln_b[l]), sg_w[l].astype(BF16),
                              sg_b[l].reshape(B_GROUPS, B_CHUNK, 1))

        lru = functools.partial(_lru_pass, proj, conv_w=conv_w[l], conv_b=row(conv_b[l]), batch=batch)
        zeros_h = jnp.zeros((batch, 8, cw), F32)
        hf = {}
        for direction in range(2):
            wd = jnp.concatenate([dense_blocks(lru_wa[l, direction]), dense_blocks(lru_wx[l, direction])],
                                 axis=1).astype(BF16)
            par = dict(w_dense=wd, ba=row(lru_ba[l, direction]), bx=row(lru_bx[l, direction]),
                       lam=row(lru_lam[l, direction]), reverse=direction == 1, final=direction == 1)
            h_ctx, edge = lru(hf.get("ctx"), zeros_h, seq=n_ctx, tl=n_ctx, row0=t_lat, **par)
            h_lat, _ = lru(hf.get("lat"), edge, seq=n_lat, tl=TM, row0=0, **par)
            hf = {"ctx": h_ctx, "lat": h_lat}
        o_c_lat, o_c_ctx = hf["lat"], hf["ctx"]

        n_rows = t_lat if last else t_all
        if last:
            o_a_all, o_c = o_a, o_c_lat
        else:
            o_a_all = jnp.concatenate([o_a, attn(latent_queries=False)], axis=0)
            o_c = jnp.concatenate([o_c_lat, o_c_ctx], axis=0)
        wr_t = w_router[l].T
        wr_hi = wr_t.astype(BF16)
        wr_lo = (wr_t - wr_hi.astype(F32)).astype(BF16)
        xn, h2, scores_t = _merge(x_all, o_a_all, o_b, o_c, proj, modp, w_branch[l].astype(BF16),
                                  w_out[l].astype(BF16), row(ln1_g[l]), row(ln1_b[l]), wr_hi, wr_lo, seg,
                                  n_rows=n_rows, alpha=alpha)
        x_all = _moe(xn, h2, scores_t, modp, seg_td, router_bias[l], moe_w_gate,
                     moe_w_up, moe_w_down, sh_w_gate[l].astype(BF16),
                     sh_w_up[l].astype(BF16), sh_w_down[l].astype(BF16), row(ln2_g[l]), row(ln2_b[l]), tri, ltri,
                     alpha=alpha, layer=l)
    return x_all[:t_lat].reshape(batch, n_lat, d)
```

```python
import functools
import math

import jax
import jax.numpy as jnp
from jax import lax
from jax.experimental import pallas as pl
from jax.experimental.pallas import tpu as pltpu

F32 = jnp.float32
BF16 = jnp.bfloat16
I32 = jnp.int32

A_HEADS = 4
A_QK_DIM = 64
A_V_DIM = 2 * A_QK_DIM
GRID_W = 64
ROPE_THETA = 10000.0
B_CHUNK = 128
B_GROUPS = 4
C_BLOCKS = 8
C_POW = 8.0
TOP_K = 8
N_GROUPS = 8
TOPK_GROUPS = 4
ROUTED_SCALE = 2.5
LN_EPS = 1e-6
RMS_EPS = 1e-5

TM = 512
MOE_BLK = 512
TD = 256
UNIT = 16
UNIT_LANES = 256
SORT_CHUNK = 512
EXPERT_BUFS = 4
MOD_ROWS = 16
VMEM_LIMIT = 56 * 1024 * 1024


def _cparams(sem):
    return pltpu.CompilerParams(dimension_semantics=sem, vmem_limit_bytes=VMEM_LIMIT)


def _ln(x):
    mu = jnp.mean(x, axis=-1, keepdims=True)
    xc = x - mu
    var = jnp.mean(xc * xc, axis=-1, keepdims=True)
    return xc * lax.rsqrt(var + LN_EPS)


def _gelu(x):
    cdf = 0.5 * (1.0 + jnp.tanh(math.sqrt(2.0 / math.pi) * (x + 0.044715 * (x * x * x))))
    return x * cdf


def _sigmoid(x):
    return 1.0 / (1.0 + jnp.exp(-x))


def _silu(x):
    return x * _sigmoid(x)


def _dot(a, b):
    return jnp.dot(a, b, preferred_element_type=F32)


def _dot_nt(a, b):
    return lax.dot_general(a, b, (((1,), (1,)), ((), ())), preferred_element_type=F32)


def _mod_kernel(c_ref, w_ref, b_ref, o_ref):
    o_ref[...] = _dot(_silu(c_ref[...]).astype(BF16), w_ref[...]) + b_ref[...]


def _mod_rows(c_all, w, b):
    m, d = c_all.shape
    n = w.shape[1]
    tn = 1536
    return pl.pallas_call(
        _mod_kernel,
        grid=(n // tn,),
        in_specs=[pl.BlockSpec((m, d), lambda j: (0, 0)),
                  pl.BlockSpec((d, tn), lambda j: (0, j)),
                  pl.BlockSpec((1, tn), lambda j: (0, j))],
        out_specs=pl.BlockSpec((m, tn), lambda j: (0, j)),
        out_shape=jax.ShapeDtypeStruct((m, n), F32),
        compiler_params=_cparams(("parallel",)),
        name="adaln_rows",
    )(c_all, w, b)


def _in_proj_kernel(x_ref, sh_ref, sc_ref, w_ref, b_ref, o_ref, h_scr):
    @pl.when(pl.program_id(1) == 0)
    def _():
        h = _ln(x_ref[...]) * (1.0 + sc_ref[...]) + sh_ref[...]
        h_scr[...] = h.astype(BF16)

    o_ref[...] = _dot(h_scr[...], w_ref[...]) + b_ref[...]


def _in_proj(x_all, modp, w, b, seg, tm):
    t, d = x_all.shape
    n = w.shape[1]
    tn = 512
    return pl.pallas_call(
        _in_proj_kernel,
        grid=(t // tm, n // tn),
        in_specs=[pl.BlockSpec((tm, d), lambda i, j: (i, 0)),
                  pl.BlockSpec((None, None, 1, d), lambda i, j: (0, seg(i), 0, 0)),
                  pl.BlockSpec((None, None, 1, d), lambda i, j: (1, seg(i), 0, 0)),
                  pl.BlockSpec((d, tn), lambda i, j: (0, j)),
                  pl.BlockSpec((1, tn), lambda i, j: (0, j))],
        out_specs=pl.BlockSpec((tm, tn), lambda i, j: (i, j)),
        out_shape=jax.ShapeDtypeStruct((t, n), F32),
        scratch_shapes=[pltpu.VMEM((tm, d), BF16)],
        compiler_params=_cparams(("parallel", "arbitrary")),
        name="in_proj",
    )(x_all, modp, modp, w, b)


def _rope_kernel(q_ref, k_ref, v_ref, cos_ref, sa_ref, sb_ref, qo_ref, ko_ref, vo_ref):
    cos = cos_ref[...]
    sa = sa_ref[...]
    sb = sb_ref[...]

    def rope(x):
        w = x.shape[-1]
        return x * cos + pltpu.roll(x, w - 16, 1) * sa + pltpu.roll(x, 16, 1) * sb

    qo_ref[...] = (rope(q_ref[...]) * (A_QK_DIM ** -0.5)).astype(BF16)
    ko_ref[...] = rope(k_ref[...]).astype(BF16)
    vo_ref[...] = v_ref[...].astype(BF16)


def _rope_tables(n):
    rows = n // GRID_W
    pos_row = jnp.repeat(jnp.arange(rows), GRID_W).astype(F32)
    pos_col = jnp.tile(jnp.arange(GRID_W), rows).astype(F32)
    quarter = A_QK_DIM // 4
    inv = ROPE_THETA ** (-jnp.arange(quarter, dtype=F32) / quarter)
    ang_r = pos_row[:, None] * inv
    ang_c = pos_col[:, None] * inv
    ang = jnp.concatenate([ang_r, ang_r, ang_c, ang_c], axis=-1)
    cos = jnp.cos(ang)
    sin = jnp.sin(ang)
    first_half = (jnp.arange(A_QK_DIM) % (2 * quarter)) < quarter
    sa = jnp.where(first_half, -sin, 0.0)
    sb = jnp.where(first_half, 0.0, sin)
    reps = 2 * A_HEADS

    def widen(tbl, fill):
        tbl = jnp.tile(tbl, (1, reps))
        return jnp.concatenate([tbl, jnp.full((TM, tbl.shape[1]), fill, F32)], axis=0)

    return widen(cos, 1.0), widen(sa, 0.0), widen(sb, 0.0)


def _rope(proj, tables, tbl_idx):
    t = proj.shape[0]
    w = 2 * A_HEADS * A_QK_DIM
    cos, sa, sb = tables
    col = lambda c: pl.BlockSpec((TM, w), lambda i, c=c: (i, c))
    tbl = pl.BlockSpec((TM, w), lambda i: (tbl_idx(i), 0))
    out = pl.BlockSpec((TM, w), lambda i: (i, 0))
    shp = jax.ShapeDtypeStruct((t, w), BF16)
    return pl.pallas_call(
        _rope_kernel,
        grid=(t // TM,),
        in_specs=[col(0), col(1), col(2), tbl, tbl, tbl],
        out_specs=[out, out, out],
        out_shape=[shp, shp, shp],
        compiler_params=_cparams(("parallel",)),
        name="rope_cast",
    )(proj, proj, proj, cos, sa, sb)


def _attn_kernel(*refs, lam_init, with_lat):
    if with_lat:
        lam_ref, gain_ref, q_ref, kc_ref, vc_ref, kl_ref, vl_ref, o_ref = refs
    else:
        lam_ref, gain_ref, q_ref, kc_ref, vc_ref, o_ref = refs
    lv = lam_ref[...]
    lam = (jnp.exp(jnp.sum(lv[0:1] * lv[1:2], axis=-1, keepdims=True))
           - jnp.exp(jnp.sum(lv[2:3] * lv[3:4], axis=-1, keepdims=True)) + lam_init)
    outs = []
    for c in range(2):
        cols = slice(c * A_QK_DIM, (c + 1) * A_QK_DIM)
        qc = q_ref[:, cols]
        s_c = _dot_nt(qc, kc_ref[:, cols])
        m = jnp.max(s_c, axis=-1, keepdims=True)
        if with_lat:
            s_l = _dot_nt(qc, kl_ref[:, cols])
            m = jnp.maximum(m, jnp.max(s_l, axis=-1, keepdims=True))
        p_c = jnp.exp(s_c - m)
        den = jnp.sum(p_c, axis=-1, keepdims=True)
        pv = _dot(p_c.astype(BF16), vc_ref[...])
        if with_lat:
            p_l = jnp.exp(s_l - m)
            den = den + jnp.sum(p_l, axis=-1, keepdims=True)
            pv = pv + _dot(p_l.astype(BF16), vl_ref[...])
        outs.append(pv / den)
    o = outs[0] - lam * outs[1]
    o = o * lax.rsqrt(jnp.mean(o * o, axis=-1, keepdims=True) + RMS_EPS)
    o_ref[...] = (o * gain_ref[...]) * (1.0 - lam_init)


def _attention(qr, kr, vb, lam_vecs, gain, *, lam_init, batch, n_lat, n_ctx, latent_queries):
    hw = A_V_DIM
    ctx_blk0 = batch * n_lat // n_ctx
    lam_spec = pl.BlockSpec(lam_vecs.shape, lambda b, h, i: (0, 0))
    gain_spec = pl.BlockSpec((None, 1, hw), lambda b, h, i: (h, 0, 0))
    kv_ctx = pl.BlockSpec((n_ctx, hw), lambda b, h, i: (ctx_blk0 + b, h))
    if latent_queries:
        tq = 256
        nq = n_lat // tq
        q_spec = pl.BlockSpec((tq, hw), lambda b, h, i: (b * nq + i, h))
        kv_lat = pl.BlockSpec((n_lat, hw), lambda b, h, i: (b, h))
        in_specs = [lam_spec, gain_spec, q_spec, kv_ctx, kv_ctx, kv_lat, kv_lat]
        args = (lam_vecs, gain, qr, kr, vb, kr, vb)
        rows = batch * n_lat
    else:
        tq = n_ctx
        nq = 1
        q_spec = pl.BlockSpec((tq, hw), lambda b, h, i: (ctx_blk0 + b, h))
        in_specs = [lam_spec, gain_spec, q_spec, kv_ctx, kv_ctx]
        args = (lam_vecs, gain, qr, kr, vb)
        rows = batch * n_ctx
    return pl.pallas_call(
        functools.partial(_attn_kernel, lam_init=lam_init, with_lat=latent_queries),
        grid=(batch, A_HEADS, nq),
        in_specs=in_specs,
        out_specs=pl.BlockSpec((tq, hw), lambda b, h, i: (b * nq + i, h)),
        out_shape=jax.ShapeDtypeStruct((rows, A_HEADS * hw), F32),
        compiler_params=_cparams(("parallel", "parallel", "arbitrary")),
        name="diff_attn_lat" if latent_queries else "diff_attn_ctx",
    )(*args)


def _sg_kernel(u_ref, s_ref, g_ref, b_ref, w_ref, bs_ref, o_ref):
    gw = B_CHUNK
    for ci in range(TM // B_CHUNK):
        rows = slice(ci * B_CHUNK, (ci + 1) * B_CHUNK)
        for g in range(B_GROUPS):
            cols = slice(g * gw, (g + 1) * gw)
            vn = _ln(_gelu(s_ref[rows, cols])) * g_ref[:, cols] + b_ref[:, cols]
            mixed = _dot(w_ref[g], vn.astype(BF16)) + bs_ref[g]
            o_ref[rows, cols] = _gelu(u_ref[rows, cols]) * mixed


def _spatial_gating(proj, ln_g, ln_b, w_s, b_s):
    t = proj.shape[0]
    w = B_GROUPS * B_CHUNK
    full = lambda a: pl.BlockSpec(a.shape, lambda i: (0,) * a.ndim)
    return pl.pallas_call(
        _sg_kernel,
        grid=(t // TM,),
        in_specs=[pl.BlockSpec((TM, w), lambda i: (i, 3)),
                  pl.BlockSpec((TM, w), lambda i: (i, 4)),
                  full(ln_g), full(ln_b), full(w_s), full(b_s)],
        out_specs=pl.BlockSpec((TM, w), lambda i: (i, 0)),
        out_shape=jax.ShapeDtypeStruct((t, w), F32),
        compiler_params=_cparams(("parallel",)),
        name="spatial_gating",
    )(proj, proj, ln_g, ln_b, w_s, b_s)


def _lru_kernel(*refs, reverse, final, n_tiles, tl):
    if final:
        (x_ref, xp_ref, xn_ref, cw_ref, cb_ref, w_ref, ba_ref, bx_ref, lam_ref, h0_ref,
         hf_ref, y_ref, o_ref, hl_ref, carry) = refs
    else:
        (x_ref, xp_ref, xn_ref, cw_ref, cb_ref, w_ref, ba_ref, bx_ref, lam_ref, h0_ref,
         o_ref, hl_ref, carry) = refs
    step = pl.program_id(1)

    @pl.when(step == 0)
    def _():
        carry[...] = h0_ref[...]

    ti = (n_tiles - 1 - step) if reverse else step
    x = x_ref[...]
    cw = x.shape[-1]
    row = lax.broadcasted_iota(I32, x.shape, 0)
    has_prev = ti > 0
    has_next = ti < n_tiles - 1
    p6 = jnp.where(has_prev, xp_ref[6:7, :], 0.0)
    p7 = jnp.where(has_prev, xp_ref[7:8, :], 0.0)
    n0 = jnp.where(has_next, xn_ref[0:1, :], 0.0)
    xm1 = jnp.where(row == 0, p7, pltpu.roll(x, 1, 0))
    xm2 = jnp.where(row == 0, p6, jnp.where(row == 1, p7, pltpu.roll(x, 2, 0)))
    xp1 = jnp.where(row == tl - 1, n0, pltpu.roll(x, tl - 1, 0))
    taps = cw_ref[...]
    xs = taps[0:1] * xm2 + taps[1:2] * xm1 + taps[2:3] * x + taps[3:4] * xp1 + cb_ref[...]

    z = _dot(xs.astype(BF16), w_ref[...])
    r = _sigmoid(z[:, :cw] + ba_ref[...])
    ig = _sigmoid(z[:, cw:] + bx_ref[...])
    nl = -lam_ref[...]
    softplus = jnp.maximum(nl, 0.0) + jnp.log(1.0 + jnp.exp(-jnp.abs(nl)))
    log_a = -C_POW * r * softplus
    a = jnp.exp(log_a)
    u = jnp.sqrt(1.0 - jnp.exp(2.0 * log_a)) * (ig * xs)

    d = 1
    while d < tl:
        if reverse:
            ok = row < tl - d
            shift = tl - d
        else:
            ok = row >= d
            shift = d
        a_sh = jnp.where(ok, pltpu.roll(a, shift, 0), 1.0)
        u_sh = jnp.where(ok, pltpu.roll(u, shift, 0), 0.0)
        u = u + a * u_sh
        a = a * a_sh
        d *= 2
    h = u + a * carry[0:1, :]
    edge = h[0:1, :] if reverse else h[tl - 1:tl, :]
    carry[...] = jnp.broadcast_to(edge, carry.shape)
    hl_ref[...] = jnp.broadcast_to(edge, hl_ref.shape)
    if final:
        o_ref[...] = _gelu(y_ref[...]) * (hf_ref[...] + h)
    else:
        o_ref[...] = h


def _lru_pass(proj, hf, h0, conv_w, conv_b, w_dense, ba, bx, lam, *, batch, seq, tl, row0, reverse, final):
    t_all = proj.shape[0]
    cw = conv_w.shape[1]
    n_tiles = seq // tl
    base = row0 // tl
    sub = tl // 8
    last8 = t_all // 8 - 1

    def tile(b, s):
        ti = (n_tiles - 1 - s) if reverse else s
        return base + b * n_tiles + ti

    def loc(b, s):
        ti = (n_tiles - 1 - s) if reverse else s
        return b * n_tiles + ti

    full = lambda a: pl.BlockSpec(a.shape, lambda b, s: (0,) * a.ndim)
    in_specs = [pl.BlockSpec((tl, cw), lambda b, s: (tile(b, s), 5)),
                pl.BlockSpec((8, cw), lambda b, s: (jnp.maximum(tile(b, s) * sub - 1, 0), 5)),
                pl.BlockSpec((8, cw), lambda b, s: (jnp.minimum((tile(b, s) + 1) * sub, last8), 5)),
                full(conv_w), full(conv_b), full(w_dense), full(ba), full(bx), full(lam),
                pl.BlockSpec((None, 8, cw), lambda b, s: (b, 0, 0))]
    args = [proj, proj, proj, conv_w, conv_b, w_dense, ba, bx, lam, h0]
    if final:
        in_specs += [pl.BlockSpec((tl, cw), lambda b, s: (loc(b, s), 0)),
                     pl.BlockSpec((tl, cw), lambda b, s: (tile(b, s), 6))]
        args += [hf, proj]
    return pl.pallas_call(
        functools.partial(_lru_kernel, reverse=reverse, final=final, n_tiles=n_tiles, tl=tl),
        grid=(batch, n_tiles),
        in_specs=in_specs,
        out_specs=[pl.BlockSpec((tl, cw), lambda b, s: (loc(b, s), 0)),
                   pl.BlockSpec((None, 8, cw), lambda b, s: (b, 0, 0))],
        out_shape=[jax.ShapeDtypeStruct((batch * seq, cw), F32),
                   jax.ShapeDtypeStruct((batch, 8, cw), F32)],
        scratch_shapes=[pltpu.VMEM((8, cw), F32)],
        compiler_params=_cparams(("parallel", "arbitrary")),
        name="rglru_%s_%s" % ("bwd" if reverse else "fwd", "lat" if row0 == 0 else "ctx"),
    )(*args)


def _merge_kernel(x_ref, oa_ref, ob_ref, oc_ref, g0, g1, g2, g3, g4, g5, wb_ref, wo_ref,
                  gate1_ref, sh2_ref, sc2_ref, lng_ref, lnb_ref, wrh_ref, wrl_ref,
                  xn_ref, h2_ref, sc_ref, *, alpha):
    gates = ((g0, g1), (g2, g3), (g4, g5))
    mix = None
    for r, o_ref in enumerate((oa_ref, ob_ref, oc_ref)):
        proj = _dot(o_ref[...].astype(BF16), wb_ref[r])
        gate = jnp.concatenate([_sigmoid(gates[r][0][...]), _sigmoid(gates[r][1][...])], axis=-1)
        term = gate * proj
        mix = term if mix is None else mix + term
    out = _dot(mix.astype(BF16), wo_ref[...])
    xn = _ln(alpha * x_ref[...] + gate1_ref[...] * out) * lng_ref[...] + lnb_ref[...]
    xn_ref[...] = xn
    h2 = _ln(xn) * (1.0 + sc2_ref[...]) + sh2_ref[...]
    h_hi = h2.astype(BF16)
    h2_ref[...] = h_hi
    h_lo = (h2 - h_hi.astype(F32)).astype(BF16)
    logits = _dot_nt(wrh_ref[...], h_hi) + (_dot_nt(wrh_ref[...], h_lo) + _dot_nt(wrl_ref[...], h_hi))
    sc_ref[...] = _sigmoid(logits)


def _merge(x_all, o_a, o_b, o_c, proj, modp, w_branch, w_out, ln_g, ln_b, wr_hi, wr_lo, seg, *, n_rows, alpha):
    d = x_all.shape[1]
    bw = o_a.shape[1]
    n_exp = wr_hi.shape[0]
    tok = lambda w: pl.BlockSpec((TM, w), lambda i: (i, 0))
    gate = lambda c: pl.BlockSpec((TM, bw), lambda i, c=c: (i, c))
    mod = lambda k: pl.BlockSpec((None, None, 1, d), lambda i, k=k: (k, seg(i), 0, 0))
    full = lambda a: pl.BlockSpec(a.shape, lambda i: (0,) * a.ndim)
    return pl.pallas_call(
        functools.partial(_merge_kernel, alpha=alpha),
        grid=(n_rows // TM,),
        in_specs=[tok(d), tok(bw), tok(bw), tok(bw)] + [gate(7 + c) for c in range(6)]
                 + [full(w_branch), full(w_out), mod(2), mod(3), mod(4), full(ln_g), full(ln_b),
                    full(wr_hi), full(wr_lo)],
        out_specs=[tok(d), tok(d), pl.BlockSpec((n_exp, TM), lambda i: (0, i))],
        out_shape=[jax.ShapeDtypeStruct((n_rows, d), F32),
                   jax.ShapeDtypeStruct((n_rows, d), BF16),
                   jax.ShapeDtypeStruct((n_exp, n_rows), F32)],
        compiler_params=_cparams(("parallel",)),
        name="merge_residual_router",
    )(x_all, o_a, o_b, o_c, proj, proj, proj, proj, proj, proj, w_branch, w_out,
      modp, modp, modp, ln_g, ln_b, wr_hi, wr_lo)


def _tile_rows(n_exp):
    return TOP_K * TD + n_exp * UNIT


def _route_kernel(s_ref, bias_ref, tri_ref, ltri_ref, pos_ref, w_ref, unit_ref, tot_ref, carry):
    @pl.when(pl.program_id(0) == 0)
    def _():
        carry[...] = jnp.zeros_like(carry)

    s = s_ref[...]
    n_exp, tn = s.shape
    per = n_exp // N_GROUPS
    neg = -jnp.inf
    biased = s + bias_ref[...]
    sub = lax.broadcasted_iota(I32, (per, tn), 0)
    gs_rows = []
    for g in range(N_GROUPS):
        blk = biased[g * per:(g + 1) * per, :]
        m1 = jnp.max(blk, axis=0, keepdims=True)
        first = jnp.min(jnp.where(blk == m1, sub, per), axis=0, keepdims=True)
        m2 = jnp.max(jnp.where(sub == first, neg, blk), axis=0, keepdims=True)
        gs_rows.append(m1 + m2)
    gs = jnp.concatenate(gs_rows, axis=0)
    gi = lax.broadcasted_iota(I32, gs.shape, 0)
    g_ok = jnp.zeros(gs.shape, F32)
    cur = gs
    for _ in range(TOPK_GROUPS):
        m = jnp.max(cur, axis=0, keepdims=True)
        pick = jnp.min(jnp.where(cur == m, gi, N_GROUPS), axis=0, keepdims=True)
        hit = gi == pick
        g_ok = jnp.where(hit, 1.0, g_ok)
        cur = jnp.where(hit, neg, cur)
    ok_rows = [jnp.broadcast_to(g_ok[g:g + 1, :], (per, tn)) for g in range(N_GROUPS)]
    expert_ok = jnp.concatenate(ok_rows, axis=0)
    masked = jnp.where(expert_ok > 0.0, biased, neg)
    ei = lax.broadcasted_iota(I32, s.shape, 0)
    pick_s, hits = [], []
    sel = jnp.zeros(s.shape, F32)
    for _ in range(TOP_K):
        m = jnp.max(masked, axis=0, keepdims=True)
        pick = jnp.min(jnp.where(masked == m, ei, n_exp), axis=0, keepdims=True)
        hit = ei == pick
        pick_s.append(jnp.sum(jnp.where(hit, s, 0.0), axis=0, keepdims=True))
        hits.append(hit)
        sel = jnp.where(hit, 1.0, sel)
        masked = jnp.where(hit, neg, masked)
    tot = pick_s[0]
    for k in range(1, TOP_K):
        tot = tot + pick_s[k]
    w_ref[...] = jnp.concatenate([p / tot * ROUTED_SCALE for p in pick_s], axis=0)

    incl = _dot(sel.astype(BF16), tri_ref[...])
    count = incl[:, tn - 1:tn]
    run = jnp.floor((count + (UNIT - 1)) * (1.0 / UNIT)) * UNIT
    run_start = _dot(ltri_ref[...], jnp.broadcast_to(run, (n_exp, 128)).astype(BF16))[:, 0:1]
    row_in_tile = incl - sel + run_start
    pos = [jnp.sum(jnp.where(hit, row_in_tile, 0.0), axis=0, keepdims=True) for hit in hits]
    pos_ref[...] = jnp.concatenate(pos, axis=0).astype(I32)

    region_used = carry[:, 0:1]
    u = lax.broadcasted_iota(I32, (n_exp, UNIT_LANES), 1).astype(F32)
    eu = lax.broadcasted_iota(I32, (n_exp, UNIT_LANES), 0).astype(F32)
    u0 = run_start * (1.0 / UNIT)
    nu = run * (1.0 / UNIT)
    inside = jnp.logical_and(u >= u0, u < u0 + nu)
    rel = jnp.sum(jnp.where(inside, u - u0 + region_used * (1.0 / UNIT), 0.0), axis=0, keepdims=True)
    exp_of = jnp.sum(jnp.where(inside, eu, 0.0), axis=0, keepdims=True)
    n_units = jnp.broadcast_to(jnp.sum(nu, axis=0, keepdims=True), (1, UNIT_LANES))
    pad = jnp.zeros((5, UNIT_LANES), F32)
    unit_ref[...] = jnp.concatenate([rel, exp_of, n_units, pad], axis=0).astype(I32)

    total = region_used + run
    carry[...] = jnp.broadcast_to(total, carry.shape)
    tot_ref[...] = jnp.broadcast_to(total, tot_ref.shape)


def _route(scores_t, bias_col, tri, ltri):
    n_exp, t = scores_t.shape
    nt = t // TD
    return pl.pallas_call(
        _route_kernel,
        grid=(nt,),
        in_specs=[pl.BlockSpec((n_exp, TD), lambda i: (0, i)),
                  pl.BlockSpec((n_exp, 1), lambda i: (0, 0)),
                  pl.BlockSpec((TD, TD), lambda i: (0, 0)),
                  pl.BlockSpec((n_exp, n_exp), lambda i: (0, 0))],
        out_specs=[pl.BlockSpec((TOP_K, TD), lambda i: (0, i)),
                   pl.BlockSpec((TOP_K, TD), lambda i: (0, i)),
                   pl.BlockSpec((8, UNIT_LANES), lambda i: (0, i)),
                   pl.BlockSpec((n_exp, 128), lambda i: (0, 0))],
        out_shape=[jax.ShapeDtypeStruct((TOP_K, t), I32),
                   jax.ShapeDtypeStruct((TOP_K, t), F32),
                   jax.ShapeDtypeStruct((8, nt * UNIT_LANES), I32),
                   jax.ShapeDtypeStruct((n_exp, 128), F32)],
        scratch_shapes=[pltpu.VMEM((n_exp, 128), F32)],
        compiler_params=_cparams(("arbitrary",)),
        name="route_topk",
    )(scores_t, bias_col, tri, ltri)


def _unit_dest_kernel(start_ref, unit_ref, o_ref, *, n_exp):
    tbl = unit_ref[...]
    exp_of = tbl[1:2, :]
    dst = tbl[0:1, :]
    for e in range(n_exp):
        dst = dst + jnp.where(exp_of == e, start_ref[e], 0)
    o_ref[...] = jnp.concatenate([dst, tbl[2:3, :], jnp.zeros((6, tbl.shape[1]), I32)], axis=0)


def _unit_dest(region_start_units, units):
    blk = lambda: pl.BlockSpec((8, UNIT_LANES), lambda i, s: (0, i))
    return pl.pallas_call(
        functools.partial(_unit_dest_kernel, n_exp=region_start_units.shape[0]),
        grid_spec=pltpu.PrefetchScalarGridSpec(
            num_scalar_prefetch=1, grid=(units.shape[1] // UNIT_LANES,), in_specs=[blk()], out_specs=blk()),
        out_shape=jax.ShapeDtypeStruct(units.shape, I32),
        compiler_params=_cparams(("parallel",)),
        name="route_units",
    )(region_start_units, units)


def _unit_rows(i):
    return pl.ds(pl.multiple_of(i * UNIT, UNIT), UNIT)


def _for_each(count, fn, group=8):
    main = lax.shift_right_logical(count, int(math.log2(group)))

    def many(i, c):
        for r in range(group):
            fn(i * group + r)
        return c

    lax.fori_loop(0, main, many, 0)
    lax.fori_loop(main * group, count, lambda i, c: (fn(i), c)[1], 0)


def _dispatch_kernel(tail_lo_ref, tail_hi_ref, nu_ref, unit_ref, pos_ref, h_ref, xs_ref, s_scr, zero_scr,
                     sem, zsem, *, n_exp):
    rows = s_scr.shape[1]
    td = h_ref.shape[0]
    step = pl.program_id(0)
    last = pl.num_programs(0) - 1
    slot = step % 2

    def zero_copy(g):
        return pltpu.make_async_copy(zero_scr, xs_ref.at[_unit_rows(g), :], zsem.at[0])

    def unit_copy(u, sl):
        return pltpu.make_async_copy(s_scr.at[sl, _unit_rows(u), :], xs_ref.at[_unit_rows(unit_ref[0, u]), :],
                                     sem.at[sl])

    def drain(count, sl):
        _for_each(count, lambda u: unit_copy(0, sl).wait())

    @pl.when(step >= 2)
    def _():
        drain(nu_ref[step - 2], slot)

    @pl.when(step == 0)
    def _():
        zero_scr[...] = jnp.zeros_like(zero_scr)

        def per_expert(e, c):
            lax.fori_loop(tail_lo_ref[e], tail_hi_ref[e], lambda g, cc: (zero_copy(g).start(), cc)[1], 0)
            lax.fori_loop(tail_lo_ref[e], tail_hi_ref[e], lambda g, cc: (zero_copy(g).wait(), cc)[1], 0)
            return c

        lax.fori_loop(0, n_exp, per_expert, 0)

    pos = pos_ref[...]
    h = h_ref[...]
    for r0 in range(0, rows, SORT_CHUNK):
        srow = lax.broadcasted_iota(I32, (SORT_CHUNK, td), 0) + r0
        onehot = jnp.zeros((SORT_CHUNK, td), F32)
        for k in range(pos.shape[0]):
            onehot = jnp.where(srow == pos[k:k + 1, :], 1.0, onehot)
        s_scr[slot, r0:r0 + SORT_CHUNK, :] = _dot(onehot.astype(BF16), h).astype(BF16)

    _for_each(nu_ref[step], lambda u: unit_copy(u, slot).start())

    @pl.when(step == last)
    def _():
        drain(nu_ref[step], slot)

        @pl.when(step >= 1)
        def _():
            drain(nu_ref[step - 1], 1 - slot)


def _dispatch(tail_lo, tail_hi, n_units, unit_dst, pos, h2, n_slots):
    t, d = h2.shape
    n_exp = tail_lo.shape[0]
    return pl.pallas_call(
        functools.partial(_dispatch_kernel, n_exp=n_exp),
        grid_spec=pltpu.PrefetchScalarGridSpec(
            num_scalar_prefetch=3, grid=(t // TD,),
            in_specs=[pl.BlockSpec((8, UNIT_LANES), lambda i, a, b, c: (0, i), memory_space=pltpu.SMEM),
                      pl.BlockSpec((TOP_K, TD), lambda i, a, b, c: (0, i)),
                      pl.BlockSpec((TD, d), lambda i, a, b, c: (i, 0))],
            out_specs=pl.BlockSpec(memory_space=pl.ANY),
            scratch_shapes=[pltpu.VMEM((2, _tile_rows(n_exp), d), BF16), pltpu.VMEM((UNIT, d), BF16),
                            pltpu.SemaphoreType.DMA((2,)), pltpu.SemaphoreType.DMA((1,))]),
        out_shape=jax.ShapeDtypeStruct((n_slots, d), BF16),
        compiler_params=_cparams(("arbitrary",)),
        name="moe_dispatch",
    )(tail_lo, tail_hi, n_units, unit_dst, pos, h2)


def _expert_kernel(blk0_ref, nblk_ref, tot_ref, wg_ref, wu_ref, wd_ref, xs_ref, ys_ref, xbuf, obuf, wgu_scr, wd_scr,
                   sem_in, sem_out):
    e = pl.program_id(0)
    n = nblk_ref[e]
    b0 = blk0_ref[e]
    de = wg_ref.shape[1]
    wgu_scr[:, :de] = wg_ref[...].astype(BF16)
    wgu_scr[:, de:] = wu_ref[...].astype(BF16)
    wd_scr[...] = wd_ref[...].astype(BF16)

    total = tot_ref[0]

    def rows(g):
        return pl.ds(pl.multiple_of(g * MOE_BLK, MOE_BLK), MOE_BLK)

    def in_copy(g):
        sl = g % EXPERT_BUFS
        return pltpu.make_async_copy(xs_ref.at[rows(g), :], xbuf.at[sl], sem_in.at[sl])

    def out_copy(g):
        sl = g % EXPERT_BUFS
        return pltpu.make_async_copy(obuf.at[sl], ys_ref.at[rows(g), :], sem_out.at[sl])

    @pl.when(e == 0)
    def _():
        for g in range(EXPERT_BUFS - 1):
            @pl.when(g < total)
            def _():
                in_copy(g).start()

    def block(j, c):
        g = b0 + j
        sl = g % EXPERT_BUFS
        in_copy(g).wait()

        @pl.when(g + (EXPERT_BUFS - 1) < total)
        def _():
            in_copy(g + (EXPERT_BUFS - 1)).start()

        @pl.when(g >= EXPERT_BUFS)
        def _():
            out_copy(g - EXPERT_BUFS).wait()

        gu = _dot(xbuf[sl], wgu_scr[...])
        act = _silu(gu[:, :de]) * gu[:, de:]
        obuf[sl] = _dot(act.astype(BF16), wd_scr[...]).astype(BF16)
        out_copy(g).start()
        return c

    lax.fori_loop(0, n, block, 0)

    @pl.when(e == pl.num_programs(0) - 1)
    def _():
        for back in range(1, EXPERT_BUFS + 1):
            @pl.when(total >= back)
            def _():
                out_copy(total - back).wait()


def _experts(blk0, nblk, n_blocks, xs, w_gate, w_up, w_down, layer):
    n_slots, d = xs.shape
    _, n_exp, _, de = w_gate.shape
    return pl.pallas_call(
        _expert_kernel,
        grid_spec=pltpu.PrefetchScalarGridSpec(
            num_scalar_prefetch=3, grid=(n_exp,),
            in_specs=[pl.BlockSpec((None, None, d, de), lambda e, a, b, c: (layer, e, 0, 0)),
                      pl.BlockSpec((None, None, d, de), lambda e, a, b, c: (layer, e, 0, 0)),
                      pl.BlockSpec((None, None, de, d), lambda e, a, b, c: (layer, e, 0, 0)),
                      pl.BlockSpec(memory_space=pl.ANY)],
            out_specs=pl.BlockSpec(memory_space=pl.ANY),
            scratch_shapes=[pltpu.VMEM((EXPERT_BUFS, MOE_BLK, d), BF16), pltpu.VMEM((EXPERT_BUFS, MOE_BLK, d), BF16),
                            pltpu.VMEM((d, 2 * de), BF16), pltpu.VMEM((de, d), BF16),
                            pltpu.SemaphoreType.DMA((EXPERT_BUFS,)), pltpu.SemaphoreType.DMA((EXPERT_BUFS,))]),
        out_shape=jax.ShapeDtypeStruct((n_slots, d), BF16),
        compiler_params=_cparams(("arbitrary",)),
        name="moe_experts",
    )(blk0, nblk, n_blocks, w_gate, w_up, w_down, xs)


def _combine_kernel(unit_ref, next_ref, pos_ref, w_ref, x_ref, h_ref, gate2_ref, sg_ref, su_ref, sd_ref, lng_ref,
                    lnb_ref, ys_ref, o_ref, s_scr, sem, *, alpha):
    rows = s_scr.shape[1]
    td = x_ref.shape[0]
    step = pl.program_id(0)
    slot = step % 2

    def unit_copy(tbl, u, sl):
        return pltpu.make_async_copy(ys_ref.at[_unit_rows(tbl[0, u]), :], s_scr.at[sl, _unit_rows(u), :], sem.at[sl])

    def fetch(tbl, sl):
        _for_each(tbl[1, 0], lambda u: unit_copy(tbl, u, sl).start())

    @pl.when(step == 0)
    def _():
        s_scr[...] = jnp.zeros_like(s_scr)
        fetch(unit_ref, 0)

    @pl.when(step + 1 < pl.num_programs(0))
    def _():
        fetch(next_ref, 1 - slot)

    h = h_ref[...]
    acc = _dot((_silu(_dot(h, sg_ref[...])) * _dot(h, su_ref[...])).astype(BF16), sd_ref[...])
    pos = pos_ref[...]
    w = w_ref[...]
    _for_each(unit_ref[1, 0], lambda u: unit_copy(unit_ref, 0, slot).wait())
    for r0 in range(0, rows, SORT_CHUNK):
        lane = lax.broadcasted_iota(I32, (td, SORT_CHUNK), 1) + r0
        wmat = jnp.zeros((td, SORT_CHUNK), F32)
        for k in range(pos.shape[1]):
            wmat = jnp.where(lane == pos[:, k:k + 1], w[:, k:k + 1], wmat)
        w_hi = wmat.astype(BF16)
        w_lo = (wmat - w_hi.astype(F32)).astype(BF16)
        blk = s_scr[slot, r0:r0 + SORT_CHUNK, :]
        acc = acc + (_dot(w_hi, blk) + _dot(w_lo, blk))
    o_ref[...] = _ln(alpha * x_ref[...] + gate2_ref[...] * acc) * lng_ref[...] + lnb_ref[...]


def _combine(unit_dst, pos_tok, w_tok, xn, h2, modp, sh_gate, sh_up, sh_down, ln_g, ln_b, ys, seg_td, *,
             alpha, n_exp):
    t, d = xn.shape
    k = pos_tok.shape[1]
    nt = t // TD
    tok = pl.BlockSpec((TD, d), lambda i: (i, 0))
    full = lambda a: pl.BlockSpec(a.shape, lambda i: (0,) * a.ndim)
    return pl.pallas_call(
        functools.partial(_combine_kernel, alpha=alpha),
        grid=(nt,),
        in_specs=[pl.BlockSpec((8, UNIT_LANES), lambda i: (0, i), memory_space=pltpu.SMEM),
                  pl.BlockSpec((8, UNIT_LANES), lambda i: (0, jnp.minimum(i + 1, nt - 1)), memory_space=pltpu.SMEM),
                  pl.BlockSpec((TD, k), lambda i: (i, 0)),
                  pl.BlockSpec((TD, k), lambda i: (i, 0)),
                  tok, tok,
                  pl.BlockSpec((None, None, 1, d), lambda i: (5, seg_td(i), 0, 0)),
                  full(sh_gate), full(sh_up), full(sh_down), full(ln_g), full(ln_b),
                  pl.BlockSpec(memory_space=pl.ANY)],
        out_specs=tok,
        out_shape=jax.ShapeDtypeStruct((t, d), F32),
        scratch_shapes=[pltpu.VMEM((2, _tile_rows(n_exp), d), BF16), pltpu.SemaphoreType.DMA((2,))],
        compiler_params=_cparams(("arbitrary",)),
        name="moe_combine",
    )(unit_dst, unit_dst, pos_tok, w_tok, xn, h2, modp, sh_gate, sh_up, sh_down, ln_g, ln_b, ys)


def _moe(xn, h2, scores_t, modp, seg_td, router_bias, w_gate, w_up, w_down, sh_gate, sh_up, sh_down,
         ln_g, ln_b, tri, ltri, *, alpha, layer):
    t = xn.shape[0]
    n_exp = scores_t.shape[0]
    assert _tile_rows(n_exp) % SORT_CHUNK == 0 and _tile_rows(n_exp) <= UNIT * UNIT_LANES
    pos, w_sel, units, tot = _route(scores_t, router_bias.reshape(n_exp, 1), tri, ltri)
    used = tot[:, 0].astype(I32)
    region = (used + MOE_BLK - 1) // MOE_BLK * MOE_BLK
    region_end = jnp.cumsum(region)
    region_start = region_end - region
    n_slots = TOP_K * t + n_exp * UNIT * (t // TD) + n_exp * MOE_BLK
    unit_dst = _unit_dest((region_start // UNIT).astype(I32), units)
    xs = _dispatch(((region_start + used) // UNIT).astype(I32), (region_end // UNIT).astype(I32),
                   unit_dst[1, ::UNIT_LANES], unit_dst, pos, h2, n_slots)
    ys = _experts((region_start // MOE_BLK).astype(I32), (region // MOE_BLK).astype(I32),
                  (region_end[-1:] // MOE_BLK).astype(I32), xs, w_gate, w_up, w_down, layer)
    return _combine(unit_dst, pos.T, w_sel.T, xn, h2, modp, sh_gate, sh_up, sh_down, ln_g, ln_b, ys, seg_td,
                    alpha=alpha, n_exp=n_exp)


def kernel(x, c, ctx, c_ctx, w_mod, b_mod, w_in, b_in, lam_q1, lam_k1, lam_q2, lam_k2, attn_norm_g, sg_ln_g, sg_ln_b, sg_w, sg_b, conv_w, conv_b, lru_wa, lru_ba, lru_wx, lru_bx, lru_lam, w_branch, w_out, ln1_g, ln1_b, w_router, router_bias, moe_w_gate, moe_w_up, moe_w_down, sh_w_gate, sh_w_up, sh_w_down, ln2_g, ln2_b):
    batch, n_lat, d = x.shape
    n_ctx = ctx.shape[1]
    depth = w_mod.shape[0]
    n_exp = w_router.shape[2]
    t_lat = batch * n_lat
    t_ctx = batch * n_ctx
    t_all = t_lat + t_ctx
    assert n_lat % TM == 0 and t_ctx % TM == 0 and n_ctx % B_CHUNK == 0 and t_lat % n_ctx == 0
    assert batch + 1 <= MOD_ROWS and TM % TD == 0
    alpha = (2 * depth) ** 0.25
    cw = conv_w.shape[2]
    tiles_per_batch = n_lat // TM

    seg = lambda i: jnp.minimum(i // tiles_per_batch, batch)
    seg_td = lambda i: jnp.minimum(i // (n_lat // TD), batch)
    tbl_idx = lambda i: jnp.where(i < t_lat // TM, i % tiles_per_batch, tiles_per_batch)
    tm_in = 2 * TM if (n_lat % (2 * TM) == 0 and t_ctx % (2 * TM) == 0) else TM
    seg_in = lambda i: jnp.minimum(i // (n_lat // tm_in), batch)

    x_all = jnp.concatenate([x.reshape(t_lat, d), ctx.reshape(t_ctx, d)], axis=0)
    c_all = jnp.zeros((MOD_ROWS, d), F32).at[:batch].set(c).at[batch].set(c_ctx)
    tables = _rope_tables(n_lat)
    tri = (jnp.arange(TD)[:, None] <= jnp.arange(TD)[None, :]).astype(BF16)
    ltri = (jnp.arange(n_exp)[None, :] < jnp.arange(n_exp)[:, None]).astype(BF16)
    row = lambda v: v.reshape(1, -1)

    def dense_blocks(w):
        nb, bi, bj = w.shape
        eye = jnp.eye(nb, dtype=w.dtype)
        return (w[:, :, None, :] * eye[:, None, :, None]).reshape(nb * bi, nb * bj)

    for l in range(depth):
        last = l == depth - 1
        lam_init = 0.8 - 0.6 * math.exp(-0.3 * l)
        mod = _mod_rows(c_all, w_mod[l].astype(BF16), row(b_mod[l]))
        modp = mod.reshape(MOD_ROWS, 6, 1, d).transpose(1, 0, 2, 3)

        proj = _in_proj(x_all, modp, w_in[l].astype(BF16), row(b_in[l]), seg_in, tm_in)
        qr, kr, vb = _rope(proj, tables, tbl_idx)

        lam_vecs = jnp.stack([lam_q1[l], lam_k1[l], lam_q2[l], lam_k2[l]])
        gain = attn_norm_g[l].reshape(A_HEADS, 1, A_V_DIM)
        attn = functools.partial(_attention, qr, kr, vb, lam_vecs, gain, lam_init=lam_init,
                                 batch=batch, n_lat=n_lat, n_ctx=n_ctx)
        o_a = attn(latent_queries=True)

        o_b = _spatial_gating(proj, row(sg_ln_g[l]), row(sg_ln_b[l]), sg_w[l].astype(BF16),
                              sg_b[l].reshape(B_GROUPS, B_CHUNK, 1))

        lru = functools.partial(_lru_pass, proj, conv_w=conv_w[l], conv_b=row(conv_b[l]), batch=batch)
        zeros_h = jnp.zeros((batch, 8, cw), F32)
        hf = {}
        for direction in range(2):
            wd = jnp.concatenate([dense_blocks(lru_wa[l, direction]), dense_blocks(lru_wx[l, direction])],
                                 axis=1).astype(BF16)
            par = dict(w_dense=wd, ba=row(lru_ba[l, direction]), bx=row(lru_bx[l, direction]),
                       lam=row(lru_lam[l, direction]), reverse=direction == 1, final=direction == 1)
            h_ctx, edge = lru(hf.get("ctx"), zeros_h, seq=n_ctx, tl=n_ctx, row0=t_lat, **par)
            h_lat, _ = lru(hf.get("lat"), edge, seq=n_lat, tl=TM, row0=0, **par)
            hf = {"ctx": h_ctx, "lat": h_lat}
        o_c_lat, o_c_ctx = hf["lat"], hf["ctx"]

        n_rows = t_lat if last else t_all
        if last:
            o_a_all, o_c = o_a, o_c_lat
        else:
            o_a_all = jnp.concatenate([o_a, attn(latent_queries=False)], axis=0)
            o_c = jnp.concatenate([o_c_lat, o_c_ctx], axis=0)
        wr_t = w_router[l].T
        wr_hi = wr_t.astype(BF16)
        wr_lo = (wr_t - wr_hi.astype(F32)).astype(BF16)
        xn, h2, scores_t = _merge(x_all, o_a_all, o_b, o_c, proj, modp, w_branch[l].astype(BF16),
                                  w_out[l].astype(BF16), row(ln1_g[l]), row(ln1_b[l]), wr_hi, wr_lo, seg,
                                  n_rows=n_rows, alpha=alpha)
        x_all = _moe(xn, h2, scores_t, modp, seg_td, router_bias[l], moe_w_gate,
                     moe_w_up, moe_w_down, sh_w_gate[l].astype(BF16),
                     sh_w_up[l].astype(BF16), sh_w_down[l].astype(BF16), row(ln2_g[l]), row(ln2_b[l]), tri, ltri,
                     alpha=alpha, layer=l)
    return x_all[:t_lat].reshape(batch, n_lat, d)
```

```python
import functools
import math

import jax
import jax.numpy as jnp
from jax import lax
from jax.experimental import pallas as pl
from jax.experimental.pallas import tpu as pltpu

F32 = jnp.float32
BF16 = jnp.bfloat16
I32 = jnp.int32

A_HEADS = 4
A_QK_DIM = 64
A_V_DIM = 2 * A_QK_DIM
GRID_W = 64
ROPE_THETA = 10000.0
B_CHUNK = 128
B_GROUPS = 4
C_BLOCKS = 8
C_POW = 8.0
TOP_K = 8
N_GROUPS = 8
TOPK_GROUPS = 4
ROUTED_SCALE = 2.5
LN_EPS = 1e-6
RMS_EPS = 1e-5

TM = 512
MOE_BLK = 512
TD = 256
UNIT = 16
UNIT_LANES = 256
SORT_CHUNK = 512
EXPERT_BUFS = 4
ATTN_SUB = 2048
MOD_ROWS = 16
VMEM_LIMIT = 56 * 1024 * 1024


def _cparams(sem):
    return pltpu.CompilerParams(dimension_semantics=sem, vmem_limit_bytes=VMEM_LIMIT)


def _ln(x):
    mu = jnp.mean(x, axis=-1, keepdims=True)
    xc = x - mu
    var = jnp.mean(xc * xc, axis=-1, keepdims=True)
    return xc * lax.rsqrt(var + LN_EPS)


def _gelu(x):
    cdf = 0.5 * (1.0 + jnp.tanh(math.sqrt(2.0 / math.pi) * (x + 0.044715 * (x * x * x))))
    return x * cdf


def _sigmoid(x):
    return 1.0 / (1.0 + jnp.exp(-x))


def _silu(x):
    return x * _sigmoid(x)


def _dot(a, b):
    return jnp.dot(a, b, preferred_element_type=F32)


def _dot_nt(a, b):
    return lax.dot_general(a, b, (((1,), (1,)), ((), ())), preferred_element_type=F32)


def _mod_kernel(c_ref, w_ref, b_ref, o_ref):
    o_ref[...] = _dot(_silu(c_ref[...]).astype(BF16), w_ref[...]) + b_ref[...]


def _mod_rows(c_all, w, b):
    m, d = c_all.shape
    n = w.shape[1]
    tn = 1536
    return pl.pallas_call(
        _mod_kernel,
        grid=(n // tn,),
        in_specs=[pl.BlockSpec((m, d), lambda j: (0, 0)),
                  pl.BlockSpec((d, tn), lambda j: (0, j)),
                  pl.BlockSpec((1, tn), lambda j: (0, j))],
        out_specs=pl.BlockSpec((m, tn), lambda j: (0, j)),
        out_shape=jax.ShapeDtypeStruct((m, n), F32),
        compiler_params=_cparams(("parallel",)),
        name="adaln_rows",
    )(c_all, w, b)


QKV_TILES = 3
COL_U, COL_S, COL_X, COL_Y, COL_G = 0, 1, 2, 3, 4


def _in_proj_kernel(x_ref, sh_ref, sc_ref, w_ref, b_ref, cos_ref, sa_ref, sb_ref,
                    q_ref, k_ref, v_ref, o_ref, h_scr):
    j = pl.program_id(1)

    @pl.when(j == 0)
    def _():
        h = _ln(x_ref[...]) * (1.0 + sc_ref[...]) + sh_ref[...]
        h_scr[...] = h.astype(BF16)

    acc = _dot(h_scr[...], w_ref[...]) + b_ref[...]

    def rope(dst_ref, scale):
        hw = cos_ref.shape[1]
        for c0 in range(0, acc.shape[1], hw):
            x = acc[:, c0:c0 + hw]
            r = x * cos_ref[...] + pltpu.roll(x, hw - 16, 1) * sa_ref[...] + pltpu.roll(x, 16, 1) * sb_ref[...]
            dst_ref[:, c0:c0 + hw] = (r * scale).astype(BF16)

    @pl.when(j == 0)
    def _():
        rope(q_ref, A_QK_DIM ** -0.5 * math.log2(math.e))

    @pl.when(j == 1)
    def _():
        rope(k_ref, 1.0)

    @pl.when(j == 2)
    def _():
        v_ref[...] = acc.astype(BF16)

    @pl.when(j >= QKV_TILES)
    def _():
        o_ref[...] = acc


def _in_proj(x_all, modp, w, b, tables, seg, tbl_idx, tm):
    t, d = x_all.shape
    n = w.shape[1]
    tn = 2 * A_HEADS * A_QK_DIM
    cos, sa, sb = tables
    tbl = pl.BlockSpec((tm, cos.shape[1]), lambda i, j: (tbl_idx(i), 0))
    qkv = pl.BlockSpec((tm, tn), lambda i, j: (i, 0))
    qkv_shape = jax.ShapeDtypeStruct((t, tn), BF16)
    return pl.pallas_call(
        _in_proj_kernel,
        grid=(t // tm, n // tn),
        in_specs=[pl.BlockSpec((tm, d), lambda i, j: (i, 0)),
                  pl.BlockSpec((None, None, 1, d), lambda i, j: (0, seg(i), 0, 0)),
                  pl.BlockSpec((None, None, 1, d), lambda i, j: (1, seg(i), 0, 0)),
                  pl.BlockSpec((d, tn), lambda i, j: (0, j)),
                  pl.BlockSpec((1, tn), lambda i, j: (0, j)),
                  tbl, tbl, tbl],
        out_specs=[qkv, qkv, qkv,
                   pl.BlockSpec((tm, tn), lambda i, j: (i, jnp.maximum(j - QKV_TILES, 0)))],
        out_shape=[qkv_shape, qkv_shape, qkv_shape,
                   jax.ShapeDtypeStruct((t, n - QKV_TILES * tn), F32)],
        scratch_shapes=[pltpu.VMEM((tm, d), BF16)],
        compiler_params=_cparams(("parallel", "arbitrary")),
        name="in_proj",
    )(x_all, modp, modp, w, b, cos, sa, sb)


def _rope_tables(n, tm):
    rows = n // GRID_W
    pos_row = jnp.repeat(jnp.arange(rows), GRID_W).astype(F32)
    pos_col = jnp.tile(jnp.arange(GRID_W), rows).astype(F32)
    quarter = A_QK_DIM // 4
    inv = ROPE_THETA ** (-jnp.arange(quarter, dtype=F32) / quarter)
    ang_r = pos_row[:, None] * inv
    ang_c = pos_col[:, None] * inv
    ang = jnp.concatenate([ang_r, ang_r, ang_c, ang_c], axis=-1)
    cos = jnp.cos(ang)
    sin = jnp.sin(ang)
    first_half = (jnp.arange(A_QK_DIM) % (2 * quarter)) < quarter
    sa = jnp.where(first_half, -sin, 0.0)
    sb = jnp.where(first_half, 0.0, sin)

    def widen(tbl, fill):
        tbl = jnp.tile(tbl, (1, 2))
        return jnp.concatenate([tbl, jnp.full((tm, tbl.shape[1]), fill, F32)], axis=0)

    return widen(cos, 1.0), widen(sa, 0.0), widen(sb, 0.0)


def _attn_kernel(*refs, lam_init, with_lat, sub):
    if with_lat:
        lam_ref, gain_ref, q_ref, kc_ref, vc_ref, kl_ref, vl_ref, o_ref = refs
    else:
        lam_ref, gain_ref, q_ref, kc_ref, vc_ref, _, o_ref = refs
    lv = lam_ref[...]
    lam = (jnp.exp(jnp.sum(lv[0:1] * lv[1:2], axis=-1, keepdims=True))
           - jnp.exp(jnp.sum(lv[2:3] * lv[3:4], axis=-1, keepdims=True)) + lam_init)
    chunks = [(kc_ref, vc_ref, 0, kc_ref.shape[0])]
    if with_lat:
        ck = min(sub, kl_ref.shape[0])
        chunks += [(kl_ref, vl_ref, r0, ck) for r0 in range(0, kl_ref.shape[0], ck)]
    probs, dens = [], []
    for c in range(2):
        cols = slice(c * A_QK_DIM, (c + 1) * A_QK_DIM)
        qc = q_ref[:, cols]
        scores = [_dot_nt(qc, k_ref[r0:r0 + n, cols]) for k_ref, _, r0, n in chunks]
        m = None
        for s in scores:
            ms = jnp.max(s, axis=-1, keepdims=True)
            m = ms if m is None else jnp.maximum(m, ms)
        p = [jnp.exp2(s - m) for s in scores]
        den = None
        for pj in p:
            ds = jnp.sum(pj, axis=-1, keepdims=True)
            den = ds if den is None else den + ds
        probs.append(p)
        dens.append(den)
    r1 = 1.0 / dens[0]
    r2 = lam / dens[1]
    o = None
    for p1, p2, (_, v_ref, r0, n) in zip(probs[0], probs[1], chunks):
        w = p1 * r1 - p2 * r2
        ps = _dot(w.astype(BF16), v_ref[r0:r0 + n, :])
        o = ps if o is None else o + ps
    o = o * lax.rsqrt(jnp.mean(o * o, axis=-1, keepdims=True) + RMS_EPS)
    o_ref[...] = (o * gain_ref[...]) * (1.0 - lam_init)


def _attention(qr, kr, vb, lam_vecs, gain, *, lam_init, batch, n_lat, n_ctx, latent_queries, out_rows, into=None):
    hw = A_V_DIM
    ctx_blk0 = batch * n_lat // n_ctx
    lam_spec = pl.BlockSpec(lam_vecs.shape, lambda b, h, i: (0, 0))
    gain_spec = pl.BlockSpec((None, 1, hw), lambda b, h, i: (h, 0, 0))
    kv_ctx = pl.BlockSpec((n_ctx, hw), lambda b, h, i: (ctx_blk0 + b, h))
    if latent_queries:
        tq = 256
        nq = n_lat // tq
        q_spec = pl.BlockSpec((tq, hw), lambda b, h, i: (b * nq + i, h))
        kv_lat = pl.BlockSpec((n_lat, hw), lambda b, h, i: (b, h))
        in_specs = [lam_spec, gain_spec, q_spec, kv_ctx, kv_ctx, kv_lat, kv_lat]
        args = (lam_vecs, gain, qr, kr, vb, kr, vb)
        out_blk0 = 0
        aliases = {}
    else:
        tq = n_ctx
        nq = 1
        q_spec = pl.BlockSpec((tq, hw), lambda b, h, i: (ctx_blk0 + b, h))
        in_specs = [lam_spec, gain_spec, q_spec, kv_ctx, kv_ctx, pl.BlockSpec(memory_space=pl.ANY)]
        args = (lam_vecs, gain, qr, kr, vb, into)
        out_blk0 = ctx_blk0
        aliases = {len(args) - 1: 0}
    return pl.pallas_call(
        functools.partial(_attn_kernel, lam_init=lam_init, with_lat=latent_queries, sub=ATTN_SUB),
        grid=(batch, A_HEADS, nq),
        in_specs=in_specs,
        out_specs=pl.BlockSpec((tq, hw), lambda b, h, i: (out_blk0 + b * nq + i, h)),
        out_shape=jax.ShapeDtypeStruct((out_rows, A_HEADS * hw), F32),
        input_output_aliases=aliases,
        compiler_params=_cparams(("parallel", "parallel", "arbitrary")),
        name="diff_attn_lat" if latent_queries else "diff_attn_ctx",
    )(*args)


def _sg_kernel(u_ref, s_ref, g_ref, b_ref, w_ref, bs_ref, o_ref):
    gw = B_CHUNK
    for ci in range(TM // B_CHUNK):
        rows = slice(ci * B_CHUNK, (ci + 1) * B_CHUNK)
        for g in range(B_GROUPS):
            cols = slice(g * gw, (g + 1) * gw)
            vn = _ln(_gelu(s_ref[rows, cols])) * g_ref[:, cols] + b_ref[:, cols]
            mixed = _dot(w_ref[g], vn.astype(BF16)) + bs_ref[g]
            o_ref[rows, cols] = _gelu(u_ref[rows, cols]) * mixed


def _spatial_gating(proj, ln_g, ln_b, w_s, b_s):
    t = proj.shape[0]
    w = B_GROUPS * B_CHUNK
    full = lambda a: pl.BlockSpec(a.shape, lambda i: (0,) * a.ndim)
    return pl.pallas_call(
        _sg_kernel,
        grid=(t // TM,),
        in_specs=[pl.BlockSpec((TM, w), lambda i: (i, COL_U)),
                  pl.BlockSpec((TM, w), lambda i: (i, COL_S)),
                  full(ln_g), full(ln_b), full(w_s), full(b_s)],
        out_specs=pl.BlockSpec((TM, w), lambda i: (i, 0)),
        out_shape=jax.ShapeDtypeStruct((t, w), F32),
        compiler_params=_cparams(("parallel",)),
        name="spatial_gating",
    )(proj, proj, ln_g, ln_b, w_s, b_s)


def _lru_kernel(*refs, reverse, final, n_tiles, tl, aliased):
    if aliased:
        refs = refs[:12] + refs[13:]
    if final:
        (x_ref, xp_ref, xn_ref, cw_ref, cb_ref, w_ref, ba_ref, bx_ref, lam_ref, h0_ref,
         hf_ref, y_ref, o_ref, hl_ref, carry) = refs
    else:
        (x_ref, xp_ref, xn_ref, cw_ref, cb_ref, w_ref, ba_ref, bx_ref, lam_ref, h0_ref,
         o_ref, hl_ref, carry) = refs
    step = pl.program_id(1)

    @pl.when(step == 0)
    def _():
        carry[...] = h0_ref[...]

    ti = (n_tiles - 1 - step) if reverse else step
    x = x_ref[...]
    cw = x.shape[-1]
    row = lax.broadcasted_iota(I32, x.shape, 0)
    has_prev = ti > 0
    has_next = ti < n_tiles - 1
    p6 = jnp.where(has_prev, xp_ref[6:7, :], 0.0)
    p7 = jnp.where(has_prev, xp_ref[7:8, :], 0.0)
    n0 = jnp.where(has_next, xn_ref[0:1, :], 0.0)
    xm1 = jnp.where(row == 0, p7, pltpu.roll(x, 1, 0))
    xm2 = jnp.where(row == 0, p6, jnp.where(row == 1, p7, pltpu.roll(x, 2, 0)))
    xp1 = jnp.where(row == tl - 1, n0, pltpu.roll(x, tl - 1, 0))
    taps = cw_ref[...]
    xs = taps[0:1] * xm2 + taps[1:2] * xm1 + taps[2:3] * x + taps[3:4] * xp1 + cb_ref[...]

    z = _dot(xs.astype(BF16), w_ref[...])
    r = _sigmoid(z[:, :cw] + ba_ref[...])
    ig = _sigmoid(z[:, cw:] + bx_ref[...])
    nl = -lam_ref[...]
    softplus = jnp.maximum(nl, 0.0) + jnp.log(1.0 + jnp.exp(-jnp.abs(nl)))
    log_a = -C_POW * r * softplus
    a = jnp.exp(log_a)
    u = jnp.sqrt(1.0 - jnp.exp(2.0 * log_a)) * (ig * xs)

    d = 1
    while d < tl:
        if reverse:
            ok = row < tl - d
            shift = tl - d
        else:
            ok = row >= d
            shift = d
        a_sh = jnp.where(ok, pltpu.roll(a, shift, 0), 1.0)
        u_sh = jnp.where(ok, pltpu.roll(u, shift, 0), 0.0)
        u = u + a * u_sh
        a = a * a_sh
        d *= 2
    h = u + a * carry[0:1, :]
    edge = h[0:1, :] if reverse else h[tl - 1:tl, :]
    carry[...] = jnp.broadcast_to(edge, carry.shape)
    hl_ref[...] = jnp.broadcast_to(edge, hl_ref.shape)
    if final:
        o_ref[...] = _gelu(y_ref[...]) * (hf_ref[...] + h)
    else:
        o_ref[...] = h


def _lru_pass(proj, hf, h0, conv_w, conv_b, w_dense, ba, bx, lam, *, batch, seq, tl, row0, reverse, final,
              out_rows=None, out_row0=0, into=None):
    t_all = proj.shape[0]
    out_rows = batch * seq if out_rows is None else out_rows
    out_blk0 = out_row0 // tl
    assert into is None or final
    cw = conv_w.shape[1]
    n_tiles = seq // tl
    base = row0 // tl
    sub = tl // 8
    last8 = t_all // 8 - 1

    def tile(b, s):
        ti = (n_tiles - 1 - s) if reverse else s
        return base + b * n_tiles + ti

    def loc(b, s):
        ti = (n_tiles - 1 - s) if reverse else s
        return b * n_tiles + ti

    full = lambda a: pl.BlockSpec(a.shape, lambda b, s: (0,) * a.ndim)
    in_specs = [pl.BlockSpec((tl, cw), lambda b, s: (tile(b, s), COL_X)),
                pl.BlockSpec((8, cw), lambda b, s: (jnp.maximum(tile(b, s) * sub - 1, 0), COL_X)),
                pl.BlockSpec((8, cw), lambda b, s: (jnp.minimum((tile(b, s) + 1) * sub, last8), COL_X)),
                full(conv_w), full(conv_b), full(w_dense), full(ba), full(bx), full(lam),
                pl.BlockSpec((None, 8, cw), lambda b, s: (b, 0, 0))]
    args = [proj, proj, proj, conv_w, conv_b, w_dense, ba, bx, lam, h0]
    if final:
        in_specs += [pl.BlockSpec((tl, cw), lambda b, s: (loc(b, s), 0)),
                     pl.BlockSpec((tl, cw), lambda b, s: (tile(b, s), COL_Y))]
        args += [hf, proj]
    aliases = {}
    if into is not None:
        aliases = {len(args): 0}
        in_specs += [pl.BlockSpec(memory_space=pl.ANY)]
        args += [into]
    return pl.pallas_call(
        functools.partial(_lru_kernel, reverse=reverse, final=final, n_tiles=n_tiles, tl=tl,
                          aliased=into is not None),
        grid=(batch, n_tiles),
        in_specs=in_specs,
        out_specs=[pl.BlockSpec((tl, cw), lambda b, s: (out_blk0 + loc(b, s), 0)),
                   pl.BlockSpec((None, 8, cw), lambda b, s: (b, 0, 0))],
        out_shape=[jax.ShapeDtypeStruct((out_rows, cw), F32),
                   jax.ShapeDtypeStruct((batch, 8, cw), F32)],
        scratch_shapes=[pltpu.VMEM((8, cw), F32)],
        input_output_aliases=aliases,
        compiler_params=_cparams(("parallel", "arbitrary")),
        name="rglru_%s_%s" % ("bwd" if reverse else "fwd", "lat" if row0 == 0 else "ctx"),
    )(*args)


def _merge_kernel(x_ref, oa_ref, ob_ref, oc_ref, g0, g1, g2, g3, g4, g5, wb_ref, wo_ref,
                  gate1_ref, sh2_ref, sc2_ref, lng_ref, lnb_ref, wrh_ref, wrl_ref,
                  xn_ref, h2_ref, sc_ref, *, alpha):
    gates = ((g0, g1), (g2, g3), (g4, g5))
    mix = None
    for r, o_ref in enumerate((oa_ref, ob_ref, oc_ref)):
        proj = _dot(o_ref[...].astype(BF16), wb_ref[r])
        gate = jnp.concatenate([_sigmoid(gates[r][0][...]), _sigmoid(gates[r][1][...])], axis=-1)
        term = gate * proj
        mix = term if mix is None else mix + term
    out = _dot(mix.astype(BF16), wo_ref[...])
    xn = _ln(alpha * x_ref[...] + gate1_ref[...] * out) * lng_ref[...] + lnb_ref[...]
    xn_ref[...] = xn
    h2 = _ln(xn) * (1.0 + sc2_ref[...]) + sh2_ref[...]
    h_hi = h2.astype(BF16)
    h2_ref[...] = h_hi
    h_lo = (h2 - h_hi.astype(F32)).astype(BF16)
    logits = _dot_nt(wrh_ref[...], h_hi) + (_dot_nt(wrh_ref[...], h_lo) + _dot_nt(wrl_ref[...], h_hi))
    sc_ref[...] = _sigmoid(logits)


def _merge(x_all, o_a, o_b, o_c, proj, modp, w_branch, w_out, ln_g, ln_b, wr_hi, wr_lo, seg, *, n_rows, alpha):
    d = x_all.shape[1]
    bw = o_a.shape[1]
    n_exp = wr_hi.shape[0]
    tok = lambda w: pl.BlockSpec((TM, w), lambda i: (i, 0))
    gate = lambda c: pl.BlockSpec((TM, bw), lambda i, c=c: (i, c))
    mod = lambda k: pl.BlockSpec((None, None, 1, d), lambda i, k=k: (k, seg(i), 0, 0))
    full = lambda a: pl.BlockSpec(a.shape, lambda i: (0,) * a.ndim)
    return pl.pallas_call(
        functools.partial(_merge_kernel, alpha=alpha),
        grid=(n_rows // TM,),
        in_specs=[tok(d), tok(bw), tok(bw), tok(bw)] + [gate(COL_G + c) for c in range(6)]
                 + [full(w_branch), full(w_out), mod(2), mod(3), mod(4), full(ln_g), full(ln_b),
                    full(wr_hi), full(wr_lo)],
        out_specs=[tok(d), tok(d), pl.BlockSpec((n_exp, TM), lambda i: (0, i))],
        out_shape=[jax.ShapeDtypeStruct((n_rows, d), F32),
                   jax.ShapeDtypeStruct((n_rows, d), BF16),
                   jax.ShapeDtypeStruct((n_exp, n_rows), F32)],
        compiler_params=_cparams(("parallel",)),
        name="merge_residual_router",
    )(x_all, o_a, o_b, o_c, proj, proj, proj, proj, proj, proj, w_branch, w_out,
      modp, modp, modp, ln_g, ln_b, wr_hi, wr_lo)


def _tile_rows(n_exp):
    return TOP_K * TD + n_exp * UNIT


def _route_kernel(s_ref, bias_ref, tri_ref, ltri_ref, pos_ref, w_ref, unit_ref, tot_ref, carry):
    @pl.when(pl.program_id(0) == 0)
    def _():
        carry[...] = jnp.zeros_like(carry)

    s = s_ref[...]
    n_exp, tn = s.shape
    per = n_exp // N_GROUPS
    neg = -jnp.inf
    biased = s + bias_ref[...]
    sub = lax.broadcasted_iota(I32, (per, tn), 0)
    gs_rows = []
    for g in range(N_GROUPS):
        blk = biased[g * per:(g + 1) * per, :]
        m1 = jnp.max(blk, axis=0, keepdims=True)
        first = jnp.min(jnp.where(blk == m1, sub, per), axis=0, keepdims=True)
        m2 = jnp.max(jnp.where(sub == first, neg, blk), axis=0, keepdims=True)
        gs_rows.append(m1 + m2)
    gs = jnp.concatenate(gs_rows, axis=0)
    gi = lax.broadcasted_iota(I32, gs.shape, 0)
    g_ok = jnp.zeros(gs.shape, F32)
    cur = gs
    for _ in range(TOPK_GROUPS):
        m = jnp.max(cur, axis=0, keepdims=True)
        pick = jnp.min(jnp.where(cur == m, gi, N_GROUPS), axis=0, keepdims=True)
        hit = gi == pick
        g_ok = jnp.where(hit, 1.0, g_ok)
        cur = jnp.where(hit, neg, cur)
    ok_rows = [jnp.broadcast_to(g_ok[g:g + 1, :], (per, tn)) for g in range(N_GROUPS)]
    expert_ok = jnp.concatenate(ok_rows, axis=0)
    masked = jnp.where(expert_ok > 0.0, biased, neg)
    ei = lax.broadcasted_iota(I32, s.shape, 0)
    pick_s, hits = [], []
    sel = jnp.zeros(s.shape, F32)
    for _ in range(TOP_K):
        m = jnp.max(masked, axis=0, keepdims=True)
        pick = jnp.min(jnp.where(masked == m, ei, n_exp), axis=0, keepdims=True)
        hit = ei == pick
        pick_s.append(jnp.sum(jnp.where(hit, s, 0.0), axis=0, keepdims=True))
        hits.append(hit)
        sel = jnp.where(hit, 1.0, sel)
        masked = jnp.where(hit, neg, masked)
    tot = pick_s[0]
    for k in range(1, TOP_K):
        tot = tot + pick_s[k]
    w_ref[...] = jnp.concatenate([p / tot * ROUTED_SCALE for p in pick_s], axis=0)

    incl = _dot(sel.astype(BF16), tri_ref[...])
    count = incl[:, tn - 1:tn]
    run = jnp.floor((count + (UNIT - 1)) * (1.0 / UNIT)) * UNIT
    run_start = _dot(ltri_ref[...], jnp.broadcast_to(run, (n_exp, 128)).astype(BF16))[:, 0:1]
    row_in_tile = incl - sel + run_start
    pos = [jnp.sum(jnp.where(hit, row_in_tile, 0.0), axis=0, keepdims=True) for hit in hits]
    pos_ref[...] = jnp.concatenate(pos, axis=0).astype(I32)

    region_used = carry[:, 0:1]
    u = lax.broadcasted_iota(I32, (n_exp, UNIT_LANES), 1).astype(F32)
    eu = lax.broadcasted_iota(I32, (n_exp, UNIT_LANES), 0).astype(F32)
    u0 = run_start * (1.0 / UNIT)
    nu = run * (1.0 / UNIT)
    inside = jnp.logical_and(u >= u0, u < u0 + nu)
    rel = jnp.sum(jnp.where(inside, u - u0 + region_used * (1.0 / UNIT), 0.0), axis=0, keepdims=True)
    exp_of = jnp.sum(jnp.where(inside, eu, 0.0), axis=0, keepdims=True)
    n_units = jnp.broadcast_to(jnp.sum(nu, axis=0, keepdims=True), (1, UNIT_LANES))
    pad = jnp.zeros((5, UNIT_LANES), F32)
    unit_ref[...] = jnp.concatenate([rel, exp_of, n_units, pad], axis=0).astype(I32)

    total = region_used + run
    carry[...] = jnp.broadcast_to(total, carry.shape)
    tot_ref[...] = jnp.broadcast_to(total, tot_ref.shape)


def _route(scores_t, bias_col, tri, ltri):
    n_exp, t = scores_t.shape
    nt = t // TD
    return pl.pallas_call(
        _route_kernel,
        grid=(nt,),
        in_specs=[pl.BlockSpec((n_exp, TD), lambda i: (0, i)),
                  pl.BlockSpec((n_exp, 1), lambda i: (0, 0)),
                  pl.BlockSpec((TD, TD), lambda i: (0, 0)),
                  pl.BlockSpec((n_exp, n_exp), lambda i: (0, 0))],
        out_specs=[pl.BlockSpec((TOP_K, TD), lambda i: (0, i)),
                   pl.BlockSpec((TOP_K, TD), lambda i: (0, i)),
                   pl.BlockSpec((8, UNIT_LANES), lambda i: (0, i)),
                   pl.BlockSpec((n_exp, 128), lambda i: (0, 0))],
        out_shape=[jax.ShapeDtypeStruct((TOP_K, t), I32),
                   jax.ShapeDtypeStruct((TOP_K, t), F32),
                   jax.ShapeDtypeStruct((8, nt * UNIT_LANES), I32),
                   jax.ShapeDtypeStruct((n_exp, 128), F32)],
        scratch_shapes=[pltpu.VMEM((n_exp, 128), F32)],
        compiler_params=_cparams(("arbitrary",)),
        name="route_topk",
    )(scores_t, bias_col, tri, ltri)


def _unit_dest_kernel(start_ref, unit_ref, o_ref, *, n_exp):
    tbl = unit_ref[...]
    exp_of = tbl[1:2, :]
    dst = tbl[0:1, :]
    for e in range(n_exp):
        dst = dst + jnp.where(exp_of == e, start_ref[e], 0)
    o_ref[...] = jnp.concatenate([dst, tbl[2:3, :], jnp.zeros((6, tbl.shape[1]), I32)], axis=0)


def _unit_dest(region_start_units, units):
    blk = lambda: pl.BlockSpec((8, UNIT_LANES), lambda i, s: (0, i))
    return pl.pallas_call(
        functools.partial(_unit_dest_kernel, n_exp=region_start_units.shape[0]),
        grid_spec=pltpu.PrefetchScalarGridSpec(
            num_scalar_prefetch=1, grid=(units.shape[1] // UNIT_LANES,), in_specs=[blk()], out_specs=blk()),
        out_shape=jax.ShapeDtypeStruct(units.shape, I32),
        compiler_params=_cparams(("parallel",)),
        name="route_units",
    )(region_start_units, units)


def _unit_rows(i):
    return pl.ds(pl.multiple_of(i * UNIT, UNIT), UNIT)


def _for_each(count, fn, group=8):
    main = lax.shift_right_logical(count, int(math.log2(group)))

    def many(i, c):
        for r in range(group):
            fn(i * group + r)
        return c

    lax.fori_loop(0, main, many, 0)
    lax.fori_loop(main * group, count, lambda i, c: (fn(i), c)[1], 0)


def _dispatch_kernel(tail_lo_ref, tail_hi_ref, nu_ref, unit_ref, pos_ref, h_ref, xs_ref, s_scr, zero_scr,
                     sem, zsem, *, n_exp):
    rows = s_scr.shape[1]
    td = h_ref.shape[0]
    step = pl.program_id(0)
    last = pl.num_programs(0) - 1
    slot = step % 2

    def zero_copy(g):
        return pltpu.make_async_copy(zero_scr, xs_ref.at[_unit_rows(g), :], zsem.at[0])

    def unit_copy(u, sl):
        return pltpu.make_async_copy(s_scr.at[sl, _unit_rows(u), :], xs_ref.at[_unit_rows(unit_ref[0, u]), :],
                                     sem.at[sl])

    def drain(count, sl):
        _for_each(count, lambda u: unit_copy(0, sl).wait())

    @pl.when(step >= 2)
    def _():
        drain(nu_ref[step - 2], slot)

    @pl.when(step == 0)
    def _():
        zero_scr[...] = jnp.zeros_like(zero_scr)

        def per_expert(e, c):
            lax.fori_loop(tail_lo_ref[e], tail_hi_ref[e], lambda g, cc: (zero_copy(g).start(), cc)[1], 0)
            lax.fori_loop(tail_lo_ref[e], tail_hi_ref[e], lambda g, cc: (zero_copy(g).wait(), cc)[1], 0)
            return c

        lax.fori_loop(0, n_exp, per_expert, 0)

    pos = pos_ref[...]
    h = h_ref[...]
    for r0 in range(0, rows, SORT_CHUNK):
        srow = lax.broadcasted_iota(I32, (SORT_CHUNK, td), 0) + r0
        onehot = jnp.zeros((SORT_CHUNK, td), F32)
        for k in range(pos.shape[0]):
            onehot = jnp.where(srow == pos[k:k + 1, :], 1.0, onehot)
        s_scr[slot, r0:r0 + SORT_CHUNK, :] = _dot(onehot.astype(BF16), h).astype(BF16)

    _for_each(nu_ref[step], lambda u: unit_copy(u, slot).start())

    @pl.when(step == last)
    def _():
        drain(nu_ref[step], slot)

        @pl.when(step >= 1)
        def _():
            drain(nu_ref[step - 1], 1 - slot)


def _dispatch(tail_lo, tail_hi, n_units, unit_dst, pos, h2, n_slots):
    t, d = h2.shape
    n_exp = tail_lo.shape[0]
    return pl.pallas_call(
        functools.partial(_dispatch_kernel, n_exp=n_exp),
        grid_spec=pltpu.PrefetchScalarGridSpec(
            num_scalar_prefetch=3, grid=(t // TD,),
            in_specs=[pl.BlockSpec((8, UNIT_LANES), lambda i, a, b, c: (0, i), memory_space=pltpu.SMEM),
                      pl.BlockSpec((TOP_K, TD), lambda i, a, b, c: (0, i)),
                      pl.BlockSpec((TD, d), lambda i, a, b, c: (i, 0))],
            out_specs=pl.BlockSpec(memory_space=pl.ANY),
            scratch_shapes=[pltpu.VMEM((2, _tile_rows(n_exp), d), BF16), pltpu.VMEM((UNIT, d), BF16),
                            pltpu.SemaphoreType.DMA((2,)), pltpu.SemaphoreType.DMA((1,))]),
        out_shape=jax.ShapeDtypeStruct((n_slots, d), BF16),
        compiler_params=_cparams(("arbitrary",)),
        name="moe_dispatch",
    )(tail_lo, tail_hi, n_units, unit_dst, pos, h2)


def _expert_kernel(blk0_ref, nblk_ref, tot_ref, wg_ref, wu_ref, wd_ref, xs_ref, ys_ref, xbuf, obuf, wgu_scr, wd_scr,
                   sem_in, sem_out):
    e = pl.program_id(0)
    n = nblk_ref[e]
    b0 = blk0_ref[e]
    de = wg_ref.shape[1]
    wgu_scr[:, :de] = wg_ref[...].astype(BF16)
    wgu_scr[:, de:] = wu_ref[...].astype(BF16)
    wd_scr[...] = wd_ref[...].astype(BF16)

    total = tot_ref[0]

    def rows(g):
        return pl.ds(pl.multiple_of(g * MOE_BLK, MOE_BLK), MOE_BLK)

    def in_copy(g):
        sl = g % EXPERT_BUFS
        return pltpu.make_async_copy(xs_ref.at[rows(g), :], xbuf.at[sl], sem_in.at[sl])

    def out_copy(g):
        sl = g % EXPERT_BUFS
        return pltpu.make_async_copy(obuf.at[sl], ys_ref.at[rows(g), :], sem_out.at[sl])

    @pl.when(e == 0)
    def _():
        for g in range(EXPERT_BUFS - 1):
            @pl.when(g < total)
            def _():
                in_copy(g).start()

    def block(j, c):
        g = b0 + j
        sl = g % EXPERT_BUFS
        in_copy(g).wait()

        @pl.when(g + (EXPERT_BUFS - 1) < total)
        def _():
            in_copy(g + (EXPERT_BUFS - 1)).start()

        @pl.when(g >= EXPERT_BUFS)
        def _():
            out_copy(g - EXPERT_BUFS).wait()

        gu = _dot(xbuf[sl], wgu_scr[...])
        act = _silu(gu[:, :de]) * gu[:, de:]
        obuf[sl] = _dot(act.astype(BF16), wd_scr[...]).astype(BF16)
        out_copy(g).start()
        return c

    lax.fori_loop(0, n, block, 0)

    @pl.when(e == pl.num_programs(0) - 1)
    def _():
        for back in range(1, EXPERT_BUFS + 1):
            @pl.when(total >= back)
            def _():
                out_copy(total - back).wait()


def _experts(blk0, nblk, n_blocks, xs, w_gate, w_up, w_down, layer):
    n_slots, d = xs.shape
    _, n_exp, _, de = w_gate.shape
    return pl.pallas_call(
        _expert_kernel,
        grid_spec=pltpu.PrefetchScalarGridSpec(
            num_scalar_prefetch=3, grid=(n_exp,),
            in_specs=[pl.BlockSpec((None, None, d, de), lambda e, a, b, c: (layer, e, 0, 0)),
                      pl.BlockSpec((None, None, d, de), lambda e, a, b, c: (layer, e, 0, 0)),
                      pl.BlockSpec((None, None, de, d), lambda e, a, b, c: (layer, e, 0, 0)),
                      pl.BlockSpec(memory_space=pl.ANY)],
            out_specs=pl.BlockSpec(memory_space=pl.ANY),
            scratch_shapes=[pltpu.VMEM((EXPERT_BUFS, MOE_BLK, d), BF16), pltpu.VMEM((EXPERT_BUFS, MOE_BLK, d), BF16),
                            pltpu.VMEM((d, 2 * de), BF16), pltpu.VMEM((de, d), BF16),
                            pltpu.SemaphoreType.DMA((EXPERT_BUFS,)), pltpu.SemaphoreType.DMA((EXPERT_BUFS,))]),
        out_shape=jax.ShapeDtypeStruct((n_slots, d), BF16),
        compiler_params=_cparams(("arbitrary",)),
        name="moe_experts",
    )(blk0, nblk, n_blocks, w_gate, w_up, w_down, xs)


def _combine_kernel(unit_ref, next_ref, pos_ref, w_ref, x_ref, h_ref, gate2_ref, sg_ref, su_ref, sd_ref, lng_ref,
                    lnb_ref, ys_ref, o_ref, s_scr, sem, *, alpha):
    rows = s_scr.shape[1]
    td = x_ref.shape[0]
    step = pl.program_id(0)
    slot = step % 2

    def unit_copy(tbl, u, sl):
        return pltpu.make_async_copy(ys_ref.at[_unit_rows(tbl[0, u]), :], s_scr.at[sl, _unit_rows(u), :], sem.at[sl])

    def fetch(tbl, sl):
        _for_each(tbl[1, 0], lambda u: unit_copy(tbl, u, sl).start())

    @pl.when(step == 0)
    def _():
        s_scr[...] = jnp.zeros_like(s_scr)
        fetch(unit_ref, 0)

    @pl.when(step + 1 < pl.num_programs(0))
    def _():
        fetch(next_ref, 1 - slot)

    h = h_ref[...]
    acc = _dot((_silu(_dot(h, sg_ref[...])) * _dot(h, su_ref[...])).astype(BF16), sd_ref[...])
    pos = pos_ref[...]
    w = w_ref[...]
    _for_each(unit_ref[1, 0], lambda u: unit_copy(unit_ref, 0, slot).wait())
    for r0 in range(0, rows, SORT_CHUNK):
        lane = lax.broadcasted_iota(I32, (td, SORT_CHUNK), 1) + r0
        wmat = jnp.zeros((td, SORT_CHUNK), F32)
        for k in range(pos.shape[1]):
            wmat = jnp.where(lane == pos[:, k:k + 1], w[:, k:k + 1], wmat)
        w_hi = wmat.astype(BF16)
        w_lo = (wmat - w_hi.astype(F32)).astype(BF16)
        blk = s_scr[slot, r0:r0 + SORT_CHUNK, :]
        acc = acc + (_dot(w_hi, blk) + _dot(w_lo, blk))
    o_ref[...] = _ln(alpha * x_ref[...] + gate2_ref[...] * acc) * lng_ref[...] + lnb_ref[...]


def _combine(unit_dst, pos_tok, w_tok, xn, h2, modp, sh_gate, sh_up, sh_down, ln_g, ln_b, ys, seg_td, *,
             alpha, n_exp):
    t, d = xn.shape
    k = pos_tok.shape[1]
    nt = t // TD
    tok = pl.BlockSpec((TD, d), lambda i: (i, 0))
    full = lambda a: pl.BlockSpec(a.shape, lambda i: (0,) * a.ndim)
    return pl.pallas_call(
        functools.partial(_combine_kernel, alpha=alpha),
        grid=(nt,),
        in_specs=[pl.BlockSpec((8, UNIT_LANES), lambda i: (0, i), memory_space=pltpu.SMEM),
                  pl.BlockSpec((8, UNIT_LANES), lambda i: (0, jnp.minimum(i + 1, nt - 1)), memory_space=pltpu.SMEM),
                  pl.BlockSpec((TD, k), lambda i: (i, 0)),
                  pl.BlockSpec((TD, k), lambda i: (i, 0)),
                  tok, tok,
                  pl.BlockSpec((None, None, 1, d), lambda i: (5, seg_td(i), 0, 0)),
                  full(sh_gate), full(sh_up), full(sh_down), full(ln_g), full(ln_b),
                  pl.BlockSpec(memory_space=pl.ANY)],
        out_specs=tok,
        out_shape=jax.ShapeDtypeStruct((t, d), F32),
        scratch_shapes=[pltpu.VMEM((2, _tile_rows(n_exp), d), BF16), pltpu.SemaphoreType.DMA((2,))],
        compiler_params=_cparams(("arbitrary",)),
        name="moe_combine",
    )(unit_dst, unit_dst, pos_tok, w_tok, xn, h2, modp, sh_gate, sh_up, sh_down, ln_g, ln_b, ys)


def _moe(xn, h2, scores_t, modp, seg_td, router_bias, w_gate, w_up, w_down, sh_gate, sh_up, sh_down,
         ln_g, ln_b, tri, ltri, *, alpha, layer):
    t = xn.shape[0]
    n_exp = scores_t.shape[0]
    assert _tile_rows(n_exp) % SORT_CHUNK == 0 and _tile_rows(n_exp) <= UNIT * UNIT_LANES
    pos, w_sel, units, tot = _route(scores_t, router_bias.reshape(n_exp, 1), tri, ltri)
    used = tot[:, 0].astype(I32)
    region = (used + MOE_BLK - 1) // MOE_BLK * MOE_BLK
    region_end = jnp.cumsum(region)
    region_start = region_end - region
    n_slots = TOP_K * t + n_exp * UNIT * (t // TD) + n_exp * MOE_BLK
    unit_dst = _unit_dest((region_start // UNIT).astype(I32), units)
    xs = _dispatch(((region_start + used) // UNIT).astype(I32), (region_end // UNIT).astype(I32),
                   unit_dst[1, ::UNIT_LANES], unit_dst, pos, h2, n_slots)
    ys = _experts((region_start // MOE_BLK).astype(I32), (region // MOE_BLK).astype(I32),
                  (region_end[-1:] // MOE_BLK).astype(I32), xs, w_gate, w_up, w_down, layer)
    return _combine(unit_dst, pos.T, w_sel.T, xn, h2, modp, sh_gate, sh_up, sh_down, ln_g, ln_b, ys, seg_td,
                    alpha=alpha, n_exp=n_exp)


def kernel(x, c, ctx, c_ctx, w_mod, b_mod, w_in, b_in, lam_q1, lam_k1, lam_q2, lam_k2, attn_norm_g, sg_ln_g, sg_ln_b, sg_w, sg_b, conv_w, conv_b, lru_wa, lru_ba, lru_wx, lru_bx, lru_lam, w_branch, w_out, ln1_g, ln1_b, w_router, router_bias, moe_w_gate, moe_w_up, moe_w_down, sh_w_gate, sh_w_up, sh_w_down, ln2_g, ln2_b):
    batch, n_lat, d = x.shape
    n_ctx = ctx.shape[1]
    depth = w_mod.shape[0]
    n_exp = w_router.shape[2]
    t_lat = batch * n_lat
    t_ctx = batch * n_ctx
    t_all = t_lat + t_ctx
    assert n_lat % TM == 0 and t_ctx % TM == 0 and n_ctx % B_CHUNK == 0 and t_lat % n_ctx == 0
    assert batch + 1 <= MOD_ROWS and TM % TD == 0
    alpha = (2 * depth) ** 0.25
    cw = conv_w.shape[2]
    tiles_per_batch = n_lat // TM

    seg = lambda i: jnp.minimum(i // tiles_per_batch, batch)
    seg_td = lambda i: jnp.minimum(i // (n_lat // TD), batch)
    tm_in = 2 * TM if (n_lat % (2 * TM) == 0 and t_ctx % (2 * TM) == 0) else TM
    seg_in = lambda i: jnp.minimum(i // (n_lat // tm_in), batch)
    tbl_idx = lambda i: jnp.where(i < t_lat // tm_in, i % (n_lat // tm_in), n_lat // tm_in)

    x_all = jnp.concatenate([x.reshape(t_lat, d), ctx.reshape(t_ctx, d)], axis=0)
    c_all = jnp.zeros((MOD_ROWS, d), F32).at[:batch].set(c).at[batch].set(c_ctx)
    tables = _rope_tables(n_lat, tm_in)
    tri = (jnp.arange(TD)[:, None] <= jnp.arange(TD)[None, :]).astype(BF16)
    ltri = (jnp.arange(n_exp)[None, :] < jnp.arange(n_exp)[:, None]).astype(BF16)
    row = lambda v: v.reshape(1, -1)

    def dense_blocks(w):
        nb, bi, bj = w.shape
        eye = jnp.eye(nb, dtype=w.dtype)
        return (w[:, :, None, :] * eye[:, None, :, None]).reshape(nb * bi, nb * bj)

    for l in range(depth):
        last = l == depth - 1
        lam_init = 0.8 - 0.6 * math.exp(-0.3 * l)
        mod = _mod_rows(c_all, w_mod[l].astype(BF16), row(b_mod[l]))
        modp = mod.reshape(MOD_ROWS, 6, 1, d).transpose(1, 0, 2, 3)

        qr, kr, vb, proj = _in_proj(x_all, modp, w_in[l].astype(BF16), row(b_in[l]), tables, seg_in, tbl_idx, tm_in)

        lam_vecs = jnp.stack([lam_q1[l], lam_k1[l], lam_q2[l], lam_k2[l]])
        gain = attn_norm_g[l].reshape(A_HEADS, 1, A_V_DIM)
        attn = functools.partial(_attention, qr, kr, vb, lam_vecs, gain, lam_init=lam_init,
                                 batch=batch, n_lat=n_lat, n_ctx=n_ctx)
        n_rows = t_lat if last else t_all
        o_a = attn(latent_queries=True, out_rows=n_rows)
        if not last:
            o_a = attn(latent_queries=False, out_rows=n_rows, into=o_a)

        o_b = _spatial_gating(proj, row(sg_ln_g[l]), row(sg_ln_b[l]), sg_w[l].astype(BF16),
                              sg_b[l].reshape(B_GROUPS, B_CHUNK, 1))

        lru = functools.partial(_lru_pass, proj, conv_w=conv_w[l], conv_b=row(conv_b[l]), batch=batch)
        zeros_h = jnp.zeros((batch, 8, cw), F32)
        hf = {}
        for direction in range(2):
            wd = jnp.concatenate([dense_blocks(lru_wa[l, direction]), dense_blocks(lru_wx[l, direction])],
                                 axis=1).astype(BF16)
            par = dict(w_dense=wd, ba=row(lru_ba[l, direction]), bx=row(lru_bx[l, direction]),
                       lam=row(lru_lam[l, direction]), reverse=direction == 1, final=direction == 1)
            shared = direction == 1 and not last
            h_ctx, edge = lru(hf.get("ctx"), zeros_h, seq=n_ctx, tl=n_ctx, row0=t_lat,
                              out_rows=t_all if shared else None, out_row0=t_lat if shared else 0, **par)
            h_lat, _ = lru(hf.get("lat"), edge, seq=n_lat, tl=TM, row0=0,
                           out_rows=t_all if shared else None, into=h_ctx if shared else None, **par)
            hf = {"ctx": h_ctx, "lat": h_lat}
        o_c = hf["lat"]
        wr_t = w_router[l].T
        wr_hi = wr_t.astype(BF16)
        wr_lo = (wr_t - wr_hi.astype(F32)).astype(BF16)
        xn, h2, scores_t = _merge(x_all, o_a, o_b, o_c, proj, modp, w_branch[l].astype(BF16),
                                  w_out[l].astype(BF16), row(ln1_g[l]), row(ln1_b[l]), wr_hi, wr_lo, seg,
                                  n_rows=n_rows, alpha=alpha)
        x_all = _moe(xn, h2, scores_t, modp, seg_td, router_bias[l], moe_w_gate,
                     moe_w_up, moe_w_down, sh_w_gate[l].astype(BF16),
                     sh_w_up[l].astype(BF16), sh_w_down[l].astype(BF16), row(ln2_g[l]), row(ln2_b[l]), tri, ltri,
                     alpha=alpha, layer=l)
    return x_all[:t_lat].reshape(batch, n_lat, d)
```

```python
import functools
import math

import jax
import jax.numpy as jnp
from jax import lax
from jax.experimental import pallas as pl
from jax.experimental.pallas import tpu as pltpu

F32 = jnp.float32
BF16 = jnp.bfloat16
I32 = jnp.int32

A_HEADS = 4
A_QK_DIM = 64
A_V_DIM = 2 * A_QK_DIM
GRID_W = 64
ROPE_THETA = 10000.0
B_CHUNK = 128
B_GROUPS = 4
C_BLOCKS = 8
C_POW = 8.0
TOP_K = 8
N_GROUPS = 8
TOPK_GROUPS = 4
ROUTED_SCALE = 2.5
LN_EPS = 1e-6
RMS_EPS = 1e-5

TM = 512
MOE_BLK = 512
TD = 256
UNIT = 16
UNIT_LANES = 256
SORT_CHUNK = 512
EXPERT_BUFS = 4
HALO = 16
ATTN_SUB = 2048
MOD_ROWS = 16
VMEM_LIMIT = 56 * 1024 * 1024


def _cparams(sem):
    return pltpu.CompilerParams(dimension_semantics=sem, vmem_limit_bytes=VMEM_LIMIT)


def _ln(x):
    mu = jnp.mean(x, axis=-1, keepdims=True)
    xc = x - mu
    var = jnp.mean(xc * xc, axis=-1, keepdims=True)
    return xc * lax.rsqrt(var + LN_EPS)


def _gelu(x):
    cdf = 0.5 * (1.0 + jnp.tanh(math.sqrt(2.0 / math.pi) * (x + 0.044715 * (x * x * x))))
    return x * cdf


def _sigmoid(x):
    return 1.0 / (1.0 + jnp.exp(-x))


def _silu(x):
    return x * _sigmoid(x)


def _dot(a, b):
    return jnp.dot(a, b, preferred_element_type=F32)


def _dot_nt(a, b):
    return lax.dot_general(a, b, (((1,), (1,)), ((), ())), preferred_element_type=F32)


def _mod_kernel(c_ref, w_ref, b_ref, o_ref):
    o_ref[...] = _dot(_silu(c_ref[...]).astype(BF16), w_ref[...]) + b_ref[...]


def _mod_rows(c_all, w, b):
    m, d = c_all.shape
    n = w.shape[1]
    tn = 1536
    return pl.pallas_call(
        _mod_kernel,
        grid=(n // tn,),
        in_specs=[pl.BlockSpec((m, d), lambda j: (0, 0)),
                  pl.BlockSpec((d, tn), lambda j: (0, j)),
                  pl.BlockSpec((1, tn), lambda j: (0, j))],
        out_specs=pl.BlockSpec((m, tn), lambda j: (0, j)),
        out_shape=jax.ShapeDtypeStruct((m, n), F32),
        compiler_params=_cparams(("parallel",)),
        name="adaln_rows",
    )(c_all, w, b)


QKV_TILES = 3
COL_U, COL_S, COL_X, COL_Y, COL_G = 0, 1, 2, 3, 4


def _in_proj_kernel(x_ref, sh_ref, sc_ref, w_ref, b_ref, cos_ref, sa_ref, sb_ref,
                    q_ref, k_ref, v_ref, o_ref, h_scr):
    j = pl.program_id(1)

    @pl.when(j == 0)
    def _():
        h = _ln(x_ref[...]) * (1.0 + sc_ref[...]) + sh_ref[...]
        h_scr[...] = h.astype(BF16)

    acc = _dot(h_scr[...], w_ref[...]) + b_ref[...]

    def rope(dst_ref, scale):
        hw = cos_ref.shape[1]
        for c0 in range(0, acc.shape[1], hw):
            x = acc[:, c0:c0 + hw]
            r = x * cos_ref[...] + pltpu.roll(x, hw - 16, 1) * sa_ref[...] + pltpu.roll(x, 16, 1) * sb_ref[...]
            dst_ref[:, c0:c0 + hw] = (r * scale).astype(BF16)

    @pl.when(j == 0)
    def _():
        rope(q_ref, A_QK_DIM ** -0.5 * math.log2(math.e))

    @pl.when(j == 1)
    def _():
        rope(k_ref, 1.0)

    @pl.when(j == 2)
    def _():
        v_ref[...] = acc.astype(BF16)

    @pl.when(j >= QKV_TILES)
    def _():
        o_ref[...] = acc.astype(o_ref.dtype)


def _in_proj(x_all, modp, w, b, tables, seg, tbl_idx, tm):
    t, d = x_all.shape
    n = w.shape[1]
    tn = 2 * A_HEADS * A_QK_DIM
    cos, sa, sb = tables
    tbl = pl.BlockSpec((tm, cos.shape[1]), lambda i, j: (tbl_idx(i), 0))
    qkv = pl.BlockSpec((tm, tn), lambda i, j: (i, 0))
    qkv_shape = jax.ShapeDtypeStruct((t, tn), BF16)
    return pl.pallas_call(
        _in_proj_kernel,
        grid=(t // tm, n // tn),
        in_specs=[pl.BlockSpec((tm, d), lambda i, j: (i, 0)),
                  pl.BlockSpec((None, None, 1, d), lambda i, j: (0, seg(i), 0, 0)),
                  pl.BlockSpec((None, None, 1, d), lambda i, j: (1, seg(i), 0, 0)),
                  pl.BlockSpec((d, tn), lambda i, j: (0, j)),
                  pl.BlockSpec((1, tn), lambda i, j: (0, j)),
                  tbl, tbl, tbl],
        out_specs=[qkv, qkv, qkv,
                   pl.BlockSpec((tm, tn), lambda i, j: (i, jnp.maximum(j - QKV_TILES, 0)))],
        out_shape=[qkv_shape, qkv_shape, qkv_shape,
                   jax.ShapeDtypeStruct((t, n - QKV_TILES * tn), BF16)],
        scratch_shapes=[pltpu.VMEM((tm, d), BF16)],
        compiler_params=_cparams(("parallel", "arbitrary")),
        name="in_proj",
    )(x_all, modp, modp, w, b, cos, sa, sb)


def _rope_tables(n, tm):
    rows = n // GRID_W
    pos_row = jnp.repeat(jnp.arange(rows), GRID_W).astype(F32)
    pos_col = jnp.tile(jnp.arange(GRID_W), rows).astype(F32)
    quarter = A_QK_DIM // 4
    inv = ROPE_THETA ** (-jnp.arange(quarter, dtype=F32) / quarter)
    ang_r = pos_row[:, None] * inv
    ang_c = pos_col[:, None] * inv
    ang = jnp.concatenate([ang_r, ang_r, ang_c, ang_c], axis=-1)
    cos = jnp.cos(ang)
    sin = jnp.sin(ang)
    first_half = (jnp.arange(A_QK_DIM) % (2 * quarter)) < quarter
    sa = jnp.where(first_half, -sin, 0.0)
    sb = jnp.where(first_half, 0.0, sin)

    def widen(tbl, fill):
        tbl = jnp.tile(tbl, (1, 2))
        return jnp.concatenate([tbl, jnp.full((tm, tbl.shape[1]), fill, F32)], axis=0)

    return widen(cos, 1.0), widen(sa, 0.0), widen(sb, 0.0)


def _attn_kernel(*refs, lam_init, with_lat, sub):
    if with_lat:
        lam_ref, gain_ref, q_ref, kc_ref, vc_ref, kl_ref, vl_ref, o_ref = refs
    else:
        lam_ref, gain_ref, q_ref, kc_ref, vc_ref, _, o_ref = refs
    lv = lam_ref[...]
    lam = (jnp.exp(jnp.sum(lv[0:1] * lv[1:2], axis=-1, keepdims=True))
           - jnp.exp(jnp.sum(lv[2:3] * lv[3:4], axis=-1, keepdims=True)) + lam_init)
    chunks = [(kc_ref, vc_ref, 0, kc_ref.shape[0])]
    if with_lat:
        ck = min(sub, kl_ref.shape[0])
        chunks += [(kl_ref, vl_ref, r0, ck) for r0 in range(0, kl_ref.shape[0], ck)]
    probs, dens = [], []
    for c in range(2):
        cols = slice(c * A_QK_DIM, (c + 1) * A_QK_DIM)
        qc = q_ref[:, cols]
        scores = [_dot_nt(qc, k_ref[r0:r0 + n, cols]) for k_ref, _, r0, n in chunks]
        m = None
        for s in scores:
            ms = jnp.max(s, axis=-1, keepdims=True)
            m = ms if m is None else jnp.maximum(m, ms)
        p = [jnp.exp2(s - m) for s in scores]
        den = None
        for pj in p:
            ds = jnp.sum(pj, axis=-1, keepdims=True)
            den = ds if den is None else den + ds
        probs.append(p)
        dens.append(den)
    r1 = 1.0 / dens[0]
    r2 = lam / dens[1]
    o = None
    for p1, p2, (_, v_ref, r0, n) in zip(probs[0], probs[1], chunks):
        w = p1 * r1 - p2 * r2
        ps = _dot(w.astype(BF16), v_ref[r0:r0 + n, :])
        o = ps if o is None else o + ps
    o = o * lax.rsqrt(jnp.mean(o * o, axis=-1, keepdims=True) + RMS_EPS)
    o_ref[...] = (o * gain_ref[...]) * (1.0 - lam_init)


def _attention(qr, kr, vb, lam_vecs, gain, *, lam_init, batch, n_lat, n_ctx, latent_queries, out_rows, into=None):
    hw = A_V_DIM
    ctx_blk0 = batch * n_lat // n_ctx
    lam_spec = pl.BlockSpec(lam_vecs.shape, lambda b, h, i: (0, 0))
    gain_spec = pl.BlockSpec((None, 1, hw), lambda b, h, i: (h, 0, 0))
    kv_ctx = pl.BlockSpec((n_ctx, hw), lambda b, h, i: (ctx_blk0 + b, h))
    if latent_queries:
        tq = 256
        nq = n_lat // tq
        q_spec = pl.BlockSpec((tq, hw), lambda b, h, i: (b * nq + i, h))
        kv_lat = pl.BlockSpec((n_lat, hw), lambda b, h, i: (b, h))
        in_specs = [lam_spec, gain_spec, q_spec, kv_ctx, kv_ctx, kv_lat, kv_lat]
        args = (lam_vecs, gain, qr, kr, vb, kr, vb)
        out_blk0 = 0
        aliases = {}
    else:
        tq = n_ctx
        nq = 1
        q_spec = pl.BlockSpec((tq, hw), lambda b, h, i: (ctx_blk0 + b, h))
        in_specs = [lam_spec, gain_spec, q_spec, kv_ctx, kv_ctx, pl.BlockSpec(memory_space=pl.ANY)]
        args = (lam_vecs, gain, qr, kr, vb, into)
        out_blk0 = ctx_blk0
        aliases = {len(args) - 1: 0}
    return pl.pallas_call(
        functools.partial(_attn_kernel, lam_init=lam_init, with_lat=latent_queries, sub=ATTN_SUB),
        grid=(batch, A_HEADS, nq),
        in_specs=in_specs,
        out_specs=pl.BlockSpec((tq, hw), lambda b, h, i: (out_blk0 + b * nq + i, h)),
        out_shape=jax.ShapeDtypeStruct((out_rows, A_HEADS * hw), F32),
        input_output_aliases=aliases,
        compiler_params=_cparams(("parallel", "parallel", "arbitrary")),
        name="diff_attn_lat" if latent_queries else "diff_attn_ctx",
    )(*args)


def _sg_kernel(u_ref, s_ref, g_ref, b_ref, w_ref, bs_ref, o_ref):
    gw = B_CHUNK
    for ci in range(TM // B_CHUNK):
        rows = slice(ci * B_CHUNK, (ci + 1) * B_CHUNK)
        for g in range(B_GROUPS):
            cols = slice(g * gw, (g + 1) * gw)
            vn = _ln(_gelu(s_ref[rows, cols].astype(F32))) * g_ref[:, cols] + b_ref[:, cols]
            mixed = _dot(w_ref[g], vn.astype(BF16)) + bs_ref[g]
            o_ref[rows, cols] = _gelu(u_ref[rows, cols].astype(F32)) * mixed


def _spatial_gating(proj, ln_g, ln_b, w_s, b_s):
    t = proj.shape[0]
    w = B_GROUPS * B_CHUNK
    full = lambda a: pl.BlockSpec(a.shape, lambda i: (0,) * a.ndim)
    return pl.pallas_call(
        _sg_kernel,
        grid=(t // TM,),
        in_specs=[pl.BlockSpec((TM, w), lambda i: (i, COL_U)),
                  pl.BlockSpec((TM, w), lambda i: (i, COL_S)),
                  full(ln_g), full(ln_b), full(w_s), full(b_s)],
        out_specs=pl.BlockSpec((TM, w), lambda i: (i, 0)),
        out_shape=jax.ShapeDtypeStruct((t, w), F32),
        compiler_params=_cparams(("parallel",)),
        name="spatial_gating",
    )(proj, proj, ln_g, ln_b, w_s, b_s)


def _lru_kernel(*refs, reverse, final, n_tiles, tl, aliased):
    if aliased:
        refs = refs[:12] + refs[13:]
    if final:
        (x_ref, xp_ref, xn_ref, cw_ref, cb_ref, w_ref, ba_ref, bx_ref, lam_ref, h0_ref,
         hf_ref, y_ref, o_ref, hl_ref, carry) = refs
    else:
        (x_ref, xp_ref, xn_ref, cw_ref, cb_ref, w_ref, ba_ref, bx_ref, lam_ref, h0_ref,
         o_ref, hl_ref, carry) = refs
    step = pl.program_id(1)

    @pl.when(step == 0)
    def _():
        carry[...] = h0_ref[...]

    ti = (n_tiles - 1 - step) if reverse else step
    x = x_ref[...].astype(F32)
    cw = x.shape[-1]
    row = lax.broadcasted_iota(I32, x.shape, 0)
    has_prev = ti > 0
    has_next = ti < n_tiles - 1
    p6 = jnp.where(has_prev, xp_ref[HALO - 2:HALO - 1, :].astype(F32), 0.0)
    p7 = jnp.where(has_prev, xp_ref[HALO - 1:HALO, :].astype(F32), 0.0)
    n0 = jnp.where(has_next, xn_ref[0:1, :].astype(F32), 0.0)
    xm1 = jnp.where(row == 0, p7, pltpu.roll(x, 1, 0))
    xm2 = jnp.where(row == 0, p6, jnp.where(row == 1, p7, pltpu.roll(x, 2, 0)))
    xp1 = jnp.where(row == tl - 1, n0, pltpu.roll(x, tl - 1, 0))
    taps = cw_ref[...]
    xs = taps[0:1] * xm2 + taps[1:2] * xm1 + taps[2:3] * x + taps[3:4] * xp1 + cb_ref[...]

    z = _dot(xs.astype(BF16), w_ref[...])
    r = _sigmoid(z[:, :cw] + ba_ref[...])
    ig = _sigmoid(z[:, cw:] + bx_ref[...])
    nl = -lam_ref[...]
    softplus = jnp.maximum(nl, 0.0) + jnp.log(1.0 + jnp.exp(-jnp.abs(nl)))
    log_a = -C_POW * r * softplus
    a = jnp.exp(log_a)
    u = jnp.sqrt(1.0 - jnp.exp(2.0 * log_a)) * (ig * xs)

    d = 1
    while d < tl:
        if reverse:
            ok = row < tl - d
            shift = tl - d
        else:
            ok = row >= d
            shift = d
        a_sh = jnp.where(ok, pltpu.roll(a, shift, 0), 1.0)
        u_sh = jnp.where(ok, pltpu.roll(u, shift, 0), 0.0)
        u = u + a * u_sh
        a = a * a_sh
        d *= 2
    h = u + a * carry[0:1, :]
    edge = h[0:1, :] if reverse else h[tl - 1:tl, :]
    carry[...] = jnp.broadcast_to(edge, carry.shape)
    hl_ref[...] = jnp.broadcast_to(edge, hl_ref.shape)
    if final:
        o_ref[...] = _gelu(y_ref[...].astype(F32)) * (hf_ref[...] + h)
    else:
        o_ref[...] = h


def _lru_pass(proj, hf, h0, conv_w, conv_b, w_dense, ba, bx, lam, *, batch, seq, tl, row0, reverse, final,
              out_rows=None, out_row0=0, into=None):
    t_all = proj.shape[0]
    out_rows = batch * seq if out_rows is None else out_rows
    out_blk0 = out_row0 // tl
    assert into is None or final
    cw = conv_w.shape[1]
    n_tiles = seq // tl
    base = row0 // tl
    sub = tl // HALO
    last_halo = t_all // HALO - 1

    def tile(b, s):
        ti = (n_tiles - 1 - s) if reverse else s
        return base + b * n_tiles + ti

    def loc(b, s):
        ti = (n_tiles - 1 - s) if reverse else s
        return b * n_tiles + ti

    full = lambda a: pl.BlockSpec(a.shape, lambda b, s: (0,) * a.ndim)
    in_specs = [pl.BlockSpec((tl, cw), lambda b, s: (tile(b, s), COL_X)),
                pl.BlockSpec((HALO, cw), lambda b, s: (jnp.maximum(tile(b, s) * sub - 1, 0), COL_X)),
                pl.BlockSpec((HALO, cw), lambda b, s: (jnp.minimum((tile(b, s) + 1) * sub, last_halo), COL_X)),
                full(conv_w), full(conv_b), full(w_dense), full(ba), full(bx), full(lam),
                pl.BlockSpec((None, 8, cw), lambda b, s: (b, 0, 0))]
    args = [proj, proj, proj, conv_w, conv_b, w_dense, ba, bx, lam, h0]
    if final:
        in_specs += [pl.BlockSpec((tl, cw), lambda b, s: (loc(b, s), 0)),
                     pl.BlockSpec((tl, cw), lambda b, s: (tile(b, s), COL_Y))]
        args += [hf, proj]
    aliases = {}
    if into is not None:
        aliases = {len(args): 0}
        in_specs += [pl.BlockSpec(memory_space=pl.ANY)]
        args += [into]
    return pl.pallas_call(
        functools.partial(_lru_kernel, reverse=reverse, final=final, n_tiles=n_tiles, tl=tl,
                          aliased=into is not None),
        grid=(batch, n_tiles),
        in_specs=in_specs,
        out_specs=[pl.BlockSpec((tl, cw), lambda b, s: (out_blk0 + loc(b, s), 0)),
                   pl.BlockSpec((None, 8, cw), lambda b, s: (b, 0, 0))],
        out_shape=[jax.ShapeDtypeStruct((out_rows, cw), F32),
                   jax.ShapeDtypeStruct((batch, 8, cw), F32)],
        scratch_shapes=[pltpu.VMEM((8, cw), F32)],
        input_output_aliases=aliases,
        compiler_params=_cparams(("parallel", "arbitrary")),
        name="rglru_%s_%s" % ("bwd" if reverse else "fwd", "lat" if row0 == 0 else "ctx"),
    )(*args)


def _merge_kernel(x_ref, oa_ref, ob_ref, oc_ref, g0, g1, g2, g3, g4, g5, wb_ref, wo_ref,
                  gate1_ref, sh2_ref, sc2_ref, lng_ref, lnb_ref, wrh_ref, wrl_ref,
                  xn_ref, h2_ref, sc_ref, *, alpha):
    gates = ((g0, g1), (g2, g3), (g4, g5))
    mix = None
    for r, o_ref in enumerate((oa_ref, ob_ref, oc_ref)):
        proj = _dot(o_ref[...].astype(BF16), wb_ref[r])
        gate = jnp.concatenate([_sigmoid(gates[r][0][...].astype(F32)), _sigmoid(gates[r][1][...].astype(F32))],
                               axis=-1)
        term = gate * proj
        mix = term if mix is None else mix + term
    out = _dot(mix.astype(BF16), wo_ref[...])
    xn = _ln(alpha * x_ref[...] + gate1_ref[...] * out) * lng_ref[...] + lnb_ref[...]
    xn_ref[...] = xn
    h2 = _ln(xn) * (1.0 + sc2_ref[...]) + sh2_ref[...]
    h_hi = h2.astype(BF16)
    h2_ref[...] = h_hi
    h_lo = (h2 - h_hi.astype(F32)).astype(BF16)
    logits = _dot_nt(wrh_ref[...], h_hi) + (_dot_nt(wrh_ref[...], h_lo) + _dot_nt(wrl_ref[...], h_hi))
    sc_ref[...] = _sigmoid(logits)


def _merge(x_all, o_a, o_b, o_c, proj, modp, w_branch, w_out, ln_g, ln_b, wr_hi, wr_lo, seg, *, n_rows, alpha):
    d = x_all.shape[1]
    bw = o_a.shape[1]
    n_exp = wr_hi.shape[0]
    tok = lambda w: pl.BlockSpec((TM, w), lambda i: (i, 0))
    gate = lambda c: pl.BlockSpec((TM, bw), lambda i, c=c: (i, c))
    mod = lambda k: pl.BlockSpec((None, None, 1, d), lambda i, k=k: (k, seg(i), 0, 0))
    full = lambda a: pl.BlockSpec(a.shape, lambda i: (0,) * a.ndim)
    return pl.pallas_call(
        functools.partial(_merge_kernel, alpha=alpha),
        grid=(n_rows // TM,),
        in_specs=[tok(d), tok(bw), tok(bw), tok(bw)] + [gate(COL_G + c) for c in range(6)]
                 + [full(w_branch), full(w_out), mod(2), mod(3), mod(4), full(ln_g), full(ln_b),
                    full(wr_hi), full(wr_lo)],
        out_specs=[tok(d), tok(d), pl.BlockSpec((n_exp, TM), lambda i: (0, i))],
        out_shape=[jax.ShapeDtypeStruct((n_rows, d), F32),
                   jax.ShapeDtypeStruct((n_rows, d), BF16),
                   jax.ShapeDtypeStruct((n_exp, n_rows), F32)],
        compiler_params=_cparams(("parallel",)),
        name="merge_residual_router",
    )(x_all, o_a, o_b, o_c, proj, proj, proj, proj, proj, proj, w_branch, w_out,
      modp, modp, modp, ln_g, ln_b, wr_hi, wr_lo)


def _tile_rows(n_exp):
    return TOP_K * TD + n_exp * UNIT


def _route_kernel(s_ref, bias_ref, tri_ref, ltri_ref, pos_ref, w_ref, unit_ref, tot_ref, carry):
    @pl.when(pl.program_id(0) == 0)
    def _():
        carry[...] = jnp.zeros_like(carry)

    s = s_ref[...]
    n_exp, tn = s.shape
    per = n_exp // N_GROUPS
    neg = -jnp.inf
    biased = s + bias_ref[...]
    sub = lax.broadcasted_iota(I32, (per, tn), 0)
    gs_rows = []
    for g in range(N_GROUPS):
        blk = biased[g * per:(g + 1) * per, :]
        m1 = jnp.max(blk, axis=0, keepdims=True)
        first = jnp.min(jnp.where(blk == m1, sub, per), axis=0, keepdims=True)
        m2 = jnp.max(jnp.where(sub == first, neg, blk), axis=0, keepdims=True)
        gs_rows.append(m1 + m2)
    gs = jnp.concatenate(gs_rows, axis=0)
    gi = lax.broadcasted_iota(I32, gs.shape, 0)
    g_ok = jnp.zeros(gs.shape, F32)
    cur = gs
    for _ in range(TOPK_GROUPS):
        m = jnp.max(cur, axis=0, keepdims=True)
        pick = jnp.min(jnp.where(cur == m, gi, N_GROUPS), axis=0, keepdims=True)
        hit = gi == pick
        g_ok = jnp.where(hit, 1.0, g_ok)
        cur = jnp.where(hit, neg, cur)
    ok_rows = [jnp.broadcast_to(g_ok[g:g + 1, :], (per, tn)) for g in range(N_GROUPS)]
    expert_ok = jnp.concatenate(ok_rows, axis=0)
    masked = jnp.where(expert_ok > 0.0, biased, neg)
    ei = lax.broadcasted_iota(I32, s.shape, 0)
    pick_s, hits = [], []
    sel = jnp.zeros(s.shape, F32)
    for _ in range(TOP_K):
        m = jnp.max(masked, axis=0, keepdims=True)
        pick = jnp.min(jnp.where(masked == m, ei, n_exp), axis=0, keepdims=True)
        hit = ei == pick
        pick_s.append(jnp.sum(jnp.where(hit, s, 0.0), axis=0, keepdims=True))
        hits.append(hit)
        sel = jnp.where(hit, 1.0, sel)
        masked = jnp.where(hit, neg, masked)
    tot = pick_s[0]
    for k in range(1, TOP_K):
        tot = tot + pick_s[k]
    w_ref[...] = jnp.concatenate([p / tot * ROUTED_SCALE for p in pick_s], axis=0)

    incl = _dot(sel.astype(BF16), tri_ref[...])
    count = incl[:, tn - 1:tn]
    run = jnp.floor((count + (UNIT - 1)) * (1.0 / UNIT)) * UNIT
    run_start = _dot(ltri_ref[...], jnp.broadcast_to(run, (n_exp, 128)).astype(BF16))[:, 0:1]
    row_in_tile = incl - sel + run_start
    pos = [jnp.sum(jnp.where(hit, row_in_tile, 0.0), axis=0, keepdims=True) for hit in hits]
    pos_ref[...] = jnp.concatenate(pos, axis=0).astype(I32)

    region_used = carry[:, 0:1]
    u = lax.broadcasted_iota(I32, (n_exp, UNIT_LANES), 1).astype(F32)
    eu = lax.broadcasted_iota(I32, (n_exp, UNIT_LANES), 0).astype(F32)
    u0 = run_start * (1.0 / UNIT)
    nu = run * (1.0 / UNIT)
    inside = jnp.logical_and(u >= u0, u < u0 + nu)
    rel = jnp.sum(jnp.where(inside, u - u0 + region_used * (1.0 / UNIT), 0.0), axis=0, keepdims=True)
    exp_of = jnp.sum(jnp.where(inside, eu, 0.0), axis=0, keepdims=True)
    n_units = jnp.broadcast_to(jnp.sum(nu, axis=0, keepdims=True), (1, UNIT_LANES))
    pad = jnp.zeros((5, UNIT_LANES), F32)
    unit_ref[...] = jnp.concatenate([rel, exp_of, n_units, pad], axis=0).astype(I32)

    total = region_used + run
    carry[...] = jnp.broadcast_to(total, carry.shape)
    tot_ref[...] = jnp.broadcast_to(total, tot_ref.shape)


def _route(scores_t, bias_col, tri, ltri):
    n_exp, t = scores_t.shape
    nt = t // TD
    return pl.pallas_call(
        _route_kernel,
        grid=(nt,),
        in_specs=[pl.BlockSpec((n_exp, TD), lambda i: (0, i)),
                  pl.BlockSpec((n_exp, 1), lambda i: (0, 0)),
                  pl.BlockSpec((TD, TD), lambda i: (0, 0)),
                  pl.BlockSpec((n_exp, n_exp), lambda i: (0, 0))],
        out_specs=[pl.BlockSpec((TOP_K, TD), lambda i: (0, i)),
                   pl.BlockSpec((TOP_K, TD), lambda i: (0, i)),
                   pl.BlockSpec((8, UNIT_LANES), lambda i: (0, i)),
                   pl.BlockSpec((n_exp, 128), lambda i: (0, 0))],
        out_shape=[jax.ShapeDtypeStruct((TOP_K, t), I32),
                   jax.ShapeDtypeStruct((TOP_K, t), F32),
                   jax.ShapeDtypeStruct((8, nt * UNIT_LANES), I32),
                   jax.ShapeDtypeStruct((n_exp, 128), F32)],
        scratch_shapes=[pltpu.VMEM((n_exp, 128), F32)],
        compiler_params=_cparams(("arbitrary",)),
        name="route_topk",
    )(scores_t, bias_col, tri, ltri)


def _unit_dest_kernel(start_ref, unit_ref, o_ref, *, n_exp):
    tbl = unit_ref[...]
    exp_of = tbl[1:2, :]
    dst = tbl[0:1, :]
    for e in range(n_exp):
        dst = dst + jnp.where(exp_of == e, start_ref[e], 0)
    o_ref[...] = jnp.concatenate([dst, tbl[2:3, :], jnp.zeros((6, tbl.shape[1]), I32)], axis=0)


def _unit_dest(region_start_units, units):
    blk = lambda: pl.BlockSpec((8, UNIT_LANES), lambda i, s: (0, i))
    return pl.pallas_call(
        functools.partial(_unit_dest_kernel, n_exp=region_start_units.shape[0]),
        grid_spec=pltpu.PrefetchScalarGridSpec(
            num_scalar_prefetch=1, grid=(units.shape[1] // UNIT_LANES,), in_specs=[blk()], out_specs=blk()),
        out_shape=jax.ShapeDtypeStruct(units.shape, I32),
        compiler_params=_cparams(("parallel",)),
        name="route_units",
    )(region_start_units, units)


def _unit_rows(i):
    return pl.ds(pl.multiple_of(i * UNIT, UNIT), UNIT)


def _for_each(count, fn, group=8):
    main = lax.shift_right_logical(count, int(math.log2(group)))

    def many(i, c):
        for r in range(group):
            fn(i * group + r)
        return c

    lax.fori_loop(0, main, many, 0)
    lax.fori_loop(main * group, count, lambda i, c: (fn(i), c)[1], 0)


def _dispatch_kernel(tail_lo_ref, tail_hi_ref, nu_ref, unit_ref, pos_ref, h_ref, xs_ref, s_scr, zero_scr,
                     sem, zsem, *, n_exp):
    rows = s_scr.shape[1]
    td = h_ref.shape[0]
    step = pl.program_id(0)
    last = pl.num_programs(0) - 1
    slot = step % 2

    def zero_copy(g):
        return pltpu.make_async_copy(zero_scr, xs_ref.at[_unit_rows(g), :], zsem.at[0])

    def unit_copy(u, sl):
        return pltpu.make_async_copy(s_scr.at[sl, _unit_rows(u), :], xs_ref.at[_unit_rows(unit_ref[0, u]), :],
                                     sem.at[sl])

    def drain(count, sl):
        _for_each(count, lambda u: unit_copy(0, sl).wait())

    @pl.when(step >= 2)
    def _():
        drain(nu_ref[step - 2], slot)

    @pl.when(step == 0)
    def _():
        zero_scr[...] = jnp.zeros_like(zero_scr)

        def per_expert(e, c):
            lax.fori_loop(tail_lo_ref[e], tail_hi_ref[e], lambda g, cc: (zero_copy(g).start(), cc)[1], 0)
            lax.fori_loop(tail_lo_ref[e], tail_hi_ref[e], lambda g, cc: (zero_copy(g).wait(), cc)[1], 0)
            return c

        lax.fori_loop(0, n_exp, per_expert, 0)

    pos = pos_ref[...]
    h = h_ref[...]
    for r0 in range(0, rows, SORT_CHUNK):
        srow = lax.broadcasted_iota(I32, (SORT_CHUNK, td), 0) + r0
        onehot = jnp.zeros((SORT_CHUNK, td), F32)
        for k in range(pos.shape[0]):
            onehot = jnp.where(srow == pos[k:k + 1, :], 1.0, onehot)
        s_scr[slot, r0:r0 + SORT_CHUNK, :] = _dot(onehot.astype(BF16), h).astype(BF16)

    _for_each(nu_ref[step], lambda u: unit_copy(u, slot).start())

    @pl.when(step == last)
    def _():
        drain(nu_ref[step], slot)

        @pl.when(step >= 1)
        def _():
            drain(nu_ref[step - 1], 1 - slot)


def _dispatch(tail_lo, tail_hi, n_units, unit_dst, pos, h2, n_slots):
    t, d = h2.shape
    n_exp = tail_lo.shape[0]
    return pl.pallas_call(
        functools.partial(_dispatch_kernel, n_exp=n_exp),
        grid_spec=pltpu.PrefetchScalarGridSpec(
            num_scalar_prefetch=3, grid=(t // TD,),
            in_specs=[pl.BlockSpec((8, UNIT_LANES), lambda i, a, b, c: (0, i), memory_space=pltpu.SMEM),
                      pl.BlockSpec((TOP_K, TD), lambda i, a, b, c: (0, i)),
                      pl.BlockSpec((TD, d), lambda i, a, b, c: (i, 0))],
            out_specs=pl.BlockSpec(memory_space=pl.ANY),
            scratch_shapes=[pltpu.VMEM((2, _tile_rows(n_exp), d), BF16), pltpu.VMEM((UNIT, d), BF16),
                            pltpu.SemaphoreType.DMA((2,)), pltpu.SemaphoreType.DMA((1,))]),
        out_shape=jax.ShapeDtypeStruct((n_slots, d), BF16),
        compiler_params=_cparams(("arbitrary",)),
        name="moe_dispatch",
    )(tail_lo, tail_hi, n_units, unit_dst, pos, h2)


def _expert_kernel(blk0_ref, nblk_ref, tot_ref, wg_ref, wu_ref, wd_ref, xs_ref, ys_ref, xbuf, obuf, wgu_scr, wd_scr,
                   sem_in, sem_out):
    e = pl.program_id(0)
    n = nblk_ref[e]
    b0 = blk0_ref[e]
    de = wg_ref.shape[1]
    wgu_scr[:, :de] = wg_ref[...].astype(BF16)
    wgu_scr[:, de:] = wu_ref[...].astype(BF16)
    wd_scr[...] = wd_ref[...].astype(BF16)

    total = tot_ref[0]

    def rows(g):
        return pl.ds(pl.multiple_of(g * MOE_BLK, MOE_BLK), MOE_BLK)

    def in_copy(g):
        sl = g % EXPERT_BUFS
        return pltpu.make_async_copy(xs_ref.at[rows(g), :], xbuf.at[sl], sem_in.at[sl])

    def out_copy(g):
        sl = g % EXPERT_BUFS
        return pltpu.make_async_copy(obuf.at[sl], ys_ref.at[rows(g), :], sem_out.at[sl])

    @pl.when(e == 0)
    def _():
        for g in range(EXPERT_BUFS - 1):
            @pl.when(g < total)
            def _():
                in_copy(g).start()

    def block(j, c):
        g = b0 + j
        sl = g % EXPERT_BUFS
        in_copy(g).wait()

        @pl.when(g + (EXPERT_BUFS - 1) < total)
        def _():
            in_copy(g + (EXPERT_BUFS - 1)).start()

        @pl.when(g >= EXPERT_BUFS)
        def _():
            out_copy(g - EXPERT_BUFS).wait()

        gu = _dot(xbuf[sl], wgu_scr[...])
        act = _silu(gu[:, :de]) * gu[:, de:]
        obuf[sl] = _dot(act.astype(BF16), wd_scr[...]).astype(BF16)
        out_copy(g).start()
        return c

    lax.fori_loop(0, n, block, 0)

    @pl.when(e == pl.num_programs(0) - 1)
    def _():
        for back in range(1, EXPERT_BUFS + 1):
            @pl.when(total >= back)
            def _():
                out_copy(total - back).wait()


def _experts(blk0, nblk, n_blocks, xs, w_gate, w_up, w_down, layer):
    n_slots, d = xs.shape
    _, n_exp, _, de = w_gate.shape
    return pl.pallas_call(
        _expert_kernel,
        grid_spec=pltpu.PrefetchScalarGridSpec(
            num_scalar_prefetch=3, grid=(n_exp,),
            in_specs=[pl.BlockSpec((None, None, d, de), lambda e, a, b, c: (layer, e, 0, 0)),
                      pl.BlockSpec((None, None, d, de), lambda e, a, b, c: (layer, e, 0, 0)),
                      pl.BlockSpec((None, None, de, d), lambda e, a, b, c: (layer, e, 0, 0)),
                      pl.BlockSpec(memory_space=pl.ANY)],
            out_specs=pl.BlockSpec(memory_space=pl.ANY),
            scratch_shapes=[pltpu.VMEM((EXPERT_BUFS, MOE_BLK, d), BF16), pltpu.VMEM((EXPERT_BUFS, MOE_BLK, d), BF16),
                            pltpu.VMEM((d, 2 * de), BF16), pltpu.VMEM((de, d), BF16),
                            pltpu.SemaphoreType.DMA((EXPERT_BUFS,)), pltpu.SemaphoreType.DMA((EXPERT_BUFS,))]),
        out_shape=jax.ShapeDtypeStruct((n_slots, d), BF16),
        compiler_params=_cparams(("arbitrary",)),
        name="moe_experts",
    )(blk0, nblk, n_blocks, w_gate, w_up, w_down, xs)


def _combine_kernel(unit_ref, next_ref, pos_ref, w_ref, x_ref, h_ref, gate2_ref, sg_ref, su_ref, sd_ref, lng_ref,
                    lnb_ref, ys_ref, o_ref, s_scr, sem, *, alpha):
    rows = s_scr.shape[1]
    td = x_ref.shape[0]
    step = pl.program_id(0)
    slot = step % 2

    def unit_copy(tbl, u, sl):
        return pltpu.make_async_copy(ys_ref.at[_unit_rows(tbl[0, u]), :], s_scr.at[sl, _unit_rows(u), :], sem.at[sl])

    def fetch(tbl, sl):
        _for_each(tbl[1, 0], lambda u: unit_copy(tbl, u, sl).start())

    @pl.when(step == 0)
    def _():
        s_scr[...] = jnp.zeros_like(s_scr)
        fetch(unit_ref, 0)

    @pl.when(step + 1 < pl.num_programs(0))
    def _():
        fetch(next_ref, 1 - slot)

    h = h_ref[...]
    acc = _dot((_silu(_dot(h, sg_ref[...])) * _dot(h, su_ref[...])).astype(BF16), sd_ref[...])
    pos = pos_ref[...]
    w = w_ref[...]
    _for_each(unit_ref[1, 0], lambda u: unit_copy(unit_ref, 0, slot).wait())
    for r0 in range(0, rows, SORT_CHUNK):
        lane = lax.broadcasted_iota(I32, (td, SORT_CHUNK), 1) + r0
        wmat = jnp.zeros((td, SORT_CHUNK), F32)
        for k in range(pos.shape[1]):
            wmat = jnp.where(lane == pos[:, k:k + 1], w[:, k:k + 1], wmat)
        w_hi = wmat.astype(BF16)
        w_lo = (wmat - w_hi.astype(F32)).astype(BF16)
        both = _dot(jnp.concatenate([w_hi, w_lo], axis=0), s_scr[slot, r0:r0 + SORT_CHUNK, :])
        acc = acc + (both[:td] + both[td:])
    o_ref[...] = _ln(alpha * x_ref[...] + gate2_ref[...] * acc) * lng_ref[...] + lnb_ref[...]


def _combine(unit_dst, pos_tok, w_tok, xn, h2, modp, sh_gate, sh_up, sh_down, ln_g, ln_b, ys, seg_td, *,
             alpha, n_exp):
    t, d = xn.shape
    k = pos_tok.shape[1]
    nt = t // TD
    tok = pl.BlockSpec((TD, d), lambda i: (i, 0))
    full = lambda a: pl.BlockSpec(a.shape, lambda i: (0,) * a.ndim)
    return pl.pallas_call(
        functools.partial(_combine_kernel, alpha=alpha),
        grid=(nt,),
        in_specs=[pl.BlockSpec((8, UNIT_LANES), lambda i: (0, i), memory_space=pltpu.SMEM),
                  pl.BlockSpec((8, UNIT_LANES), lambda i: (0, jnp.minimum(i + 1, nt - 1)), memory_space=pltpu.SMEM),
                  pl.BlockSpec((TD, k), lambda i: (i, 0)),
                  pl.BlockSpec((TD, k), lambda i: (i, 0)),
                  tok, tok,
                  pl.BlockSpec((None, None, 1, d), lambda i: (5, seg_td(i), 0, 0)),
                  full(sh_gate), full(sh_up), full(sh_down), full(ln_g), full(ln_b),
                  pl.BlockSpec(memory_space=pl.ANY)],
        out_specs=tok,
        out_shape=jax.ShapeDtypeStruct((t, d), F32),
        scratch_shapes=[pltpu.VMEM((2, _tile_rows(n_exp), d), BF16), pltpu.SemaphoreType.DMA((2,))],
        compiler_params=_cparams(("arbitrary",)),
        name="moe_combine",
    )(unit_dst, unit_dst, pos_tok, w_tok, xn, h2, modp, sh_gate, sh_up, sh_down, ln_g, ln_b, ys)


def _moe(xn, h2, scores_t, modp, seg_td, router_bias, w_gate, w_up, w_down, sh_gate, sh_up, sh_down,
         ln_g, ln_b, tri, ltri, *, alpha, layer):
    t = xn.shape[0]
    n_exp = scores_t.shape[0]
    assert _tile_rows(n_exp) % SORT_CHUNK == 0 and _tile_rows(n_exp) <= UNIT * UNIT_LANES
    pos, w_sel, units, tot = _route(scores_t, router_bias.reshape(n_exp, 1), tri, ltri)
    used = tot[:, 0].astype(I32)
    region = (used + MOE_BLK - 1) // MOE_BLK * MOE_BLK
    region_end = jnp.cumsum(region)
    region_start = region_end - region
    n_slots = TOP_K * t + n_exp * UNIT * (t // TD) + n_exp * MOE_BLK
    unit_dst = _unit_dest((region_start // UNIT).astype(I32), units)
    xs = _dispatch(((region_start + used) // UNIT).astype(I32), (region_end // UNIT).astype(I32),
                   unit_dst[1, ::UNIT_LANES], unit_dst, pos, h2, n_slots)
    ys = _experts((region_start // MOE_BLK).astype(I32), (region // MOE_BLK).astype(I32),
                  (region_end[-1:] // MOE_BLK).astype(I32), xs, w_gate, w_up, w_down, layer)
    return _combine(unit_dst, pos.T, w_sel.T, xn, h2, modp, sh_gate, sh_up, sh_down, ln_g, ln_b, ys, seg_td,
                    alpha=alpha, n_exp=n_exp)


def kernel(x, c, ctx, c_ctx, w_mod, b_mod, w_in, b_in, lam_q1, lam_k1, lam_q2, lam_k2, attn_norm_g, sg_ln_g, sg_ln_b, sg_w, sg_b, conv_w, conv_b, lru_wa, lru_ba, lru_wx, lru_bx, lru_lam, w_branch, w_out, ln1_g, ln1_b, w_router, router_bias, moe_w_gate, moe_w_up, moe_w_down, sh_w_gate, sh_w_up, sh_w_down, ln2_g, ln2_b):
    batch, n_lat, d = x.shape
    n_ctx = ctx.shape[1]
    depth = w_mod.shape[0]
    n_exp = w_router.shape[2]
    t_lat = batch * n_lat
    t_ctx = batch * n_ctx
    t_all = t_lat + t_ctx
    assert n_lat % TM == 0 and t_ctx % TM == 0 and n_ctx % B_CHUNK == 0 and t_lat % n_ctx == 0
    assert batch + 1 <= MOD_ROWS and TM % TD == 0
    alpha = (2 * depth) ** 0.25
    cw = conv_w.shape[2]
    tiles_per_batch = n_lat // TM

    seg = lambda i: jnp.minimum(i // tiles_per_batch, batch)
    seg_td = lambda i: jnp.minimum(i // (n_lat // TD), batch)
    tm_in = 2 * TM if (n_lat % (2 * TM) == 0 and t_ctx % (2 * TM) == 0) else TM
    seg_in = lambda i: jnp.minimum(i // (n_lat // tm_in), batch)
    tbl_idx = lambda i: jnp.where(i < t_lat // tm_in, i % (n_lat // tm_in), n_lat // tm_in)

    x_all = jnp.concatenate([x.reshape(t_lat, d), ctx.reshape(t_ctx, d)], axis=0)
    c_all = jnp.zeros((MOD_ROWS, d), F32).at[:batch].set(c).at[batch].set(c_ctx)
    tables = _rope_tables(n_lat, tm_in)
    tri = (jnp.arange(TD)[:, None] <= jnp.arange(TD)[None, :]).astype(BF16)
    ltri = (jnp.arange(n_exp)[None, :] < jnp.arange(n_exp)[:, None]).astype(BF16)
    row = lambda v: v.reshape(1, -1)

    def dense_blocks(w):
        nb, bi, bj = w.shape
        eye = jnp.eye(nb, dtype=w.dtype)
        return (w[:, :, None, :] * eye[:, None, :, None]).reshape(nb * bi, nb * bj)

    for l in range(depth):
        last = l == depth - 1
        lam_init = 0.8 - 0.6 * math.exp(-0.3 * l)
        mod = _mod_rows(c_all, w_mod[l].astype(BF16), row(b_mod[l]))
        modp = mod.reshape(MOD_ROWS, 6, 1, d).transpose(1, 0, 2, 3)

        qr, kr, vb, proj = _in_proj(x_all, modp, w_in[l].astype(BF16), row(b_in[l]), tables, seg_in, tbl_idx, tm_in)

        lam_vecs = jnp.stack([lam_q1[l], lam_k1[l], lam_q2[l], lam_k2[l]])
        gain = attn_norm_g[l].reshape(A_HEADS, 1, A_V_DIM)
        attn = functools.partial(_attention, qr, kr, vb, lam_vecs, gain, lam_init=lam_init,
                                 batch=batch, n_lat=n_lat, n_ctx=n_ctx)
        n_rows = t_lat if last else t_all
        o_a = attn(latent_queries=True, out_rows=n_rows)
        if not last:
            o_a = attn(latent_queries=False, out_rows=n_rows, into=o_a)

        o_b = _spatial_gating(proj, row(sg_ln_g[l]), row(sg_ln_b[l]), sg_w[l].astype(BF16),
                              sg_b[l].reshape(B_GROUPS, B_CHUNK, 1))

        lru = functools.partial(_lru_pass, proj, conv_w=conv_w[l], conv_b=row(conv_b[l]), batch=batch)
        zeros_h = jnp.zeros((batch, 8, cw), F32)
        hf = {}
        for direction in range(2):
            wd = jnp.concatenate([dense_blocks(lru_wa[l, direction]), dense_blocks(lru_wx[l, direction])],
                                 axis=1).astype(BF16)
            par = dict(w_dense=wd, ba=row(lru_ba[l, direction]), bx=row(lru_bx[l, direction]),
                       lam=row(lru_lam[l, direction]), reverse=direction == 1, final=direction == 1)
            shared = direction == 1 and not last
            h_ctx, edge = lru(hf.get("ctx"), zeros_h, seq=n_ctx, tl=n_ctx, row0=t_lat,
                              out_rows=t_all if shared else None, out_row0=t_lat if shared else 0, **par)
            h_lat, _ = lru(hf.get("lat"), edge, seq=n_lat, tl=TM, row0=0,
                           out_rows=t_all if shared else None, into=h_ctx if shared else None, **par)
            hf = {"ctx": h_ctx, "lat": h_lat}
        o_c = hf["lat"]
        wr_t = w_router[l].T
        wr_hi = wr_t.astype(BF16)
        wr_lo = (wr_t - wr_hi.astype(F32)).astype(BF16)
        xn, h2, scores_t = _merge(x_all, o_a, o_b, o_c, proj, modp, w_branch[l].astype(BF16),
                                  w_out[l].astype(BF16), row(ln1_g[l]), row(ln1_b[l]), wr_hi, wr_lo, seg,
                                  n_rows=n_rows, alpha=alpha)
        x_all = _moe(xn, h2, scores_t, modp, seg_td, router_bias[l], moe_w_gate,
                     moe_w_up, moe_w_down, sh_w_gate[l].astype(BF16),
                     sh_w_up[l].astype(BF16), sh_w_down[l].astype(BF16), row(ln2_g[l]), row(ln2_b[l]), tri, ltri,
                     alpha=alpha, layer=l)
    return x_all[:t_lat].reshape(batch, n_lat, d)
```

```python
import functools
import math

import jax
import jax.numpy as jnp
from jax import lax
from jax.experimental import pallas as pl
from jax.experimental.pallas import tpu as pltpu

F32 = jnp.float32
BF16 = jnp.bfloat16
I32 = jnp.int32

A_HEADS = 4
A_QK_DIM = 64
A_V_DIM = 2 * A_QK_DIM
GRID_W = 64
ROPE_THETA = 10000.0
B_CHUNK = 128
B_GROUPS = 4
C_BLOCKS = 8
C_POW = 8.0
TOP_K = 8
N_GROUPS = 8
TOPK_GROUPS = 4
ROUTED_SCALE = 2.5
LN_EPS = 1e-6
RMS_EPS = 1e-5

TM = 512
MOE_BLK = 512
TD = 256
UNIT = 16
UNIT_LANES = 256
SORT_CHUNK = 512
EXPERT_BUFS = 4
HALO = 16
ATTN_HEADS = 4
ATTN_SUB = 2048
MOD_ROWS = 16
VMEM_LIMIT = 56 * 1024 * 1024


def _cparams(sem):
    return pltpu.CompilerParams(dimension_semantics=sem, vmem_limit_bytes=VMEM_LIMIT)


def _ln(x):
    mu = jnp.mean(x, axis=-1, keepdims=True)
    xc = x - mu
    var = jnp.mean(xc * xc, axis=-1, keepdims=True)
    return xc * lax.rsqrt(var + LN_EPS)


def _gelu(x):
    cdf = 0.5 * (1.0 + jnp.tanh(math.sqrt(2.0 / math.pi) * (x + 0.044715 * (x * x * x))))
    return x * cdf


def _sigmoid(x):
    return 1.0 / (1.0 + jnp.exp(-x))


def _silu(x):
    return x * _sigmoid(x)


def _dot(a, b):
    return jnp.dot(a, b, preferred_element_type=F32)


def _dot_nt(a, b):
    return lax.dot_general(a, b, (((1,), (1,)), ((), ())), preferred_element_type=F32)


def _mod_kernel(c_ref, w_ref, b_ref, o_ref):
    o_ref[...] = _dot(_silu(c_ref[...]).astype(BF16), w_ref[...]) + b_ref[...]


def _mod_rows(c_all, w, b):
    m, d = c_all.shape
    n = w.shape[1]
    tn = 1536
    return pl.pallas_call(
        _mod_kernel,
        grid=(n // tn,),
        in_specs=[pl.BlockSpec((m, d), lambda j: (0, 0)),
                  pl.BlockSpec((d, tn), lambda j: (0, j)),
                  pl.BlockSpec((1, tn), lambda j: (0, j))],
        out_specs=pl.BlockSpec((m, tn), lambda j: (0, j)),
        out_shape=jax.ShapeDtypeStruct((m, n), F32),
        compiler_params=_cparams(("parallel",)),
        name="adaln_rows",
    )(c_all, w, b)


REST_TILE = 1280
QKV_TILES = 3
COL_U, COL_S, COL_X, COL_Y, COL_G = 0, 1, 2, 3, 4


def _in_proj_kernel(x_ref, sh_ref, sc_ref, w_ref, b_ref, cos_ref, sa_ref, sb_ref,
                    q_ref, k_ref, v_ref, h_ref):
    j = pl.program_id(1)

    @pl.when(j == 0)
    def _():
        h = _ln(x_ref[...]) * (1.0 + sc_ref[...]) + sh_ref[...]
        h_ref[...] = h.astype(BF16)

    acc = _dot(h_ref[...], w_ref[...]) + b_ref[...]

    def rope(dst_ref, scale):
        hw = cos_ref.shape[1]
        for c0 in range(0, acc.shape[1], hw):
            x = acc[:, c0:c0 + hw]
            r = x * cos_ref[...] + pltpu.roll(x, hw - 16, 1) * sa_ref[...] + pltpu.roll(x, 16, 1) * sb_ref[...]
            dst_ref[:, c0:c0 + hw] = (r * scale).astype(BF16)

    @pl.when(j == 0)
    def _():
        rope(q_ref, A_QK_DIM ** -0.5 * math.log2(math.e))

    @pl.when(j == 1)
    def _():
        rope(k_ref, 1.0)

    @pl.when(j == 2)
    def _():
        v_ref[...] = acc.astype(BF16)


def _rest_proj_kernel(h_ref, w_ref, b_ref, o_ref):
    o_ref[...] = (_dot(h_ref[...], w_ref[...]) + b_ref[...]).astype(o_ref.dtype)


def _in_proj(x_all, modp, w, b, tables, seg, tbl_idx, tm):
    t, d = x_all.shape
    tn = 2 * A_HEADS * A_QK_DIM
    n_qkv = QKV_TILES * tn
    n_rest = w.shape[1] - n_qkv
    cos, sa, sb = tables
    tbl = pl.BlockSpec((tm, cos.shape[1]), lambda i, j: (tbl_idx(i), 0))
    qkv = pl.BlockSpec((tm, tn), lambda i, j: (i, 0))
    qkv_shape = jax.ShapeDtypeStruct((t, tn), BF16)
    q, k, v, h = pl.pallas_call(
        _in_proj_kernel,
        grid=(t // tm, QKV_TILES),
        in_specs=[pl.BlockSpec((tm, d), lambda i, j: (i, 0)),
                  pl.BlockSpec((None, None, 1, d), lambda i, j: (0, seg(i), 0, 0)),
                  pl.BlockSpec((None, None, 1, d), lambda i, j: (1, seg(i), 0, 0)),
                  pl.BlockSpec((d, tn), lambda i, j: (0, j)),
                  pl.BlockSpec((1, tn), lambda i, j: (0, j)),
                  tbl, tbl, tbl],
        out_specs=[qkv, qkv, qkv, pl.BlockSpec((tm, d), lambda i, j: (i, 0))],
        out_shape=[qkv_shape, qkv_shape, qkv_shape, jax.ShapeDtypeStruct((t, d), BF16)],
        compiler_params=_cparams(("parallel", "arbitrary")),
        name="in_proj_qkv",
    )(x_all, modp, modp, w, b, cos, sa, sb)
    tr = REST_TILE if n_rest % REST_TILE == 0 else tn
    rest = pl.pallas_call(
        _rest_proj_kernel,
        grid=(t // tm, n_rest // tr),
        in_specs=[pl.BlockSpec((tm, d), lambda i, j: (i, 0)),
                  pl.BlockSpec((d, tr), lambda i, j: (0, j)),
                  pl.BlockSpec((1, tr), lambda i, j: (0, j))],
        out_specs=pl.BlockSpec((tm, tr), lambda i, j: (i, j)),
        out_shape=jax.ShapeDtypeStruct((t, n_rest), BF16),
        compiler_params=_cparams(("parallel", "parallel")),
        name="in_proj_rest",
    )(h, w[:, n_qkv:], b[:, n_qkv:])
    return q, k, v, rest


def _rope_tables(n, tm):
    rows = n // GRID_W
    pos_row = jnp.repeat(jnp.arange(rows), GRID_W).astype(F32)
    pos_col = jnp.tile(jnp.arange(GRID_W), rows).astype(F32)
    quarter = A_QK_DIM // 4
    inv = ROPE_THETA ** (-jnp.arange(quarter, dtype=F32) / quarter)
    ang_r = pos_row[:, None] * inv
    ang_c = pos_col[:, None] * inv
    ang = jnp.concatenate([ang_r, ang_r, ang_c, ang_c], axis=-1)
    cos = jnp.cos(ang)
    sin = jnp.sin(ang)
    first_half = (jnp.arange(A_QK_DIM) % (2 * quarter)) < quarter
    sa = jnp.where(first_half, -sin, 0.0)
    sb = jnp.where(first_half, 0.0, sin)

    def widen(tbl, fill):
        tbl = jnp.tile(tbl, (1, 2))
        return jnp.concatenate([tbl, jnp.full((tm, tbl.shape[1]), fill, F32)], axis=0)

    return widen(cos, 1.0), widen(sa, 0.0), widen(sb, 0.0)


def _attn_kernel(*refs, lam_init, with_lat, sub):
    if with_lat:
        lam_ref, gain_ref, q_ref, kc_ref, vc_ref, kl_ref, vl_ref, o_ref = refs
    else:
        lam_ref, gain_ref, q_ref, kc_ref, vc_ref, _, o_ref = refs
    lv = lam_ref[...]
    lam = (jnp.exp(jnp.sum(lv[0:1] * lv[1:2], axis=-1, keepdims=True))
           - jnp.exp(jnp.sum(lv[2:3] * lv[3:4], axis=-1, keepdims=True)) + lam_init)
    chunks = [(kc_ref, vc_ref, 0, kc_ref.shape[0])]
    if with_lat:
        ck = min(sub, kl_ref.shape[0])
        chunks += [(kl_ref, vl_ref, r0, ck) for r0 in range(0, kl_ref.shape[0], ck)]
    for hh in range(gain_ref.shape[0]):
        head = slice(hh * A_V_DIM, (hh + 1) * A_V_DIM)
        probs, dens = [], []
        for c in range(2):
            cols = slice(hh * A_V_DIM + c * A_QK_DIM, hh * A_V_DIM + (c + 1) * A_QK_DIM)
            qc = q_ref[:, cols]
            scores = [_dot_nt(qc, k_ref[r0:r0 + n, cols]) for k_ref, _, r0, n in chunks]
            m = None
            for s in scores:
                ms = jnp.max(s, axis=-1, keepdims=True)
                m = ms if m is None else jnp.maximum(m, ms)
            p = [jnp.exp2(s - m) for s in scores]
            den = None
            for pj in p:
                ds = jnp.sum(pj, axis=-1, keepdims=True)
                den = ds if den is None else den + ds
            probs.append(p)
            dens.append(den)
        r1 = 1.0 / dens[0]
        r2 = lam / dens[1]
        o = None
        for p1, p2, (_, v_ref, r0, n) in zip(probs[0], probs[1], chunks):
            w = p1 * r1 - p2 * r2
            ps = _dot(w.astype(BF16), v_ref[r0:r0 + n, head])
            o = ps if o is None else o + ps
        o = o * lax.rsqrt(jnp.mean(o * o, axis=-1, keepdims=True) + RMS_EPS)
        o_ref[:, head] = (o * gain_ref[hh]) * (1.0 - lam_init)


def _attention(qr, kr, vb, lam_vecs, gain, *, lam_init, batch, n_lat, n_ctx, latent_queries, out_rows, into=None):
    hw = ATTN_HEADS * A_V_DIM
    ctx_blk0 = batch * n_lat // n_ctx
    lam_spec = pl.BlockSpec(lam_vecs.shape, lambda b, h, i: (0, 0))
    gain_spec = pl.BlockSpec((ATTN_HEADS, 1, A_V_DIM), lambda b, h, i: (h, 0, 0))
    kv_ctx = pl.BlockSpec((n_ctx, hw), lambda b, h, i: (ctx_blk0 + b, h))
    if latent_queries:
        tq = 256
        nq = n_lat // tq
        q_spec = pl.BlockSpec((tq, hw), lambda b, h, i: (b * nq + i, h))
        kv_lat = pl.BlockSpec((n_lat, hw), lambda b, h, i: (b, h))
        in_specs = [lam_spec, gain_spec, q_spec, kv_ctx, kv_ctx, kv_lat, kv_lat]
        args = (lam_vecs, gain, qr, kr, vb, kr, vb)
        out_blk0 = 0
        aliases = {}
    else:
        tq = n_ctx
        nq = 1
        q_spec = pl.BlockSpec((tq, hw), lambda b, h, i: (ctx_blk0 + b, h))
        in_specs = [lam_spec, gain_spec, q_spec, kv_ctx, kv_ctx, pl.BlockSpec(memory_space=pl.ANY)]
        args = (lam_vecs, gain, qr, kr, vb, into)
        out_blk0 = ctx_blk0
        aliases = {len(args) - 1: 0}
    return pl.pallas_call(
        functools.partial(_attn_kernel, lam_init=lam_init, with_lat=latent_queries, sub=ATTN_SUB),
        grid=(batch, A_HEADS // ATTN_HEADS, nq),
        in_specs=in_specs,
        out_specs=pl.BlockSpec((tq, hw), lambda b, h, i: (out_blk0 + b * nq + i, h)),
        out_shape=jax.ShapeDtypeStruct((out_rows, A_HEADS * A_V_DIM), F32),
        input_output_aliases=aliases,
        compiler_params=_cparams(("parallel", "parallel", "arbitrary")),
        name="diff_attn_lat" if latent_queries else "diff_attn_ctx",
    )(*args)


def _sg_kernel(u_ref, s_ref, g_ref, b_ref, w_ref, bs_ref, o_ref):
    gw = B_CHUNK
    for ci in range(TM // B_CHUNK):
        rows = slice(ci * B_CHUNK, (ci + 1) * B_CHUNK)
        for g in range(B_GROUPS):
            cols = slice(g * gw, (g + 1) * gw)
            vn = _ln(_gelu(s_ref[rows, cols].astype(F32))) * g_ref[:, cols] + b_ref[:, cols]
            mixed = _dot(w_ref[g], vn.astype(BF16)) + bs_ref[g]
            o_ref[rows, cols] = _gelu(u_ref[rows, cols].astype(F32)) * mixed


def _spatial_gating(proj, ln_g, ln_b, w_s, b_s):
    t = proj.shape[0]
    w = B_GROUPS * B_CHUNK
    full = lambda a: pl.BlockSpec(a.shape, lambda i: (0,) * a.ndim)
    return pl.pallas_call(
        _sg_kernel,
        grid=(t // TM,),
        in_specs=[pl.BlockSpec((TM, w), lambda i: (i, COL_U)),
                  pl.BlockSpec((TM, w), lambda i: (i, COL_S)),
                  full(ln_g), full(ln_b), full(w_s), full(b_s)],
        out_specs=pl.BlockSpec((TM, w), lambda i: (i, 0)),
        out_shape=jax.ShapeDtypeStruct((t, w), F32),
        compiler_params=_cparams(("parallel",)),
        name="spatial_gating",
    )(proj, proj, ln_g, ln_b, w_s, b_s)


def _lru_kernel(*refs, reverse, final, n_tiles, tl, aliased):
    if aliased:
        refs = refs[:12] + refs[13:]
    if final:
        (x_ref, xp_ref, xn_ref, cw_ref, cb_ref, w_ref, ba_ref, bx_ref, lam_ref, h0_ref,
         hf_ref, y_ref, o_ref, hl_ref, carry) = refs
    else:
        (x_ref, xp_ref, xn_ref, cw_ref, cb_ref, w_ref, ba_ref, bx_ref, lam_ref, h0_ref,
         o_ref, hl_ref, carry) = refs
    step = pl.program_id(1)

    @pl.when(step == 0)
    def _():
        carry[...] = h0_ref[...]

    ti = (n_tiles - 1 - step) if reverse else step
    x = x_ref[...].astype(F32)
    cw = x.shape[-1]
    row = lax.broadcasted_iota(I32, x.shape, 0)
    has_prev = ti > 0
    has_next = ti < n_tiles - 1
    p6 = jnp.where(has_prev, xp_ref[HALO - 2:HALO - 1, :].astype(F32), 0.0)
    p7 = jnp.where(has_prev, xp_ref[HALO - 1:HALO, :].astype(F32), 0.0)
    n0 = jnp.where(has_next, xn_ref[0:1, :].astype(F32), 0.0)
    xm1 = jnp.where(row == 0, p7, pltpu.roll(x, 1, 0))
    xm2 = jnp.where(row == 0, p6, jnp.where(row == 1, p7, pltpu.roll(x, 2, 0)))
    xp1 = jnp.where(row == tl - 1, n0, pltpu.roll(x, tl - 1, 0))
    taps = cw_ref[...]
    xs = taps[0:1] * xm2 + taps[1:2] * xm1 + taps[2:3] * x + taps[3:4] * xp1 + cb_ref[...]

    z = _dot(xs.astype(BF16), w_ref[...])
    r = _sigmoid(z[:, :cw] + ba_ref[...])
    ig = _sigmoid(z[:, cw:] + bx_ref[...])
    nl = -lam_ref[...]
    softplus = jnp.maximum(nl, 0.0) + jnp.log(1.0 + jnp.exp(-jnp.abs(nl)))
    log_a = -C_POW * r * softplus
    a = jnp.exp(log_a)
    u = jnp.sqrt(1.0 - jnp.exp(2.0 * log_a)) * (ig * xs)

    d = 1
    while d < tl:
        if reverse:
            ok = row < tl - d
            shift = tl - d
        else:
            ok = row >= d
            shift = d
        a_sh = jnp.where(ok, pltpu.roll(a, shift, 0), 1.0)
        u_sh = jnp.where(ok, pltpu.roll(u, shift, 0), 0.0)
        u = u + a * u_sh
        a = a * a_sh
        d *= 2
    h = u + a * carry[0:1, :]
    edge = h[0:1, :] if reverse else h[tl - 1:tl, :]
    carry[...] = jnp.broadcast_to(edge, carry.shape)
    hl_ref[...] = jnp.broadcast_to(edge, hl_ref.shape)
    if final:
        o_ref[...] = _gelu(y_ref[...].astype(F32)) * (hf_ref[...] + h)
    else:
        o_ref[...] = h


def _lru_pass(proj, hf, h0, conv_w, conv_b, w_dense, ba, bx, lam, *, batch, seq, tl, row0, reverse, final,
              out_rows=None, out_row0=0, into=None):
    t_all = proj.shape[0]
    out_rows = batch * seq if out_rows is None else out_rows
    out_blk0 = out_row0 // tl
    assert into is None or final
    cw = conv_w.shape[1]
    n_tiles = seq // tl
    base = row0 // tl
    sub = tl // HALO
    last_halo = t_all // HALO - 1

    def tile(b, s):
        ti = (n_tiles - 1 - s) if reverse else s
        return base + b * n_tiles + ti

    def loc(b, s):
        ti = (n_tiles - 1 - s) if reverse else s
        return b * n_tiles + ti

    full = lambda a: pl.BlockSpec(a.shape, lambda b, s: (0,) * a.ndim)
    in_specs = [pl.BlockSpec((tl, cw), lambda b, s: (tile(b, s), COL_X)),
                pl.BlockSpec((HALO, cw), lambda b, s: (jnp.maximum(tile(b, s) * sub - 1, 0), COL_X)),
                pl.BlockSpec((HALO, cw), lambda b, s: (jnp.minimum((tile(b, s) + 1) * sub, last_halo), COL_X)),
                full(conv_w), full(conv_b), full(w_dense), full(ba), full(bx), full(lam),
                pl.BlockSpec((None, 8, cw), lambda b, s: (b, 0, 0))]
    args = [proj, proj, proj, conv_w, conv_b, w_dense, ba, bx, lam, h0]
    if final:
        in_specs += [pl.BlockSpec((tl, cw), lambda b, s: (loc(b, s), 0)),
                     pl.BlockSpec((tl, cw), lambda b, s: (tile(b, s), COL_Y))]
        args += [hf, proj]
    aliases = {}
    if into is not None:
        aliases = {len(args): 0}
        in_specs += [pl.BlockSpec(memory_space=pl.ANY)]
        args += [into]
    return pl.pallas_call(
        functools.partial(_lru_kernel, reverse=reverse, final=final, n_tiles=n_tiles, tl=tl,
                          aliased=into is not None),
        grid=(batch, n_tiles),
        in_specs=in_specs,
        out_specs=[pl.BlockSpec((tl, cw), lambda b, s: (out_blk0 + loc(b, s), 0)),
                   pl.BlockSpec((None, 8, cw), lambda b, s: (b, 0, 0))],
        out_shape=[jax.ShapeDtypeStruct((out_rows, cw), F32),
                   jax.ShapeDtypeStruct((batch, 8, cw), F32)],
        scratch_shapes=[pltpu.VMEM((8, cw), F32)],
        input_output_aliases=aliases,
        compiler_params=_cparams(("parallel", "arbitrary")),
        name="rglru_%s_%s" % ("bwd" if reverse else "fwd", "lat" if row0 == 0 else "ctx"),
    )(*args)


def _merge_kernel(x_ref, oa_ref, ob_ref, oc_ref, g0, g1, g2, g3, g4, g5, wb_ref, wo_ref,
                  gate1_ref, sh2_ref, sc2_ref, lng_ref, lnb_ref, wrh_ref, wrl_ref,
                  xn_ref, h2_ref, sc_ref, *, alpha):
    gates = ((g0, g1), (g2, g3), (g4, g5))
    mix = None
    for r, o_ref in enumerate((oa_ref, ob_ref, oc_ref)):
        proj = _dot(o_ref[...].astype(BF16), wb_ref[r])
        gate = jnp.concatenate([_sigmoid(gates[r][0][...].astype(F32)), _sigmoid(gates[r][1][...].astype(F32))],
                               axis=-1)
        term = gate * proj
        mix = term if mix is None else mix + term
    out = _dot(mix.astype(BF16), wo_ref[...])
    xn = _ln(alpha * x_ref[...] + gate1_ref[...] * out) * lng_ref[...] + lnb_ref[...]
    xn_ref[...] = xn
    h2 = _ln(xn) * (1.0 + sc2_ref[...]) + sh2_ref[...]
    h_hi = h2.astype(BF16)
    h2_ref[...] = h_hi
    h_lo = (h2 - h_hi.astype(F32)).astype(BF16)
    logits = _dot_nt(wrh_ref[...], h_hi) + (_dot_nt(wrh_ref[...], h_lo) + _dot_nt(wrl_ref[...], h_hi))
    sc_ref[...] = _sigmoid(logits)


def _merge(x_all, o_a, o_b, o_c, proj, modp, w_branch, w_out, ln_g, ln_b, wr_hi, wr_lo, seg, *, n_rows, alpha):
    d = x_all.shape[1]
    bw = o_a.shape[1]
    n_exp = wr_hi.shape[0]
    tok = lambda w: pl.BlockSpec((TM, w), lambda i: (i, 0))
    gate = lambda c: pl.BlockSpec((TM, bw), lambda i, c=c: (i, c))
    mod = lambda k: pl.BlockSpec((None, None, 1, d), lambda i, k=k: (k, seg(i), 0, 0))
    full = lambda a: pl.BlockSpec(a.shape, lambda i: (0,) * a.ndim)
    return pl.pallas_call(
        functools.partial(_merge_kernel, alpha=alpha),
        grid=(n_rows // TM,),
        in_specs=[tok(d), tok(bw), tok(bw), tok(bw)] + [gate(COL_G + c) for c in range(6)]
                 + [full(w_branch), full(w_out), mod(2), mod(3), mod(4), full(ln_g), full(ln_b),
                    full(wr_hi), full(wr_lo)],
        out_specs=[tok(d), tok(d), pl.BlockSpec((n_exp, TM), lambda i: (0, i))],
        out_shape=[jax.ShapeDtypeStruct((n_rows, d), F32),
                   jax.ShapeDtypeStruct((n_rows, d), BF16),
                   jax.ShapeDtypeStruct((n_exp, n_rows), F32)],
        compiler_params=_cparams(("parallel",)),
        name="merge_residual_router",
    )(x_all, o_a, o_b, o_c, proj, proj, proj, proj, proj, proj, w_branch, w_out,
      modp, modp, modp, ln_g, ln_b, wr_hi, wr_lo)


def _tile_rows(n_exp):
    return TOP_K * TD + n_exp * UNIT


def _route_kernel(s_ref, bias_ref, tri_ref, ltri_ref, pos_ref, w_ref, unit_ref, tot_ref, carry):
    @pl.when(pl.program_id(0) == 0)
    def _():
        carry[...] = jnp.zeros_like(carry)

    s = s_ref[...]
    n_exp, tn = s.shape
    per = n_exp // N_GROUPS
    neg = -jnp.inf
    biased = s + bias_ref[...]
    sub = lax.broadcasted_iota(I32, (per, tn), 0)
    gs_rows = []
    for g in range(N_GROUPS):
        blk = biased[g * per:(g + 1) * per, :]
        m1 = jnp.max(blk, axis=0, keepdims=True)
        first = jnp.min(jnp.where(blk == m1, sub, per), axis=0, keepdims=True)
        m2 = jnp.max(jnp.where(sub == first, neg, blk), axis=0, keepdims=True)
        gs_rows.append(m1 + m2)
    gs = jnp.concatenate(gs_rows, axis=0)
    gi = lax.broadcasted_iota(I32, gs.shape, 0)
    g_ok = jnp.zeros(gs.shape, F32)
    cur = gs
    for _ in range(TOPK_GROUPS):
        m = jnp.max(cur, axis=0, keepdims=True)
        pick = jnp.min(jnp.where(cur == m, gi, N_GROUPS), axis=0, keepdims=True)
        hit = gi == pick
        g_ok = jnp.where(hit, 1.0, g_ok)
        cur = jnp.where(hit, neg, cur)
    ok_rows = [jnp.broadcast_to(g_ok[g:g + 1, :], (per, tn)) for g in range(N_GROUPS)]
    expert_ok = jnp.concatenate(ok_rows, axis=0)
    masked = jnp.where(expert_ok > 0.0, biased, neg)
    ei = lax.broadcasted_iota(I32, s.shape, 0)
    pick_s, hits = [], []
    sel = jnp.zeros(s.shape, F32)
    for _ in range(TOP_K):
        m = jnp.max(masked, axis=0, keepdims=True)
        pick = jnp.min(jnp.where(masked == m, ei, n_exp), axis=0, keepdims=True)
        hit = ei == pick
        pick_s.append(jnp.sum(jnp.where(hit, s, 0.0), axis=0, keepdims=True))
        hits.append(hit)
        sel = jnp.where(hit, 1.0, sel)
        masked = jnp.where(hit, neg, masked)
    tot = pick_s[0]
    for k in range(1, TOP_K):
        tot = tot + pick_s[k]
    w_ref[...] = jnp.concatenate([p / tot * ROUTED_SCALE for p in pick_s], axis=0)

    incl = _dot(sel.astype(BF16), tri_ref[...])
    count = incl[:, tn - 1:tn]
    run = jnp.floor((count + (UNIT - 1)) * (1.0 / UNIT)) * UNIT
    run_start = _dot(ltri_ref[...], jnp.broadcast_to(run, (n_exp, 128)).astype(BF16))[:, 0:1]
    row_in_tile = incl - sel + run_start
    pos = [jnp.sum(jnp.where(hit, row_in_tile, 0.0), axis=0, keepdims=True) for hit in hits]
    pos_ref[...] = jnp.concatenate(pos, axis=0).astype(I32)

    region_used = carry[:, 0:1]
    u = lax.broadcasted_iota(I32, (n_exp, UNIT_LANES), 1).astype(F32)
    eu = lax.broadcasted_iota(I32, (n_exp, UNIT_LANES), 0).astype(F32)
    u0 = run_start * (1.0 / UNIT)
    nu = run * (1.0 / UNIT)
    inside = jnp.logical_and(u >= u0, u < u0 + nu)
    rel = jnp.sum(jnp.where(inside, u - u0 + region_used * (1.0 / UNIT), 0.0), axis=0, keepdims=True)
    exp_of = jnp.sum(jnp.where(inside, eu, 0.0), axis=0, keepdims=True)
    n_units = jnp.broadcast_to(jnp.sum(nu, axis=0, keepdims=True), (1, UNIT_LANES))
    pad = jnp.zeros((5, UNIT_LANES), F32)
    unit_ref[...] = jnp.concatenate([rel, exp_of, n_units, pad], axis=0).astype(I32)

    total = region_used + run
    carry[...] = jnp.broadcast_to(total, carry.shape)
    tot_ref[...] = jnp.broadcast_to(total, tot_ref.shape)


def _route(scores_t, bias_col, tri, ltri):
    n_exp, t = scores_t.shape
    nt = t // TD
    return pl.pallas_call(
        _route_kernel,
        grid=(nt,),
        in_specs=[pl.BlockSpec((n_exp, TD), lambda i: (0, i)),
                  pl.BlockSpec((n_exp, 1), lambda i: (0, 0)),
                  pl.BlockSpec((TD, TD), lambda i: (0, 0)),
                  pl.BlockSpec((n_exp, n_exp), lambda i: (0, 0))],
        out_specs=[pl.BlockSpec((TOP_K, TD), lambda i: (0, i)),
                   pl.BlockSpec((TOP_K, TD), lambda i: (0, i)),
                   pl.BlockSpec((8, UNIT_LANES), lambda i: (0, i)),
                   pl.BlockSpec((n_exp, 128), lambda i: (0, 0))],
        out_shape=[jax.ShapeDtypeStruct((TOP_K, t), I32),
                   jax.ShapeDtypeStruct((TOP_K, t), F32),
                   jax.ShapeDtypeStruct((8, nt * UNIT_LANES), I32),
                   jax.ShapeDtypeStruct((n_exp, 128), F32)],
        scratch_shapes=[pltpu.VMEM((n_exp, 128), F32)],
        compiler_params=_cparams(("arbitrary",)),
        name="route_topk",
    )(scores_t, bias_col, tri, ltri)


def _unit_dest_kernel(start_ref, unit_ref, o_ref, *, n_exp):
    tbl = unit_ref[...]
    exp_of = tbl[1:2, :]
    dst = tbl[0:1, :]
    for e in range(n_exp):
        dst = dst + jnp.where(exp_of == e, start_ref[e], 0)
    o_ref[...] = jnp.concatenate([dst, tbl[2:3, :], jnp.zeros((6, tbl.shape[1]), I32)], axis=0)


def _unit_dest(region_start_units, units):
    blk = lambda: pl.BlockSpec((8, UNIT_LANES), lambda i, s: (0, i))
    return pl.pallas_call(
        functools.partial(_unit_dest_kernel, n_exp=region_start_units.shape[0]),
        grid_spec=pltpu.PrefetchScalarGridSpec(
            num_scalar_prefetch=1, grid=(units.shape[1] // UNIT_LANES,), in_specs=[blk()], out_specs=blk()),
        out_shape=jax.ShapeDtypeStruct(units.shape, I32),
        compiler_params=_cparams(("parallel",)),
        name="route_units",
    )(region_start_units, units)


def _unit_rows(i):
    return pl.ds(pl.multiple_of(i * UNIT, UNIT), UNIT)


def _for_each(count, fn, group=8):
    main = lax.shift_right_logical(count, int(math.log2(group)))

    def many(i, c):
        for r in range(group):
            fn(i * group + r)
        return c

    lax.fori_loop(0, main, many, 0)
    lax.fori_loop(main * group, count, lambda i, c: (fn(i), c)[1], 0)


def _dispatch_kernel(tail_lo_ref, tail_hi_ref, nu_ref, unit_ref, pos_ref, h_ref, xs_ref, s_scr, zero_scr,
                     sem, zsem, *, n_exp):
    rows = s_scr.shape[1]
    td = h_ref.shape[0]
    step = pl.program_id(0)
    last = pl.num_programs(0) - 1
    slot = step % 2

    def zero_copy(g):
        return pltpu.make_async_copy(zero_scr, xs_ref.at[_unit_rows(g), :], zsem.at[0])

    def unit_copy(u, sl):
        return pltpu.make_async_copy(s_scr.at[sl, _unit_rows(u), :], xs_ref.at[_unit_rows(unit_ref[0, u]), :],
                                     sem.at[sl])

    def drain(count, sl):
        _for_each(count, lambda u: unit_copy(0, sl).wait())

    @pl.when(step >= 2)
    def _():
        drain(nu_ref[step - 2], slot)

    @pl.when(step == 0)
    def _():
        zero_scr[...] = jnp.zeros_like(zero_scr)

        def per_expert(e, c):
            lax.fori_loop(tail_lo_ref[e], tail_hi_ref[e], lambda g, cc: (zero_copy(g).start(), cc)[1], 0)
            lax.fori_loop(tail_lo_ref[e], tail_hi_ref[e], lambda g, cc: (zero_copy(g).wait(), cc)[1], 0)
            return c

        lax.fori_loop(0, n_exp, per_expert, 0)

    pos = pos_ref[...]
    h = h_ref[...]
    for r0 in range(0, rows, SORT_CHUNK):
        srow = lax.broadcasted_iota(I32, (SORT_CHUNK, td), 0) + r0
        onehot = jnp.zeros((SORT_CHUNK, td), F32)
        for k in range(pos.shape[0]):
            onehot = jnp.where(srow == pos[k:k + 1, :], 1.0, onehot)
        s_scr[slot, r0:r0 + SORT_CHUNK, :] = _dot(onehot.astype(BF16), h).astype(BF16)

    _for_each(nu_ref[step], lambda u: unit_copy(u, slot).start())

    @pl.when(step == last)
    def _():
        drain(nu_ref[step], slot)

        @pl.when(step >= 1)
        def _():
            drain(nu_ref[step - 1], 1 - slot)


def _dispatch(tail_lo, tail_hi, n_units, unit_dst, pos, h2, n_slots):
    t, d = h2.shape
    n_exp = tail_lo.shape[0]
    return pl.pallas_call(
        functools.partial(_dispatch_kernel, n_exp=n_exp),
        grid_spec=pltpu.PrefetchScalarGridSpec(
            num_scalar_prefetch=3, grid=(t // TD,),
            in_specs=[pl.BlockSpec((8, UNIT_LANES), lambda i, a, b, c: (0, i), memory_space=pltpu.SMEM),
                      pl.BlockSpec((TOP_K, TD), lambda i, a, b, c: (0, i)),
                      pl.BlockSpec((TD, d), lambda i, a, b, c: (i, 0))],
            out_specs=pl.BlockSpec(memory_space=pl.ANY),
            scratch_shapes=[pltpu.VMEM((2, _tile_rows(n_exp), d), BF16), pltpu.VMEM((UNIT, d), BF16),
                            pltpu.SemaphoreType.DMA((2,)), pltpu.SemaphoreType.DMA((1,))]),
        out_shape=jax.ShapeDtypeStruct((n_slots, d), BF16),
        compiler_params=_cparams(("arbitrary",)),
        name="moe_dispatch",
    )(tail_lo, tail_hi, n_units, unit_dst, pos, h2)


def _expert_kernel(blk0_ref, nblk_ref, tot_ref, wg_ref, wu_ref, wd_ref, xs_ref, ys_ref, xbuf, obuf, wgu_scr, wd_scr,
                   sem_in, sem_out):
    e = pl.program_id(0)
    n = nblk_ref[e]
    b0 = blk0_ref[e]
    de = wg_ref.shape[1]
    wgu_scr[:, :de] = wg_ref[...].astype(BF16)
    wgu_scr[:, de:] = wu_ref[...].astype(BF16)
    wd_scr[...] = wd_ref[...].astype(BF16)

    total = tot_ref[0]

    def rows(g):
        return pl.ds(pl.multiple_of(g * MOE_BLK, MOE_BLK), MOE_BLK)

    def in_copy(g):
        sl = g % EXPERT_BUFS
        return pltpu.make_async_copy(xs_ref.at[rows(g), :], xbuf.at[sl], sem_in.at[sl])

    def out_copy(g):
        sl = g % EXPERT_BUFS
        return pltpu.make_async_copy(obuf.at[sl], ys_ref.at[rows(g), :], sem_out.at[sl])

    @pl.when(e == 0)
    def _():
        for g in range(EXPERT_BUFS - 1):
            @pl.when(g < total)
            def _():
                in_copy(g).start()

    def block(j, c):
        g = b0 + j
        sl = g % EXPERT_BUFS
        in_copy(g).wait()

        @pl.when(g + (EXPERT_BUFS - 1) < total)
        def _():
            in_copy(g + (EXPERT_BUFS - 1)).start()

        @pl.when(g >= EXPERT_BUFS)
        def _():
            out_copy(g - EXPERT_BUFS).wait()

        gu = _dot(xbuf[sl], wgu_scr[...])
        act = _silu(gu[:, :de]) * gu[:, de:]
        obuf[sl] = _dot(act.astype(BF16), wd_scr[...]).astype(BF16)
        out_copy(g).start()
        return c

    lax.fori_loop(0, n, block, 0)

    @pl.when(e == pl.num_programs(0) - 1)
    def _():
        for back in range(1, EXPERT_BUFS + 1):
            @pl.when(total >= back)
            def _():
                out_copy(total - back).wait()


def _experts(blk0, nblk, n_blocks, xs, w_gate, w_up, w_down, layer):
    n_slots, d = xs.shape
    _, n_exp, _, de = w_gate.shape
    return pl.pallas_call(
        _expert_kernel,
        grid_spec=pltpu.PrefetchScalarGridSpec(
            num_scalar_prefetch=3, grid=(n_exp,),
            in_specs=[pl.BlockSpec((None, None, d, de), lambda e, a, b, c: (layer, e, 0, 0)),
                      pl.BlockSpec((None, None, d, de), lambda e, a, b, c: (layer, e, 0, 0)),
                      pl.BlockSpec((None, None, de, d), lambda e, a, b, c: (layer, e, 0, 0)),
                      pl.BlockSpec(memory_space=pl.ANY)],
            out_specs=pl.BlockSpec(memory_space=pl.ANY),
            scratch_shapes=[pltpu.VMEM((EXPERT_BUFS, MOE_BLK, d), BF16), pltpu.VMEM((EXPERT_BUFS, MOE_BLK, d), BF16),
                            pltpu.VMEM((d, 2 * de), BF16), pltpu.VMEM((de, d), BF16),
                            pltpu.SemaphoreType.DMA((EXPERT_BUFS,)), pltpu.SemaphoreType.DMA((EXPERT_BUFS,))]),
        out_shape=jax.ShapeDtypeStruct((n_slots, d), BF16),
        compiler_params=_cparams(("arbitrary",)),
        name="moe_experts",
    )(blk0, nblk, n_blocks, w_gate, w_up, w_down, xs)


def _combine_kernel(unit_ref, next_ref, pos_ref, w_ref, x_ref, h_ref, gate2_ref, sg_ref, su_ref, sd_ref, lng_ref,
                    lnb_ref, ys_ref, o_ref, s_scr, sem, *, alpha):
    rows = s_scr.shape[1]
    td = x_ref.shape[0]
    step = pl.program_id(0)
    slot = step % 2

    def unit_copy(tbl, u, sl):
        return pltpu.make_async_copy(ys_ref.at[_unit_rows(tbl[0, u]), :], s_scr.at[sl, _unit_rows(u), :], sem.at[sl])

    def fetch(tbl, sl):
        _for_each(tbl[1, 0], lambda u: unit_copy(tbl, u, sl).start())

    @pl.when(step == 0)
    def _():
        s_scr[...] = jnp.zeros_like(s_scr)
        fetch(unit_ref, 0)

    @pl.when(step + 1 < pl.num_programs(0))
    def _():
        fetch(next_ref, 1 - slot)

    h = h_ref[...]
    acc = _dot((_silu(_dot(h, sg_ref[...])) * _dot(h, su_ref[...])).astype(BF16), sd_ref[...])
    pos = pos_ref[...]
    w = w_ref[...]
    _for_each(unit_ref[1, 0], lambda u: unit_copy(unit_ref, 0, slot).wait())
    for r0 in range(0, rows, SORT_CHUNK):
        lane = lax.broadcasted_iota(I32, (td, SORT_CHUNK), 1) + r0
        wmat = jnp.zeros((td, SORT_CHUNK), F32)
        for k in range(pos.shape[1]):
            wmat = jnp.where(lane == pos[:, k:k + 1], w[:, k:k + 1], wmat)
        w_hi = wmat.astype(BF16)
        w_lo = (wmat - w_hi.astype(F32)).astype(BF16)
        both = _dot(jnp.concatenate([w_hi, w_lo], axis=0), s_scr[slot, r0:r0 + SORT_CHUNK, :])
        acc = acc + (both[:td] + both[td:])
    o_ref[...] = _ln(alpha * x_ref[...] + gate2_ref[...] * acc) * lng_ref[...] + lnb_ref[...]


def _combine(unit_dst, pos_tok, w_tok, xn, h2, modp, sh_gate, sh_up, sh_down, ln_g, ln_b, ys, seg_td, *,
             alpha, n_exp):
    t, d = xn.shape
    k = pos_tok.shape[1]
    nt = t // TD
    tok = pl.BlockSpec((TD, d), lambda i: (i, 0))
    full = lambda a: pl.BlockSpec(a.shape, lambda i: (0,) * a.ndim)
    return pl.pallas_call(
        functools.partial(_combine_kernel, alpha=alpha),
        grid=(nt,),
        in_specs=[pl.BlockSpec((8, UNIT_LANES), lambda i: (0, i), memory_space=pltpu.SMEM),
                  pl.BlockSpec((8, UNIT_LANES), lambda i: (0, jnp.minimum(i + 1, nt - 1)), memory_space=pltpu.SMEM),
                  pl.BlockSpec((TD, k), lambda i: (i, 0)),
                  pl.BlockSpec((TD, k), lambda i: (i, 0)),
                  tok, tok,
                  pl.BlockSpec((None, None, 1, d), lambda i: (5, seg_td(i), 0, 0)),
                  full(sh_gate), full(sh_up), full(sh_down), full(ln_g), full(ln_b),
                  pl.BlockSpec(memory_space=pl.ANY)],
        out_specs=tok,
        out_shape=jax.ShapeDtypeStruct((t, d), F32),
        scratch_shapes=[pltpu.VMEM((2, _tile_rows(n_exp), d), BF16), pltpu.SemaphoreType.DMA((2,))],
        compiler_params=_cparams(("arbitrary",)),
        name="moe_combine",
    )(unit_dst, unit_dst, pos_tok, w_tok, xn, h2, modp, sh_gate, sh_up, sh_down, ln_g, ln_b, ys)


def _moe(xn, h2, scores_t, modp, seg_td, router_bias, w_gate, w_up, w_down, sh_gate, sh_up, sh_down,
         ln_g, ln_b, tri, ltri, *, alpha, layer):
    t = xn.shape[0]
    n_exp = scores_t.shape[0]
    assert _tile_rows(n_exp) % SORT_CHUNK == 0 and _tile_rows(n_exp) <= UNIT * UNIT_LANES
    pos, w_sel, units, tot = _route(scores_t, router_bias.reshape(n_exp, 1), tri, ltri)
    used = tot[:, 0].astype(I32)
    region = (used + MOE_BLK - 1) // MOE_BLK * MOE_BLK
    region_end = jnp.cumsum(region)
    region_start = region_end - region
    n_slots = TOP_K * t + n_exp * UNIT * (t // TD) + n_exp * MOE_BLK
    unit_dst = _unit_dest((region_start // UNIT).astype(I32), units)
    xs = _dispatch(((region_start + used) // UNIT).astype(I32), (region_end // UNIT).astype(I32),
                   unit_dst[1, ::UNIT_LANES], unit_dst, pos, h2, n_slots)
    ys = _experts((region_start // MOE_BLK).astype(I32), (region // MOE_BLK).astype(I32),
                  (region_end[-1:] // MOE_BLK).astype(I32), xs, w_gate, w_up, w_down, layer)
    return _combine(unit_dst, pos.T, w_sel.T, xn, h2, modp, sh_gate, sh_up, sh_down, ln_g, ln_b, ys, seg_td,
                    alpha=alpha, n_exp=n_exp)


def kernel(x, c, ctx, c_ctx, w_mod, b_mod, w_in, b_in, lam_q1, lam_k1, lam_q2, lam_k2, attn_norm_g, sg_ln_g, sg_ln_b, sg_w, sg_b, conv_w, conv_b, lru_wa, lru_ba, lru_wx, lru_bx, lru_lam, w_branch, w_out, ln1_g, ln1_b, w_router, router_bias, moe_w_gate, moe_w_up, moe_w_down, sh_w_gate, sh_w_up, sh_w_down, ln2_g, ln2_b):
    batch, n_lat, d = x.shape
    n_ctx = ctx.shape[1]
    depth = w_mod.shape[0]
    n_exp = w_router.shape[2]
    t_lat = batch * n_lat
    t_ctx = batch * n_ctx
    t_all = t_lat + t_ctx
    assert n_lat % TM == 0 and t_ctx % TM == 0 and n_ctx % B_CHUNK == 0 and t_lat % n_ctx == 0
    assert batch + 1 <= MOD_ROWS and TM % TD == 0
    alpha = (2 * depth) ** 0.25
    cw = conv_w.shape[2]
    tiles_per_batch = n_lat // TM

    seg = lambda i: jnp.minimum(i // tiles_per_batch, batch)
    seg_td = lambda i: jnp.minimum(i // (n_lat // TD), batch)
    tm_in = 2 * TM if (n_lat % (2 * TM) == 0 and t_ctx % (2 * TM) == 0) else TM
    seg_in = lambda i: jnp.minimum(i // (n_lat // tm_in), batch)
    tbl_idx = lambda i: jnp.where(i < t_lat // tm_in, i % (n_lat // tm_in), n_lat // tm_in)

    x_all = jnp.concatenate([x.reshape(t_lat, d), ctx.reshape(t_ctx, d)], axis=0)
    c_all = jnp.zeros((MOD_ROWS, d), F32).at[:batch].set(c).at[batch].set(c_ctx)
    tables = _rope_tables(n_lat, tm_in)
    tri = (jnp.arange(TD)[:, None] <= jnp.arange(TD)[None, :]).astype(BF16)
    ltri = (jnp.arange(n_exp)[None, :] < jnp.arange(n_exp)[:, None]).astype(BF16)
    row = lambda v: v.reshape(1, -1)

    def dense_blocks(w):
        nb, bi, bj = w.shape
        eye = jnp.eye(nb, dtype=w.dtype)
        return (w[:, :, None, :] * eye[:, None, :, None]).reshape(nb * bi, nb * bj)

    for l in range(depth):
        last = l == depth - 1
        lam_init = 0.8 - 0.6 * math.exp(-0.3 * l)
        mod = _mod_rows(c_all, w_mod[l].astype(BF16), row(b_mod[l]))
        modp = mod.reshape(MOD_ROWS, 6, 1, d).transpose(1, 0, 2, 3)

        qr, kr, vb, proj = _in_proj(x_all, modp, w_in[l].astype(BF16), row(b_in[l]), tables, seg_in, tbl_idx, tm_in)

        lam_vecs = jnp.stack([lam_q1[l], lam_k1[l], lam_q2[l], lam_k2[l]])
        gain = attn_norm_g[l].reshape(A_HEADS, 1, A_V_DIM)
        attn = functools.partial(_attention, qr, kr, vb, lam_vecs, gain, lam_init=lam_init,
                                 batch=batch, n_lat=n_lat, n_ctx=n_ctx)
        n_rows = t_lat if last else t_all
        o_a = attn(latent_queries=True, out_rows=n_rows)
        if not last:
            o_a = attn(latent_queries=False, out_rows=n_rows, into=o_a)

        o_b = _spatial_gating(proj, row(sg_ln_g[l]), row(sg_ln_b[l]), sg_w[l].astype(BF16),
                              sg_b[l].reshape(B_GROUPS, B_CHUNK, 1))

        lru = functools.partial(_lru_pass, proj, conv_w=conv_w[l], conv_b=row(conv_b[l]), batch=batch)
        zeros_h = jnp.zeros((batch, 8, cw), F32)
        hf = {}
        for direction in range(2):
            wd = jnp.concatenate([dense_blocks(lru_wa[l, direction]), dense_blocks(lru_wx[l, direction])],
                                 axis=1).astype(BF16)
            par = dict(w_dense=wd, ba=row(lru_ba[l, direction]), bx=row(lru_bx[l, direction]),
                       lam=row(lru_lam[l, direction]), reverse=direction == 1, final=direction == 1)
            shared = direction == 1 and not last
            h_ctx, edge = lru(hf.get("ctx"), zeros_h, seq=n_ctx, tl=n_ctx, row0=t_lat,
                              out_rows=t_all if shared else None, out_row0=t_lat if shared else 0, **par)
            h_lat, _ = lru(hf.get("lat"), edge, seq=n_lat, tl=TM, row0=0,
                           out_rows=t_all if shared else None, into=h_ctx if shared else None, **par)
            hf = {"ctx": h_ctx, "lat": h_lat}
        o_c = hf["lat"]
        wr_t = w_router[l].T
        wr_hi = wr_t.astype(BF16)
        wr_lo = (wr_t - wr_hi.astype(F32)).astype(BF16)
        xn, h2, scores_t = _merge(x_all, o_a, o_b, o_c, proj, modp, w_branch[l].astype(BF16),
                                  w_out[l].astype(BF16), row(ln1_g[l]), row(ln1_b[l]), wr_hi, wr_lo, seg,
                                  n_rows=n_rows, alpha=alpha)
        x_all = _moe(xn, h2, scores_t, modp, seg_td, router_bias[l], moe_w_gate,
                     moe_w_up, moe_w_down, sh_w_gate[l].astype(BF16),
                     sh_w_up[l].astype(BF16), sh_w_down[l].astype(BF16), row(ln2_g[l]), row(ln2_b[l]), tri, ltri,
                     alpha=alpha, layer=l)
    return x_all[:t_lat].reshape(batch, n_lat, d)
```

```python
import functools
import math

import jax
import jax.numpy as jnp
from jax import lax
from jax.experimental import pallas as pl
from jax.experimental.pallas import tpu as pltpu

F32 = jnp.float32
BF16 = jnp.bfloat16
I32 = jnp.int32

A_HEADS = 4
A_QK_DIM = 64
A_V_DIM = 2 * A_QK_DIM
GRID_W = 64
ROPE_THETA = 10000.0
B_CHUNK = 128
B_GROUPS = 4
C_BLOCKS = 8
C_POW = 8.0
TOP_K = 8
N_GROUPS = 8
TOPK_GROUPS = 4
ROUTED_SCALE = 2.5
LN_EPS = 1e-6
RMS_EPS = 1e-5

TM = 512
MOE_BLK = 512
TD = 256
UNIT = 16
UNIT_LANES = 256
SORT_CHUNK = 512
W_LANES = 128
EXPERT_BUFS = 4
HALO = 16
ATTN_HEADS = 4
ATTN_SUB = 2048
MOD_ROWS = 16
VMEM_LIMIT = 56 * 1024 * 1024


def _cparams(sem):
    return pltpu.CompilerParams(dimension_semantics=sem, vmem_limit_bytes=VMEM_LIMIT)


def _ln(x):
    mu = jnp.mean(x, axis=-1, keepdims=True)
    xc = x - mu
    var = jnp.mean(xc * xc, axis=-1, keepdims=True)
    return xc * lax.rsqrt(var + LN_EPS)


def _gelu(x):
    cdf = 0.5 * (1.0 + jnp.tanh(math.sqrt(2.0 / math.pi) * (x + 0.044715 * (x * x * x))))
    return x * cdf


def _sigmoid(x):
    return 1.0 / (1.0 + jnp.exp(-x))


def _silu(x):
    return x * _sigmoid(x)


def _dot(a, b):
    return jnp.dot(a, b, preferred_element_type=F32)


def _dot_nt(a, b):
    return lax.dot_general(a, b, (((1,), (1,)), ((), ())), preferred_element_type=F32)


def _mod_kernel(c_ref, w_ref, b_ref, o_ref):
    o_ref[...] = _dot(_silu(c_ref[...]).astype(BF16), w_ref[...]) + b_ref[...]


def _mod_rows(c_all, w, b):
    m, d = c_all.shape
    n = w.shape[1]
    tn = 1536
    return pl.pallas_call(
        _mod_kernel,
        grid=(n // tn,),
        in_specs=[pl.BlockSpec((m, d), lambda j: (0, 0)),
                  pl.BlockSpec((d, tn), lambda j: (0, j)),
                  pl.BlockSpec((1, tn), lambda j: (0, j))],
        out_specs=pl.BlockSpec((m, tn), lambda j: (0, j)),
        out_shape=jax.ShapeDtypeStruct((m, n), F32),
        compiler_params=_cparams(("parallel",)),
        name="adaln_rows",
    )(c_all, w, b)


REST_TILE = 1280
QKV_TILES = 3
COL_U, COL_S, COL_X, COL_Y, COL_G = 0, 1, 2, 3, 4


def _in_proj_kernel(x_ref, sh_ref, sc_ref, w_ref, b_ref, cos_ref, sa_ref, sb_ref,
                    q_ref, k_ref, v_ref, h_ref):
    j = pl.program_id(1)

    @pl.when(j == 0)
    def _():
        h = _ln(x_ref[...]) * (1.0 + sc_ref[...]) + sh_ref[...]
        h_ref[...] = h.astype(BF16)

    acc = _dot(h_ref[...], w_ref[...]) + b_ref[...]

    def rope(dst_ref, scale):
        hw = cos_ref.shape[1]
        for c0 in range(0, acc.shape[1], hw):
            x = acc[:, c0:c0 + hw]
            r = x * cos_ref[...] + pltpu.roll(x, hw - 16, 1) * sa_ref[...] + pltpu.roll(x, 16, 1) * sb_ref[...]
            dst_ref[:, c0:c0 + hw] = (r * scale).astype(BF16)

    @pl.when(j == 0)
    def _():
        rope(q_ref, A_QK_DIM ** -0.5 * math.log2(math.e))

    @pl.when(j == 1)
    def _():
        rope(k_ref, 1.0)

    @pl.when(j == 2)
    def _():
        v_ref[...] = acc.astype(BF16)


def _rest_proj_kernel(h_ref, w_ref, b_ref, o_ref):
    o_ref[...] = (_dot(h_ref[...], w_ref[...]) + b_ref[...]).astype(o_ref.dtype)


def _in_proj(x_all, modp, w, b, tables, seg, tbl_idx, tm):
    t, d = x_all.shape
    tn = 2 * A_HEADS * A_QK_DIM
    n_qkv = QKV_TILES * tn
    n_rest = w.shape[1] - n_qkv
    cos, sa, sb = tables
    tbl = pl.BlockSpec((tm, cos.shape[1]), lambda i, j: (tbl_idx(i), 0))
    qkv = pl.BlockSpec((tm, tn), lambda i, j: (i, 0))
    qkv_shape = jax.ShapeDtypeStruct((t, tn), BF16)
    q, k, v, h = pl.pallas_call(
        _in_proj_kernel,
        grid=(t // tm, QKV_TILES),
        in_specs=[pl.BlockSpec((tm, d), lambda i, j: (i, 0)),
                  pl.BlockSpec((None, None, 1, d), lambda i, j: (0, seg(i), 0, 0)),
                  pl.BlockSpec((None, None, 1, d), lambda i, j: (1, seg(i), 0, 0)),
                  pl.BlockSpec((d, tn), lambda i, j: (0, j)),
                  pl.BlockSpec((1, tn), lambda i, j: (0, j)),
                  tbl, tbl, tbl],
        out_specs=[qkv, qkv, qkv, pl.BlockSpec((tm, d), lambda i, j: (i, 0))],
        out_shape=[qkv_shape, qkv_shape, qkv_shape, jax.ShapeDtypeStruct((t, d), BF16)],
        compiler_params=_cparams(("parallel", "arbitrary")),
        name="in_proj_qkv",
    )(x_all, modp, modp, w, b, cos, sa, sb)
    tr = REST_TILE if n_rest % REST_TILE == 0 else tn
    rest = pl.pallas_call(
        _rest_proj_kernel,
        grid=(t // tm, n_rest // tr),
        in_specs=[pl.BlockSpec((tm, d), lambda i, j: (i, 0)),
                  pl.BlockSpec((d, tr), lambda i, j: (0, j)),
                  pl.BlockSpec((1, tr), lambda i, j: (0, j))],
        out_specs=pl.BlockSpec((tm, tr), lambda i, j: (i, j)),
        out_shape=jax.ShapeDtypeStruct((t, n_rest), BF16),
        compiler_params=_cparams(("parallel", "parallel")),
        name="in_proj_rest",
    )(h, w[:, n_qkv:], b[:, n_qkv:])
    return q, k, v, rest


def _rope_tables(n, tm):
    rows = n // GRID_W
    pos_row = jnp.repeat(jnp.arange(rows), GRID_W).astype(F32)
    pos_col = jnp.tile(jnp.arange(GRID_W), rows).astype(F32)
    quarter = A_QK_DIM // 4
    inv = ROPE_THETA ** (-jnp.arange(quarter, dtype=F32) / quarter)
    ang_r = pos_row[:, None] * inv
    ang_c = pos_col[:, None] * inv
    ang = jnp.concatenate([ang_r, ang_r, ang_c, ang_c], axis=-1)
    cos = jnp.cos(ang)
    sin = jnp.sin(ang)
    first_half = (jnp.arange(A_QK_DIM) % (2 * quarter)) < quarter
    sa = jnp.where(first_half, -sin, 0.0)
    sb = jnp.where(first_half, 0.0, sin)

    def widen(tbl, fill):
        tbl = jnp.tile(tbl, (1, 2))
        return jnp.concatenate([tbl, jnp.full((tm, tbl.shape[1]), fill, F32)], axis=0)

    return widen(cos, 1.0), widen(sa, 0.0), widen(sb, 0.0)


def _attn_kernel(*refs, lam_init, with_lat, sub):
    if with_lat:
        lam_ref, gain_ref, q_ref, kc_ref, vc_ref, kl_ref, vl_ref, o_ref = refs
    else:
        lam_ref, gain_ref, q_ref, kc_ref, vc_ref, _, o_ref = refs
    lv = lam_ref[...]
    lam = (jnp.exp(jnp.sum(lv[0:1] * lv[1:2], axis=-1, keepdims=True))
           - jnp.exp(jnp.sum(lv[2:3] * lv[3:4], axis=-1, keepdims=True)) + lam_init)
    chunks = [(kc_ref, vc_ref, 0, kc_ref.shape[0])]
    if with_lat:
        ck = min(sub, kl_ref.shape[0])
        chunks += [(kl_ref, vl_ref, r0, ck) for r0 in range(0, kl_ref.shape[0], ck)]
    for hh in range(gain_ref.shape[0]):
        head = slice(hh * A_V_DIM, (hh + 1) * A_V_DIM)
        probs, dens = [], []
        for c in range(2):
            cols = slice(hh * A_V_DIM + c * A_QK_DIM, hh * A_V_DIM + (c + 1) * A_QK_DIM)
            qc = q_ref[:, cols]
            scores = [_dot_nt(qc, k_ref[r0:r0 + n, cols]) for k_ref, _, r0, n in chunks]
            m = None
            for s in scores:
                ms = jnp.max(s, axis=-1, keepdims=True)
                m = ms if m is None else jnp.maximum(m, ms)
            p = [jnp.exp2(s - m) for s in scores]
            den = None
            for pj in p:
                ds = jnp.sum(pj, axis=-1, keepdims=True)
                den = ds if den is None else den + ds
            probs.append(p)
            dens.append(den)
        r1 = 1.0 / dens[0]
        r2 = lam / dens[1]
        o = None
        for p1, p2, (_, v_ref, r0, n) in zip(probs[0], probs[1], chunks):
            w = p1 * r1 - p2 * r2
            ps = _dot(w.astype(BF16), v_ref[r0:r0 + n, head])
            o = ps if o is None else o + ps
        o = o * lax.rsqrt(jnp.mean(o * o, axis=-1, keepdims=True) + RMS_EPS)
        o_ref[:, head] = (o * gain_ref[hh]) * (1.0 - lam_init)


def _attention(qr, kr, vb, lam_vecs, gain, *, lam_init, batch, n_lat, n_ctx, latent_queries, out_rows, into=None):
    hw = ATTN_HEADS * A_V_DIM
    ctx_blk0 = batch * n_lat // n_ctx
    lam_spec = pl.BlockSpec(lam_vecs.shape, lambda b, h, i: (0, 0))
    gain_spec = pl.BlockSpec((ATTN_HEADS, 1, A_V_DIM), lambda b, h, i: (h, 0, 0))
    kv_ctx = pl.BlockSpec((n_ctx, hw), lambda b, h, i: (ctx_blk0 + b, h))
    if latent_queries:
        tq = 256
        nq = n_lat // tq
        q_spec = pl.BlockSpec((tq, hw), lambda b, h, i: (b * nq + i, h))
        kv_lat = pl.BlockSpec((n_lat, hw), lambda b, h, i: (b, h))
        in_specs = [lam_spec, gain_spec, q_spec, kv_ctx, kv_ctx, kv_lat, kv_lat]
        args = (lam_vecs, gain, qr, kr, vb, kr, vb)
        out_blk0 = 0
        aliases = {}
    else:
        tq = n_ctx
        nq = 1
        q_spec = pl.BlockSpec((tq, hw), lambda b, h, i: (ctx_blk0 + b, h))
        in_specs = [lam_spec, gain_spec, q_spec, kv_ctx, kv_ctx, pl.BlockSpec(memory_space=pl.ANY)]
        args = (lam_vecs, gain, qr, kr, vb, into)
        out_blk0 = ctx_blk0
        aliases = {len(args) - 1: 0}
    return pl.pallas_call(
        functools.partial(_attn_kernel, lam_init=lam_init, with_lat=latent_queries, sub=ATTN_SUB),
        grid=(batch, A_HEADS // ATTN_HEADS, nq),
        in_specs=in_specs,
        out_specs=pl.BlockSpec((tq, hw), lambda b, h, i: (out_blk0 + b * nq + i, h)),
        out_shape=jax.ShapeDtypeStruct((out_rows, A_HEADS * A_V_DIM), F32),
        input_output_aliases=aliases,
        compiler_params=_cparams(("parallel", "parallel", "arbitrary")),
        name="diff_attn_lat" if latent_queries else "diff_attn_ctx",
    )(*args)


def _sg_kernel(u_ref, s_ref, g_ref, b_ref, w_ref, bs_ref, o_ref):
    gw = B_CHUNK
    for ci in range(TM // B_CHUNK):
        rows = slice(ci * B_CHUNK, (ci + 1) * B_CHUNK)
        for g in range(B_GROUPS):
            cols = slice(g * gw, (g + 1) * gw)
            vn = _ln(_gelu(s_ref[rows, cols].astype(F32))) * g_ref[:, cols] + b_ref[:, cols]
            mixed = _dot(w_ref[g], vn.astype(BF16)) + bs_ref[g]
            o_ref[rows, cols] = _gelu(u_ref[rows, cols].astype(F32)) * mixed


def _spatial_gating(proj, ln_g, ln_b, w_s, b_s):
    t = proj.shape[0]
    w = B_GROUPS * B_CHUNK
    full = lambda a: pl.BlockSpec(a.shape, lambda i: (0,) * a.ndim)
    return pl.pallas_call(
        _sg_kernel,
        grid=(t // TM,),
        in_specs=[pl.BlockSpec((TM, w), lambda i: (i, COL_U)),
                  pl.BlockSpec((TM, w), lambda i: (i, COL_S)),
                  full(ln_g), full(ln_b), full(w_s), full(b_s)],
        out_specs=pl.BlockSpec((TM, w), lambda i: (i, 0)),
        out_shape=jax.ShapeDtypeStruct((t, w), F32),
        compiler_params=_cparams(("parallel",)),
        name="spatial_gating",
    )(proj, proj, ln_g, ln_b, w_s, b_s)


def _lru_kernel(*refs, reverse, final, n_tiles, tl, aliased):
    if aliased:
        refs = refs[:12] + refs[13:]
    if final:
        (x_ref, xp_ref, xn_ref, cw_ref, cb_ref, w_ref, ba_ref, bx_ref, lam_ref, h0_ref,
         hf_ref, y_ref, o_ref, hl_ref, carry) = refs
    else:
        (x_ref, xp_ref, xn_ref, cw_ref, cb_ref, w_ref, ba_ref, bx_ref, lam_ref, h0_ref,
         o_ref, hl_ref, carry) = refs
    step = pl.program_id(1)

    @pl.when(step == 0)
    def _():
        carry[...] = h0_ref[...]

    ti = (n_tiles - 1 - step) if reverse else step
    x = x_ref[...].astype(F32)
    cw = x.shape[-1]
    row = lax.broadcasted_iota(I32, x.shape, 0)
    has_prev = ti > 0
    has_next = ti < n_tiles - 1
    p6 = jnp.where(has_prev, xp_ref[HALO - 2:HALO - 1, :].astype(F32), 0.0)
    p7 = jnp.where(has_prev, xp_ref[HALO - 1:HALO, :].astype(F32), 0.0)
    n0 = jnp.where(has_next, xn_ref[0:1, :].astype(F32), 0.0)
    xm1 = jnp.where(row == 0, p7, pltpu.roll(x, 1, 0))
    xm2 = jnp.where(row == 0, p6, jnp.where(row == 1, p7, pltpu.roll(x, 2, 0)))
    xp1 = jnp.where(row == tl - 1, n0, pltpu.roll(x, tl - 1, 0))
    taps = cw_ref[...]
    xs = taps[0:1] * xm2 + taps[1:2] * xm1 + taps[2:3] * x + taps[3:4] * xp1 + cb_ref[...]

    z = _dot(xs.astype(BF16), w_ref[...])
    r = _sigmoid(z[:, :cw] + ba_ref[...])
    ig = _sigmoid(z[:, cw:] + bx_ref[...])
    nl = -lam_ref[...]
    softplus = jnp.maximum(nl, 0.0) + jnp.log(1.0 + jnp.exp(-jnp.abs(nl)))
    log_a = -C_POW * r * softplus
    a = jnp.exp(log_a)
    u = jnp.sqrt(1.0 - jnp.exp(2.0 * log_a)) * (ig * xs)

    d = 1
    while d < tl:
        if reverse:
            ok = row < tl - d
            shift = tl - d
        else:
            ok = row >= d
            shift = d
        a_sh = jnp.where(ok, pltpu.roll(a, shift, 0), 1.0)
        u_sh = jnp.where(ok, pltpu.roll(u, shift, 0), 0.0)
        u = u + a * u_sh
        a = a * a_sh
        d *= 2
    h = u + a * carry[0:1, :]
    edge = h[0:1, :] if reverse else h[tl - 1:tl, :]
    carry[...] = jnp.broadcast_to(edge, carry.shape)
    hl_ref[...] = jnp.broadcast_to(edge, hl_ref.shape)
    if final:
        o_ref[...] = _gelu(y_ref[...].astype(F32)) * (hf_ref[...] + h)
    else:
        o_ref[...] = h


def _lru_pass(proj, hf, h0, conv_w, conv_b, w_dense, ba, bx, lam, *, batch, seq, tl, row0, reverse, final,
              out_rows=None, out_row0=0, into=None):
    t_all = proj.shape[0]
    out_rows = batch * seq if out_rows is None else out_rows
    out_blk0 = out_row0 // tl
    assert into is None or final
    cw = conv_w.shape[1]
    n_tiles = seq // tl
    base = row0 // tl
    sub = tl // HALO
    last_halo = t_all // HALO - 1

    def tile(b, s):
        ti = (n_tiles - 1 - s) if reverse else s
        return base + b * n_tiles + ti

    def loc(b, s):
        ti = (n_tiles - 1 - s) if reverse else s
        return b * n_tiles + ti

    full = lambda a: pl.BlockSpec(a.shape, lambda b, s: (0,) * a.ndim)
    in_specs = [pl.BlockSpec((tl, cw), lambda b, s: (tile(b, s), COL_X)),
                pl.BlockSpec((HALO, cw), lambda b, s: (jnp.maximum(tile(b, s) * sub - 1, 0), COL_X)),
                pl.BlockSpec((HALO, cw), lambda b, s: (jnp.minimum((tile(b, s) + 1) * sub, last_halo), COL_X)),
                full(conv_w), full(conv_b), full(w_dense), full(ba), full(bx), full(lam),
                pl.BlockSpec((None, 8, cw), lambda b, s: (b, 0, 0))]
    args = [proj, proj, proj, conv_w, conv_b, w_dense, ba, bx, lam, h0]
    if final:
        in_specs += [pl.BlockSpec((tl, cw), lambda b, s: (loc(b, s), 0)),
                     pl.BlockSpec((tl, cw), lambda b, s: (tile(b, s), COL_Y))]
        args += [hf, proj]
    aliases = {}
    if into is not None:
        aliases = {len(args): 0}
        in_specs += [pl.BlockSpec(memory_space=pl.ANY)]
        args += [into]
    return pl.pallas_call(
        functools.partial(_lru_kernel, reverse=reverse, final=final, n_tiles=n_tiles, tl=tl,
                          aliased=into is not None),
        grid=(batch, n_tiles),
        in_specs=in_specs,
        out_specs=[pl.BlockSpec((tl, cw), lambda b, s: (out_blk0 + loc(b, s), 0)),
                   pl.BlockSpec((None, 8, cw), lambda b, s: (b, 0, 0))],
        out_shape=[jax.ShapeDtypeStruct((out_rows, cw), F32),
                   jax.ShapeDtypeStruct((batch, 8, cw), F32)],
        scratch_shapes=[pltpu.VMEM((8, cw), F32)],
        input_output_aliases=aliases,
        compiler_params=_cparams(("parallel", "arbitrary")),
        name="rglru_%s_%s" % ("bwd" if reverse else "fwd", "lat" if row0 == 0 else "ctx"),
    )(*args)


def _merge_kernel(x_ref, oa_ref, ob_ref, oc_ref, g0, g1, g2, g3, g4, g5, wb_ref, wo_ref,
                  gate1_ref, sh2_ref, sc2_ref, lng_ref, lnb_ref, wrh_ref, wrl_ref,
                  xn_ref, h2_ref, sc_ref, *, alpha):
    gates = ((g0, g1), (g2, g3), (g4, g5))
    mix = None
    for r, o_ref in enumerate((oa_ref, ob_ref, oc_ref)):
        proj = _dot(o_ref[...].astype(BF16), wb_ref[r])
        gate = jnp.concatenate([_sigmoid(gates[r][0][...].astype(F32)), _sigmoid(gates[r][1][...].astype(F32))],
                               axis=-1)
        term = gate * proj
        mix = term if mix is None else mix + term
    out = _dot(mix.astype(BF16), wo_ref[...])
    xn = _ln(alpha * x_ref[...] + gate1_ref[...] * out) * lng_ref[...] + lnb_ref[...]
    xn_ref[...] = xn
    h2 = _ln(xn) * (1.0 + sc2_ref[...]) + sh2_ref[...]
    h_hi = h2.astype(BF16)
    h2_ref[...] = h_hi
    h_lo = (h2 - h_hi.astype(F32)).astype(BF16)
    logits = _dot_nt(wrh_ref[...], h_hi) + (_dot_nt(wrh_ref[...], h_lo) + _dot_nt(wrl_ref[...], h_hi))
    sc_ref[...] = _sigmoid(logits)


def _merge(x_all, o_a, o_b, o_c, proj, modp, w_branch, w_out, ln_g, ln_b, wr_hi, wr_lo, seg, *, n_rows, alpha):
    d = x_all.shape[1]
    bw = o_a.shape[1]
    n_exp = wr_hi.shape[0]
    tok = lambda w: pl.BlockSpec((TM, w), lambda i: (i, 0))
    gate = lambda c: pl.BlockSpec((TM, bw), lambda i, c=c: (i, c))
    mod = lambda k: pl.BlockSpec((None, None, 1, d), lambda i, k=k: (k, seg(i), 0, 0))
    full = lambda a: pl.BlockSpec(a.shape, lambda i: (0,) * a.ndim)
    return pl.pallas_call(
        functools.partial(_merge_kernel, alpha=alpha),
        grid=(n_rows // TM,),
        in_specs=[tok(d), tok(bw), tok(bw), tok(bw)] + [gate(COL_G + c) for c in range(6)]
                 + [full(w_branch), full(w_out), mod(2), mod(3), mod(4), full(ln_g), full(ln_b),
                    full(wr_hi), full(wr_lo)],
        out_specs=[tok(d), tok(d), pl.BlockSpec((n_exp, TM), lambda i: (0, i))],
        out_shape=[jax.ShapeDtypeStruct((n_rows, d), F32),
                   jax.ShapeDtypeStruct((n_rows, d), BF16),
                   jax.ShapeDtypeStruct((n_exp, n_rows), F32)],
        compiler_params=_cparams(("parallel",)),
        name="merge_residual_router",
    )(x_all, o_a, o_b, o_c, proj, proj, proj, proj, proj, proj, w_branch, w_out,
      modp, modp, modp, ln_g, ln_b, wr_hi, wr_lo)


def _tile_rows(n_exp):
    return TOP_K * TD + n_exp * UNIT


def _route_kernel(s_ref, bias_ref, tri_ref, ltri_ref, pos_ref, w_ref, rs_ref, rn_ref, unit_ref, tot_ref, carry):
    @pl.when(pl.program_id(0) == 0)
    def _():
        carry[...] = jnp.zeros_like(carry)

    s = s_ref[...]
    n_exp, tn = s.shape
    per = n_exp // N_GROUPS
    neg = -jnp.inf
    biased = s + bias_ref[...]
    sub = lax.broadcasted_iota(I32, (per, tn), 0)
    gs_rows = []
    for g in range(N_GROUPS):
        blk = biased[g * per:(g + 1) * per, :]
        m1 = jnp.max(blk, axis=0, keepdims=True)
        first = jnp.min(jnp.where(blk == m1, sub, per), axis=0, keepdims=True)
        m2 = jnp.max(jnp.where(sub == first, neg, blk), axis=0, keepdims=True)
        gs_rows.append(m1 + m2)
    gs = jnp.concatenate(gs_rows, axis=0)
    gi = lax.broadcasted_iota(I32, gs.shape, 0)
    g_ok = jnp.zeros(gs.shape, F32)
    cur = gs
    for _ in range(TOPK_GROUPS):
        m = jnp.max(cur, axis=0, keepdims=True)
        pick = jnp.min(jnp.where(cur == m, gi, N_GROUPS), axis=0, keepdims=True)
        hit = gi == pick
        g_ok = jnp.where(hit, 1.0, g_ok)
        cur = jnp.where(hit, neg, cur)
    ok_rows = [jnp.broadcast_to(g_ok[g:g + 1, :], (per, tn)) for g in range(N_GROUPS)]
    expert_ok = jnp.concatenate(ok_rows, axis=0)
    masked = jnp.where(expert_ok > 0.0, biased, neg)
    ei = lax.broadcasted_iota(I32, s.shape, 0)
    pick_s, hits = [], []
    sel = jnp.zeros(s.shape, F32)
    for _ in range(TOP_K):
        m = jnp.max(masked, axis=0, keepdims=True)
        pick = jnp.min(jnp.where(masked == m, ei, n_exp), axis=0, keepdims=True)
        hit = ei == pick
        hits.append(hit)
        pick_s.append(jnp.sum(jnp.where(hit, s, 0.0), axis=0, keepdims=True))
        sel = jnp.where(hit, 1.0, sel)
        masked = jnp.where(hit, neg, masked)
    tot = pick_s[0]
    for k in range(1, TOP_K):
        tot = tot + pick_s[k]
    w = jnp.where(sel > 0.0, s, 0.0) / tot * ROUTED_SCALE
    w_hi = w.astype(BF16)
    w_ref[...] = jnp.concatenate([w_hi, (w - w_hi.astype(F32)).astype(BF16)], axis=0)

    incl = _dot(sel.astype(BF16), tri_ref[...])
    count = incl[:, tn - 1:tn]
    run = jnp.floor((count + (UNIT - 1)) * (1.0 / UNIT)) * UNIT
    run_start = _dot(ltri_ref[...], jnp.broadcast_to(run, (n_exp, 128)).astype(BF16))[:, 0:1]
    row_in_tile = incl - sel + run_start
    pos = [jnp.sum(jnp.where(hit, row_in_tile, 0.0), axis=0, keepdims=True) for hit in hits]
    pos_ref[...] = jnp.concatenate(pos, axis=0).astype(I32)
    rs_ref[...] = jnp.broadcast_to(run_start, rs_ref.shape)
    rn_ref[...] = jnp.broadcast_to(run, rn_ref.shape)

    region_used = carry[:, 0:1]
    u = lax.broadcasted_iota(I32, (n_exp, UNIT_LANES), 1).astype(F32)
    eu = lax.broadcasted_iota(I32, (n_exp, UNIT_LANES), 0).astype(F32)
    u0 = run_start * (1.0 / UNIT)
    nu = run * (1.0 / UNIT)
    inside = jnp.logical_and(u >= u0, u < u0 + nu)
    rel = jnp.sum(jnp.where(inside, u - u0 + region_used * (1.0 / UNIT), 0.0), axis=0, keepdims=True)
    exp_of = jnp.sum(jnp.where(inside, eu, 0.0), axis=0, keepdims=True)
    n_units = jnp.broadcast_to(jnp.sum(nu, axis=0, keepdims=True), (1, UNIT_LANES))
    pad = jnp.zeros((5, UNIT_LANES), F32)
    unit_ref[...] = jnp.concatenate([rel, exp_of, n_units, pad], axis=0).astype(I32)

    total = region_used + run
    carry[...] = jnp.broadcast_to(total, carry.shape)
    tot_ref[...] = jnp.broadcast_to(total, tot_ref.shape)


def _route(scores_t, bias_col, tri, ltri):
    n_exp, t = scores_t.shape
    nt = t // TD
    return pl.pallas_call(
        _route_kernel,
        grid=(nt,),
        in_specs=[pl.BlockSpec((n_exp, TD), lambda i: (0, i)),
                  pl.BlockSpec((n_exp, 1), lambda i: (0, 0)),
                  pl.BlockSpec((TD, TD), lambda i: (0, 0)),
                  pl.BlockSpec((n_exp, n_exp), lambda i: (0, 0))],
        out_specs=[pl.BlockSpec((TOP_K, TD), lambda i: (0, i)),
                   pl.BlockSpec((2 * n_exp, TD), lambda i: (0, i)),
                   pl.BlockSpec((n_exp, 128), lambda i: (0, i)),
                   pl.BlockSpec((n_exp, 128), lambda i: (0, i)),
                   pl.BlockSpec((8, UNIT_LANES), lambda i: (0, i)),
                   pl.BlockSpec((n_exp, 128), lambda i: (0, 0))],
        out_shape=[jax.ShapeDtypeStruct((TOP_K, t), I32),
                   jax.ShapeDtypeStruct((2 * n_exp, t), BF16),
                   jax.ShapeDtypeStruct((n_exp, nt * 128), F32),
                   jax.ShapeDtypeStruct((n_exp, nt * 128), F32),
                   jax.ShapeDtypeStruct((8, nt * UNIT_LANES), I32),
                   jax.ShapeDtypeStruct((n_exp, 128), F32)],
        scratch_shapes=[pltpu.VMEM((n_exp, 128), F32)],
        compiler_params=_cparams(("arbitrary",)),
        name="route_topk",
    )(scores_t, bias_col, tri, ltri)


def _unit_dest_kernel(start_ref, unit_ref, o_ref, *, n_exp):
    tbl = unit_ref[...]
    exp_of = tbl[1:2, :]
    dst = tbl[0:1, :]
    for e in range(n_exp):
        dst = dst + jnp.where(exp_of == e, start_ref[e], 0)
    o_ref[...] = jnp.concatenate([dst, tbl[2:3, :], jnp.zeros((6, tbl.shape[1]), I32)], axis=0)


def _unit_dest(region_start_units, units):
    blk = lambda: pl.BlockSpec((8, UNIT_LANES), lambda i, s: (0, i))
    return pl.pallas_call(
        functools.partial(_unit_dest_kernel, n_exp=region_start_units.shape[0]),
        grid_spec=pltpu.PrefetchScalarGridSpec(
            num_scalar_prefetch=1, grid=(units.shape[1] // UNIT_LANES,), in_specs=[blk()], out_specs=blk()),
        out_shape=jax.ShapeDtypeStruct(units.shape, I32),
        compiler_params=_cparams(("parallel",)),
        name="route_units",
    )(region_start_units, units)


def _unit_rows(i):
    return pl.ds(pl.multiple_of(i * UNIT, UNIT), UNIT)


def _for_each(count, fn, group=8):
    main = lax.shift_right_logical(count, int(math.log2(group)))

    def many(i, c):
        for r in range(group):
            fn(i * group + r)
        return c

    lax.fori_loop(0, main, many, 0)
    lax.fori_loop(main * group, count, lambda i, c: (fn(i), c)[1], 0)


def _dispatch_kernel(tail_lo_ref, tail_hi_ref, nu_ref, unit_ref, pos_ref, w_ref, rs_ref, rn_ref, h_ref, xs_ref,
                     s_scr, zero_scr, sem, zsem, *, n_exp):
    rows = s_scr.shape[1]
    td = h_ref.shape[0]
    step = pl.program_id(0)
    last = pl.num_programs(0) - 1
    slot = step % 2

    def zero_copy(g):
        return pltpu.make_async_copy(zero_scr, xs_ref.at[_unit_rows(g), :], zsem.at[0])

    def unit_copy(u, sl):
        return pltpu.make_async_copy(s_scr.at[sl, _unit_rows(u), :], xs_ref.at[_unit_rows(unit_ref[0, u]), :],
                                     sem.at[sl])

    def drain(count, sl):
        _for_each(count, lambda u: unit_copy(0, sl).wait())

    @pl.when(step >= 2)
    def _():
        drain(nu_ref[step - 2], slot)

    @pl.when(step == 0)
    def _():
        zero_scr[...] = jnp.zeros_like(zero_scr)

        def per_expert(e, c):
            lax.fori_loop(tail_lo_ref[e], tail_hi_ref[e], lambda g, cc: (zero_copy(g).start(), cc)[1], 0)
            lax.fori_loop(tail_lo_ref[e], tail_hi_ref[e], lambda g, cc: (zero_copy(g).wait(), cc)[1], 0)
            return c

        lax.fori_loop(0, n_exp, per_expert, 0)

    pos = pos_ref[...]
    h = h_ref[...]
    d = h.shape[1]
    w_dense = w_ref[...]
    run_lo = rs_ref[...]
    run_hi = run_lo + rn_ref[...]
    lane = lax.broadcasted_iota(I32, (SORT_CHUNK, s_scr.shape[2] - d), 1)
    for r0 in range(0, rows, SORT_CHUNK):
        srow = lax.broadcasted_iota(I32, (SORT_CHUNK, td), 0) + r0
        onehot = jnp.zeros((SORT_CHUNK, td), F32)
        for k in range(pos.shape[0]):
            onehot = jnp.where(srow == pos[k:k + 1, :], 1.0, onehot)
        s_e = (lax.broadcasted_iota(I32, (SORT_CHUNK, w_dense.shape[0]), 0) + r0).astype(F32)
        mine = jnp.logical_and(s_e >= run_lo, s_e < run_hi).astype(BF16)
        w_row = jnp.sum(onehot * _dot(mine, w_dense), axis=1, keepdims=True)
        w_hi = w_row.astype(BF16).astype(F32)
        w_lo = w_row - w_hi
        s_scr[slot, r0:r0 + SORT_CHUNK, :d] = _dot(onehot.astype(BF16), h).astype(BF16)
        s_scr[slot, r0:r0 + SORT_CHUNK, d:] = jnp.where(lane == 0, w_hi, jnp.where(lane == 1, w_lo, 0.0)).astype(BF16)

    _for_each(nu_ref[step], lambda u: unit_copy(u, slot).start())

    @pl.when(step == last)
    def _():
        drain(nu_ref[step], slot)

        @pl.when(step >= 1)
        def _():
            drain(nu_ref[step - 1], 1 - slot)


def _dispatch(tail_lo, tail_hi, n_units, unit_dst, pos, w_dense, run_start_rows, run_rows, h2, n_slots):
    t, d = h2.shape
    n_exp = tail_lo.shape[0]
    dw = d + W_LANES
    runs = pl.BlockSpec((None, 1, 2 * n_exp), lambda i, a, b, c: (i, 0, 0))
    return pl.pallas_call(
        functools.partial(_dispatch_kernel, n_exp=n_exp),
        grid_spec=pltpu.PrefetchScalarGridSpec(
            num_scalar_prefetch=3, grid=(t // TD,),
            in_specs=[pl.BlockSpec((8, UNIT_LANES), lambda i, a, b, c: (0, i), memory_space=pltpu.SMEM),
                      pl.BlockSpec((TOP_K, TD), lambda i, a, b, c: (0, i)),
                      pl.BlockSpec((2 * n_exp, TD), lambda i, a, b, c: (0, i)),
                      runs, runs,
                      pl.BlockSpec((TD, d), lambda i, a, b, c: (i, 0))],
            out_specs=pl.BlockSpec(memory_space=pl.ANY),
            scratch_shapes=[pltpu.VMEM((2, _tile_rows(n_exp), dw), BF16), pltpu.VMEM((UNIT, dw), BF16),
                            pltpu.SemaphoreType.DMA((2,)), pltpu.SemaphoreType.DMA((1,))]),
        out_shape=jax.ShapeDtypeStruct((n_slots, dw), BF16),
        compiler_params=_cparams(("arbitrary",)),
        name="moe_dispatch",
    )(tail_lo, tail_hi, n_units, unit_dst, pos, w_dense, run_start_rows, run_rows, h2)


def _expert_kernel(blk0_ref, nblk_ref, tot_ref, wg_ref, wu_ref, wd_ref, xs_ref, ys_ref, xbuf, obuf, wgu_scr, wd_scr,
                   sem_in, sem_out):
    e = pl.program_id(0)
    n = nblk_ref[e]
    b0 = blk0_ref[e]
    de = wg_ref.shape[1]
    wgu_scr[:, :de] = wg_ref[...].astype(BF16)
    wgu_scr[:, de:] = wu_ref[...].astype(BF16)
    wd_scr[...] = wd_ref[...].astype(BF16)

    total = tot_ref[0]

    def rows(g):
        return pl.ds(pl.multiple_of(g * MOE_BLK, MOE_BLK), MOE_BLK)

    def in_copy(g):
        sl = g % EXPERT_BUFS
        return pltpu.make_async_copy(xs_ref.at[rows(g), :], xbuf.at[sl], sem_in.at[sl])

    def out_copy(g):
        sl = g % EXPERT_BUFS
        return pltpu.make_async_copy(obuf.at[sl], ys_ref.at[rows(g), :], sem_out.at[sl])

    @pl.when(e == 0)
    def _():
        for g in range(EXPERT_BUFS - 1):
            @pl.when(g < total)
            def _():
                in_copy(g).start()

    def block(j, c):
        g = b0 + j
        sl = g % EXPERT_BUFS
        in_copy(g).wait()

        @pl.when(g + (EXPERT_BUFS - 1) < total)
        def _():
            in_copy(g + (EXPERT_BUFS - 1)).start()

        @pl.when(g >= EXPERT_BUFS)
        def _():
            out_copy(g - EXPERT_BUFS).wait()

        d = wd_scr.shape[1]
        gu = _dot(xbuf[sl, :, :d], wgu_scr[...])
        act = _silu(gu[:, :de]) * gu[:, de:]
        w_row = xbuf[sl, :, d:d + 1].astype(F32) + xbuf[sl, :, d + 1:d + 2].astype(F32)
        obuf[sl] = (_dot(act.astype(BF16), wd_scr[...]) * w_row).astype(BF16)
        out_copy(g).start()
        return c

    lax.fori_loop(0, n, block, 0)

    @pl.when(e == pl.num_programs(0) - 1)
    def _():
        for back in range(1, EXPERT_BUFS + 1):
            @pl.when(total >= back)
            def _():
                out_copy(total - back).wait()


def _experts(blk0, nblk, n_blocks, xs, w_gate, w_up, w_down, layer):
    n_slots, dw = xs.shape
    _, n_exp, d, de = w_gate.shape
    return pl.pallas_call(
        _expert_kernel,
        grid_spec=pltpu.PrefetchScalarGridSpec(
            num_scalar_prefetch=3, grid=(n_exp,),
            in_specs=[pl.BlockSpec((None, None, d, de), lambda e, a, b, c: (layer, e, 0, 0)),
                      pl.BlockSpec((None, None, d, de), lambda e, a, b, c: (layer, e, 0, 0)),
                      pl.BlockSpec((None, None, de, d), lambda e, a, b, c: (layer, e, 0, 0)),
                      pl.BlockSpec(memory_space=pl.ANY)],
            out_specs=pl.BlockSpec(memory_space=pl.ANY),
            scratch_shapes=[pltpu.VMEM((EXPERT_BUFS, MOE_BLK, dw), BF16), pltpu.VMEM((EXPERT_BUFS, MOE_BLK, d), BF16),
                            pltpu.VMEM((d, 2 * de), BF16), pltpu.VMEM((de, d), BF16),
                            pltpu.SemaphoreType.DMA((EXPERT_BUFS,)), pltpu.SemaphoreType.DMA((EXPERT_BUFS,))]),
        out_shape=jax.ShapeDtypeStruct((n_slots, d), BF16),
        compiler_params=_cparams(("arbitrary",)),
        name="moe_experts",
    )(blk0, nblk, n_blocks, w_gate, w_up, w_down, xs)


def _combine_kernel(unit_ref, next_ref, pos_ref, x_ref, h_ref, gate2_ref,
                    sg_ref, su_ref, sd_ref, lng_ref, lnb_ref, ys_ref, o_ref, s_scr, sem, *, alpha):
    rows = s_scr.shape[1]
    td = x_ref.shape[0]
    step = pl.program_id(0)
    slot = step % 2

    def unit_copy(tbl, u, sl):
        return pltpu.make_async_copy(ys_ref.at[_unit_rows(tbl[0, u]), :], s_scr.at[sl, _unit_rows(u), :], sem.at[sl])

    def fetch(tbl, sl):
        _for_each(tbl[1, 0], lambda u: unit_copy(tbl, u, sl).start())

    @pl.when(step == 0)
    def _():
        s_scr[...] = jnp.zeros_like(s_scr)
        fetch(unit_ref, 0)

    @pl.when(step + 1 < pl.num_programs(0))
    def _():
        fetch(next_ref, 1 - slot)

    h = h_ref[...]
    acc = _dot((_silu(_dot(h, sg_ref[...])) * _dot(h, su_ref[...])).astype(BF16), sd_ref[...])
    pos = pos_ref[...]
    _for_each(unit_ref[1, 0], lambda u: unit_copy(unit_ref, 0, slot).wait())
    for r0 in range(0, rows, SORT_CHUNK):
        lane = lax.broadcasted_iota(I32, (td, SORT_CHUNK), 1) + r0
        onehot = jnp.zeros((td, SORT_CHUNK), F32)
        for k in range(pos.shape[1]):
            onehot = jnp.where(lane == pos[:, k:k + 1], 1.0, onehot)
        acc = acc + _dot(onehot.astype(BF16), s_scr[slot, r0:r0 + SORT_CHUNK, :])
    o_ref[...] = _ln(alpha * x_ref[...] + gate2_ref[...] * acc) * lng_ref[...] + lnb_ref[...]


def _combine(unit_dst, pos_tok, xn, h2, modp, sh_gate, sh_up, sh_down, ln_g, ln_b, ys, seg_td, *, alpha, n_exp):
    t, d = xn.shape
    nt = t // TD
    tok = pl.BlockSpec((TD, d), lambda i: (i, 0))
    full = lambda a: pl.BlockSpec(a.shape, lambda i: (0,) * a.ndim)
    return pl.pallas_call(
        functools.partial(_combine_kernel, alpha=alpha),
        grid=(nt,),
        in_specs=[pl.BlockSpec((8, UNIT_LANES), lambda i: (0, i), memory_space=pltpu.SMEM),
                  pl.BlockSpec((8, UNIT_LANES), lambda i: (0, jnp.minimum(i + 1, nt - 1)), memory_space=pltpu.SMEM),
                  pl.BlockSpec((TD, pos_tok.shape[1]), lambda i: (i, 0)),
                  tok, tok,
                  pl.BlockSpec((None, None, 1, d), lambda i: (5, seg_td(i), 0, 0)),
                  full(sh_gate), full(sh_up), full(sh_down), full(ln_g), full(ln_b),
                  pl.BlockSpec(memory_space=pl.ANY)],
        out_specs=tok,
        out_shape=jax.ShapeDtypeStruct((t, d), F32),
        scratch_shapes=[pltpu.VMEM((2, _tile_rows(n_exp), d), BF16), pltpu.SemaphoreType.DMA((2,))],
        compiler_params=_cparams(("arbitrary",)),
        name="moe_combine",
    )(unit_dst, unit_dst, pos_tok, xn, h2, modp, sh_gate, sh_up, sh_down, ln_g, ln_b, ys)


def _moe(xn, h2, scores_t, modp, seg_td, router_bias, w_gate, w_up, w_down, sh_gate, sh_up, sh_down,
         ln_g, ln_b, tri, ltri, *, alpha, layer):
    t = xn.shape[0]
    n_exp = scores_t.shape[0]
    assert _tile_rows(n_exp) % SORT_CHUNK == 0 and _tile_rows(n_exp) <= UNIT * UNIT_LANES
    pos, w_sel, run_start, run_len, units, tot = _route(scores_t, router_bias.reshape(n_exp, 1), tri, ltri)
    as_rows = lambda a: jnp.tile(a[:, ::128].T, (1, 2))[:, None, :]
    used = tot[:, 0].astype(I32)
    region = (used + MOE_BLK - 1) // MOE_BLK * MOE_BLK
    region_end = jnp.cumsum(region)
    region_start = region_end - region
    n_slots = TOP_K * t + n_exp * UNIT * (t // TD) + n_exp * MOE_BLK
    unit_dst = _unit_dest((region_start // UNIT).astype(I32), units)
    xs = _dispatch(((region_start + used) // UNIT).astype(I32), (region_end // UNIT).astype(I32),
                   unit_dst[1, ::UNIT_LANES], unit_dst, pos, w_sel, as_rows(run_start), as_rows(run_len), h2,
                   n_slots)
    ys = _experts((region_start // MOE_BLK).astype(I32), (region // MOE_BLK).astype(I32),
                  (region_end[-1:] // MOE_BLK).astype(I32), xs, w_gate, w_up, w_down, layer)
    return _combine(unit_dst, pos.T, xn, h2, modp, sh_gate, sh_up, sh_down, ln_g, ln_b, ys, seg_td,
                    alpha=alpha, n_exp=n_exp)


def kernel(x, c, ctx, c_ctx, w_mod, b_mod, w_in, b_in, lam_q1, lam_k1, lam_q2, lam_k2, attn_norm_g, sg_ln_g, sg_ln_b, sg_w, sg_b, conv_w, conv_b, lru_wa, lru_ba, lru_wx, lru_bx, lru_lam, w_branch, w_out, ln1_g, ln1_b, w_router, router_bias, moe_w_gate, moe_w_up, moe_w_down, sh_w_gate, sh_w_up, sh_w_down, ln2_g, ln2_b):
    batch, n_lat, d = x.shape
    n_ctx = ctx.shape[1]
    depth = w_mod.shape[0]
    n_exp = w_router.shape[2]
    t_lat = batch * n_lat
    t_ctx = batch * n_ctx
    t_all = t_lat + t_ctx
    assert n_lat % TM == 0 and t_ctx % TM == 0 and n_ctx % B_CHUNK == 0 and t_lat % n_ctx == 0
    assert batch + 1 <= MOD_ROWS and TM % TD == 0
    alpha = (2 * depth) ** 0.25
    cw = conv_w.shape[2]
    tiles_per_batch = n_lat // TM

    seg = lambda i: jnp.minimum(i // tiles_per_batch, batch)
    seg_td = lambda i: jnp.minimum(i // (n_lat // TD), batch)
    tm_in = 2 * TM if (n_lat % (2 * TM) == 0 and t_ctx % (2 * TM) == 0) else TM
    seg_in = lambda i: jnp.minimum(i // (n_lat // tm_in), batch)
    tbl_idx = lambda i: jnp.where(i < t_lat // tm_in, i % (n_lat // tm_in), n_lat // tm_in)

    x_all = jnp.concatenate([x.reshape(t_lat, d), ctx.reshape(t_ctx, d)], axis=0)
    c_all = jnp.zeros((MOD_ROWS, d), F32).at[:batch].set(c).at[batch].set(c_ctx)
    tables = _rope_tables(n_lat, tm_in)
    tri = (jnp.arange(TD)[:, None] <= jnp.arange(TD)[None, :]).astype(BF16)
    ltri = (jnp.arange(n_exp)[None, :] < jnp.arange(n_exp)[:, None]).astype(BF16)
    row = lambda v: v.reshape(1, -1)

    def dense_blocks(w):
        nb, bi, bj = w.shape
        eye = jnp.eye(nb, dtype=w.dtype)
        return (w[:, :, None, :] * eye[:, None, :, None]).reshape(nb * bi, nb * bj)

    for l in range(depth):
        last = l == depth - 1
        lam_init = 0.8 - 0.6 * math.exp(-0.3 * l)
        mod = _mod_rows(c_all, w_mod[l].astype(BF16), row(b_mod[l]))
        modp = mod.reshape(MOD_ROWS, 6, 1, d).transpose(1, 0, 2, 3)

        qr, kr, vb, proj = _in_proj(x_all, modp, w_in[l].astype(BF16), row(b_in[l]), tables, seg_in, tbl_idx, tm_in)

        lam_vecs = jnp.stack([lam_q1[l], lam_k1[l], lam_q2[l], lam_k2[l]])
        gain = attn_norm_g[l].reshape(A_HEADS, 1, A_V_DIM)
        attn = functools.partial(_attention, qr, kr, vb, lam_vecs, gain, lam_init=lam_init,
                                 batch=batch, n_lat=n_lat, n_ctx=n_ctx)
        n_rows = t_lat if last else t_all
        o_a = attn(latent_queries=True, out_rows=n_rows)
        if not last:
            o_a = attn(latent_queries=False, out_rows=n_rows, into=o_a)

        o_b = _spatial_gating(proj, row(sg_ln_g[l]), row(sg_ln_b[l]), sg_w[l].astype(BF16),
                              sg_b[l].reshape(B_GROUPS, B_CHUNK, 1))

        lru = functools.partial(_lru_pass, proj, conv_w=conv_w[l], conv_b=row(conv_b[l]), batch=batch)
        zeros_h = jnp.zeros((batch, 8, cw), F32)
        hf = {}
        for direction in range(2):
            wd = jnp.concatenate([dense_blocks(lru_wa[l, direction]), dense_blocks(lru_wx[l, direction])],
                                 axis=1).astype(BF16)
            par = dict(w_dense=wd, ba=row(lru_ba[l, direction]), bx=row(lru_bx[l, direction]),
                       lam=row(lru_lam[l, direction]), reverse=direction == 1, final=direction == 1)
            shared = direction == 1 and not last
            h_ctx, edge = lru(hf.get("ctx"), zeros_h, seq=n_ctx, tl=n_ctx, row0=t_lat,
                              out_rows=t_all if shared else None, out_row0=t_lat if shared else 0, **par)
            h_lat, _ = lru(hf.get("lat"), edge, seq=n_lat, tl=TM, row0=0,
                           out_rows=t_all if shared else None, into=h_ctx if shared else None, **par)
            hf = {"ctx": h_ctx, "lat": h_lat}
        o_c = hf["lat"]
        wr_t = w_router[l].T
        wr_hi = wr_t.astype(BF16)
        wr_lo = (wr_t - wr_hi.astype(F32)).astype(BF16)
        xn, h2, scores_t = _merge(x_all, o_a, o_b, o_c, proj, modp, w_branch[l].astype(BF16),
                                  w_out[l].astype(BF16), row(ln1_g[l]), row(ln1_b[l]), wr_hi, wr_lo, seg,
                                  n_rows=n_rows, alpha=alpha)
        x_all = _moe(xn, h2, scores_t, modp, seg_td, router_bias[l], moe_w_gate,
                     moe_w_up, moe_w_down, sh_w_gate[l].astype(BF16),
                     sh_w_up[l].astype(BF16), sh_w_down[l].astype(BF16), row(ln2_g[l]), row(ln2_b[l]), tri, ltri,
                     alpha=alpha, layer=l)
    return x_all[:t_lat].reshape(batch, n_lat, d)
```

```python
import functools
import math

import jax
import jax.numpy as jnp
from jax import lax
from jax.experimental import pallas as pl
from jax.experimental.pallas import tpu as pltpu

F32 = jnp.float32
BF16 = jnp.bfloat16
I32 = jnp.int32

A_HEADS = 4
A_QK_DIM = 64
A_V_DIM = 2 * A_QK_DIM
GRID_W = 64
ROPE_THETA = 10000.0
B_CHUNK = 128
B_GROUPS = 4
C_BLOCKS = 8
C_POW = 8.0
TOP_K = 8
N_GROUPS = 8
TOPK_GROUPS = 4
ROUTED_SCALE = 2.5
LN_EPS = 1e-6
RMS_EPS = 1e-5

TM = 512
MOE_BLK = 512
TD = 256
UNIT = 16
UNIT_LANES = 256
SORT_CHUNK = 512
W_LANES = 128
EXPERT_BUFS = 4
HALO = 16
ATTN_HEADS = 4
ATTN_SUB = 2048
MOD_ROWS = 16
VMEM_LIMIT = 56 * 1024 * 1024


def _cparams(sem):
    return pltpu.CompilerParams(dimension_semantics=sem, vmem_limit_bytes=VMEM_LIMIT)


def _ln(x):
    mu = jnp.mean(x, axis=-1, keepdims=True)
    xc = x - mu
    var = jnp.mean(xc * xc, axis=-1, keepdims=True)
    return xc * lax.rsqrt(var + LN_EPS)


def _gelu(x):
    cdf = 0.5 * (1.0 + jnp.tanh(math.sqrt(2.0 / math.pi) * (x + 0.044715 * (x * x * x))))
    return x * cdf


def _sigmoid(x):
    return 1.0 / (1.0 + jnp.exp(-x))


def _silu(x):
    return x * _sigmoid(x)


def _dot(a, b):
    return jnp.dot(a, b, preferred_element_type=F32)


def _dot_nt(a, b):
    return lax.dot_general(a, b, (((1,), (1,)), ((), ())), preferred_element_type=F32)


def _mod_kernel(c_ref, w_ref, b_ref, o_ref):
    o_ref[...] = _dot(_silu(c_ref[...]).astype(BF16), w_ref[...]) + b_ref[...]


def _mod_rows(c_all, w, b):
    m, d = c_all.shape
    n = w.shape[1]
    tn = 1536
    return pl.pallas_call(
        _mod_kernel,
        grid=(n // tn,),
        in_specs=[pl.BlockSpec((m, d), lambda j: (0, 0)),
                  pl.BlockSpec((d, tn), lambda j: (0, j)),
                  pl.BlockSpec((1, tn), lambda j: (0, j))],
        out_specs=pl.BlockSpec((m, tn), lambda j: (0, j)),
        out_shape=jax.ShapeDtypeStruct((m, n), F32),
        compiler_params=_cparams(("parallel",)),
        name="adaln_rows",
    )(c_all, w, b)


REST_TILE = 1280
QKV_TILES = 3
COL_U, COL_S, COL_X, COL_Y, COL_G = 0, 1, 2, 3, 4


def _in_proj_kernel(x_ref, sh_ref, sc_ref, w_ref, b_ref, cos_ref, sa_ref, sb_ref,
                    q_ref, k_ref, v_ref, h_ref):
    j = pl.program_id(1)

    @pl.when(j == 0)
    def _():
        h = _ln(x_ref[...]) * (1.0 + sc_ref[...]) + sh_ref[...]
        h_ref[...] = h.astype(BF16)

    acc = _dot(h_ref[...], w_ref[...]) + b_ref[...]

    def rope(dst_ref, scale):
        hw = cos_ref.shape[1]
        for c0 in range(0, acc.shape[1], hw):
            x = acc[:, c0:c0 + hw]
            r = x * cos_ref[...] + pltpu.roll(x, hw - 16, 1) * sa_ref[...] + pltpu.roll(x, 16, 1) * sb_ref[...]
            dst_ref[:, c0:c0 + hw] = (r * scale).astype(BF16)

    @pl.when(j == 0)
    def _():
        rope(q_ref, A_QK_DIM ** -0.5 * math.log2(math.e))

    @pl.when(j == 1)
    def _():
        rope(k_ref, 1.0)

    @pl.when(j == 2)
    def _():
        v_ref[...] = acc.astype(BF16)


def _rest_proj_kernel(h_ref, w_ref, b_ref, o_ref):
    o_ref[...] = (_dot(h_ref[...], w_ref[...]) + b_ref[...]).astype(o_ref.dtype)


def _in_proj(x_all, modp, w, b, tables, seg, tbl_idx, tm):
    t, d = x_all.shape
    tn = 2 * A_HEADS * A_QK_DIM
    n_qkv = QKV_TILES * tn
    n_rest = w.shape[1] - n_qkv
    cos, sa, sb = tables
    tbl = pl.BlockSpec((tm, cos.shape[1]), lambda i, j: (tbl_idx(i), 0))
    qkv = pl.BlockSpec((tm, tn), lambda i, j: (i, 0))
    qkv_shape = jax.ShapeDtypeStruct((t, tn), BF16)
    q, k, v, h = pl.pallas_call(
        _in_proj_kernel,
        grid=(t // tm, QKV_TILES),
        in_specs=[pl.BlockSpec((tm, d), lambda i, j: (i, 0)),
                  pl.BlockSpec((None, None, 1, d), lambda i, j: (0, seg(i), 0, 0)),
                  pl.BlockSpec((None, None, 1, d), lambda i, j: (1, seg(i), 0, 0)),
                  pl.BlockSpec((d, tn), lambda i, j: (0, j)),
                  pl.BlockSpec((1, tn), lambda i, j: (0, j)),
                  tbl, tbl, tbl],
        out_specs=[qkv, qkv, qkv, pl.BlockSpec((tm, d), lambda i, j: (i, 0))],
        out_shape=[qkv_shape, qkv_shape, qkv_shape, jax.ShapeDtypeStruct((t, d), BF16)],
        compiler_params=_cparams(("parallel", "arbitrary")),
        name="in_proj_qkv",
    )(x_all, modp, modp, w, b, cos, sa, sb)
    tr = REST_TILE if n_rest % REST_TILE == 0 else tn
    rest = pl.pallas_call(
        _rest_proj_kernel,
        grid=(t // tm, n_rest // tr),
        in_specs=[pl.BlockSpec((tm, d), lambda i, j: (i, 0)),
                  pl.BlockSpec((d, tr), lambda i, j: (0, j)),
                  pl.BlockSpec((1, tr), lambda i, j: (0, j))],
        out_specs=pl.BlockSpec((tm, tr), lambda i, j: (i, j)),
        out_shape=jax.ShapeDtypeStruct((t, n_rest), BF16),
        compiler_params=_cparams(("parallel", "parallel")),
        name="in_proj_rest",
    )(h, w[:, n_qkv:], b[:, n_qkv:])
    return q, k, v, rest


def _rope_tables(n, tm):
    rows = n // GRID_W
    pos_row = jnp.repeat(jnp.arange(rows), GRID_W).astype(F32)
    pos_col = jnp.tile(jnp.arange(GRID_W), rows).astype(F32)
    quarter = A_QK_DIM // 4
    inv = ROPE_THETA ** (-jnp.arange(quarter, dtype=F32) / quarter)
    ang_r = pos_row[:, None] * inv
    ang_c = pos_col[:, None] * inv
    ang = jnp.concatenate([ang_r, ang_r, ang_c, ang_c], axis=-1)
    cos = jnp.cos(ang)
    sin = jnp.sin(ang)
    first_half = (jnp.arange(A_QK_DIM) % (2 * quarter)) < quarter
    sa = jnp.where(first_half, -sin, 0.0)
    sb = jnp.where(first_half, 0.0, sin)

    def widen(tbl, fill):
        tbl = jnp.tile(tbl, (1, 2))
        return jnp.concatenate([tbl, jnp.full((tm, tbl.shape[1]), fill, F32)], axis=0)

    return widen(cos, 1.0), widen(sa, 0.0), widen(sb, 0.0)


def _attn_kernel(*refs, lam_init, with_lat, sub):
    if with_lat:
        lam_ref, gain_ref, q_ref, kc_ref, vc_ref, kl_ref, vl_ref, o_ref = refs
    else:
        lam_ref, gain_ref, q_ref, kc_ref, vc_ref, _, o_ref = refs
    lv = lam_ref[...]
    lam = (jnp.exp(jnp.sum(lv[0:1] * lv[1:2], axis=-1, keepdims=True))
           - jnp.exp(jnp.sum(lv[2:3] * lv[3:4], axis=-1, keepdims=True)) + lam_init)
    chunks = [(kc_ref, vc_ref, 0, kc_ref.shape[0])]
    if with_lat:
        ck = min(sub, kl_ref.shape[0])
        chunks += [(kl_ref, vl_ref, r0, ck) for r0 in range(0, kl_ref.shape[0], ck)]
    for hh in range(gain_ref.shape[0]):
        head = slice(hh * A_V_DIM, (hh + 1) * A_V_DIM)
        probs, dens = [], []
        for c in range(2):
            cols = slice(hh * A_V_DIM + c * A_QK_DIM, hh * A_V_DIM + (c + 1) * A_QK_DIM)
            qc = q_ref[:, cols]
            scores = [_dot_nt(k_ref[r0:r0 + n, cols], qc) for k_ref, _, r0, n in chunks]
            m = None
            for s in scores:
                ms = jnp.max(s, axis=0, keepdims=True)
                m = ms if m is None else jnp.maximum(m, ms)
            p = [jnp.exp2(s - m) for s in scores]
            den = None
            for pj in p:
                ds = jnp.sum(pj, axis=0, keepdims=True)
                den = ds if den is None else den + ds
            probs.append(p)
            dens.append(den)
        r1 = 1.0 / dens[0]
        r2 = lam / dens[1]
        o = None
        for p1, p2, (_, v_ref, r0, n) in zip(probs[0], probs[1], chunks):
            w = p1 * r1 - p2 * r2
            ps = _dot(v_ref[head, r0:r0 + n], w.astype(BF16))
            o = ps if o is None else o + ps
        o = o * lax.rsqrt(jnp.mean(o * o, axis=0, keepdims=True) + RMS_EPS)
        o_ref[:, head] = ((o * gain_ref[hh]) * (1.0 - lam_init)).T


def _attention(qr, kr, vt, lam_vecs, gain, *, lam_init, batch, n_lat, n_ctx, latent_queries, out_rows, into=None):
    hw = ATTN_HEADS * A_V_DIM
    ctx_blk0 = batch * n_lat // n_ctx
    lam_spec = pl.BlockSpec(lam_vecs.shape, lambda b, h, i: (0, 0))
    gain_spec = pl.BlockSpec((ATTN_HEADS, A_V_DIM, 1), lambda b, h, i: (h, 0, 0))
    kv_ctx = pl.BlockSpec((n_ctx, hw), lambda b, h, i: (ctx_blk0 + b, h))
    vt_ctx = pl.BlockSpec((hw, n_ctx), lambda b, h, i: (h, ctx_blk0 + b))
    if latent_queries:
        tq = 256
        nq = n_lat // tq
        q_spec = pl.BlockSpec((tq, hw), lambda b, h, i: (b * nq + i, h))
        kv_lat = pl.BlockSpec((n_lat, hw), lambda b, h, i: (b, h))
        vt_lat = pl.BlockSpec((hw, n_lat), lambda b, h, i: (h, b))
        in_specs = [lam_spec, gain_spec, q_spec, kv_ctx, vt_ctx, kv_lat, vt_lat]
        args = (lam_vecs, gain, qr, kr, vt, kr, vt)
        out_blk0 = 0
        aliases = {}
    else:
        tq = n_ctx
        nq = 1
        q_spec = pl.BlockSpec((tq, hw), lambda b, h, i: (ctx_blk0 + b, h))
        in_specs = [lam_spec, gain_spec, q_spec, kv_ctx, vt_ctx, pl.BlockSpec(memory_space=pl.ANY)]
        args = (lam_vecs, gain, qr, kr, vt, into)
        out_blk0 = ctx_blk0
        aliases = {len(args) - 1: 0}
    return pl.pallas_call(
        functools.partial(_attn_kernel, lam_init=lam_init, with_lat=latent_queries, sub=ATTN_SUB),
        grid=(batch, A_HEADS // ATTN_HEADS, nq),
        in_specs=in_specs,
        out_specs=pl.BlockSpec((tq, hw), lambda b, h, i: (out_blk0 + b * nq + i, h)),
        out_shape=jax.ShapeDtypeStruct((out_rows, A_HEADS * A_V_DIM), F32),
        input_output_aliases=aliases,
        compiler_params=_cparams(("parallel", "parallel", "arbitrary")),
        name="diff_attn_lat" if latent_queries else "diff_attn_ctx",
    )(*args)


def _sg_kernel(u_ref, s_ref, g_ref, b_ref, w_ref, bs_ref, o_ref):
    gw = B_CHUNK
    for ci in range(TM // B_CHUNK):
        rows = slice(ci * B_CHUNK, (ci + 1) * B_CHUNK)
        for g in range(B_GROUPS):
            cols = slice(g * gw, (g + 1) * gw)
            vn = _ln(_gelu(s_ref[rows, cols].astype(F32))) * g_ref[:, cols] + b_ref[:, cols]
            mixed = _dot(w_ref[g], vn.astype(BF16)) + bs_ref[g]
            o_ref[rows, cols] = _gelu(u_ref[rows, cols].astype(F32)) * mixed


def _spatial_gating(proj, ln_g, ln_b, w_s, b_s):
    t = proj.shape[0]
    w = B_GROUPS * B_CHUNK
    full = lambda a: pl.BlockSpec(a.shape, lambda i: (0,) * a.ndim)
    return pl.pallas_call(
        _sg_kernel,
        grid=(t // TM,),
        in_specs=[pl.BlockSpec((TM, w), lambda i: (i, COL_U)),
                  pl.BlockSpec((TM, w), lambda i: (i, COL_S)),
                  full(ln_g), full(ln_b), full(w_s), full(b_s)],
        out_specs=pl.BlockSpec((TM, w), lambda i: (i, 0)),
        out_shape=jax.ShapeDtypeStruct((t, w), F32),
        compiler_params=_cparams(("parallel",)),
        name="spatial_gating",
    )(proj, proj, ln_g, ln_b, w_s, b_s)


def _lru_kernel(*refs, reverse, final, n_tiles, tl, aliased):
    if aliased:
        refs = refs[:12] + refs[13:]
    if final:
        (x_ref, xp_ref, xn_ref, cw_ref, cb_ref, w_ref, ba_ref, bx_ref, lam_ref, h0_ref,
         hf_ref, y_ref, o_ref, hl_ref, carry) = refs
    else:
        (x_ref, xp_ref, xn_ref, cw_ref, cb_ref, w_ref, ba_ref, bx_ref, lam_ref, h0_ref,
         o_ref, hl_ref, carry) = refs
    step = pl.program_id(1)

    @pl.when(step == 0)
    def _():
        carry[...] = h0_ref[...]

    ti = (n_tiles - 1 - step) if reverse else step
    x = x_ref[...].astype(F32)
    cw = x.shape[-1]
    row = lax.broadcasted_iota(I32, x.shape, 0)
    has_prev = ti > 0
    has_next = ti < n_tiles - 1
    p6 = jnp.where(has_prev, xp_ref[HALO - 2:HALO - 1, :].astype(F32), 0.0)
    p7 = jnp.where(has_prev, xp_ref[HALO - 1:HALO, :].astype(F32), 0.0)
    n0 = jnp.where(has_next, xn_ref[0:1, :].astype(F32), 0.0)
    xm1 = jnp.where(row == 0, p7, pltpu.roll(x, 1, 0))
    xm2 = jnp.where(row == 0, p6, jnp.where(row == 1, p7, pltpu.roll(x, 2, 0)))
    xp1 = jnp.where(row == tl - 1, n0, pltpu.roll(x, tl - 1, 0))
    taps = cw_ref[...]
    xs = taps[0:1] * xm2 + taps[1:2] * xm1 + taps[2:3] * x + taps[3:4] * xp1 + cb_ref[...]

    z = _dot(xs.astype(BF16), w_ref[...])
    r = _sigmoid(z[:, :cw] + ba_ref[...])
    ig = _sigmoid(z[:, cw:] + bx_ref[...])
    nl = -lam_ref[...]
    softplus = jnp.maximum(nl, 0.0) + jnp.log(1.0 + jnp.exp(-jnp.abs(nl)))
    log_a = -C_POW * r * softplus
    a = jnp.exp(log_a)
    u = jnp.sqrt(1.0 - jnp.exp(2.0 * log_a)) * (ig * xs)

    d = 1
    while d < tl:
        if reverse:
            ok = row < tl - d
            shift = tl - d
        else:
            ok = row >= d
            shift = d
        a_sh = jnp.where(ok, pltpu.roll(a, shift, 0), 1.0)
        u_sh = jnp.where(ok, pltpu.roll(u, shift, 0), 0.0)
        u = u + a * u_sh
        a = a * a_sh
        d *= 2
    h = u + a * carry[0:1, :]
    edge = h[0:1, :] if reverse else h[tl - 1:tl, :]
    carry[...] = jnp.broadcast_to(edge, carry.shape)
    hl_ref[...] = jnp.broadcast_to(edge, hl_ref.shape)
    if final:
        o_ref[...] = _gelu(y_ref[...].astype(F32)) * (hf_ref[...] + h)
    else:
        o_ref[...] = h


def _lru_pass(proj, hf, h0, conv_w, conv_b, w_dense, ba, bx, lam, *, batch, seq, tl, row0, reverse, final,
              out_rows=None, out_row0=0, into=None):
    t_all = proj.shape[0]
    out_rows = batch * seq if out_rows is None else out_rows
    out_blk0 = out_row0 // tl
    assert into is None or final
    cw = conv_w.shape[1]
    n_tiles = seq // tl
    base = row0 // tl
    sub = tl // HALO
    last_halo = t_all // HALO - 1

    def tile(b, s):
        ti = (n_tiles - 1 - s) if reverse else s
        return base + b * n_tiles + ti

    def loc(b, s):
        ti = (n_tiles - 1 - s) if reverse else s
        return b * n_tiles + ti

    full = lambda a: pl.BlockSpec(a.shape, lambda b, s: (0,) * a.ndim)
    in_specs = [pl.BlockSpec((tl, cw), lambda b, s: (tile(b, s), COL_X)),
                pl.BlockSpec((HALO, cw), lambda b, s: (jnp.maximum(tile(b, s) * sub - 1, 0), COL_X)),
                pl.BlockSpec((HALO, cw), lambda b, s: (jnp.minimum((tile(b, s) + 1) * sub, last_halo), COL_X)),
                full(conv_w), full(conv_b), full(w_dense), full(ba), full(bx), full(lam),
                pl.BlockSpec((None, 8, cw), lambda b, s: (b, 0, 0))]
    args = [proj, proj, proj, conv_w, conv_b, w_dense, ba, bx, lam, h0]
    if final:
        in_specs += [pl.BlockSpec((tl, cw), lambda b, s: (loc(b, s), 0)),
                     pl.BlockSpec((tl, cw), lambda b, s: (tile(b, s), COL_Y))]
        args += [hf, proj]
    aliases = {}
    if into is not None:
        aliases = {len(args): 0}
        in_specs += [pl.BlockSpec(memory_space=pl.ANY)]
        args += [into]
    return pl.pallas_call(
        functools.partial(_lru_kernel, reverse=reverse, final=final, n_tiles=n_tiles, tl=tl,
                          aliased=into is not None),
        grid=(batch, n_tiles),
        in_specs=in_specs,
        out_specs=[pl.BlockSpec((tl, cw), lambda b, s: (out_blk0 + loc(b, s), 0)),
                   pl.BlockSpec((None, 8, cw), lambda b, s: (b, 0, 0))],
        out_shape=[jax.ShapeDtypeStruct((out_rows, cw), F32),
                   jax.ShapeDtypeStruct((batch, 8, cw), F32)],
        scratch_shapes=[pltpu.VMEM((8, cw), F32)],
        input_output_aliases=aliases,
        compiler_params=_cparams(("parallel", "arbitrary")),
        name="rglru_%s_%s" % ("bwd" if reverse else "fwd", "lat" if row0 == 0 else "ctx"),
    )(*args)


def _merge_kernel(x_ref, oa_ref, ob_ref, oc_ref, g0, g1, g2, g3, g4, g5, wb_ref, wo_ref,
                  gate1_ref, sh2_ref, sc2_ref, lng_ref, lnb_ref, wrh_ref, wrl_ref,
                  xn_ref, h2_ref, sc_ref, *, alpha):
    gates = ((g0, g1), (g2, g3), (g4, g5))
    mix = None
    for r, o_ref in enumerate((oa_ref, ob_ref, oc_ref)):
        proj = _dot(o_ref[...].astype(BF16), wb_ref[r])
        gate = jnp.concatenate([_sigmoid(gates[r][0][...].astype(F32)), _sigmoid(gates[r][1][...].astype(F32))],
                               axis=-1)
        term = gate * proj
        mix = term if mix is None else mix + term
    out = _dot(mix.astype(BF16), wo_ref[...])
    xn = _ln(alpha * x_ref[...] + gate1_ref[...] * out) * lng_ref[...] + lnb_ref[...]
    xn_ref[...] = xn
    h2 = _ln(xn) * (1.0 + sc2_ref[...]) + sh2_ref[...]
    h_hi = h2.astype(BF16)
    h2_ref[...] = h_hi
    h_lo = (h2 - h_hi.astype(F32)).astype(BF16)
    logits = _dot_nt(wrh_ref[...], h_hi) + (_dot_nt(wrh_ref[...], h_lo) + _dot_nt(wrl_ref[...], h_hi))
    sc_ref[...] = _sigmoid(logits)


def _merge(x_all, o_a, o_b, o_c, proj, modp, w_branch, w_out, ln_g, ln_b, wr_hi, wr_lo, seg, *, n_rows, alpha):
    d = x_all.shape[1]
    bw = o_a.shape[1]
    n_exp = wr_hi.shape[0]
    tok = lambda w: pl.BlockSpec((TM, w), lambda i: (i, 0))
    gate = lambda c: pl.BlockSpec((TM, bw), lambda i, c=c: (i, c))
    mod = lambda k: pl.BlockSpec((None, None, 1, d), lambda i, k=k: (k, seg(i), 0, 0))
    full = lambda a: pl.BlockSpec(a.shape, lambda i: (0,) * a.ndim)
    return pl.pallas_call(
        functools.partial(_merge_kernel, alpha=alpha),
        grid=(n_rows // TM,),
        in_specs=[tok(d), tok(bw), tok(bw), tok(bw)] + [gate(COL_G + c) for c in range(6)]
                 + [full(w_branch), full(w_out), mod(2), mod(3), mod(4), full(ln_g), full(ln_b),
                    full(wr_hi), full(wr_lo)],
        out_specs=[tok(d), tok(d), pl.BlockSpec((n_exp, TM), lambda i: (0, i))],
        out_shape=[jax.ShapeDtypeStruct((n_rows, d), F32),
                   jax.ShapeDtypeStruct((n_rows, d), BF16),
                   jax.ShapeDtypeStruct((n_exp, n_rows), F32)],
        compiler_params=_cparams(("parallel",)),
        name="merge_residual_router",
    )(x_all, o_a, o_b, o_c, proj, proj, proj, proj, proj, proj, w_branch, w_out,
      modp, modp, modp, ln_g, ln_b, wr_hi, wr_lo)


def _tile_rows(n_exp):
    return TOP_K * TD + n_exp * UNIT


def _route_kernel(s_ref, bias_ref, tri_ref, ltri_ref, pos_ref, w_ref, rs_ref, rn_ref, unit_ref, tot_ref, carry):
    @pl.when(pl.program_id(0) == 0)
    def _():
        carry[...] = jnp.zeros_like(carry)

    s = s_ref[...]
    n_exp, tn = s.shape
    per = n_exp // N_GROUPS
    neg = -jnp.inf
    biased = s + bias_ref[...]
    sub = lax.broadcasted_iota(I32, (per, tn), 0)
    gs_rows = []
    for g in range(N_GROUPS):
        blk = biased[g * per:(g + 1) * per, :]
        m1 = jnp.max(blk, axis=0, keepdims=True)
        first = jnp.min(jnp.where(blk == m1, sub, per), axis=0, keepdims=True)
        m2 = jnp.max(jnp.where(sub == first, neg, blk), axis=0, keepdims=True)
        gs_rows.append(m1 + m2)
    gs = jnp.concatenate(gs_rows, axis=0)
    gi = lax.broadcasted_iota(I32, gs.shape, 0)
    g_ok = jnp.zeros(gs.shape, F32)
    cur = gs
    for _ in range(TOPK_GROUPS):
        m = jnp.max(cur, axis=0, keepdims=True)
        pick = jnp.min(jnp.where(cur == m, gi, N_GROUPS), axis=0, keepdims=True)
        hit = gi == pick
        g_ok = jnp.where(hit, 1.0, g_ok)
        cur = jnp.where(hit, neg, cur)
    ok_rows = [jnp.broadcast_to(g_ok[g:g + 1, :], (per, tn)) for g in range(N_GROUPS)]
    expert_ok = jnp.concatenate(ok_rows, axis=0)
    masked = jnp.where(expert_ok > 0.0, biased, neg)
    ei = lax.broadcasted_iota(I32, s.shape, 0)
    pick_s, hits = [], []
    sel = jnp.zeros(s.shape, F32)
    for _ in range(TOP_K):
        m = jnp.max(masked, axis=0, keepdims=True)
        pick = jnp.min(jnp.where(masked == m, ei, n_exp), axis=0, keepdims=True)
        hit = ei == pick
        hits.append(hit)
        pick_s.append(jnp.sum(jnp.where(hit, s, 0.0), axis=0, keepdims=True))
        sel = jnp.where(hit, 1.0, sel)
        masked = jnp.where(hit, neg, masked)
    tot = pick_s[0]
    for k in range(1, TOP_K):
        tot = tot + pick_s[k]
    w = jnp.where(sel > 0.0, s, 0.0) / tot * ROUTED_SCALE
    w_hi = w.astype(BF16)
    w_ref[...] = jnp.concatenate([w_hi, (w - w_hi.astype(F32)).astype(BF16)], axis=0)

    incl = _dot(sel.astype(BF16), tri_ref[...])
    count = incl[:, tn - 1:tn]
    run = jnp.floor((count + (UNIT - 1)) * (1.0 / UNIT)) * UNIT
    run_start = _dot(ltri_ref[...], jnp.broadcast_to(run, (n_exp, 128)).astype(BF16))[:, 0:1]
    row_in_tile = incl - sel + run_start
    pos = [jnp.sum(jnp.where(hit, row_in_tile, 0.0), axis=0, keepdims=True) for hit in hits]
    pos_ref[...] = jnp.concatenate(pos, axis=0).astype(I32)
    rs_ref[...] = jnp.broadcast_to(run_start, rs_ref.shape)
    rn_ref[...] = jnp.broadcast_to(run, rn_ref.shape)

    region_used = carry[:, 0:1]
    u = lax.broadcasted_iota(I32, (n_exp, UNIT_LANES), 1).astype(F32)
    eu = lax.broadcasted_iota(I32, (n_exp, UNIT_LANES), 0).astype(F32)
    u0 = run_start * (1.0 / UNIT)
    nu = run * (1.0 / UNIT)
    inside = jnp.logical_and(u >= u0, u < u0 + nu)
    rel = jnp.sum(jnp.where(inside, u - u0 + region_used * (1.0 / UNIT), 0.0), axis=0, keepdims=True)
    exp_of = jnp.sum(jnp.where(inside, eu, 0.0), axis=0, keepdims=True)
    n_units = jnp.broadcast_to(jnp.sum(nu, axis=0, keepdims=True), (1, UNIT_LANES))
    pad = jnp.zeros((5, UNIT_LANES), F32)
    unit_ref[...] = jnp.concatenate([rel, exp_of, n_units, pad], axis=0).astype(I32)

    total = region_used + run
    carry[...] = jnp.broadcast_to(total, carry.shape)
    tot_ref[...] = jnp.broadcast_to(total, tot_ref.shape)


def _route(scores_t, bias_col, tri, ltri):
    n_exp, t = scores_t.shape
    nt = t // TD
    return pl.pallas_call(
        _route_kernel,
        grid=(nt,),
        in_specs=[pl.BlockSpec((n_exp, TD), lambda i: (0, i)),
                  pl.BlockSpec((n_exp, 1), lambda i: (0, 0)),
                  pl.BlockSpec((TD, TD), lambda i: (0, 0)),
                  pl.BlockSpec((n_exp, n_exp), lambda i: (0, 0))],
        out_specs=[pl.BlockSpec((TOP_K, TD), lambda i: (0, i)),
                   pl.BlockSpec((2 * n_exp, TD), lambda i: (0, i)),
                   pl.BlockSpec((n_exp, 128), lambda i: (0, i)),
                   pl.BlockSpec((n_exp, 128), lambda i: (0, i)),
                   pl.BlockSpec((8, UNIT_LANES), lambda i: (0, i)),
                   pl.BlockSpec((n_exp, 128), lambda i: (0, 0))],
        out_shape=[jax.ShapeDtypeStruct((TOP_K, t), I32),
                   jax.ShapeDtypeStruct((2 * n_exp, t), BF16),
                   jax.ShapeDtypeStruct((n_exp, nt * 128), F32),
                   jax.ShapeDtypeStruct((n_exp, nt * 128), F32),
                   jax.ShapeDtypeStruct((8, nt * UNIT_LANES), I32),
                   jax.ShapeDtypeStruct((n_exp, 128), F32)],
        scratch_shapes=[pltpu.VMEM((n_exp, 128), F32)],
        compiler_params=_cparams(("arbitrary",)),
        name="route_topk",
    )(scores_t, bias_col, tri, ltri)


def _unit_dest_kernel(start_ref, unit_ref, o_ref, *, n_exp):
    tbl = unit_ref[...]
    exp_of = tbl[1:2, :]
    dst = tbl[0:1, :]
    for e in range(n_exp):
        dst = dst + jnp.where(exp_of == e, start_ref[e], 0)
    o_ref[...] = jnp.concatenate([dst, tbl[2:3, :], jnp.zeros((6, tbl.shape[1]), I32)], axis=0)


def _unit_dest(region_start_units, units):
    blk = lambda: pl.BlockSpec((8, UNIT_LANES), lambda i, s: (0, i))
    return pl.pallas_call(
        functools.partial(_unit_dest_kernel, n_exp=region_start_units.shape[0]),
        grid_spec=pltpu.PrefetchScalarGridSpec(
            num_scalar_prefetch=1, grid=(units.shape[1] // UNIT_LANES,), in_specs=[blk()], out_specs=blk()),
        out_shape=jax.ShapeDtypeStruct(units.shape, I32),
        compiler_params=_cparams(("parallel",)),
        name="route_units",
    )(region_start_units, units)


def _unit_rows(i):
    return pl.ds(pl.multiple_of(i * UNIT, UNIT), UNIT)


def _for_each(count, fn, group=8):
    main = lax.shift_right_logical(count, int(math.log2(group)))

    def many(i, c):
        for r in range(group):
            fn(i * group + r)
        return c

    lax.fori_loop(0, main, many, 0)
    lax.fori_loop(main * group, count, lambda i, c: (fn(i), c)[1], 0)


def _dispatch_kernel(tail_lo_ref, tail_hi_ref, nu_ref, unit_ref, pos_ref, w_ref, rs_ref, rn_ref, h_ref, xs_ref,
                     s_scr, zero_scr, sem, zsem, *, n_exp):
    rows = s_scr.shape[1]
    td = h_ref.shape[0]
    step = pl.program_id(0)
    last = pl.num_programs(0) - 1
    slot = step % 2

    def zero_copy(g):
        return pltpu.make_async_copy(zero_scr, xs_ref.at[_unit_rows(g), :], zsem.at[0])

    def unit_copy(u, sl):
        return pltpu.make_async_copy(s_scr.at[sl, _unit_rows(u), :], xs_ref.at[_unit_rows(unit_ref[0, u]), :],
                                     sem.at[sl])

    def drain(count, sl):
        _for_each(count, lambda u: unit_copy(0, sl).wait())

    @pl.when(step >= 2)
    def _():
        drain(nu_ref[step - 2], slot)

    @pl.when(step == 0)
    def _():
        zero_scr[...] = jnp.zeros_like(zero_scr)

        def per_expert(e, c):
            lax.fori_loop(tail_lo_ref[e], tail_hi_ref[e], lambda g, cc: (zero_copy(g).start(), cc)[1], 0)
            lax.fori_loop(tail_lo_ref[e], tail_hi_ref[e], lambda g, cc: (zero_copy(g).wait(), cc)[1], 0)
            return c

        lax.fori_loop(0, n_exp, per_expert, 0)

    pos = pos_ref[...]
    h = h_ref[...]
    d = h.shape[1]
    w_dense = w_ref[...]
    run_lo = rs_ref[...]
    run_hi = run_lo + rn_ref[...]
    lane = lax.broadcasted_iota(I32, (SORT_CHUNK, s_scr.shape[2] - d), 1)
    for r0 in range(0, rows, SORT_CHUNK):
        srow = lax.broadcasted_iota(I32, (SORT_CHUNK, td), 0) + r0
        onehot = jnp.zeros((SORT_CHUNK, td), F32)
        for k in range(pos.shape[0]):
            onehot = jnp.where(srow == pos[k:k + 1, :], 1.0, onehot)
        s_e = (lax.broadcasted_iota(I32, (SORT_CHUNK, w_dense.shape[0]), 0) + r0).astype(F32)
        mine = jnp.logical_and(s_e >= run_lo, s_e < run_hi).astype(BF16)
        w_row = jnp.sum(onehot * _dot(mine, w_dense), axis=1, keepdims=True)
        w_hi = w_row.astype(BF16).astype(F32)
        w_lo = w_row - w_hi
        s_scr[slot, r0:r0 + SORT_CHUNK, :d] = _dot(onehot.astype(BF16), h).astype(BF16)
        s_scr[slot, r0:r0 + SORT_CHUNK, d:] = jnp.where(lane == 0, w_hi, jnp.where(lane == 1, w_lo, 0.0)).astype(BF16)

    _for_each(nu_ref[step], lambda u: unit_copy(u, slot).start())

    @pl.when(step == last)
    def _():
        drain(nu_ref[step], slot)

        @pl.when(step >= 1)
        def _():
            drain(nu_ref[step - 1], 1 - slot)


def _dispatch(tail_lo, tail_hi, n_units, unit_dst, pos, w_dense, run_start_rows, run_rows, h2, n_slots):
    t, d = h2.shape
    n_exp = tail_lo.shape[0]
    dw = d + W_LANES
    runs = pl.BlockSpec((None, 1, 2 * n_exp), lambda i, a, b, c: (i, 0, 0))
    return pl.pallas_call(
        functools.partial(_dispatch_kernel, n_exp=n_exp),
        grid_spec=pltpu.PrefetchScalarGridSpec(
            num_scalar_prefetch=3, grid=(t // TD,),
            in_specs=[pl.BlockSpec((8, UNIT_LANES), lambda i, a, b, c: (0, i), memory_space=pltpu.SMEM),
                      pl.BlockSpec((TOP_K, TD), lambda i, a, b, c: (0, i)),
                      pl.BlockSpec((2 * n_exp, TD), lambda i, a, b, c: (0, i)),
                      runs, runs,
                      pl.BlockSpec((TD, d), lambda i, a, b, c: (i, 0))],
            out_specs=pl.BlockSpec(memory_space=pl.ANY),
            scratch_shapes=[pltpu.VMEM((2, _tile_rows(n_exp), dw), BF16), pltpu.VMEM((UNIT, dw), BF16),
                            pltpu.SemaphoreType.DMA((2,)), pltpu.SemaphoreType.DMA((1,))]),
        out_shape=jax.ShapeDtypeStruct((n_slots, dw), BF16),
        compiler_params=_cparams(("arbitrary",)),
        name="moe_dispatch",
    )(tail_lo, tail_hi, n_units, unit_dst, pos, w_dense, run_start_rows, run_rows, h2)


def _expert_kernel(blk0_ref, nblk_ref, tot_ref, wg_ref, wu_ref, wd_ref, xs_ref, ys_ref, xbuf, obuf, wgu_scr, wd_scr,
                   sem_in, sem_out):
    e = pl.program_id(0)
    n = nblk_ref[e]
    b0 = blk0_ref[e]
    de = wg_ref.shape[1]
    wgu_scr[:, :de] = wg_ref[...].astype(BF16)
    wgu_scr[:, de:] = wu_ref[...].astype(BF16)
    wd_scr[...] = wd_ref[...].astype(BF16)

    total = tot_ref[0]

    def rows(g):
        return pl.ds(pl.multiple_of(g * MOE_BLK, MOE_BLK), MOE_BLK)

    def in_copy(g):
        sl = g % EXPERT_BUFS
        return pltpu.make_async_copy(xs_ref.at[rows(g), :], xbuf.at[sl], sem_in.at[sl])

    def out_copy(g):
        sl = g % EXPERT_BUFS
        return pltpu.make_async_copy(obuf.at[sl], ys_ref.at[rows(g), :], sem_out.at[sl])

    @pl.when(e == 0)
    def _():
        for g in range(EXPERT_BUFS - 1):
            @pl.when(g < total)
            def _():
                in_copy(g).start()

    def block(j, c):
        g = b0 + j
        sl = g % EXPERT_BUFS
        in_copy(g).wait()

        @pl.when(g + (EXPERT_BUFS - 1) < total)
        def _():
            in_copy(g + (EXPERT_BUFS - 1)).start()

        @pl.when(g >= EXPERT_BUFS)
        def _():
            out_copy(g - EXPERT_BUFS).wait()

        d = wd_scr.shape[1]
        gu = _dot(xbuf[sl, :, :d], wgu_scr[...])
        act = _silu(gu[:, :de]) * gu[:, de:]
        w_row = xbuf[sl, :, d:d + 1].astype(F32) + xbuf[sl, :, d + 1:d + 2].astype(F32)
        obuf[sl] = (_dot(act.astype(BF16), wd_scr[...]) * w_row).astype(BF16)
        out_copy(g).start()
        return c

    lax.fori_loop(0, n, block, 0)

    @pl.when(e == pl.num_programs(0) - 1)
    def _():
        for back in range(1, EXPERT_BUFS + 1):
            @pl.when(total >= back)
            def _():
                out_copy(total - back).wait()


def _experts(blk0, nblk, n_blocks, xs, w_gate, w_up, w_down, layer):
    n_slots, dw = xs.shape
    _, n_exp, d, de = w_gate.shape
    return pl.pallas_call(
        _expert_kernel,
        grid_spec=pltpu.PrefetchScalarGridSpec(
            num_scalar_prefetch=3, grid=(n_exp,),
            in_specs=[pl.BlockSpec((None, None, d, de), lambda e, a, b, c: (layer, e, 0, 0)),
                      pl.BlockSpec((None, None, d, de), lambda e, a, b, c: (layer, e, 0, 0)),
                      pl.BlockSpec((None, None, de, d), lambda e, a, b, c: (layer, e, 0, 0)),
                      pl.BlockSpec(memory_space=pl.ANY)],
            out_specs=pl.BlockSpec(memory_space=pl.ANY),
            scratch_shapes=[pltpu.VMEM((EXPERT_BUFS, MOE_BLK, dw), BF16), pltpu.VMEM((EXPERT_BUFS, MOE_BLK, d), BF16),
                            pltpu.VMEM((d, 2 * de), BF16), pltpu.VMEM((de, d), BF16),
                            pltpu.SemaphoreType.DMA((EXPERT_BUFS,)), pltpu.SemaphoreType.DMA((EXPERT_BUFS,))]),
        out_shape=jax.ShapeDtypeStruct((n_slots, d), BF16),
        compiler_params=_cparams(("arbitrary",)),
        name="moe_experts",
    )(blk0, nblk, n_blocks, w_gate, w_up, w_down, xs)


def _combine_kernel(unit_ref, next_ref, pos_ref, x_ref, h_ref, gate2_ref,
                    sg_ref, su_ref, sd_ref, lng_ref, lnb_ref, ys_ref, o_ref, s_scr, sem, *, alpha):
    rows = s_scr.shape[1]
    td = x_ref.shape[0]
    step = pl.program_id(0)
    slot = step % 2

    def unit_copy(tbl, u, sl):
        return pltpu.make_async_copy(ys_ref.at[_unit_rows(tbl[0, u]), :], s_scr.at[sl, _unit_rows(u), :], sem.at[sl])

    def fetch(tbl, sl):
        _for_each(tbl[1, 0], lambda u: unit_copy(tbl, u, sl).start())

    @pl.when(step == 0)
    def _():
        s_scr[...] = jnp.zeros_like(s_scr)
        fetch(unit_ref, 0)

    @pl.when(step + 1 < pl.num_programs(0))
    def _():
        fetch(next_ref, 1 - slot)

    h = h_ref[...]
    acc = _dot((_silu(_dot(h, sg_ref[...])) * _dot(h, su_ref[...])).astype(BF16), sd_ref[...])
    pos = pos_ref[...]
    _for_each(unit_ref[1, 0], lambda u: unit_copy(unit_ref, 0, slot).wait())
    for r0 in range(0, rows, SORT_CHUNK):
        lane = lax.broadcasted_iota(I32, (td, SORT_CHUNK), 1) + r0
        onehot = jnp.zeros((td, SORT_CHUNK), F32)
        for k in range(pos.shape[1]):
            onehot = jnp.where(lane == pos[:, k:k + 1], 1.0, onehot)
        acc = acc + _dot(onehot.astype(BF16), s_scr[slot, r0:r0 + SORT_CHUNK, :])
    o_ref[...] = _ln(alpha * x_ref[...] + gate2_ref[...] * acc) * lng_ref[...] + lnb_ref[...]


def _combine(unit_dst, pos_tok, xn, h2, modp, sh_gate, sh_up, sh_down, ln_g, ln_b, ys, seg_td, *, alpha, n_exp):
    t, d = xn.shape
    nt = t // TD
    tok = pl.BlockSpec((TD, d), lambda i: (i, 0))
    full = lambda a: pl.BlockSpec(a.shape, lambda i: (0,) * a.ndim)
    return pl.pallas_call(
        functools.partial(_combine_kernel, alpha=alpha),
        grid=(nt,),
        in_specs=[pl.BlockSpec((8, UNIT_LANES), lambda i: (0, i), memory_space=pltpu.SMEM),
                  pl.BlockSpec((8, UNIT_LANES), lambda i: (0, jnp.minimum(i + 1, nt - 1)), memory_space=pltpu.SMEM),
                  pl.BlockSpec((TD, pos_tok.shape[1]), lambda i: (i, 0)),
                  tok, tok,
                  pl.BlockSpec((None, None, 1, d), lambda i: (5, seg_td(i), 0, 0)),
                  full(sh_gate), full(sh_up), full(sh_down), full(ln_g), full(ln_b),
                  pl.BlockSpec(memory_space=pl.ANY)],
        out_specs=tok,
        out_shape=jax.ShapeDtypeStruct((t, d), F32),
        scratch_shapes=[pltpu.VMEM((2, _tile_rows(n_exp), d), BF16), pltpu.SemaphoreType.DMA((2,))],
        compiler_params=_cparams(("arbitrary",)),
        name="moe_combine",
    )(unit_dst, unit_dst, pos_tok, xn, h2, modp, sh_gate, sh_up, sh_down, ln_g, ln_b, ys)


def _moe(xn, h2, scores_t, modp, seg_td, router_bias, w_gate, w_up, w_down, sh_gate, sh_up, sh_down,
         ln_g, ln_b, tri, ltri, *, alpha, layer):
    t = xn.shape[0]
    n_exp = scores_t.shape[0]
    assert _tile_rows(n_exp) % SORT_CHUNK == 0 and _tile_rows(n_exp) <= UNIT * UNIT_LANES
    pos, w_sel, run_start, run_len, units, tot = _route(scores_t, router_bias.reshape(n_exp, 1), tri, ltri)
    as_rows = lambda a: jnp.tile(a[:, ::128].T, (1, 2))[:, None, :]
    used = tot[:, 0].astype(I32)
    region = (used + MOE_BLK - 1) // MOE_BLK * MOE_BLK
    region_end = jnp.cumsum(region)
    region_start = region_end - region
    n_slots = TOP_K * t + n_exp * UNIT * (t // TD) + n_exp * MOE_BLK
    unit_dst = _unit_dest((region_start // UNIT).astype(I32), units)
    xs = _dispatch(((region_start + used) // UNIT).astype(I32), (region_end // UNIT).astype(I32),
                   unit_dst[1, ::UNIT_LANES], unit_dst, pos, w_sel, as_rows(run_start), as_rows(run_len), h2,
                   n_slots)
    ys = _experts((region_start // MOE_BLK).astype(I32), (region // MOE_BLK).astype(I32),
                  (region_end[-1:] // MOE_BLK).astype(I32), xs, w_gate, w_up, w_down, layer)
    return _combine(unit_dst, pos.T, xn, h2, modp, sh_gate, sh_up, sh_down, ln_g, ln_b, ys, seg_td,
                    alpha=alpha, n_exp=n_exp)


def kernel(x, c, ctx, c_ctx, w_mod, b_mod, w_in, b_in, lam_q1, lam_k1, lam_q2, lam_k2, attn_norm_g, sg_ln_g, sg_ln_b, sg_w, sg_b, conv_w, conv_b, lru_wa, lru_ba, lru_wx, lru_bx, lru_lam, w_branch, w_out, ln1_g, ln1_b, w_router, router_bias, moe_w_gate, moe_w_up, moe_w_down, sh_w_gate, sh_w_up, sh_w_down, ln2_g, ln2_b):
    batch, n_lat, d = x.shape
    n_ctx = ctx.shape[1]
    depth = w_mod.shape[0]
    n_exp = w_router.shape[2]
    t_lat = batch * n_lat
    t_ctx = batch * n_ctx
    t_all = t_lat + t_ctx
    assert n_lat % TM == 0 and t_ctx % TM == 0 and n_ctx % B_CHUNK == 0 and t_lat % n_ctx == 0
    assert batch + 1 <= MOD_ROWS and TM % TD == 0
    alpha = (2 * depth) ** 0.25
    cw = conv_w.shape[2]
    tiles_per_batch = n_lat // TM

    seg = lambda i: jnp.minimum(i // tiles_per_batch, batch)
    seg_td = lambda i: jnp.minimum(i // (n_lat // TD), batch)
    tm_in = 2 * TM if (n_lat % (2 * TM) == 0 and t_ctx % (2 * TM) == 0) else TM
    seg_in = lambda i: jnp.minimum(i // (n_lat // tm_in), batch)
    tbl_idx = lambda i: jnp.where(i < t_lat // tm_in, i % (n_lat // tm_in), n_lat // tm_in)

    x_all = jnp.concatenate([x.reshape(t_lat, d), ctx.reshape(t_ctx, d)], axis=0)
    c_all = jnp.zeros((MOD_ROWS, d), F32).at[:batch].set(c).at[batch].set(c_ctx)
    tables = _rope_tables(n_lat, tm_in)
    tri = (jnp.arange(TD)[:, None] <= jnp.arange(TD)[None, :]).astype(BF16)
    ltri = (jnp.arange(n_exp)[None, :] < jnp.arange(n_exp)[:, None]).astype(BF16)
    row = lambda v: v.reshape(1, -1)

    def dense_blocks(w):
        nb, bi, bj = w.shape
        eye = jnp.eye(nb, dtype=w.dtype)
        return (w[:, :, None, :] * eye[:, None, :, None]).reshape(nb * bi, nb * bj)

    for l in range(depth):
        last = l == depth - 1
        lam_init = 0.8 - 0.6 * math.exp(-0.3 * l)
        mod = _mod_rows(c_all, w_mod[l].astype(BF16), row(b_mod[l]))
        modp = mod.reshape(MOD_ROWS, 6, 1, d).transpose(1, 0, 2, 3)

        qr, kr, vb, proj = _in_proj(x_all, modp, w_in[l].astype(BF16), row(b_in[l]), tables, seg_in, tbl_idx, tm_in)

        lam_vecs = jnp.stack([lam_q1[l], lam_k1[l], lam_q2[l], lam_k2[l]])
        gain = attn_norm_g[l].reshape(A_HEADS, A_V_DIM, 1)
        attn = functools.partial(_attention, qr, kr, vb.T, lam_vecs, gain, lam_init=lam_init,
                                 batch=batch, n_lat=n_lat, n_ctx=n_ctx)
        n_rows = t_lat if last else t_all
        o_a = attn(latent_queries=True, out_rows=n_rows)
        if not last:
            o_a = attn(latent_queries=False, out_rows=n_rows, into=o_a)

        o_b = _spatial_gating(proj, row(sg_ln_g[l]), row(sg_ln_b[l]), sg_w[l].astype(BF16),
                              sg_b[l].reshape(B_GROUPS, B_CHUNK, 1))

        lru = functools.partial(_lru_pass, proj, conv_w=conv_w[l], conv_b=row(conv_b[l]), batch=batch)
        zeros_h = jnp.zeros((batch, 8, cw), F32)
        hf = {}
        for direction in range(2):
            wd = jnp.concatenate([dense_blocks(lru_wa[l, direction]), dense_blocks(lru_wx[l, direction])],
                                 axis=1).astype(BF16)
            par = dict(w_dense=wd, ba=row(lru_ba[l, direction]), bx=row(lru_bx[l, direction]),
                       lam=row(lru_lam[l, direction]), reverse=direction == 1, final=direction == 1)
            shared = direction == 1 and not last
            h_ctx, edge = lru(hf.get("ctx"), zeros_h, seq=n_ctx, tl=n_ctx, row0=t_lat,
                              out_rows=t_all if shared else None, out_row0=t_lat if shared else 0, **par)
            h_lat, _ = lru(hf.get("lat"), edge, seq=n_lat, tl=TM, row0=0,
                           out_rows=t_all if shared else None, into=h_ctx if shared else None, **par)
            hf = {"ctx": h_ctx, "lat": h_lat}
        o_c = hf["lat"]
        wr_t = w_router[l].T
        wr_hi = wr_t.astype(BF16)
        wr_lo = (wr_t - wr_hi.astype(F32)).astype(BF16)
        xn, h2, scores_t = _merge(x_all, o_a, o_b, o_c, proj, modp, w_branch[l].astype(BF16),
                                  w_out[l].astype(BF16), row(ln1_g[l]), row(ln1_b[l]), wr_hi, wr_lo, seg,
                                  n_rows=n_rows, alpha=alpha)
        x_all = _moe(xn, h2, scores_t, modp, seg_td, router_bias[l], moe_w_gate,
                     moe_w_up, moe_w_down, sh_w_gate[l].astype(BF16),
                     sh_w_up[l].astype(BF16), sh_w_down[l].astype(BF16), row(ln2_g[l]), row(ln2_b[l]), tri, ltri,
                     alpha=alpha, layer=l)
    return x_all[:t_lat].reshape(batch, n_lat, d)
```

```python
import functools
import math

import jax
import jax.numpy as jnp
from jax import lax
from jax.experimental import pallas as pl
from jax.experimental.pallas import tpu as pltpu

F32 = jnp.float32
BF16 = jnp.bfloat16
I32 = jnp.int32

A_HEADS = 4
A_QK_DIM = 64
A_V_DIM = 2 * A_QK_DIM
GRID_W = 64
ROPE_THETA = 10000.0
B_CHUNK = 128
B_GROUPS = 4
C_BLOCKS = 8
C_POW = 8.0
TOP_K = 8
N_GROUPS = 8
TOPK_GROUPS = 4
ROUTED_SCALE = 2.5
LN_EPS = 1e-6
RMS_EPS = 1e-5

TM = 512
MOE_BLK = 512
TD = 256
UNIT = 16
UNIT_LANES = 256
SORT_CHUNK = 512
W_LANES = 128
EXPERT_BUFS = 4
LRU_BLOCK = 64
HALO = 16
ATTN_HEADS = 4
ATTN_SUB = 1024
MOD_ROWS = 16
VMEM_LIMIT = 56 * 1024 * 1024


def _cparams(sem):
    return pltpu.CompilerParams(dimension_semantics=sem, vmem_limit_bytes=VMEM_LIMIT)


def _ln(x):
    mu = jnp.mean(x, axis=-1, keepdims=True)
    xc = x - mu
    var = jnp.mean(xc * xc, axis=-1, keepdims=True)
    return xc * lax.rsqrt(var + LN_EPS)


def _gelu(x):
    cdf = 0.5 * (1.0 + jnp.tanh(math.sqrt(2.0 / math.pi) * (x + 0.044715 * (x * x * x))))
    return x * cdf


def _sigmoid(x):
    return 1.0 / (1.0 + jnp.exp(-x))


def _silu(x):
    return x * _sigmoid(x)


def _dot(a, b):
    return jnp.dot(a, b, preferred_element_type=F32)


def _dot_nt(a, b):
    return lax.dot_general(a, b, (((1,), (1,)), ((), ())), preferred_element_type=F32)


def _mod_kernel(c_ref, w_ref, b_ref, o_ref):
    o_ref[...] = _dot(_silu(c_ref[...]).astype(BF16), w_ref[...]) + b_ref[...]


def _mod_rows(c_all, w, b):
    m, d = c_all.shape
    n = w.shape[1]
    tn = 1536
    return pl.pallas_call(
        _mod_kernel,
        grid=(n // tn,),
        in_specs=[pl.BlockSpec((m, d), lambda j: (0, 0)),
                  pl.BlockSpec((d, tn), lambda j: (0, j)),
                  pl.BlockSpec((1, tn), lambda j: (0, j))],
        out_specs=pl.BlockSpec((m, tn), lambda j: (0, j)),
        out_shape=jax.ShapeDtypeStruct((m, n), F32),
        compiler_params=_cparams(("parallel",)),
        name="adaln_rows",
    )(c_all, w, b)


REST_TILE = 1280
QKV_TILES = 3
COL_U, COL_S, COL_X, COL_Y, COL_G = 0, 1, 2, 3, 4


def _in_proj_kernel(x_ref, sh_ref, sc_ref, w_ref, b_ref, cos_ref, sa_ref, sb_ref,
                    q_ref, k_ref, v_ref, h_ref):
    j = pl.program_id(1)

    @pl.when(j == 0)
    def _():
        h = _ln(x_ref[...]) * (1.0 + sc_ref[...]) + sh_ref[...]
        h_ref[...] = h.astype(BF16)

    acc = _dot(h_ref[...], w_ref[...]) + b_ref[...]

    def rope(dst_ref, scale):
        hw = cos_ref.shape[1]
        for c0 in range(0, acc.shape[1], hw):
            x = acc[:, c0:c0 + hw]
            r = x * cos_ref[...] + pltpu.roll(x, hw - 16, 1) * sa_ref[...] + pltpu.roll(x, 16, 1) * sb_ref[...]
            dst_ref[:, c0:c0 + hw] = (r * scale).astype(BF16)

    @pl.when(j == 0)
    def _():
        rope(q_ref, A_QK_DIM ** -0.5 * math.log2(math.e))

    @pl.when(j == 1)
    def _():
        rope(k_ref, 1.0)

    @pl.when(j == 2)
    def _():
        v_ref[...] = acc.astype(BF16)


def _rest_proj_kernel(h_ref, w_ref, b_ref, o_ref):
    o_ref[...] = (_dot(h_ref[...], w_ref[...]) + b_ref[...]).astype(o_ref.dtype)


def _in_proj(x_all, modp, w, b, tables, seg, tbl_idx, tm):
    t, d = x_all.shape
    tn = 2 * A_HEADS * A_QK_DIM
    n_qkv = QKV_TILES * tn
    n_rest = w.shape[1] - n_qkv
    cos, sa, sb = tables
    tbl = pl.BlockSpec((tm, cos.shape[1]), lambda i, j: (tbl_idx(i), 0))
    qkv = pl.BlockSpec((tm, tn), lambda i, j: (i, 0))
    qkv_shape = jax.ShapeDtypeStruct((t, tn), BF16)
    q, k, v, h = pl.pallas_call(
        _in_proj_kernel,
        grid=(t // tm, QKV_TILES),
        in_specs=[pl.BlockSpec((tm, d), lambda i, j: (i, 0)),
                  pl.BlockSpec((None, None, 1, d), lambda i, j: (0, seg(i), 0, 0)),
                  pl.BlockSpec((None, None, 1, d), lambda i, j: (1, seg(i), 0, 0)),
                  pl.BlockSpec((d, tn), lambda i, j: (0, j)),
                  pl.BlockSpec((1, tn), lambda i, j: (0, j)),
                  tbl, tbl, tbl],
        out_specs=[qkv, qkv, qkv, pl.BlockSpec((tm, d), lambda i, j: (i, 0))],
        out_shape=[qkv_shape, qkv_shape, qkv_shape, jax.ShapeDtypeStruct((t, d), BF16)],
        compiler_params=_cparams(("parallel", "arbitrary")),
        name="in_proj_qkv",
    )(x_all, modp, modp, w, b, cos, sa, sb)
    tr = REST_TILE if n_rest % REST_TILE == 0 else tn
    rest = pl.pallas_call(
        _rest_proj_kernel,
        grid=(t // tm, n_rest // tr),
        in_specs=[pl.BlockSpec((tm, d), lambda i, j: (i, 0)),
                  pl.BlockSpec((d, tr), lambda i, j: (0, j)),
                  pl.BlockSpec((1, tr), lambda i, j: (0, j))],
        out_specs=pl.BlockSpec((tm, tr), lambda i, j: (i, j)),
        out_shape=jax.ShapeDtypeStruct((t, n_rest), BF16),
        compiler_params=_cparams(("parallel", "parallel")),
        name="in_proj_rest",
    )(h, w[:, n_qkv:], b[:, n_qkv:])
    return q, k, v, rest


def _rope_tables(n, tm):
    rows = n // GRID_W
    pos_row = jnp.repeat(jnp.arange(rows), GRID_W).astype(F32)
    pos_col = jnp.tile(jnp.arange(GRID_W), rows).astype(F32)
    quarter = A_QK_DIM // 4
    inv = ROPE_THETA ** (-jnp.arange(quarter, dtype=F32) / quarter)
    ang_r = pos_row[:, None] * inv
    ang_c = pos_col[:, None] * inv
    ang = jnp.concatenate([ang_r, ang_r, ang_c, ang_c], axis=-1)
    cos = jnp.cos(ang)
    sin = jnp.sin(ang)
    first_half = (jnp.arange(A_QK_DIM) % (2 * quarter)) < quarter
    sa = jnp.where(first_half, -sin, 0.0)
    sb = jnp.where(first_half, 0.0, sin)

    def widen(tbl, fill):
        tbl = jnp.tile(tbl, (1, 2))
        return jnp.concatenate([tbl, jnp.full((tm, tbl.shape[1]), fill, F32)], axis=0)

    return widen(cos, 1.0), widen(sa, 0.0), widen(sb, 0.0)


def _attn_kernel(*refs, lam_init, with_lat, sub):
    if with_lat:
        lam_ref, gain_ref, q_ref, kc_ref, vc_ref, kl_ref, vl_ref, o_ref = refs
    else:
        lam_ref, gain_ref, q_ref, kc_ref, vc_ref, _, o_ref = refs
    lv = lam_ref[...]
    lam = (jnp.exp(jnp.sum(lv[0:1] * lv[1:2], axis=-1, keepdims=True))
           - jnp.exp(jnp.sum(lv[2:3] * lv[3:4], axis=-1, keepdims=True)) + lam_init)
    chunks = [(kc_ref, vc_ref, 0, kc_ref.shape[0])]
    if with_lat:
        ck = min(sub, kl_ref.shape[0])
        chunks += [(kl_ref, vl_ref, r0, ck) for r0 in range(0, kl_ref.shape[0], ck)]
    for hh in range(gain_ref.shape[0]):
        head = slice(hh * A_V_DIM, (hh + 1) * A_V_DIM)
        probs, dens = [], []
        for c in range(2):
            cols = slice(hh * A_V_DIM + c * A_QK_DIM, hh * A_V_DIM + (c + 1) * A_QK_DIM)
            qc = q_ref[:, cols]
            scores = [_dot_nt(k_ref[r0:r0 + n, cols], qc) for k_ref, _, r0, n in chunks]
            m = None
            for s in scores:
                ms = jnp.max(s, axis=0, keepdims=True)
                m = ms if m is None else jnp.maximum(m, ms)
            p = [jnp.exp2(s - m) for s in scores]
            den = None
            for pj in p:
                ds = jnp.sum(pj, axis=0, keepdims=True)
                den = ds if den is None else den + ds
            probs.append(p)
            dens.append(den)
        ratio = lam * dens[0] / dens[1]
        o = None
        for p1, p2, (_, v_ref, r0, n) in zip(probs[0], probs[1], chunks):
            w = p1 - p2 * ratio
            ps = _dot(v_ref[head, r0:r0 + n], w.astype(BF16))
            o = ps if o is None else o + ps
        o = o / dens[0]
        o = o * lax.rsqrt(jnp.mean(o * o, axis=0, keepdims=True) + RMS_EPS)
        o_ref[:, head] = ((o * gain_ref[hh]) * (1.0 - lam_init)).T


def _attention(qr, kr, vt, lam_vecs, gain, *, lam_init, batch, n_lat, n_ctx, latent_queries, out_rows, into=None):
    hw = ATTN_HEADS * A_V_DIM
    ctx_blk0 = batch * n_lat // n_ctx
    lam_spec = pl.BlockSpec(lam_vecs.shape, lambda b, h, i: (0, 0))
    gain_spec = pl.BlockSpec((ATTN_HEADS, A_V_DIM, 1), lambda b, h, i: (h, 0, 0))
    kv_ctx = pl.BlockSpec((n_ctx, hw), lambda b, h, i: (ctx_blk0 + b, h))
    vt_ctx = pl.BlockSpec((hw, n_ctx), lambda b, h, i: (h, ctx_blk0 + b))
    if latent_queries:
        tq = 256
        nq = n_lat // tq
        q_spec = pl.BlockSpec((tq, hw), lambda b, h, i: (b * nq + i, h))
        kv_lat = pl.BlockSpec((n_lat, hw), lambda b, h, i: (b, h))
        vt_lat = pl.BlockSpec((hw, n_lat), lambda b, h, i: (h, b))
        in_specs = [lam_spec, gain_spec, q_spec, kv_ctx, vt_ctx, kv_lat, vt_lat]
        args = (lam_vecs, gain, qr, kr, vt, kr, vt)
        out_blk0 = 0
        aliases = {}
    else:
        tq = n_ctx
        nq = 1
        q_spec = pl.BlockSpec((tq, hw), lambda b, h, i: (ctx_blk0 + b, h))
        in_specs = [lam_spec, gain_spec, q_spec, kv_ctx, vt_ctx, pl.BlockSpec(memory_space=pl.ANY)]
        args = (lam_vecs, gain, qr, kr, vt, into)
        out_blk0 = ctx_blk0
        aliases = {len(args) - 1: 0}
    return pl.pallas_call(
        functools.partial(_attn_kernel, lam_init=lam_init, with_lat=latent_queries, sub=ATTN_SUB),
        grid=(batch, A_HEADS // ATTN_HEADS, nq),
        in_specs=in_specs,
        out_specs=pl.BlockSpec((tq, hw), lambda b, h, i: (out_blk0 + b * nq + i, h)),
        out_shape=jax.ShapeDtypeStruct((out_rows, A_HEADS * A_V_DIM), F32),
        input_output_aliases=aliases,
        compiler_params=_cparams(("parallel", "parallel", "arbitrary")),
        name="diff_attn_lat" if latent_queries else "diff_attn_ctx",
    )(*args)


def _sg_kernel(u_ref, s_ref, g_ref, b_ref, w_ref, bs_ref, o_ref):
    gw = B_CHUNK
    for ci in range(TM // B_CHUNK):
        rows = slice(ci * B_CHUNK, (ci + 1) * B_CHUNK)
        for g in range(B_GROUPS):
            cols = slice(g * gw, (g + 1) * gw)
            vn = _ln(_gelu(s_ref[rows, cols].astype(F32))) * g_ref[:, cols] + b_ref[:, cols]
            mixed = _dot(w_ref[g], vn.astype(BF16)) + bs_ref[g]
            o_ref[rows, cols] = _gelu(u_ref[rows, cols].astype(F32)) * mixed


def _spatial_gating(proj, ln_g, ln_b, w_s, b_s):
    t = proj.shape[0]
    w = B_GROUPS * B_CHUNK
    full = lambda a: pl.BlockSpec(a.shape, lambda i: (0,) * a.ndim)
    return pl.pallas_call(
        _sg_kernel,
        grid=(t // TM,),
        in_specs=[pl.BlockSpec((TM, w), lambda i: (i, COL_U)),
                  pl.BlockSpec((TM, w), lambda i: (i, COL_S)),
                  full(ln_g), full(ln_b), full(w_s), full(b_s)],
        out_specs=pl.BlockSpec((TM, w), lambda i: (i, 0)),
        out_shape=jax.ShapeDtypeStruct((t, w), F32),
        compiler_params=_cparams(("parallel",)),
        name="spatial_gating",
    )(proj, proj, ln_g, ln_b, w_s, b_s)


def _lru_kernel(*refs, reverse, final, n_tiles, tl, aliased):
    if aliased:
        refs = refs[:12] + refs[13:]
    if final:
        (x_ref, xp_ref, xn_ref, cw_ref, cb_ref, w_ref, ba_ref, bx_ref, lam_ref, h0_ref,
         hf_ref, y_ref, o_ref, hl_ref, carry) = refs
    else:
        (x_ref, xp_ref, xn_ref, cw_ref, cb_ref, w_ref, ba_ref, bx_ref, lam_ref, h0_ref,
         o_ref, hl_ref, carry) = refs
    step = pl.program_id(1)

    @pl.when(step == 0)
    def _():
        carry[...] = h0_ref[...]

    ti = (n_tiles - 1 - step) if reverse else step
    x = x_ref[...].astype(F32)
    cw = x.shape[-1]
    row = lax.broadcasted_iota(I32, x.shape, 0)
    has_prev = ti > 0
    has_next = ti < n_tiles - 1
    p6 = jnp.where(has_prev, xp_ref[HALO - 2:HALO - 1, :].astype(F32), 0.0)
    p7 = jnp.where(has_prev, xp_ref[HALO - 1:HALO, :].astype(F32), 0.0)
    n0 = jnp.where(has_next, xn_ref[0:1, :].astype(F32), 0.0)
    xm1 = jnp.where(row == 0, p7, pltpu.roll(x, 1, 0))
    xm2 = jnp.where(row == 0, p6, jnp.where(row == 1, p7, pltpu.roll(x, 2, 0)))
    xp1 = jnp.where(row == tl - 1, n0, pltpu.roll(x, tl - 1, 0))
    taps = cw_ref[...]
    xs = taps[0:1] * xm2 + taps[1:2] * xm1 + taps[2:3] * x + taps[3:4] * xp1 + cb_ref[...]

    z = _dot(xs.astype(BF16), w_ref[...])
    r = _sigmoid(z[:, :cw] + ba_ref[...])
    ig = _sigmoid(z[:, cw:] + bx_ref[...])
    nl = -lam_ref[...]
    softplus = jnp.maximum(nl, 0.0) + jnp.log(1.0 + jnp.exp(-jnp.abs(nl)))
    log_a = -C_POW * r * softplus
    a = jnp.exp(log_a)
    u = jnp.sqrt(1.0 - jnp.exp(2.0 * log_a)) * (ig * xs)

    sb = min(LRU_BLOCK, tl)
    in_blk = row % sb
    d = 1
    while d < sb:
        if reverse:
            ok = in_blk < sb - d
            shift = tl - d
        else:
            ok = in_blk >= d
            shift = d
        a_sh = jnp.where(ok, pltpu.roll(a, shift, 0), 1.0)
        u_sh = jnp.where(ok, pltpu.roll(u, shift, 0), 0.0)
        u = u + a * u_sh
        a = a * a_sh
        d *= 2
    state = carry[0:1, :]
    blocks = [None] * (tl // sb)
    for b in (reversed(range(tl // sb)) if reverse else range(tl // sb)):
        hb = u[b * sb:(b + 1) * sb] + a[b * sb:(b + 1) * sb] * state
        state = hb[0:1, :] if reverse else hb[sb - 1:sb, :]
        blocks[b] = hb
    h = jnp.concatenate(blocks, axis=0)
    edge = state
    carry[...] = jnp.broadcast_to(edge, carry.shape)
    hl_ref[...] = jnp.broadcast_to(edge, hl_ref.shape)
    if final:
        o_ref[...] = _gelu(y_ref[...].astype(F32)) * (hf_ref[...] + h)
    else:
        o_ref[...] = h


def _lru_pass(proj, hf, h0, conv_w, conv_b, w_dense, ba, bx, lam, *, batch, seq, tl, row0, reverse, final,
              out_rows=None, out_row0=0, into=None):
    t_all = proj.shape[0]
    out_rows = batch * seq if out_rows is None else out_rows
    out_blk0 = out_row0 // tl
    assert into is None or final
    cw = conv_w.shape[1]
    n_tiles = seq // tl
    base = row0 // tl
    sub = tl // HALO
    last_halo = t_all // HALO - 1

    def tile(b, s):
        ti = (n_tiles - 1 - s) if reverse else s
        return base + b * n_tiles + ti

    def loc(b, s):
        ti = (n_tiles - 1 - s) if reverse else s
        return b * n_tiles + ti

    full = lambda a: pl.BlockSpec(a.shape, lambda b, s: (0,) * a.ndim)
    in_specs = [pl.BlockSpec((tl, cw), lambda b, s: (tile(b, s), COL_X)),
                pl.BlockSpec((HALO, cw), lambda b, s: (jnp.maximum(tile(b, s) * sub - 1, 0), COL_X)),
                pl.BlockSpec((HALO, cw), lambda b, s: (jnp.minimum((tile(b, s) + 1) * sub, last_halo), COL_X)),
                full(conv_w), full(conv_b), full(w_dense), full(ba), full(bx), full(lam),
                pl.BlockSpec((None, 8, cw), lambda b, s: (b, 0, 0))]
    args = [proj, proj, proj, conv_w, conv_b, w_dense, ba, bx, lam, h0]
    if final:
        in_specs += [pl.BlockSpec((tl, cw), lambda b, s: (loc(b, s), 0)),
                     pl.BlockSpec((tl, cw), lambda b, s: (tile(b, s), COL_Y))]
        args += [hf, proj]
    aliases = {}
    if into is not None:
        aliases = {len(args): 0}
        in_specs += [pl.BlockSpec(memory_space=pl.ANY)]
        args += [into]
    return pl.pallas_call(
        functools.partial(_lru_kernel, reverse=reverse, final=final, n_tiles=n_tiles, tl=tl,
                          aliased=into is not None),
        grid=(batch, n_tiles),
        in_specs=in_specs,
        out_specs=[pl.BlockSpec((tl, cw), lambda b, s: (out_blk0 + loc(b, s), 0)),
                   pl.BlockSpec((None, 8, cw), lambda b, s: (b, 0, 0))],
        out_shape=[jax.ShapeDtypeStruct((out_rows, cw), F32),
                   jax.ShapeDtypeStruct((batch, 8, cw), F32)],
        scratch_shapes=[pltpu.VMEM((8, cw), F32)],
        input_output_aliases=aliases,
        compiler_params=_cparams(("parallel", "arbitrary")),
        name="rglru_%s_%s" % ("bwd" if reverse else "fwd", "lat" if row0 == 0 else "ctx"),
    )(*args)


def _merge_kernel(x_ref, oa_ref, ob_ref, oc_ref, g0, g1, g2, g3, g4, g5, wb_ref, wo_ref,
                  gate1_ref, sh2_ref, sc2_ref, lng_ref, lnb_ref, wrh_ref, wrl_ref,
                  xn_ref, h2_ref, sc_ref, *, alpha):
    gates = ((g0, g1), (g2, g3), (g4, g5))
    mix = None
    for r, o_ref in enumerate((oa_ref, ob_ref, oc_ref)):
        proj = _dot(o_ref[...].astype(BF16), wb_ref[r])
        gate = jnp.concatenate([_sigmoid(gates[r][0][...].astype(F32)), _sigmoid(gates[r][1][...].astype(F32))],
                               axis=-1)
        term = gate * proj
        mix = term if mix is None else mix + term
    out = _dot(mix.astype(BF16), wo_ref[...])
    xn = _ln(alpha * x_ref[...] + gate1_ref[...] * out) * lng_ref[...] + lnb_ref[...]
    xn_ref[...] = xn
    h2 = _ln(xn) * (1.0 + sc2_ref[...]) + sh2_ref[...]
    h_hi = h2.astype(BF16)
    h2_ref[...] = h_hi
    h_lo = (h2 - h_hi.astype(F32)).astype(BF16)
    logits = _dot_nt(wrh_ref[...], h_hi) + (_dot_nt(wrh_ref[...], h_lo) + _dot_nt(wrl_ref[...], h_hi))
    sc_ref[...] = _sigmoid(logits)


def _merge(x_all, o_a, o_b, o_c, proj, modp, w_branch, w_out, ln_g, ln_b, wr_hi, wr_lo, seg, *, n_rows, alpha):
    d = x_all.shape[1]
    bw = o_a.shape[1]
    n_exp = wr_hi.shape[0]
    tok = lambda w: pl.BlockSpec((TM, w), lambda i: (i, 0))
    gate = lambda c: pl.BlockSpec((TM, bw), lambda i, c=c: (i, c))
    mod = lambda k: pl.BlockSpec((None, None, 1, d), lambda i, k=k: (k, seg(i), 0, 0))
    full = lambda a: pl.BlockSpec(a.shape, lambda i: (0,) * a.ndim)
    return pl.pallas_call(
        functools.partial(_merge_kernel, alpha=alpha),
        grid=(n_rows // TM,),
        in_specs=[tok(d), tok(bw), tok(bw), tok(bw)] + [gate(COL_G + c) for c in range(6)]
                 + [full(w_branch), full(w_out), mod(2), mod(3), mod(4), full(ln_g), full(ln_b),
                    full(wr_hi), full(wr_lo)],
        out_specs=[tok(d), tok(d), pl.BlockSpec((n_exp, TM), lambda i: (0, i))],
        out_shape=[jax.ShapeDtypeStruct((n_rows, d), F32),
                   jax.ShapeDtypeStruct((n_rows, d), BF16),
                   jax.ShapeDtypeStruct((n_exp, n_rows), F32)],
        compiler_params=_cparams(("parallel",)),
        name="merge_residual_router",
    )(x_all, o_a, o_b, o_c, proj, proj, proj, proj, proj, proj, w_branch, w_out,
      modp, modp, modp, ln_g, ln_b, wr_hi, wr_lo)


def _tile_rows(n_exp):
    return TOP_K * TD + n_exp * UNIT


def _route_kernel(s_ref, bias_ref, tri_ref, ltri_ref, pos_ref, w_ref, rs_ref, rn_ref, unit_ref, tot_ref, carry):
    @pl.when(pl.program_id(0) == 0)
    def _():
        carry[...] = jnp.zeros_like(carry)

    s = s_ref[...]
    n_exp, tn = s.shape
    per = n_exp // N_GROUPS
    neg = -jnp.inf
    biased = s + bias_ref[...]
    sub = lax.broadcasted_iota(I32, (per, tn), 0)
    gs_rows = []
    for g in range(N_GROUPS):
        blk = biased[g * per:(g + 1) * per, :]
        m1 = jnp.max(blk, axis=0, keepdims=True)
        first = jnp.min(jnp.where(blk == m1, sub, per), axis=0, keepdims=True)
        m2 = jnp.max(jnp.where(sub == first, neg, blk), axis=0, keepdims=True)
        gs_rows.append(m1 + m2)
    gs = jnp.concatenate(gs_rows, axis=0)
    gi = lax.broadcasted_iota(I32, gs.shape, 0)
    g_ok = jnp.zeros(gs.shape, F32)
    cur = gs
    for _ in range(TOPK_GROUPS):
        m = jnp.max(cur, axis=0, keepdims=True)
        pick = jnp.min(jnp.where(cur == m, gi, N_GROUPS), axis=0, keepdims=True)
        hit = gi == pick
        g_ok = jnp.where(hit, 1.0, g_ok)
        cur = jnp.where(hit, neg, cur)
    ok_rows = [jnp.broadcast_to(g_ok[g:g + 1, :], (per, tn)) for g in range(N_GROUPS)]
    expert_ok = jnp.concatenate(ok_rows, axis=0)
    masked = jnp.where(expert_ok > 0.0, biased, neg)
    ei = lax.broadcasted_iota(I32, s.shape, 0)
    pick_s, hits = [], []
    sel = jnp.zeros(s.shape, F32)
    for _ in range(TOP_K):
        m = jnp.max(masked, axis=0, keepdims=True)
        pick = jnp.min(jnp.where(masked == m, ei, n_exp), axis=0, keepdims=True)
        hit = ei == pick
        hits.append(hit)
        pick_s.append(jnp.sum(jnp.where(hit, s, 0.0), axis=0, keepdims=True))
        sel = jnp.where(hit, 1.0, sel)
        masked = jnp.where(hit, neg, masked)
    tot = pick_s[0]
    for k in range(1, TOP_K):
        tot = tot + pick_s[k]
    w = jnp.where(sel > 0.0, s, 0.0) / tot * ROUTED_SCALE
    w_hi = w.astype(BF16)
    w_ref[...] = jnp.concatenate([w_hi, (w - w_hi.astype(F32)).astype(BF16)], axis=0)

    incl = _dot(sel.astype(BF16), tri_ref[...])
    count = incl[:, tn - 1:tn]
    run = jnp.floor((count + (UNIT - 1)) * (1.0 / UNIT)) * UNIT
    run_start = _dot(ltri_ref[...], jnp.broadcast_to(run, (n_exp, 128)).astype(BF16))[:, 0:1]
    row_in_tile = incl - sel + run_start
    pos = [jnp.sum(jnp.where(hit, row_in_tile, 0.0), axis=0, keepdims=True) for hit in hits]
    pos_ref[...] = jnp.concatenate(pos, axis=0).astype(I32)
    rs_ref[...] = jnp.broadcast_to(run_start, rs_ref.shape)
    rn_ref[...] = jnp.broadcast_to(run, rn_ref.shape)

    region_used = carry[:, 0:1]
    u = lax.broadcasted_iota(I32, (n_exp, UNIT_LANES), 1).astype(F32)
    eu = lax.broadcasted_iota(I32, (n_exp, UNIT_LANES), 0).astype(F32)
    u0 = run_start * (1.0 / UNIT)
    nu = run * (1.0 / UNIT)
    inside = jnp.logical_and(u >= u0, u < u0 + nu)
    rel = jnp.sum(jnp.where(inside, u - u0 + region_used * (1.0 / UNIT), 0.0), axis=0, keepdims=True)
    exp_of = jnp.sum(jnp.where(inside, eu, 0.0), axis=0, keepdims=True)
    n_units = jnp.broadcast_to(jnp.sum(nu, axis=0, keepdims=True), (1, UNIT_LANES))
    pad = jnp.zeros((5, UNIT_LANES), F32)
    unit_ref[...] = jnp.concatenate([rel, exp_of, n_units, pad], axis=0).astype(I32)

    total = region_used + run
    carry[...] = jnp.broadcast_to(total, carry.shape)
    tot_ref[...] = jnp.broadcast_to(total, tot_ref.shape)


def _route(scores_t, bias_col, tri, ltri):
    n_exp, t = scores_t.shape
    nt = t // TD
    return pl.pallas_call(
        _route_kernel,
        grid=(nt,),
        in_specs=[pl.BlockSpec((n_exp, TD), lambda i: (0, i)),
                  pl.BlockSpec((n_exp, 1), lambda i: (0, 0)),
                  pl.BlockSpec((TD, TD), lambda i: (0, 0)),
                  pl.BlockSpec((n_exp, n_exp), lambda i: (0, 0))],
        out_specs=[pl.BlockSpec((TOP_K, TD), lambda i: (0, i)),
                   pl.BlockSpec((2 * n_exp, TD), lambda i: (0, i)),
                   pl.BlockSpec((n_exp, 128), lambda i: (0, i)),
                   pl.BlockSpec((n_exp, 128), lambda i: (0, i)),
                   pl.BlockSpec((8, UNIT_LANES), lambda i: (0, i)),
                   pl.BlockSpec((n_exp, 128), lambda i: (0, 0))],
        out_shape=[jax.ShapeDtypeStruct((TOP_K, t), I32),
                   jax.ShapeDtypeStruct((2 * n_exp, t), BF16),
                   jax.ShapeDtypeStruct((n_exp, nt * 128), F32),
                   jax.ShapeDtypeStruct((n_exp, nt * 128), F32),
                   jax.ShapeDtypeStruct((8, nt * UNIT_LANES), I32),
                   jax.ShapeDtypeStruct((n_exp, 128), F32)],
        scratch_shapes=[pltpu.VMEM((n_exp, 128), F32)],
        compiler_params=_cparams(("arbitrary",)),
        name="route_topk",
    )(scores_t, bias_col, tri, ltri)


def _unit_dest_kernel(start_ref, unit_ref, o_ref, *, n_exp):
    tbl = unit_ref[...]
    exp_of = tbl[1:2, :]
    dst = tbl[0:1, :]
    for e in range(n_exp):
        dst = dst + jnp.where(exp_of == e, start_ref[e], 0)
    o_ref[...] = jnp.concatenate([dst, tbl[2:3, :], jnp.zeros((6, tbl.shape[1]), I32)], axis=0)


def _unit_dest(region_start_units, units):
    blk = lambda: pl.BlockSpec((8, UNIT_LANES), lambda i, s: (0, i))
    return pl.pallas_call(
        functools.partial(_unit_dest_kernel, n_exp=region_start_units.shape[0]),
        grid_spec=pltpu.PrefetchScalarGridSpec(
            num_scalar_prefetch=1, grid=(units.shape[1] // UNIT_LANES,), in_specs=[blk()], out_specs=blk()),
        out_shape=jax.ShapeDtypeStruct(units.shape, I32),
        compiler_params=_cparams(("parallel",)),
        name="route_units",
    )(region_start_units, units)


def _unit_rows(i):
    return pl.ds(pl.multiple_of(i * UNIT, UNIT), UNIT)


def _for_each(count, fn, group=8):
    main = lax.shift_right_logical(count, int(math.log2(group)))

    def many(i, c):
        for r in range(group):
            fn(i * group + r)
        return c

    lax.fori_loop(0, main, many, 0)
    lax.fori_loop(main * group, count, lambda i, c: (fn(i), c)[1], 0)


def _dispatch_kernel(tail_lo_ref, tail_hi_ref, nu_ref, unit_ref, pos_ref, w_ref, rs_ref, rn_ref, h_ref, xs_ref,
                     s_scr, zero_scr, sem, zsem, *, n_exp):
    rows = s_scr.shape[1]
    td = h_ref.shape[0]
    step = pl.program_id(0)
    last = pl.num_programs(0) - 1
    slot = step % 2

    def zero_copy(g):
        return pltpu.make_async_copy(zero_scr, xs_ref.at[_unit_rows(g), :], zsem.at[0])

    def unit_copy(u, sl):
        return pltpu.make_async_copy(s_scr.at[sl, _unit_rows(u), :], xs_ref.at[_unit_rows(unit_ref[0, u]), :],
                                     sem.at[sl])

    def drain(count, sl):
        _for_each(count, lambda u: unit_copy(0, sl).wait())

    @pl.when(step >= 2)
    def _():
        drain(nu_ref[step - 2], slot)

    @pl.when(step == 0)
    def _():
        zero_scr[...] = jnp.zeros_like(zero_scr)

        def per_expert(e, c):
            lax.fori_loop(tail_lo_ref[e], tail_hi_ref[e], lambda g, cc: (zero_copy(g).start(), cc)[1], 0)
            lax.fori_loop(tail_lo_ref[e], tail_hi_ref[e], lambda g, cc: (zero_copy(g).wait(), cc)[1], 0)
            return c

        lax.fori_loop(0, n_exp, per_expert, 0)

    pos = pos_ref[...]
    h = h_ref[...]
    d = h.shape[1]
    w_dense = w_ref[...]
    run_lo = rs_ref[...]
    run_hi = run_lo + rn_ref[...]
    lane = lax.broadcasted_iota(I32, (SORT_CHUNK, s_scr.shape[2] - d), 1)
    for r0 in range(0, rows, SORT_CHUNK):
        srow = lax.broadcasted_iota(I32, (SORT_CHUNK, td), 0) + r0
        onehot = jnp.zeros((SORT_CHUNK, td), F32)
        for k in range(pos.shape[0]):
            onehot = jnp.where(srow == pos[k:k + 1, :], 1.0, onehot)
        s_e = (lax.broadcasted_iota(I32, (SORT_CHUNK, w_dense.shape[0]), 0) + r0).astype(F32)
        mine = jnp.logical_and(s_e >= run_lo, s_e < run_hi).astype(BF16)
        w_row = jnp.sum(onehot * _dot(mine, w_dense), axis=1, keepdims=True)
        w_hi = w_row.astype(BF16).astype(F32)
        w_lo = w_row - w_hi
        s_scr[slot, r0:r0 + SORT_CHUNK, :d] = _dot(onehot.astype(BF16), h).astype(BF16)
        s_scr[slot, r0:r0 + SORT_CHUNK, d:] = jnp.where(lane == 0, w_hi, jnp.where(lane == 1, w_lo, 0.0)).astype(BF16)

    _for_each(nu_ref[step], lambda u: unit_copy(u, slot).start())

    @pl.when(step == last)
    def _():
        drain(nu_ref[step], slot)

        @pl.when(step >= 1)
        def _():
            drain(nu_ref[step - 1], 1 - slot)


def _dispatch(tail_lo, tail_hi, n_units, unit_dst, pos, w_dense, run_start_rows, run_rows, h2, n_slots):
    t, d = h2.shape
    n_exp = tail_lo.shape[0]
    dw = d + W_LANES
    runs = pl.BlockSpec((None, 1, 2 * n_exp), lambda i, a, b, c: (i, 0, 0))
    return pl.pallas_call(
        functools.partial(_dispatch_kernel, n_exp=n_exp),
        grid_spec=pltpu.PrefetchScalarGridSpec(
            num_scalar_prefetch=3, grid=(t // TD,),
            in_specs=[pl.BlockSpec((8, UNIT_LANES), lambda i, a, b, c: (0, i), memory_space=pltpu.SMEM),
                      pl.BlockSpec((TOP_K, TD), lambda i, a, b, c: (0, i)),
                      pl.BlockSpec((2 * n_exp, TD), lambda i, a, b, c: (0, i)),
                      runs, runs,
                      pl.BlockSpec((TD, d), lambda i, a, b, c: (i, 0))],
            out_specs=pl.BlockSpec(memory_space=pl.ANY),
            scratch_shapes=[pltpu.VMEM((2, _tile_rows(n_exp), dw), BF16), pltpu.VMEM((UNIT, dw), BF16),
                            pltpu.SemaphoreType.DMA((2,)), pltpu.SemaphoreType.DMA((1,))]),
        out_shape=jax.ShapeDtypeStruct((n_slots, dw), BF16),
        compiler_params=_cparams(("arbitrary",)),
        name="moe_dispatch",
    )(tail_lo, tail_hi, n_units, unit_dst, pos, w_dense, run_start_rows, run_rows, h2)


def _expert_kernel(blk0_ref, nblk_ref, tot_ref, wg_ref, wu_ref, wd_ref, xs_ref, ys_ref, xbuf, obuf, wgu_scr, wd_scr,
                   sem_in, sem_out):
    e = pl.program_id(0)
    n = nblk_ref[e]
    b0 = blk0_ref[e]
    de = wg_ref.shape[1]
    wgu_scr[:, :de] = wg_ref[...].astype(BF16)
    wgu_scr[:, de:] = wu_ref[...].astype(BF16)
    wd_scr[...] = wd_ref[...].astype(BF16)

    total = tot_ref[0]

    def rows(g):
        return pl.ds(pl.multiple_of(g * MOE_BLK, MOE_BLK), MOE_BLK)

    def in_copy(g):
        sl = g % EXPERT_BUFS
        return pltpu.make_async_copy(xs_ref.at[rows(g), :], xbuf.at[sl], sem_in.at[sl])

    def out_copy(g):
        sl = g % EXPERT_BUFS
        return pltpu.make_async_copy(obuf.at[sl], ys_ref.at[rows(g), :], sem_out.at[sl])

    @pl.when(e == 0)
    def _():
        for g in range(EXPERT_BUFS - 1):
            @pl.when(g < total)
            def _():
                in_copy(g).start()

    def block(j, c):
        g = b0 + j
        sl = g % EXPERT_BUFS
        in_copy(g).wait()

        @pl.when(g + (EXPERT_BUFS - 1) < total)
        def _():
            in_copy(g + (EXPERT_BUFS - 1)).start()

        @pl.when(g >= EXPERT_BUFS)
        def _():
            out_copy(g - EXPERT_BUFS).wait()

        d = wd_scr.shape[1]
        gu = _dot(xbuf[sl, :, :d], wgu_scr[...])
        act = _silu(gu[:, :de]) * gu[:, de:]
        w_row = xbuf[sl, :, d:d + 1].astype(F32) + xbuf[sl, :, d + 1:d + 2].astype(F32)
        obuf[sl] = (_dot(act.astype(BF16), wd_scr[...]) * w_row).astype(BF16)
        out_copy(g).start()
        return c

    lax.fori_loop(0, n, block, 0)

    @pl.when(e == pl.num_programs(0) - 1)
    def _():
        for back in range(1, EXPERT_BUFS + 1):
            @pl.when(total >= back)
            def _():
                out_copy(total - back).wait()


def _experts(blk0, nblk, n_blocks, xs, w_gate, w_up, w_down, layer):
    n_slots, dw = xs.shape
    _, n_exp, d, de = w_gate.shape
    return pl.pallas_call(
        _expert_kernel,
        grid_spec=pltpu.PrefetchScalarGridSpec(
            num_scalar_prefetch=3, grid=(n_exp,),
            in_specs=[pl.BlockSpec((None, None, d, de), lambda e, a, b, c: (layer, e, 0, 0)),
                      pl.BlockSpec((None, None, d, de), lambda e, a, b, c: (layer, e, 0, 0)),
                      pl.BlockSpec((None, None, de, d), lambda e, a, b, c: (layer, e, 0, 0)),
                      pl.BlockSpec(memory_space=pl.ANY)],
            out_specs=pl.BlockSpec(memory_space=pl.ANY),
            scratch_shapes=[pltpu.VMEM((EXPERT_BUFS, MOE_BLK, dw), BF16), pltpu.VMEM((EXPERT_BUFS, MOE_BLK, d), BF16),
                            pltpu.VMEM((d, 2 * de), BF16), pltpu.VMEM((de, d), BF16),
                            pltpu.SemaphoreType.DMA((EXPERT_BUFS,)), pltpu.SemaphoreType.DMA((EXPERT_BUFS,))]),
        out_shape=jax.ShapeDtypeStruct((n_slots, d), BF16),
        compiler_params=_cparams(("arbitrary",)),
        name="moe_experts",
    )(blk0, nblk, n_blocks, w_gate, w_up, w_down, xs)


def _combine_kernel(unit_ref, next_ref, pos_ref, x_ref, h_ref, gate2_ref,
                    sg_ref, su_ref, sd_ref, lng_ref, lnb_ref, ys_ref, o_ref, s_scr, sem, *, alpha):
    rows = s_scr.shape[1]
    td = x_ref.shape[0]
    step = pl.program_id(0)
    slot = step % 2

    def unit_copy(tbl, u, sl):
        return pltpu.make_async_copy(ys_ref.at[_unit_rows(tbl[0, u]), :], s_scr.at[sl, _unit_rows(u), :], sem.at[sl])

    def fetch(tbl, sl):
        _for_each(tbl[1, 0], lambda u: unit_copy(tbl, u, sl).start())

    @pl.when(step == 0)
    def _():
        s_scr[...] = jnp.zeros_like(s_scr)
        fetch(unit_ref, 0)

    @pl.when(step + 1 < pl.num_programs(0))
    def _():
        fetch(next_ref, 1 - slot)

    h = h_ref[...]
    acc = _dot((_silu(_dot(h, sg_ref[...])) * _dot(h, su_ref[...])).astype(BF16), sd_ref[...])
    pos = pos_ref[...]
    _for_each(unit_ref[1, 0], lambda u: unit_copy(unit_ref, 0, slot).wait())
    for r0 in range(0, rows, SORT_CHUNK):
        lane = lax.broadcasted_iota(I32, (td, SORT_CHUNK), 1) + r0
        onehot = jnp.zeros((td, SORT_CHUNK), F32)
        for k in range(pos.shape[1]):
            onehot = jnp.where(lane == pos[:, k:k + 1], 1.0, onehot)
        acc = acc + _dot(onehot.astype(BF16), s_scr[slot, r0:r0 + SORT_CHUNK, :])
    o_ref[...] = _ln(alpha * x_ref[...] + gate2_ref[...] * acc) * lng_ref[...] + lnb_ref[...]


def _combine(unit_dst, pos_tok, xn, h2, modp, sh_gate, sh_up, sh_down, ln_g, ln_b, ys, seg_td, *, alpha, n_exp):
    t, d = xn.shape
    nt = t // TD
    tok = pl.BlockSpec((TD, d), lambda i: (i, 0))
    full = lambda a: pl.BlockSpec(a.shape, lambda i: (0,) * a.ndim)
    return pl.pallas_call(
        functools.partial(_combine_kernel, alpha=alpha),
        grid=(nt,),
        in_specs=[pl.BlockSpec((8, UNIT_LANES), lambda i: (0, i), memory_space=pltpu.SMEM),
                  pl.BlockSpec((8, UNIT_LANES), lambda i: (0, jnp.minimum(i + 1, nt - 1)), memory_space=pltpu.SMEM),
                  pl.BlockSpec((TD, pos_tok.shape[1]), lambda i: (i, 0)),
                  tok, tok,
                  pl.BlockSpec((None, None, 1, d), lambda i: (5, seg_td(i), 0, 0)),
                  full(sh_gate), full(sh_up), full(sh_down), full(ln_g), full(ln_b),
                  pl.BlockSpec(memory_space=pl.ANY)],
        out_specs=tok,
        out_shape=jax.ShapeDtypeStruct((t, d), F32),
        scratch_shapes=[pltpu.VMEM((2, _tile_rows(n_exp), d), BF16), pltpu.SemaphoreType.DMA((2,))],
        compiler_params=_cparams(("arbitrary",)),
        name="moe_combine",
    )(unit_dst, unit_dst, pos_tok, xn, h2, modp, sh_gate, sh_up, sh_down, ln_g, ln_b, ys)


def _moe(xn, h2, scores_t, modp, seg_td, router_bias, w_gate, w_up, w_down, sh_gate, sh_up, sh_down,
         ln_g, ln_b, tri, ltri, *, alpha, layer):
    t = xn.shape[0]
    n_exp = scores_t.shape[0]
    assert _tile_rows(n_exp) % SORT_CHUNK == 0 and _tile_rows(n_exp) <= UNIT * UNIT_LANES
    pos, w_sel, run_start, run_len, units, tot = _route(scores_t, router_bias.reshape(n_exp, 1), tri, ltri)
    as_rows = lambda a: jnp.tile(a[:, ::128].T, (1, 2))[:, None, :]
    used = tot[:, 0].astype(I32)
    region = (used + MOE_BLK - 1) // MOE_BLK * MOE_BLK
    region_end = jnp.cumsum(region)
    region_start = region_end - region
    n_slots = TOP_K * t + n_exp * UNIT * (t // TD) + n_exp * MOE_BLK
    unit_dst = _unit_dest((region_start // UNIT).astype(I32), units)
    xs = _dispatch(((region_start + used) // UNIT).astype(I32), (region_end // UNIT).astype(I32),
                   unit_dst[1, ::UNIT_LANES], unit_dst, pos, w_sel, as_rows(run_start), as_rows(run_len), h2,
                   n_slots)
    ys = _experts((region_start // MOE_BLK).astype(I32), (region // MOE_BLK).astype(I32),
                  (region_end[-1:] // MOE_BLK).astype(I32), xs, w_gate, w_up, w_down, layer)
    return _combine(unit_dst, pos.T, xn, h2, modp, sh_gate, sh_up, sh_down, ln_g, ln_b, ys, seg_td,
                    alpha=alpha, n_exp=n_exp)


def kernel(x, c, ctx, c_ctx, w_mod, b_mod, w_in, b_in, lam_q1, lam_k1, lam_q2, lam_k2, attn_norm_g, sg_ln_g, sg_ln_b, sg_w, sg_b, conv_w, conv_b, lru_wa, lru_ba, lru_wx, lru_bx, lru_lam, w_branch, w_out, ln1_g, ln1_b, w_router, router_bias, moe_w_gate, moe_w_up, moe_w_down, sh_w_gate, sh_w_up, sh_w_down, ln2_g, ln2_b):
    batch, n_lat, d = x.shape
    n_ctx = ctx.shape[1]
    depth = w_mod.shape[0]
    n_exp = w_router.shape[2]
    t_lat = batch * n_lat
    t_ctx = batch * n_ctx
    t_all = t_lat + t_ctx
    assert n_lat % TM == 0 and t_ctx % TM == 0 and n_ctx % B_CHUNK == 0 and t_lat % n_ctx == 0
    assert batch + 1 <= MOD_ROWS and TM % TD == 0
    alpha = (2 * depth) ** 0.25
    cw = conv_w.shape[2]
    tiles_per_batch = n_lat // TM

    seg = lambda i: jnp.minimum(i // tiles_per_batch, batch)
    seg_td = lambda i: jnp.minimum(i // (n_lat // TD), batch)
    tm_in = 2 * TM if (n_lat % (2 * TM) == 0 and t_ctx % (2 * TM) == 0) else TM
    seg_in = lambda i: jnp.minimum(i // (n_lat // tm_in), batch)
    tbl_idx = lambda i: jnp.where(i < t_lat // tm_in, i % (n_lat // tm_in), n_lat // tm_in)

    x_all = jnp.concatenate([x.reshape(t_lat, d), ctx.reshape(t_ctx, d)], axis=0)
    c_all = jnp.zeros((MOD_ROWS, d), F32).at[:batch].set(c).at[batch].set(c_ctx)
    tables = _rope_tables(n_lat, tm_in)
    tri = (jnp.arange(TD)[:, None] <= jnp.arange(TD)[None, :]).astype(BF16)
    ltri = (jnp.arange(n_exp)[None, :] < jnp.arange(n_exp)[:, None]).astype(BF16)
    row = lambda v: v.reshape(1, -1)

    def dense_blocks(w):
        nb, bi, bj = w.shape
        eye = jnp.eye(nb, dtype=w.dtype)
        return (w[:, :, None, :] * eye[:, None, :, None]).reshape(nb * bi, nb * bj)

    for l in range(depth):
        last = l == depth - 1
        lam_init = 0.8 - 0.6 * math.exp(-0.3 * l)
        mod = _mod_rows(c_all, w_mod[l].astype(BF16), row(b_mod[l]))
        modp = mod.reshape(MOD_ROWS, 6, 1, d).transpose(1, 0, 2, 3)

        qr, kr, vb, proj = _in_proj(x_all, modp, w_in[l].astype(BF16), row(b_in[l]), tables, seg_in, tbl_idx, tm_in)

        lam_vecs = jnp.stack([lam_q1[l], lam_k1[l], lam_q2[l], lam_k2[l]])
        gain = attn_norm_g[l].reshape(A_HEADS, A_V_DIM, 1)
        attn = functools.partial(_attention, qr, kr, vb.T, lam_vecs, gain, lam_init=lam_init,
                                 batch=batch, n_lat=n_lat, n_ctx=n_ctx)
        n_rows = t_lat if last else t_all
        o_a = attn(latent_queries=True, out_rows=n_rows)
        if not last:
            o_a = attn(latent_queries=False, out_rows=n_rows, into=o_a)

        o_b = _spatial_gating(proj, row(sg_ln_g[l]), row(sg_ln_b[l]), sg_w[l].astype(BF16),
                              sg_b[l].reshape(B_GROUPS, B_CHUNK, 1))

        lru = functools.partial(_lru_pass, proj, conv_w=conv_w[l], conv_b=row(conv_b[l]), batch=batch)
        zeros_h = jnp.zeros((batch, 8, cw), F32)
        hf = {}
        for direction in range(2):
            wd = jnp.concatenate([dense_blocks(lru_wa[l, direction]), dense_blocks(lru_wx[l, direction])],
                                 axis=1).astype(BF16)
            par = dict(w_dense=wd, ba=row(lru_ba[l, direction]), bx=row(lru_bx[l, direction]),
                       lam=row(lru_lam[l, direction]), reverse=direction == 1, final=direction == 1)
            shared = direction == 1 and not last
            h_ctx, edge = lru(hf.get("ctx"), zeros_h, seq=n_ctx, tl=n_ctx, row0=t_lat,
                              out_rows=t_all if shared else None, out_row0=t_lat if shared else 0, **par)
            h_lat, _ = lru(hf.get("lat"), edge, seq=n_lat, tl=TM, row0=0,
                           out_rows=t_all if shared else None, into=h_ctx if shared else None, **par)
            hf = {"ctx": h_ctx, "lat": h_lat}
        o_c = hf["lat"]
        wr_t = w_router[l].T
        wr_hi = wr_t.astype(BF16)
        wr_lo = (wr_t - wr_hi.astype(F32)).astype(BF16)
        xn, h2, scores_t = _merge(x_all, o_a, o_b, o_c, proj, modp, w_branch[l].astype(BF16),
                                  w_out[l].astype(BF16), row(ln1_g[l]), row(ln1_b[l]), wr_hi, wr_lo, seg,
                                  n_rows=n_rows, alpha=alpha)
        x_all = _moe(xn, h2, scores_t, modp, seg_td, router_bias[l], moe_w_gate,
                     moe_w_up, moe_w_down, sh_w_gate[l].astype(BF16),
                     sh_w_up[l].astype(BF16), sh_w_down[l].astype(BF16), row(ln2_g[l]), row(ln2_b[l]), tri, ltri,
                     alpha=alpha, layer=l)
    return x_all[:t_lat].reshape(batch, n_lat, d)
```

```python
import functools
import math

import jax
import jax.numpy as jnp
from jax import lax
from jax.experimental import pallas as pl
from jax.experimental.pallas import tpu as pltpu

F32 = jnp.float32
BF16 = jnp.bfloat16
I32 = jnp.int32

A_HEADS = 4
A_QK_DIM = 64
A_V_DIM = 2 * A_QK_DIM
GRID_W = 64
ROPE_THETA = 10000.0
B_CHUNK = 128
B_GROUPS = 4
C_BLOCKS = 8
C_POW = 8.0
TOP_K = 8
N_GROUPS = 8
TOPK_GROUPS = 4
ROUTED_SCALE = 2.5
LN_EPS = 1e-6
RMS_EPS = 1e-5

TM = 512
MOE_BLK = 512
TD = 256
UNIT = 16
UNIT_LANES = 256
SORT_CHUNK = 512
W_LANES = 128
EXPERT_BUFS = 4
LRU_BLOCK = 64
HALO = 16
ATTN_HEADS = 4
ATTN_SUB = 1024
MOD_ROWS = 16
VMEM_LIMIT = 56 * 1024 * 1024


def _cparams(sem):
    return pltpu.CompilerParams(dimension_semantics=sem, vmem_limit_bytes=VMEM_LIMIT)


def _ln(x):
    mu = jnp.mean(x, axis=-1, keepdims=True)
    xc = x - mu
    var = jnp.mean(xc * xc, axis=-1, keepdims=True)
    return xc * lax.rsqrt(var + LN_EPS)


def _gelu(x):
    cdf = 0.5 * (1.0 + jnp.tanh(math.sqrt(2.0 / math.pi) * (x + 0.044715 * (x * x * x))))
    return x * cdf


def _sigmoid(x):
    return 1.0 / (1.0 + jnp.exp(-x))


def _silu(x):
    return x * _sigmoid(x)


def _dot(a, b):
    return jnp.dot(a, b, preferred_element_type=F32)


def _dot_nt(a, b):
    return lax.dot_general(a, b, (((1,), (1,)), ((), ())), preferred_element_type=F32)


def _mod_kernel(c_ref, w_ref, b_ref, o_ref):
    o_ref[...] = _dot(_silu(c_ref[...]).astype(BF16), w_ref[...]) + b_ref[...]


def _mod_rows(c_all, w, b):
    m, d = c_all.shape
    n = w.shape[1]
    tn = 1536
    return pl.pallas_call(
        _mod_kernel,
        grid=(n // tn,),
        in_specs=[pl.BlockSpec((m, d), lambda j: (0, 0)),
                  pl.BlockSpec((d, tn), lambda j: (0, j)),
                  pl.BlockSpec((1, tn), lambda j: (0, j))],
        out_specs=pl.BlockSpec((m, tn), lambda j: (0, j)),
        out_shape=jax.ShapeDtypeStruct((m, n), F32),
        compiler_params=_cparams(("parallel",)),
        name="adaln_rows",
    )(c_all, w, b)


REST_TILE = 1280
QKV_TILES = 3
COL_U, COL_S, COL_X, COL_Y, COL_G = 0, 1, 2, 3, 4


def _in_proj_kernel(x_ref, sh_ref, sc_ref, w_ref, b_ref, cos_ref, sa_ref, sb_ref,
                    q_ref, k_ref, v_ref, h_ref):
    j = pl.program_id(1)

    @pl.when(j == 0)
    def _():
        h = _ln(x_ref[...]) * (1.0 + sc_ref[...]) + sh_ref[...]
        h_ref[...] = h.astype(BF16)

    acc = _dot(h_ref[...], w_ref[...]) + b_ref[...]

    def rope(dst_ref, scale):
        hw = cos_ref.shape[1]
        for c0 in range(0, acc.shape[1], hw):
            x = acc[:, c0:c0 + hw]
            r = x * cos_ref[...] + pltpu.roll(x, hw - 16, 1) * sa_ref[...] + pltpu.roll(x, 16, 1) * sb_ref[...]
            dst_ref[:, c0:c0 + hw] = (r * scale).astype(BF16)

    @pl.when(j == 0)
    def _():
        rope(q_ref, A_QK_DIM ** -0.5 * math.log2(math.e))

    @pl.when(j == 1)
    def _():
        rope(k_ref, 1.0)

    @pl.when(j == 2)
    def _():
        v_ref[...] = acc.astype(BF16)


def _rest_proj_kernel(h_ref, w_ref, b_ref, o_ref):
    o_ref[...] = (_dot(h_ref[...], w_ref[...]) + b_ref[...]).astype(o_ref.dtype)


def _in_proj(x_all, modp, w, b, tables, seg, tbl_idx, tm):
    t, d = x_all.shape
    tn = 2 * A_HEADS * A_QK_DIM
    n_qkv = QKV_TILES * tn
    n_rest = w.shape[1] - n_qkv
    cos, sa, sb = tables
    tbl = pl.BlockSpec((tm, cos.shape[1]), lambda i, j: (tbl_idx(i), 0))
    qkv = pl.BlockSpec((tm, tn), lambda i, j: (i, 0))
    qkv_shape = jax.ShapeDtypeStruct((t, tn), BF16)
    q, k, v, h = pl.pallas_call(
        _in_proj_kernel,
        grid=(t // tm, QKV_TILES),
        in_specs=[pl.BlockSpec((tm, d), lambda i, j: (i, 0)),
                  pl.BlockSpec((None, None, 1, d), lambda i, j: (0, seg(i), 0, 0)),
                  pl.BlockSpec((None, None, 1, d), lambda i, j: (1, seg(i), 0, 0)),
                  pl.BlockSpec((d, tn), lambda i, j: (0, j)),
                  pl.BlockSpec((1, tn), lambda i, j: (0, j)),
                  tbl, tbl, tbl],
        out_specs=[qkv, qkv, qkv, pl.BlockSpec((tm, d), lambda i, j: (i, 0))],
        out_shape=[qkv_shape, qkv_shape, qkv_shape, jax.ShapeDtypeStruct((t, d), BF16)],
        compiler_params=_cparams(("parallel", "arbitrary")),
        name="in_proj_qkv",
    )(x_all, modp, modp, w, b, cos, sa, sb)
    tr = REST_TILE if n_rest % REST_TILE == 0 else tn
    rest = pl.pallas_call(
        _rest_proj_kernel,
        grid=(t // tm, n_rest // tr),
        in_specs=[pl.BlockSpec((tm, d), lambda i, j: (i, 0)),
                  pl.BlockSpec((d, tr), lambda i, j: (0, j)),
                  pl.BlockSpec((1, tr), lambda i, j: (0, j))],
        out_specs=pl.BlockSpec((tm, tr), lambda i, j: (i, j)),
        out_shape=jax.ShapeDtypeStruct((t, n_rest), BF16),
        compiler_params=_cparams(("parallel", "parallel")),
        name="in_proj_rest",
    )(h, w[:, n_qkv:], b[:, n_qkv:])
    return q, k, v, rest


def _rope_tables(n, tm):
    rows = n // GRID_W
    pos_row = jnp.repeat(jnp.arange(rows), GRID_W).astype(F32)
    pos_col = jnp.tile(jnp.arange(GRID_W), rows).astype(F32)
    quarter = A_QK_DIM // 4
    inv = ROPE_THETA ** (-jnp.arange(quarter, dtype=F32) / quarter)
    ang_r = pos_row[:, None] * inv
    ang_c = pos_col[:, None] * inv
    ang = jnp.concatenate([ang_r, ang_r, ang_c, ang_c], axis=-1)
    cos = jnp.cos(ang)
    sin = jnp.sin(ang)
    first_half = (jnp.arange(A_QK_DIM) % (2 * quarter)) < quarter
    sa = jnp.where(first_half, -sin, 0.0)
    sb = jnp.where(first_half, 0.0, sin)

    def widen(tbl, fill):
        tbl = jnp.tile(tbl, (1, 2))
        return jnp.concatenate([tbl, jnp.full((tm, tbl.shape[1]), fill, F32)], axis=0)

    return widen(cos, 1.0), widen(sa, 0.0), widen(sb, 0.0)


def _attn_kernel(*refs, lam_init, with_lat, sub):
    if with_lat:
        lam_ref, gain_ref, q_ref, kc_ref, vc_ref, kl_ref, vl_ref, o_ref = refs
    else:
        lam_ref, gain_ref, q_ref, kc_ref, vc_ref, _, o_ref = refs
    lv = lam_ref[...]
    lam = (jnp.exp(jnp.sum(lv[0:1] * lv[1:2], axis=-1, keepdims=True))
           - jnp.exp(jnp.sum(lv[2:3] * lv[3:4], axis=-1, keepdims=True)) + lam_init)
    chunks = [(kc_ref, vc_ref, 0, kc_ref.shape[0])]
    if with_lat:
        ck = min(sub, kl_ref.shape[0])
        chunks += [(kl_ref, vl_ref, r0, ck) for r0 in range(0, kl_ref.shape[0], ck)]
    for hh in range(gain_ref.shape[0]):
        head = slice(hh * A_V_DIM, (hh + 1) * A_V_DIM)
        probs, dens = [], []
        for c in range(2):
            cols = slice(hh * A_V_DIM + c * A_QK_DIM, hh * A_V_DIM + (c + 1) * A_QK_DIM)
            qc = q_ref[:, cols]
            scores = [_dot_nt(k_ref[r0:r0 + n, cols], qc) for k_ref, _, r0, n in chunks]
            m = None
            for s in scores:
                ms = jnp.max(s, axis=0, keepdims=True)
                m = ms if m is None else jnp.maximum(m, ms)
            p = [jnp.exp2(s - m) for s in scores]
            den = None
            for pj in p:
                ds = jnp.sum(pj, axis=0, keepdims=True)
                den = ds if den is None else den + ds
            probs.append(p)
            dens.append(den)
        ratio = lam * dens[0] / dens[1]
        o = None
        for p1, p2, (_, v_ref, r0, n) in zip(probs[0], probs[1], chunks):
            w = p1 - p2 * ratio
            ps = _dot(v_ref[head, r0:r0 + n], w.astype(BF16))
            o = ps if o is None else o + ps
        o = o / dens[0]
        o = o * lax.rsqrt(jnp.mean(o * o, axis=0, keepdims=True) + RMS_EPS)
        o_ref[:, head] = ((o * gain_ref[hh]) * (1.0 - lam_init)).T


def _attention(qr, kr, vt, lam_vecs, gain, *, lam_init, batch, n_lat, n_ctx, latent_queries, out_rows, into=None):
    hw = ATTN_HEADS * A_V_DIM
    ctx_blk0 = batch * n_lat // n_ctx
    lam_spec = pl.BlockSpec(lam_vecs.shape, lambda b, h, i: (0, 0))
    gain_spec = pl.BlockSpec((ATTN_HEADS, A_V_DIM, 1), lambda b, h, i: (h, 0, 0))
    kv_ctx = pl.BlockSpec((n_ctx, hw), lambda b, h, i: (ctx_blk0 + b, h))
    vt_ctx = pl.BlockSpec((hw, n_ctx), lambda b, h, i: (h, ctx_blk0 + b))
    if latent_queries:
        tq = 256
        nq = n_lat // tq
        q_spec = pl.BlockSpec((tq, hw), lambda b, h, i: (b * nq + i, h))
        kv_lat = pl.BlockSpec((n_lat, hw), lambda b, h, i: (b, h))
        vt_lat = pl.BlockSpec((hw, n_lat), lambda b, h, i: (h, b))
        in_specs = [lam_spec, gain_spec, q_spec, kv_ctx, vt_ctx, kv_lat, vt_lat]
        args = (lam_vecs, gain, qr, kr, vt, kr, vt)
        out_blk0 = 0
        aliases = {}
    else:
        tq = n_ctx
        nq = 1
        q_spec = pl.BlockSpec((tq, hw), lambda b, h, i: (ctx_blk0 + b, h))
        in_specs = [lam_spec, gain_spec, q_spec, kv_ctx, vt_ctx, pl.BlockSpec(memory_space=pl.ANY)]
        args = (lam_vecs, gain, qr, kr, vt, into)
        out_blk0 = ctx_blk0
        aliases = {len(args) - 1: 0}
    return pl.pallas_call(
        functools.partial(_attn_kernel, lam_init=lam_init, with_lat=latent_queries, sub=ATTN_SUB),
        grid=(batch, A_HEADS // ATTN_HEADS, nq),
        in_specs=in_specs,
        out_specs=pl.BlockSpec((tq, hw), lambda b, h, i: (out_blk0 + b * nq + i, h)),
        out_shape=jax.ShapeDtypeStruct((out_rows, A_HEADS * A_V_DIM), F32),
        input_output_aliases=aliases,
        compiler_params=_cparams(("parallel", "parallel", "arbitrary")),
        name="diff_attn_lat" if latent_queries else "diff_attn_ctx",
    )(*args)


def _sg_kernel(u_ref, s_ref, g_ref, b_ref, w_ref, bs_ref, o_ref):
    gw = B_CHUNK
    for ci in range(TM // B_CHUNK):
        rows = slice(ci * B_CHUNK, (ci + 1) * B_CHUNK)
        for g in range(B_GROUPS):
            cols = slice(g * gw, (g + 1) * gw)
            vn = _ln(_gelu(s_ref[rows, cols].astype(F32))) * g_ref[:, cols] + b_ref[:, cols]
            mixed = _dot(w_ref[g], vn.astype(BF16)) + bs_ref[g]
            o_ref[rows, cols] = _gelu(u_ref[rows, cols].astype(F32)) * mixed


def _spatial_gating(proj, ln_g, ln_b, w_s, b_s):
    t = proj.shape[0]
    w = B_GROUPS * B_CHUNK
    full = lambda a: pl.BlockSpec(a.shape, lambda i: (0,) * a.ndim)
    return pl.pallas_call(
        _sg_kernel,
        grid=(t // TM,),
        in_specs=[pl.BlockSpec((TM, w), lambda i: (i, COL_U)),
                  pl.BlockSpec((TM, w), lambda i: (i, COL_S)),
                  full(ln_g), full(ln_b), full(w_s), full(b_s)],
        out_specs=pl.BlockSpec((TM, w), lambda i: (i, 0)),
        out_shape=jax.ShapeDtypeStruct((t, w), F32),
        compiler_params=_cparams(("parallel",)),
        name="spatial_gating",
    )(proj, proj, ln_g, ln_b, w_s, b_s)


def _lru_kernel(*refs, reverse, final, n_tiles, tl, aliased):
    if aliased:
        refs = refs[:12] + refs[13:]
    if final:
        (x_ref, xp_ref, xn_ref, cw_ref, cb_ref, w_ref, ba_ref, bx_ref, lam_ref, h0_ref,
         hf_ref, y_ref, o_ref, hl_ref, carry) = refs
    else:
        (x_ref, xp_ref, xn_ref, cw_ref, cb_ref, w_ref, ba_ref, bx_ref, lam_ref, h0_ref,
         o_ref, hl_ref, carry) = refs
    step = pl.program_id(1)

    @pl.when(step == 0)
    def _():
        carry[...] = h0_ref[...]

    ti = (n_tiles - 1 - step) if reverse else step
    x = x_ref[...].astype(F32)
    cw = x.shape[-1]
    row = lax.broadcasted_iota(I32, x.shape, 0)
    has_prev = ti > 0
    has_next = ti < n_tiles - 1
    p6 = jnp.where(has_prev, xp_ref[HALO - 2:HALO - 1, :].astype(F32), 0.0)
    p7 = jnp.where(has_prev, xp_ref[HALO - 1:HALO, :].astype(F32), 0.0)
    n0 = jnp.where(has_next, xn_ref[0:1, :].astype(F32), 0.0)
    xm1 = jnp.where(row == 0, p7, pltpu.roll(x, 1, 0))
    xm2 = jnp.where(row == 0, p6, jnp.where(row == 1, p7, pltpu.roll(x, 2, 0)))
    xp1 = jnp.where(row == tl - 1, n0, pltpu.roll(x, tl - 1, 0))
    taps = cw_ref[...]
    xs = taps[0:1] * xm2 + taps[1:2] * xm1 + taps[2:3] * x + taps[3:4] * xp1 + cb_ref[...]

    z = _dot(xs.astype(BF16), w_ref[...])
    r = _sigmoid(z[:, :cw] + ba_ref[...])
    ig = _sigmoid(z[:, cw:] + bx_ref[...])
    nl = -lam_ref[...]
    softplus = jnp.maximum(nl, 0.0) + jnp.log(1.0 + jnp.exp(-jnp.abs(nl)))
    log_a = -C_POW * r * softplus
    a = jnp.exp(log_a)
    u = jnp.sqrt(1.0 - jnp.exp(2.0 * log_a)) * (ig * xs)

    sb = min(LRU_BLOCK, tl)
    in_blk = row % sb
    d = 1
    while d < sb:
        if reverse:
            ok = in_blk < sb - d
            shift = tl - d
        else:
            ok = in_blk >= d
            shift = d
        a_sh = jnp.where(ok, pltpu.roll(a, shift, 0), 1.0)
        u_sh = jnp.where(ok, pltpu.roll(u, shift, 0), 0.0)
        u = u + a * u_sh
        a = a * a_sh
        d *= 2
    state = carry[0:1, :]
    blocks = [None] * (tl // sb)
    for b in (reversed(range(tl // sb)) if reverse else range(tl // sb)):
        hb = u[b * sb:(b + 1) * sb] + a[b * sb:(b + 1) * sb] * state
        state = hb[0:1, :] if reverse else hb[sb - 1:sb, :]
        blocks[b] = hb
    h = jnp.concatenate(blocks, axis=0)
    edge = state
    carry[...] = jnp.broadcast_to(edge, carry.shape)
    hl_ref[...] = jnp.broadcast_to(edge, hl_ref.shape)
    if final:
        o_ref[...] = _gelu(y_ref[...].astype(F32)) * (hf_ref[...] + h)
    else:
        o_ref[...] = h


def _lru_pass(proj, hf, h0, conv_w, conv_b, w_dense, ba, bx, lam, *, batch, seq, tl, row0, reverse, final,
              out_rows=None, out_row0=0, into=None):
    t_all = proj.shape[0]
    out_rows = batch * seq if out_rows is None else out_rows
    out_blk0 = out_row0 // tl
    assert into is None or final
    cw = conv_w.shape[1]
    n_tiles = seq // tl
    base = row0 // tl
    sub = tl // HALO
    last_halo = t_all // HALO - 1

    def tile(b, s):
        ti = (n_tiles - 1 - s) if reverse else s
        return base + b * n_tiles + ti

    def loc(b, s):
        ti = (n_tiles - 1 - s) if reverse else s
        return b * n_tiles + ti

    full = lambda a: pl.BlockSpec(a.shape, lambda b, s: (0,) * a.ndim)
    in_specs = [pl.BlockSpec((tl, cw), lambda b, s: (tile(b, s), COL_X)),
                pl.BlockSpec((HALO, cw), lambda b, s: (jnp.maximum(tile(b, s) * sub - 1, 0), COL_X)),
                pl.BlockSpec((HALO, cw), lambda b, s: (jnp.minimum((tile(b, s) + 1) * sub, last_halo), COL_X)),
                full(conv_w), full(conv_b), full(w_dense), full(ba), full(bx), full(lam),
                pl.BlockSpec((None, 8, cw), lambda b, s: (b, 0, 0))]
    args = [proj, proj, proj, conv_w, conv_b, w_dense, ba, bx, lam, h0]
    if final:
        in_specs += [pl.BlockSpec((tl, cw), lambda b, s: (loc(b, s), 0)),
                     pl.BlockSpec((tl, cw), lambda b, s: (tile(b, s), COL_Y))]
        args += [hf, proj]
    aliases = {}
    if into is not None:
        aliases = {len(args): 0}
        in_specs += [pl.BlockSpec(memory_space=pl.ANY)]
        args += [into]
    return pl.pallas_call(
        functools.partial(_lru_kernel, reverse=reverse, final=final, n_tiles=n_tiles, tl=tl,
                          aliased=into is not None),
        grid=(batch, n_tiles),
        in_specs=in_specs,
        out_specs=[pl.BlockSpec((tl, cw), lambda b, s: (out_blk0 + loc(b, s), 0)),
                   pl.BlockSpec((None, 8, cw), lambda b, s: (b, 0, 0))],
        out_shape=[jax.ShapeDtypeStruct((out_rows, cw), F32),
                   jax.ShapeDtypeStruct((batch, 8, cw), F32)],
        scratch_shapes=[pltpu.VMEM((8, cw), F32)],
        input_output_aliases=aliases,
        compiler_params=_cparams(("parallel", "arbitrary")),
        name="rglru_%s_%s" % ("bwd" if reverse else "fwd", "lat" if row0 == 0 else "ctx"),
    )(*args)


def _merge_kernel(x_ref, oa_ref, ob_ref, oc_ref, g0, g1, g2, g3, g4, g5, wb_ref, wo_ref,
                  gate1_ref, sh2_ref, sc2_ref, lng_ref, lnb_ref, wrh_ref, wrl_ref,
                  xn_ref, h2_ref, sc_ref, *, alpha):
    gates = ((g0, g1), (g2, g3), (g4, g5))
    mix = None
    for r, o_ref in enumerate((oa_ref, ob_ref, oc_ref)):
        proj = _dot(o_ref[...].astype(BF16), wb_ref[r])
        gate = jnp.concatenate([_sigmoid(gates[r][0][...].astype(F32)), _sigmoid(gates[r][1][...].astype(F32))],
                               axis=-1)
        term = gate * proj
        mix = term if mix is None else mix + term
    out = _dot(mix.astype(BF16), wo_ref[...])
    xn = _ln(alpha * x_ref[...] + gate1_ref[...] * out) * lng_ref[...] + lnb_ref[...]
    xn_ref[...] = xn
    h2 = _ln(xn) * (1.0 + sc2_ref[...]) + sh2_ref[...]
    h_hi = h2.astype(BF16)
    h2_ref[...] = h_hi
    h_lo = (h2 - h_hi.astype(F32)).astype(BF16)
    logits = _dot_nt(wrh_ref[...], h_hi) + (_dot_nt(wrh_ref[...], h_lo) + _dot_nt(wrl_ref[...], h_hi))
    sc_ref[...] = _sigmoid(logits)


def _merge(x_all, o_a, o_b, o_c, proj, modp, w_branch, w_out, ln_g, ln_b, wr_hi, wr_lo, seg, *, n_rows, alpha):
    d = x_all.shape[1]
    bw = o_a.shape[1]
    n_exp = wr_hi.shape[0]
    tok = lambda w: pl.BlockSpec((TM, w), lambda i: (i, 0))
    gate = lambda c: pl.BlockSpec((TM, bw), lambda i, c=c: (i, c))
    mod = lambda k: pl.BlockSpec((None, None, 1, d), lambda i, k=k: (k, seg(i), 0, 0))
    full = lambda a: pl.BlockSpec(a.shape, lambda i: (0,) * a.ndim)
    return pl.pallas_call(
        functools.partial(_merge_kernel, alpha=alpha),
        grid=(n_rows // TM,),
        in_specs=[tok(d), tok(bw), tok(bw), tok(bw)] + [gate(COL_G + c) for c in range(6)]
                 + [full(w_branch), full(w_out), mod(2), mod(3), mod(4), full(ln_g), full(ln_b),
                    full(wr_hi), full(wr_lo)],
        out_specs=[tok(d), tok(d), pl.BlockSpec((n_exp, TM), lambda i: (0, i))],
        out_shape=[jax.ShapeDtypeStruct((n_rows, d), F32),
                   jax.ShapeDtypeStruct((n_rows, d), BF16),
                   jax.ShapeDtypeStruct((n_exp, n_rows), F32)],
        compiler_params=_cparams(("parallel",)),
        name="merge_residual_router",
    )(x_all, o_a, o_b, o_c, proj, proj, proj, proj, proj, proj, w_branch, w_out,
      modp, modp, modp, ln_g, ln_b, wr_hi, wr_lo)


def _tile_rows(n_exp):
    return TOP_K * TD + n_exp * UNIT


def _route_kernel(s_ref, bias_ref, tri_ref, ltri_ref, pos_ref, w_ref, rs_ref, rn_ref, unit_ref, tot_ref, carry):
    @pl.when(pl.program_id(0) == 0)
    def _():
        carry[...] = jnp.zeros_like(carry)

    s = s_ref[...]
    n_exp, tn = s.shape
    per = n_exp // N_GROUPS
    neg = -jnp.inf
    biased = s + bias_ref[...]
    sub = lax.broadcasted_iota(I32, (per, tn), 0)
    gs_rows = []
    for g in range(N_GROUPS):
        blk = biased[g * per:(g + 1) * per, :]
        m1 = jnp.max(blk, axis=0, keepdims=True)
        first = jnp.min(jnp.where(blk == m1, sub, per), axis=0, keepdims=True)
        m2 = jnp.max(jnp.where(sub == first, neg, blk), axis=0, keepdims=True)
        gs_rows.append(m1 + m2)
    gs = jnp.concatenate(gs_rows, axis=0)
    gi = lax.broadcasted_iota(I32, gs.shape, 0)
    g_ok = jnp.zeros(gs.shape, F32)
    cur = gs
    for _ in range(TOPK_GROUPS):
        m = jnp.max(cur, axis=0, keepdims=True)
        pick = jnp.min(jnp.where(cur == m, gi, N_GROUPS), axis=0, keepdims=True)
        hit = gi == pick
        g_ok = jnp.where(hit, 1.0, g_ok)
        cur = jnp.where(hit, neg, cur)
    ok_rows = [jnp.broadcast_to(g_ok[g:g + 1, :], (per, tn)) for g in range(N_GROUPS)]
    expert_ok = jnp.concatenate(ok_rows, axis=0)
    masked = jnp.where(expert_ok > 0.0, biased, neg)
    ei = lax.broadcasted_iota(I32, s.shape, 0)
    pick_s, hits = [], []
    sel = jnp.zeros(s.shape, F32)
    for _ in range(TOP_K):
        m = jnp.max(masked, axis=0, keepdims=True)
        pick = jnp.min(jnp.where(masked == m, ei, n_exp), axis=0, keepdims=True)
        hit = ei == pick
        hits.append(hit)
        pick_s.append(jnp.sum(jnp.where(hit, s, 0.0), axis=0, keepdims=True))
        sel = jnp.where(hit, 1.0, sel)
        masked = jnp.where(hit, neg, masked)
    tot = pick_s[0]
    for k in range(1, TOP_K):
        tot = tot + pick_s[k]
    w = jnp.where(sel > 0.0, s, 0.0) / tot * ROUTED_SCALE
    w_hi = w.astype(BF16)
    w_ref[...] = jnp.concatenate([w_hi, (w - w_hi.astype(F32)).astype(BF16)], axis=0)

    incl = _dot(sel.astype(BF16), tri_ref[...])
    count = incl[:, tn - 1:tn]
    run = jnp.floor((count + (UNIT - 1)) * (1.0 / UNIT)) * UNIT
    run_start = _dot(ltri_ref[...], jnp.broadcast_to(run, (n_exp, 128)).astype(BF16))[:, 0:1]
    row_in_tile = incl - sel + run_start
    pos = [jnp.sum(jnp.where(hit, row_in_tile, 0.0), axis=0, keepdims=True) for hit in hits]
    pos_ref[...] = jnp.concatenate(pos, axis=0).astype(I32)
    rs_ref[...] = jnp.broadcast_to(run_start, rs_ref.shape)
    rn_ref[...] = jnp.broadcast_to(run, rn_ref.shape)

    half = UNIT_LANES // 2
    region_used = carry[:, 0:1]
    u0 = run_start * (1.0 / UNIT)
    nu = run * (1.0 / UNIT)
    n_pair = jnp.floor(nu * 0.5)
    n_single = nu - 2.0 * n_pair
    both = jnp.concatenate([jnp.broadcast_to(n_pair, (n_exp, 64)), jnp.broadcast_to(n_single, (n_exp, 64))], axis=1)
    first = _dot(ltri_ref[...], both.astype(BF16))
    pair0, single0 = first[:, 0:1], first[:, 64:65]
    lane = lax.broadcasted_iota(I32, (n_exp, UNIT_LANES), 1)
    is_pair = lane < half
    item = jnp.where(is_pair, lane, lane - half).astype(F32)
    start = jnp.where(is_pair, pair0, single0)
    count = jnp.where(is_pair, n_pair, n_single)
    inside = jnp.logical_and(item >= start, item < start + count)
    off = jnp.where(is_pair, 2.0 * (item - pair0), 2.0 * n_pair)
    eu = lax.broadcasted_iota(I32, (n_exp, UNIT_LANES), 0).astype(F32)
    pick = lambda v: jnp.sum(jnp.where(inside, v, 0.0), axis=0, keepdims=True)
    totals = jnp.where(is_pair[0:1], jnp.sum(n_pair, axis=0, keepdims=True), jnp.sum(n_single, axis=0, keepdims=True))
    pad = jnp.zeros((4, UNIT_LANES), F32)
    unit_ref[...] = jnp.concatenate([pick(off + region_used * (1.0 / UNIT)), pick(eu), pick(off + u0), totals, pad],
                                    axis=0).astype(I32)

    total = region_used + run
    carry[...] = jnp.broadcast_to(total, carry.shape)
    tot_ref[...] = jnp.broadcast_to(total, tot_ref.shape)


def _route(scores_t, bias_col, tri, ltri):
    n_exp, t = scores_t.shape
    nt = t // TD
    return pl.pallas_call(
        _route_kernel,
        grid=(nt,),
        in_specs=[pl.BlockSpec((n_exp, TD), lambda i: (0, i)),
                  pl.BlockSpec((n_exp, 1), lambda i: (0, 0)),
                  pl.BlockSpec((TD, TD), lambda i: (0, 0)),
                  pl.BlockSpec((n_exp, n_exp), lambda i: (0, 0))],
        out_specs=[pl.BlockSpec((TOP_K, TD), lambda i: (0, i)),
                   pl.BlockSpec((2 * n_exp, TD), lambda i: (0, i)),
                   pl.BlockSpec((n_exp, 128), lambda i: (0, i)),
                   pl.BlockSpec((n_exp, 128), lambda i: (0, i)),
                   pl.BlockSpec((8, UNIT_LANES), lambda i: (0, i)),
                   pl.BlockSpec((n_exp, 128), lambda i: (0, 0))],
        out_shape=[jax.ShapeDtypeStruct((TOP_K, t), I32),
                   jax.ShapeDtypeStruct((2 * n_exp, t), BF16),
                   jax.ShapeDtypeStruct((n_exp, nt * 128), F32),
                   jax.ShapeDtypeStruct((n_exp, nt * 128), F32),
                   jax.ShapeDtypeStruct((8, nt * UNIT_LANES), I32),
                   jax.ShapeDtypeStruct((n_exp, 128), F32)],
        scratch_shapes=[pltpu.VMEM((n_exp, 128), F32)],
        compiler_params=_cparams(("arbitrary",)),
        name="route_topk",
    )(scores_t, bias_col, tri, ltri)


def _unit_dest_kernel(start_ref, unit_ref, o_ref, *, n_exp):
    tbl = unit_ref[...]
    exp_of = tbl[1:2, :]
    dst = tbl[0:1, :]
    for e in range(n_exp):
        dst = dst + jnp.where(exp_of == e, start_ref[e], 0)
    o_ref[...] = jnp.concatenate([dst, tbl[2:4, :], jnp.zeros((5, tbl.shape[1]), I32)], axis=0)


def _unit_dest(region_start_units, units):
    blk = lambda: pl.BlockSpec((8, UNIT_LANES), lambda i, s: (0, i))
    return pl.pallas_call(
        functools.partial(_unit_dest_kernel, n_exp=region_start_units.shape[0]),
        grid_spec=pltpu.PrefetchScalarGridSpec(
            num_scalar_prefetch=1, grid=(units.shape[1] // UNIT_LANES,), in_specs=[blk()], out_specs=blk()),
        out_shape=jax.ShapeDtypeStruct(units.shape, I32),
        compiler_params=_cparams(("parallel",)),
        name="route_units",
    )(region_start_units, units)


def _unit_rows(i, n_units=1):
    return pl.ds(pl.multiple_of(i * UNIT, UNIT), n_units * UNIT)


def _copy_items(tbl, make_copy, act):
    half = UNIT_LANES // 2
    _for_each(tbl[2, 0], lambda i: getattr(make_copy(tbl, i, 2), act)())
    _for_each(tbl[2, half], lambda i: getattr(make_copy(tbl, half + i, 1), act)())


def _drain_items(n_pairs, n_singles, make_copy, tbl):
    _for_each(n_pairs, lambda i: make_copy(tbl, 0, 2).wait())
    _for_each(n_singles, lambda i: make_copy(tbl, UNIT_LANES // 2, 1).wait())


def _for_each(count, fn, group=8):
    main = lax.shift_right_logical(count, int(math.log2(group)))

    def many(i, c):
        for r in range(group):
            fn(i * group + r)
        return c

    lax.fori_loop(0, main, many, 0)
    lax.fori_loop(main * group, count, lambda i, c: (fn(i), c)[1], 0)


def _dispatch_kernel(tail_lo_ref, tail_hi_ref, np_ref, ns_ref, unit_ref, pos_ref, w_ref, rs_ref, rn_ref, h_ref,
                     xs_ref, s_scr, zero_scr, sem, zsem, *, n_exp):
    rows = s_scr.shape[1]
    td = h_ref.shape[0]
    step = pl.program_id(0)
    last = pl.num_programs(0) - 1
    slot = step % 2

    def zero_copy(g):
        return pltpu.make_async_copy(zero_scr, xs_ref.at[_unit_rows(g), :], zsem.at[0])

    def item_copy(sl):
        def make(tbl, lane, size):
            return pltpu.make_async_copy(s_scr.at[sl, _unit_rows(tbl[1, lane], size), :],
                                         xs_ref.at[_unit_rows(tbl[0, lane], size), :], sem.at[size - 1, sl])
        return make

    def drain(tile, sl):
        _drain_items(np_ref[tile], ns_ref[tile], item_copy(sl), unit_ref)

    @pl.when(step >= 2)
    def _():
        drain(step - 2, slot)

    @pl.when(step == 0)
    def _():
        zero_scr[...] = jnp.zeros_like(zero_scr)

        def per_expert(e, c):
            lax.fori_loop(tail_lo_ref[e], tail_hi_ref[e], lambda g, cc: (zero_copy(g).start(), cc)[1], 0)
            lax.fori_loop(tail_lo_ref[e], tail_hi_ref[e], lambda g, cc: (zero_copy(g).wait(), cc)[1], 0)
            return c

        lax.fori_loop(0, n_exp, per_expert, 0)

    pos = pos_ref[...]
    h = h_ref[...]
    d = h.shape[1]
    w_dense = w_ref[...]
    run_lo = rs_ref[...]
    run_hi = run_lo + rn_ref[...]
    lane = lax.broadcasted_iota(I32, (SORT_CHUNK, s_scr.shape[2] - d), 1)
    for r0 in range(0, rows, SORT_CHUNK):
        srow = lax.broadcasted_iota(I32, (SORT_CHUNK, td), 0) + r0
        onehot = jnp.zeros((SORT_CHUNK, td), F32)
        for k in range(pos.shape[0]):
            onehot = jnp.where(srow == pos[k:k + 1, :], 1.0, onehot)
        s_e = (lax.broadcasted_iota(I32, (SORT_CHUNK, w_dense.shape[0]), 0) + r0).astype(F32)
        mine = jnp.logical_and(s_e >= run_lo, s_e < run_hi).astype(BF16)
        w_row = jnp.sum(onehot * _dot(mine, w_dense), axis=1, keepdims=True)
        w_hi = w_row.astype(BF16).astype(F32)
        w_lo = w_row - w_hi
        s_scr[slot, r0:r0 + SORT_CHUNK, :d] = _dot(onehot.astype(BF16), h).astype(BF16)
        s_scr[slot, r0:r0 + SORT_CHUNK, d:] = jnp.where(lane == 0, w_hi, jnp.where(lane == 1, w_lo, 0.0)).astype(BF16)

    _copy_items(unit_ref, item_copy(slot), "start")

    @pl.when(step == last)
    def _():
        drain(step, slot)

        @pl.when(step >= 1)
        def _():
            drain(step - 1, 1 - slot)


def _dispatch(tail_lo, tail_hi, n_pairs, n_singles, unit_dst, pos, w_dense, run_start_rows, run_rows, h2, n_slots):
    t, d = h2.shape
    n_exp = tail_lo.shape[0]
    dw = d + W_LANES
    runs = pl.BlockSpec((None, 1, 2 * n_exp), lambda i, a, b, c, e: (i, 0, 0))
    return pl.pallas_call(
        functools.partial(_dispatch_kernel, n_exp=n_exp),
        grid_spec=pltpu.PrefetchScalarGridSpec(
            num_scalar_prefetch=4, grid=(t // TD,),
            in_specs=[pl.BlockSpec((8, UNIT_LANES), lambda i, a, b, c, e: (0, i), memory_space=pltpu.SMEM),
                      pl.BlockSpec((TOP_K, TD), lambda i, a, b, c, e: (0, i)),
                      pl.BlockSpec((2 * n_exp, TD), lambda i, a, b, c, e: (0, i)),
                      runs, runs,
                      pl.BlockSpec((TD, d), lambda i, a, b, c, e: (i, 0))],
            out_specs=pl.BlockSpec(memory_space=pl.ANY),
            scratch_shapes=[pltpu.VMEM((2, _tile_rows(n_exp), dw), BF16), pltpu.VMEM((UNIT, dw), BF16),
                            pltpu.SemaphoreType.DMA((2, 2)), pltpu.SemaphoreType.DMA((1,))]),
        out_shape=jax.ShapeDtypeStruct((n_slots, dw), BF16),
        compiler_params=_cparams(("arbitrary",)),
        name="moe_dispatch",
    )(tail_lo, tail_hi, n_pairs, n_singles, unit_dst, pos, w_dense, run_start_rows, run_rows, h2)


def _expert_kernel(blk0_ref, nblk_ref, tot_ref, wg_ref, wu_ref, wd_ref, xs_ref, ys_ref, xbuf, obuf, wgu_scr, wd_scr,
                   sem_in, sem_out):
    e = pl.program_id(0)
    n = nblk_ref[e]
    b0 = blk0_ref[e]
    de = wg_ref.shape[1]
    wgu_scr[:, :de] = wg_ref[...].astype(BF16)
    wgu_scr[:, de:] = wu_ref[...].astype(BF16)
    wd_scr[...] = wd_ref[...].astype(BF16)

    total = tot_ref[0]

    def rows(g):
        return pl.ds(pl.multiple_of(g * MOE_BLK, MOE_BLK), MOE_BLK)

    def in_copy(g):
        sl = g % EXPERT_BUFS
        return pltpu.make_async_copy(xs_ref.at[rows(g), :], xbuf.at[sl], sem_in.at[sl])

    def out_copy(g):
        sl = g % EXPERT_BUFS
        return pltpu.make_async_copy(obuf.at[sl], ys_ref.at[rows(g), :], sem_out.at[sl])

    @pl.when(e == 0)
    def _():
        for g in range(EXPERT_BUFS - 1):
            @pl.when(g < total)
            def _():
                in_copy(g).start()

    def block(j, c):
        g = b0 + j
        sl = g % EXPERT_BUFS
        in_copy(g).wait()

        @pl.when(g + (EXPERT_BUFS - 1) < total)
        def _():
            in_copy(g + (EXPERT_BUFS - 1)).start()

        @pl.when(g >= EXPERT_BUFS)
        def _():
            out_copy(g - EXPERT_BUFS).wait()

        d = wd_scr.shape[1]
        gu = _dot(xbuf[sl, :, :d], wgu_scr[...])
        act = _silu(gu[:, :de]) * gu[:, de:]
        w_row = xbuf[sl, :, d:d + 1].astype(F32) + xbuf[sl, :, d + 1:d + 2].astype(F32)
        obuf[sl] = (_dot(act.astype(BF16), wd_scr[...]) * w_row).astype(BF16)
        out_copy(g).start()
        return c

    lax.fori_loop(0, n, block, 0)

    @pl.when(e == pl.num_programs(0) - 1)
    def _():
        for back in range(1, EXPERT_BUFS + 1):
            @pl.when(total >= back)
            def _():
                out_copy(total - back).wait()


def _experts(blk0, nblk, n_blocks, xs, w_gate, w_up, w_down, layer):
    n_slots, dw = xs.shape
    _, n_exp, d, de = w_gate.shape
    return pl.pallas_call(
        _expert_kernel,
        grid_spec=pltpu.PrefetchScalarGridSpec(
            num_scalar_prefetch=3, grid=(n_exp,),
            in_specs=[pl.BlockSpec((None, None, d, de), lambda e, a, b, c: (layer, e, 0, 0)),
                      pl.BlockSpec((None, None, d, de), lambda e, a, b, c: (layer, e, 0, 0)),
                      pl.BlockSpec((None, None, de, d), lambda e, a, b, c: (layer, e, 0, 0)),
                      pl.BlockSpec(memory_space=pl.ANY)],
            out_specs=pl.BlockSpec(memory_space=pl.ANY),
            scratch_shapes=[pltpu.VMEM((EXPERT_BUFS, MOE_BLK, dw), BF16), pltpu.VMEM((EXPERT_BUFS, MOE_BLK, d), BF16),
                            pltpu.VMEM((d, 2 * de), BF16), pltpu.VMEM((de, d), BF16),
                            pltpu.SemaphoreType.DMA((EXPERT_BUFS,)), pltpu.SemaphoreType.DMA((EXPERT_BUFS,))]),
        out_shape=jax.ShapeDtypeStruct((n_slots, d), BF16),
        compiler_params=_cparams(("arbitrary",)),
        name="moe_experts",
    )(blk0, nblk, n_blocks, w_gate, w_up, w_down, xs)


def _combine_kernel(unit_ref, next_ref, pos_ref, x_ref, h_ref, gate2_ref,
                    sg_ref, su_ref, sd_ref, lng_ref, lnb_ref, ys_ref, o_ref, s_scr, sem, *, alpha):
    rows = s_scr.shape[1]
    td = x_ref.shape[0]
    step = pl.program_id(0)
    slot = step % 2

    def item_copy(sl):
        def make(tbl, lane, size):
            return pltpu.make_async_copy(ys_ref.at[_unit_rows(tbl[0, lane], size), :],
                                         s_scr.at[sl, _unit_rows(tbl[1, lane], size), :], sem.at[size - 1, sl])
        return make

    def fetch(tbl, sl):
        _copy_items(tbl, item_copy(sl), "start")

    @pl.when(step == 0)
    def _():
        s_scr[...] = jnp.zeros_like(s_scr)
        fetch(unit_ref, 0)

    @pl.when(step + 1 < pl.num_programs(0))
    def _():
        fetch(next_ref, 1 - slot)

    h = h_ref[...]
    acc = _dot((_silu(_dot(h, sg_ref[...])) * _dot(h, su_ref[...])).astype(BF16), sd_ref[...])
    pos = pos_ref[...]
    _drain_items(unit_ref[2, 0], unit_ref[2, UNIT_LANES // 2], item_copy(slot), unit_ref)
    for r0 in range(0, rows, SORT_CHUNK):
        lane = lax.broadcasted_iota(I32, (td, SORT_CHUNK), 1) + r0
        onehot = jnp.zeros((td, SORT_CHUNK), F32)
        for k in range(pos.shape[1]):
            onehot = jnp.where(lane == pos[:, k:k + 1], 1.0, onehot)
        acc = acc + _dot(onehot.astype(BF16), s_scr[slot, r0:r0 + SORT_CHUNK, :])
    o_ref[...] = _ln(alpha * x_ref[...] + gate2_ref[...] * acc) * lng_ref[...] + lnb_ref[...]


def _combine(unit_dst, pos_tok, xn, h2, modp, sh_gate, sh_up, sh_down, ln_g, ln_b, ys, seg_td, *, alpha, n_exp):
    t, d = xn.shape
    nt = t // TD
    tok = pl.BlockSpec((TD, d), lambda i: (i, 0))
    full = lambda a: pl.BlockSpec(a.shape, lambda i: (0,) * a.ndim)
    return pl.pallas_call(
        functools.partial(_combine_kernel, alpha=alpha),
        grid=(nt,),
        in_specs=[pl.BlockSpec((8, UNIT_LANES), lambda i: (0, i), memory_space=pltpu.SMEM),
                  pl.BlockSpec((8, UNIT_LANES), lambda i: (0, jnp.minimum(i + 1, nt - 1)), memory_space=pltpu.SMEM),
                  pl.BlockSpec((TD, pos_tok.shape[1]), lambda i: (i, 0)),
                  tok, tok,
                  pl.BlockSpec((None, None, 1, d), lambda i: (5, seg_td(i), 0, 0)),
                  full(sh_gate), full(sh_up), full(sh_down), full(ln_g), full(ln_b),
                  pl.BlockSpec(memory_space=pl.ANY)],
        out_specs=tok,
        out_shape=jax.ShapeDtypeStruct((t, d), F32),
        scratch_shapes=[pltpu.VMEM((2, _tile_rows(n_exp), d), BF16), pltpu.SemaphoreType.DMA((2, 2))],
        compiler_params=_cparams(("arbitrary",)),
        name="moe_combine",
    )(unit_dst, unit_dst, pos_tok, xn, h2, modp, sh_gate, sh_up, sh_down, ln_g, ln_b, ys)


def _moe(xn, h2, scores_t, modp, seg_td, router_bias, w_gate, w_up, w_down, sh_gate, sh_up, sh_down,
         ln_g, ln_b, tri, ltri, *, alpha, layer):
    t = xn.shape[0]
    n_exp = scores_t.shape[0]
    assert _tile_rows(n_exp) % SORT_CHUNK == 0 and _tile_rows(n_exp) <= UNIT * UNIT_LANES
    pos, w_sel, run_start, run_len, units, tot = _route(scores_t, router_bias.reshape(n_exp, 1), tri, ltri)
    as_rows = lambda a: jnp.tile(a[:, ::128].T, (1, 2))[:, None, :]
    used = tot[:, 0].astype(I32)
    region = (used + MOE_BLK - 1) // MOE_BLK * MOE_BLK
    region_end = jnp.cumsum(region)
    region_start = region_end - region
    n_slots = TOP_K * t + n_exp * UNIT * (t // TD) + n_exp * MOE_BLK
    unit_dst = _unit_dest((region_start // UNIT).astype(I32), units)
    xs = _dispatch(((region_start + used) // UNIT).astype(I32), (region_end // UNIT).astype(I32),
                   unit_dst[2, ::UNIT_LANES], unit_dst[2, UNIT_LANES // 2::UNIT_LANES], unit_dst, pos, w_sel,
                   as_rows(run_start), as_rows(run_len), h2,
                   n_slots)
    ys = _experts((region_start // MOE_BLK).astype(I32), (region // MOE_BLK).astype(I32),
                  (region_end[-1:] // MOE_BLK).astype(I32), xs, w_gate, w_up, w_down, layer)
    return _combine(unit_dst, pos.T, xn, h2, modp, sh_gate, sh_up, sh_down, ln_g, ln_b, ys, seg_td,
                    alpha=alpha, n_exp=n_exp)


def kernel(x, c, ctx, c_ctx, w_mod, b_mod, w_in, b_in, lam_q1, lam_k1, lam_q2, lam_k2, attn_norm_g, sg_ln_g, sg_ln_b, sg_w, sg_b, conv_w, conv_b, lru_wa, lru_ba, lru_wx, lru_bx, lru_lam, w_branch, w_out, ln1_g, ln1_b, w_router, router_bias, moe_w_gate, moe_w_up, moe_w_down, sh_w_gate, sh_w_up, sh_w_down, ln2_g, ln2_b):
    batch, n_lat, d = x.shape
    n_ctx = ctx.shape[1]
    depth = w_mod.shape[0]
    n_exp = w_router.shape[2]
    t_lat = batch * n_lat
    t_ctx = batch * n_ctx
    t_all = t_lat + t_ctx
    assert n_lat % TM == 0 and t_ctx % TM == 0 and n_ctx % B_CHUNK == 0 and t_lat % n_ctx == 0
    assert batch + 1 <= MOD_ROWS and TM % TD == 0
    alpha = (2 * depth) ** 0.25
    cw = conv_w.shape[2]
    tiles_per_batch = n_lat // TM

    seg = lambda i: jnp.minimum(i // tiles_per_batch, batch)
    seg_td = lambda i: jnp.minimum(i // (n_lat // TD), batch)
    tm_in = 2 * TM if (n_lat % (2 * TM) == 0 and t_ctx % (2 * TM) == 0) else TM
    seg_in = lambda i: jnp.minimum(i // (n_lat // tm_in), batch)
    tbl_idx = lambda i: jnp.where(i < t_lat // tm_in, i % (n_lat // tm_in), n_lat // tm_in)

    x_all = jnp.concatenate([x.reshape(t_lat, d), ctx.reshape(t_ctx, d)], axis=0)
    c_all = jnp.zeros((MOD_ROWS, d), F32).at[:batch].set(c).at[batch].set(c_ctx)
    tables = _rope_tables(n_lat, tm_in)
    tri = (jnp.arange(TD)[:, None] <= jnp.arange(TD)[None, :]).astype(BF16)
    ltri = (jnp.arange(n_exp)[None, :] < jnp.arange(n_exp)[:, None]).astype(BF16)
    row = lambda v: v.reshape(1, -1)

    def dense_blocks(w):
        nb, bi, bj = w.shape
        eye = jnp.eye(nb, dtype=w.dtype)
        return (w[:, :, None, :] * eye[:, None, :, None]).reshape(nb * bi, nb * bj)

    for l in range(depth):
        last = l == depth - 1
        lam_init = 0.8 - 0.6 * math.exp(-0.3 * l)
        mod = _mod_rows(c_all, w_mod[l].astype(BF16), row(b_mod[l]))
        modp = mod.reshape(MOD_ROWS, 6, 1, d).transpose(1, 0, 2, 3)

        qr, kr, vb, proj = _in_proj(x_all, modp, w_in[l].astype(BF16), row(b_in[l]), tables, seg_in, tbl_idx, tm_in)

        lam_vecs = jnp.stack([lam_q1[l], lam_k1[l], lam_q2[l], lam_k2[l]])
        gain = attn_norm_g[l].reshape(A_HEADS, A_V_DIM, 1)
        attn = functools.partial(_attention, qr, kr, vb.T, lam_vecs, gain, lam_init=lam_init,
                                 batch=batch, n_lat=n_lat, n_ctx=n_ctx)
        n_rows = t_lat if last else t_all
        o_a = attn(latent_queries=True, out_rows=n_rows)
        if not last:
            o_a = attn(latent_queries=False, out_rows=n_rows, into=o_a)

        o_b = _spatial_gating(proj, row(sg_ln_g[l]), row(sg_ln_b[l]), sg_w[l].astype(BF16),
                              sg_b[l].reshape(B_GROUPS, B_CHUNK, 1))

        lru = functools.partial(_lru_pass, proj, conv_w=conv_w[l], conv_b=row(conv_b[l]), batch=batch)
        zeros_h = jnp.zeros((batch, 8, cw), F32)
        hf = {}
        for direction in range(2):
            wd = jnp.concatenate([dense_blocks(lru_wa[l, direction]), dense_blocks(lru_wx[l, direction])],
                                 axis=1).astype(BF16)
            par = dict(w_dense=wd, ba=row(lru_ba[l, direction]), bx=row(lru_bx[l, direction]),
                       lam=row(lru_lam[l, direction]), reverse=direction == 1, final=direction == 1)
            shared = direction == 1 and not last
            h_ctx, edge = lru(hf.get("ctx"), zeros_h, seq=n_ctx, tl=n_ctx, row0=t_lat,
                              out_rows=t_all if shared else None, out_row0=t_lat if shared else 0, **par)
            h_lat, _ = lru(hf.get("lat"), edge, seq=n_lat, tl=TM, row0=0,
                           out_rows=t_all if shared else None, into=h_ctx if shared else None, **par)
            hf = {"ctx": h_ctx, "lat": h_lat}
        o_c = hf["lat"]
        wr_t = w_router[l].T
        wr_hi = wr_t.astype(BF16)
        wr_lo = (wr_t - wr_hi.astype(F32)).astype(BF16)
        xn, h2, scores_t = _merge(x_all, o_a, o_b, o_c, proj, modp, w_branch[l].astype(BF16),
                                  w_out[l].astype(BF16), row(ln1_g[l]), row(ln1_b[l]), wr_hi, wr_lo, seg,
                                  n_rows=n_rows, alpha=alpha)
        x_all = _moe(xn, h2, scores_t, modp, seg_td, router_bias[l], moe_w_gate,
                     moe_w_up, moe_w_down, sh_w_gate[l].astype(BF16),
                     sh_w_up[l].astype(BF16), sh_w_down[l].astype(BF16), row(ln2_g[l]), row(ln2_b[l]), tri, ltri,
                     alpha=alpha, layer=l)
    return x_all[:t_lat].reshape(batch, n_lat, d)
```

```python
import functools
import math

import jax
import jax.numpy as jnp
from jax import lax
from jax.experimental import pallas as pl
from jax.experimental.pallas import tpu as pltpu

F32 = jnp.float32
BF16 = jnp.bfloat16
I32 = jnp.int32

A_HEADS = 4
A_QK_DIM = 64
A_V_DIM = 2 * A_QK_DIM
GRID_W = 64
ROPE_THETA = 10000.0
B_CHUNK = 128
B_GROUPS = 4
C_BLOCKS = 8
C_POW = 8.0
TOP_K = 8
N_GROUPS = 8
TOPK_GROUPS = 4
ROUTED_SCALE = 2.5
LN_EPS = 1e-6
RMS_EPS = 1e-5

TM = 512
MOE_BLK = 512
TD = 256
UNIT = 16
UNIT_LANES = 256
SORT_CHUNK = 512
W_LANES = 128
EXPERT_BUFS = 4
LRU_BLOCK = 64
HALO = 16
ATTN_HEADS = 4
ATTN_SUB = 1024
MOD_ROWS = 16
VMEM_LIMIT = 56 * 1024 * 1024


def _cparams(sem):
    return pltpu.CompilerParams(dimension_semantics=sem, vmem_limit_bytes=VMEM_LIMIT)


def _ln(x):
    mu = jnp.mean(x, axis=-1, keepdims=True)
    xc = x - mu
    var = jnp.mean(xc * xc, axis=-1, keepdims=True)
    return xc * lax.rsqrt(var + LN_EPS)


def _gelu(x):
    cdf = 0.5 * (1.0 + jnp.tanh(math.sqrt(2.0 / math.pi) * (x + 0.044715 * (x * x * x))))
    return x * cdf


def _sigmoid(x):
    return 0.5 * jnp.tanh(0.5 * x) + 0.5


def _silu(x):
    return x * _sigmoid(x)


def _dot(a, b):
    return jnp.dot(a, b, preferred_element_type=F32)


def _dot_nt(a, b):
    return lax.dot_general(a, b, (((1,), (1,)), ((), ())), preferred_element_type=F32)


def _mod_kernel(c_ref, w_ref, b_ref, o_ref):
    o_ref[...] = _dot(_silu(c_ref[...]).astype(BF16), w_ref[...]) + b_ref[...]


def _mod_rows(c_all, w, b):
    m, d = c_all.shape
    n = w.shape[1]
    tn = 1536
    return pl.pallas_call(
        _mod_kernel,
        grid=(n // tn,),
        in_specs=[pl.BlockSpec((m, d), lambda j: (0, 0)),
                  pl.BlockSpec((d, tn), lambda j: (0, j)),
                  pl.BlockSpec((1, tn), lambda j: (0, j))],
        out_specs=pl.BlockSpec((m, tn), lambda j: (0, j)),
        out_shape=jax.ShapeDtypeStruct((m, n), F32),
        compiler_params=_cparams(("parallel",)),
        name="adaln_rows",
    )(c_all, w, b)


REST_TILE = 1280
QKV_TILES = 3
COL_U, COL_S, COL_X, COL_Y, COL_G = 0, 1, 2, 3, 4


def _in_proj_kernel(x_ref, sh_ref, sc_ref, w_ref, b_ref, cos_ref, sa_ref, sb_ref,
                    q_ref, k_ref, v_ref, h_ref):
    j = pl.program_id(1)

    @pl.when(j == 0)
    def _():
        h = _ln(x_ref[...]) * (1.0 + sc_ref[...]) + sh_ref[...]
        h_ref[...] = h.astype(BF16)

    acc = _dot(h_ref[...], w_ref[...]) + b_ref[...]

    def rope(dst_ref, scale):
        hw = cos_ref.shape[1]
        for c0 in range(0, acc.shape[1], hw):
            x = acc[:, c0:c0 + hw]
            r = x * cos_ref[...] + pltpu.roll(x, hw - 16, 1) * sa_ref[...] + pltpu.roll(x, 16, 1) * sb_ref[...]
            dst_ref[:, c0:c0 + hw] = (r * scale).astype(BF16)

    @pl.when(j == 0)
    def _():
        rope(q_ref, A_QK_DIM ** -0.5 * math.log2(math.e))

    @pl.when(j == 1)
    def _():
        rope(k_ref, 1.0)

    @pl.when(j == 2)
    def _():
        v_ref[...] = acc.astype(BF16)


def _rest_proj_kernel(h_ref, w_ref, b_ref, o_ref):
    o_ref[...] = (_dot(h_ref[...], w_ref[...]) + b_ref[...]).astype(o_ref.dtype)


def _in_proj(x_all, modp, w, b, tables, seg, tbl_idx, tm):
    t, d = x_all.shape
    tn = 2 * A_HEADS * A_QK_DIM
    n_qkv = QKV_TILES * tn
    n_rest = w.shape[1] - n_qkv
    cos, sa, sb = tables
    tbl = pl.BlockSpec((tm, cos.shape[1]), lambda i, j: (tbl_idx(i), 0))
    qkv = pl.BlockSpec((tm, tn), lambda i, j: (i, 0))
    qkv_shape = jax.ShapeDtypeStruct((t, tn), BF16)
    q, k, v, h = pl.pallas_call(
        _in_proj_kernel,
        grid=(t // tm, QKV_TILES),
        in_specs=[pl.BlockSpec((tm, d), lambda i, j: (i, 0)),
                  pl.BlockSpec((None, None, 1, d), lambda i, j: (0, seg(i), 0, 0)),
                  pl.BlockSpec((None, None, 1, d), lambda i, j: (1, seg(i), 0, 0)),
                  pl.BlockSpec((d, tn), lambda i, j: (0, j)),
                  pl.BlockSpec((1, tn), lambda i, j: (0, j)),
                  tbl, tbl, tbl],
        out_specs=[qkv, qkv, qkv, pl.BlockSpec((tm, d), lambda i, j: (i, 0))],
        out_shape=[qkv_shape, qkv_shape, qkv_shape, jax.ShapeDtypeStruct((t, d), BF16)],
        compiler_params=_cparams(("parallel", "arbitrary")),
        name="in_proj_qkv",
    )(x_all, modp, modp, w, b, cos, sa, sb)
    tr = REST_TILE if n_rest % REST_TILE == 0 else tn
    rest = pl.pallas_call(
        _rest_proj_kernel,
        grid=(t // tm, n_rest // tr),
        in_specs=[pl.BlockSpec((tm, d), lambda i, j: (i, 0)),
                  pl.BlockSpec((d, tr), lambda i, j: (0, j)),
                  pl.BlockSpec((1, tr), lambda i, j: (0, j))],
        out_specs=pl.BlockSpec((tm, tr), lambda i, j: (i, j)),
        out_shape=jax.ShapeDtypeStruct((t, n_rest), BF16),
        compiler_params=_cparams(("parallel", "parallel")),
        name="in_proj_rest",
    )(h, w[:, n_qkv:], b[:, n_qkv:])
    return q, k, v, rest


def _rope_tables(n, tm):
    rows = n // GRID_W
    pos_row = jnp.repeat(jnp.arange(rows), GRID_W).astype(F32)
    pos_col = jnp.tile(jnp.arange(GRID_W), rows).astype(F32)
    quarter = A_QK_DIM // 4
    inv = ROPE_THETA ** (-jnp.arange(quarter, dtype=F32) / quarter)
    ang_r = pos_row[:, None] * inv
    ang_c = pos_col[:, None] * inv
    ang = jnp.concatenate([ang_r, ang_r, ang_c, ang_c], axis=-1)
    cos = jnp.cos(ang)
    sin = jnp.sin(ang)
    first_half = (jnp.arange(A_QK_DIM) % (2 * quarter)) < quarter
    sa = jnp.where(first_half, -sin, 0.0)
    sb = jnp.where(first_half, 0.0, sin)

    def widen(tbl, fill):
        tbl = jnp.tile(tbl, (1, 2))
        return jnp.concatenate([tbl, jnp.full((tm, tbl.shape[1]), fill, F32)], axis=0)

    return widen(cos, 1.0), widen(sa, 0.0), widen(sb, 0.0)


def _attn_kernel(*refs, lam_init, with_lat, sub):
    if with_lat:
        lam_ref, gain_ref, q_ref, kc_ref, vc_ref, kl_ref, vl_ref, o_ref = refs
    else:
        lam_ref, gain_ref, q_ref, kc_ref, vc_ref, _, o_ref = refs
    lv = lam_ref[...]
    lam = (jnp.exp(jnp.sum(lv[0:1] * lv[1:2], axis=-1, keepdims=True))
           - jnp.exp(jnp.sum(lv[2:3] * lv[3:4], axis=-1, keepdims=True)) + lam_init)
    chunks = [(kc_ref, vc_ref, 0, kc_ref.shape[0])]
    if with_lat:
        ck = min(sub, kl_ref.shape[0])
        chunks += [(kl_ref, vl_ref, r0, ck) for r0 in range(0, kl_ref.shape[0], ck)]
    for hh in range(gain_ref.shape[0]):
        head = slice(hh * A_V_DIM, (hh + 1) * A_V_DIM)
        probs, dens = [], []
        for c in range(2):
            cols = slice(hh * A_V_DIM + c * A_QK_DIM, hh * A_V_DIM + (c + 1) * A_QK_DIM)
            qc = q_ref[:, cols]
            scores = [_dot_nt(k_ref[r0:r0 + n, cols], qc) for k_ref, _, r0, n in chunks]
            m = None
            for s in scores:
                ms = jnp.max(s, axis=0, keepdims=True)
                m = ms if m is None else jnp.maximum(m, ms)
            p = [jnp.exp2(s - m) for s in scores]
            den = None
            for pj in p:
                ds = jnp.sum(pj, axis=0, keepdims=True)
                den = ds if den is None else den + ds
            probs.append(p)
            dens.append(den)
        ratio = lam * dens[0] / dens[1]
        o = None
        for p1, p2, (_, v_ref, r0, n) in zip(probs[0], probs[1], chunks):
            w = p1 - p2 * ratio
            ps = _dot(v_ref[head, r0:r0 + n], w.astype(BF16))
            o = ps if o is None else o + ps
        o = o / dens[0]
        o = o * lax.rsqrt(jnp.mean(o * o, axis=0, keepdims=True) + RMS_EPS)
        o_ref[:, head] = ((o * gain_ref[hh]) * (1.0 - lam_init)).T


def _attention(qr, kr, vt, lam_vecs, gain, *, lam_init, batch, n_lat, n_ctx, latent_queries, out_rows, into=None):
    hw = ATTN_HEADS * A_V_DIM
    ctx_blk0 = batch * n_lat // n_ctx
    lam_spec = pl.BlockSpec(lam_vecs.shape, lambda b, h, i: (0, 0))
    gain_spec = pl.BlockSpec((ATTN_HEADS, A_V_DIM, 1), lambda b, h, i: (h, 0, 0))
    kv_ctx = pl.BlockSpec((n_ctx, hw), lambda b, h, i: (ctx_blk0 + b, h))
    vt_ctx = pl.BlockSpec((hw, n_ctx), lambda b, h, i: (h, ctx_blk0 + b))
    if latent_queries:
        tq = 256
        nq = n_lat // tq
        q_spec = pl.BlockSpec((tq, hw), lambda b, h, i: (b * nq + i, h))
        kv_lat = pl.BlockSpec((n_lat, hw), lambda b, h, i: (b, h))
        vt_lat = pl.BlockSpec((hw, n_lat), lambda b, h, i: (h, b))
        in_specs = [lam_spec, gain_spec, q_spec, kv_ctx, vt_ctx, kv_lat, vt_lat]
        args = (lam_vecs, gain, qr, kr, vt, kr, vt)
        out_blk0 = 0
        aliases = {}
    else:
        tq = n_ctx
        nq = 1
        q_spec = pl.BlockSpec((tq, hw), lambda b, h, i: (ctx_blk0 + b, h))
        in_specs = [lam_spec, gain_spec, q_spec, kv_ctx, vt_ctx, pl.BlockSpec(memory_space=pl.ANY)]
        args = (lam_vecs, gain, qr, kr, vt, into)
        out_blk0 = ctx_blk0
        aliases = {len(args) - 1: 0}
    return pl.pallas_call(
        functools.partial(_attn_kernel, lam_init=lam_init, with_lat=latent_queries, sub=ATTN_SUB),
        grid=(batch, A_HEADS // ATTN_HEADS, nq),
        in_specs=in_specs,
        out_specs=pl.BlockSpec((tq, hw), lambda b, h, i: (out_blk0 + b * nq + i, h)),
        out_shape=jax.ShapeDtypeStruct((out_rows, A_HEADS * A_V_DIM), F32),
        input_output_aliases=aliases,
        compiler_params=_cparams(("parallel", "parallel", "arbitrary")),
        name="diff_attn_lat" if latent_queries else "diff_attn_ctx",
    )(*args)


def _sg_kernel(u_ref, s_ref, g_ref, b_ref, w_ref, bs_ref, o_ref):
    gw = B_CHUNK
    for ci in range(TM // B_CHUNK):
        rows = slice(ci * B_CHUNK, (ci + 1) * B_CHUNK)
        for g in range(B_GROUPS):
            cols = slice(g * gw, (g + 1) * gw)
            vn = _ln(_gelu(s_ref[rows, cols].astype(F32))) * g_ref[:, cols] + b_ref[:, cols]
            mixed = _dot(w_ref[g], vn.astype(BF16)) + bs_ref[g]
            o_ref[rows, cols] = _gelu(u_ref[rows, cols].astype(F32)) * mixed


def _spatial_gating(proj, ln_g, ln_b, w_s, b_s):
    t = proj.shape[0]
    w = B_GROUPS * B_CHUNK
    full = lambda a: pl.BlockSpec(a.shape, lambda i: (0,) * a.ndim)
    return pl.pallas_call(
        _sg_kernel,
        grid=(t // TM,),
        in_specs=[pl.BlockSpec((TM, w), lambda i: (i, COL_U)),
                  pl.BlockSpec((TM, w), lambda i: (i, COL_S)),
                  full(ln_g), full(ln_b), full(w_s), full(b_s)],
        out_specs=pl.BlockSpec((TM, w), lambda i: (i, 0)),
        out_shape=jax.ShapeDtypeStruct((t, w), F32),
        compiler_params=_cparams(("parallel",)),
        name="spatial_gating",
    )(proj, proj, ln_g, ln_b, w_s, b_s)


def _lru_kernel(*refs, reverse, final, n_tiles, tl, aliased):
    if aliased:
        refs = refs[:12] + refs[13:]
    if final:
        (x_ref, xp_ref, xn_ref, cw_ref, cb_ref, w_ref, ba_ref, bx_ref, lam_ref, h0_ref,
         hf_ref, y_ref, o_ref, hl_ref, carry) = refs
    else:
        (x_ref, xp_ref, xn_ref, cw_ref, cb_ref, w_ref, ba_ref, bx_ref, lam_ref, h0_ref,
         o_ref, hl_ref, carry) = refs
    step = pl.program_id(1)

    @pl.when(step == 0)
    def _():
        carry[...] = h0_ref[...]

    ti = (n_tiles - 1 - step) if reverse else step
    x = x_ref[...].astype(F32)
    cw = x.shape[-1]
    row = lax.broadcasted_iota(I32, x.shape, 0)
    has_prev = ti > 0
    has_next = ti < n_tiles - 1
    p6 = jnp.where(has_prev, xp_ref[HALO - 2:HALO - 1, :].astype(F32), 0.0)
    p7 = jnp.where(has_prev, xp_ref[HALO - 1:HALO, :].astype(F32), 0.0)
    n0 = jnp.where(has_next, xn_ref[0:1, :].astype(F32), 0.0)
    xm1 = jnp.where(row == 0, p7, pltpu.roll(x, 1, 0))
    xm2 = jnp.where(row == 0, p6, jnp.where(row == 1, p7, pltpu.roll(x, 2, 0)))
    xp1 = jnp.where(row == tl - 1, n0, pltpu.roll(x, tl - 1, 0))
    taps = cw_ref[...]
    xs = taps[0:1] * xm2 + taps[1:2] * xm1 + taps[2:3] * x + taps[3:4] * xp1 + cb_ref[...]

    z = _dot(xs.astype(BF16), w_ref[...])
    r = _sigmoid(z[:, :cw] + ba_ref[...])
    ig = _sigmoid(z[:, cw:] + bx_ref[...])
    nl = -lam_ref[...]
    softplus = jnp.maximum(nl, 0.0) + jnp.log(1.0 + jnp.exp(-jnp.abs(nl)))
    log_a = -C_POW * r * softplus
    a = jnp.exp(log_a)
    u = jnp.sqrt(1.0 - jnp.exp(2.0 * log_a)) * (ig * xs)

    sb = min(LRU_BLOCK, tl)
    in_blk = row % sb
    d = 1
    while d < sb:
        if reverse:
            ok = in_blk < sb - d
            shift = tl - d
        else:
            ok = in_blk >= d
            shift = d
        a_sh = jnp.where(ok, pltpu.roll(a, shift, 0), 1.0)
        u_sh = jnp.where(ok, pltpu.roll(u, shift, 0), 0.0)
        u = u + a * u_sh
        a = a * a_sh
        d *= 2
    state = carry[0:1, :]
    blocks = [None] * (tl // sb)
    for b in (reversed(range(tl // sb)) if reverse else range(tl // sb)):
        hb = u[b * sb:(b + 1) * sb] + a[b * sb:(b + 1) * sb] * state
        state = hb[0:1, :] if reverse else hb[sb - 1:sb, :]
        blocks[b] = hb
    h = jnp.concatenate(blocks, axis=0)
    edge = state
    carry[...] = jnp.broadcast_to(edge, carry.shape)
    hl_ref[...] = jnp.broadcast_to(edge, hl_ref.shape)
    if final:
        o_ref[...] = _gelu(y_ref[...].astype(F32)) * (hf_ref[...] + h)
    else:
        o_ref[...] = h


def _lru_pass(proj, hf, h0, conv_w, conv_b, w_dense, ba, bx, lam, *, batch, seq, tl, row0, reverse, final,
              out_rows=None, out_row0=0, into=None):
    t_all = proj.shape[0]
    out_rows = batch * seq if out_rows is None else out_rows
    out_blk0 = out_row0 // tl
    assert into is None or final
    cw = conv_w.shape[1]
    n_tiles = seq // tl
    base = row0 // tl
    sub = tl // HALO
    last_halo = t_all // HALO - 1

    def tile(b, s):
        ti = (n_tiles - 1 - s) if reverse else s
        return base + b * n_tiles + ti

    def loc(b, s):
        ti = (n_tiles - 1 - s) if reverse else s
        return b * n_tiles + ti

    full = lambda a: pl.BlockSpec(a.shape, lambda b, s: (0,) * a.ndim)
    in_specs = [pl.BlockSpec((tl, cw), lambda b, s: (tile(b, s), COL_X)),
                pl.BlockSpec((HALO, cw), lambda b, s: (jnp.maximum(tile(b, s) * sub - 1, 0), COL_X)),
                pl.BlockSpec((HALO, cw), lambda b, s: (jnp.minimum((tile(b, s) + 1) * sub, last_halo), COL_X)),
                full(conv_w), full(conv_b), full(w_dense), full(ba), full(bx), full(lam),
                pl.BlockSpec((None, 8, cw), lambda b, s: (b, 0, 0))]
    args = [proj, proj, proj, conv_w, conv_b, w_dense, ba, bx, lam, h0]
    if final:
        in_specs += [pl.BlockSpec((tl, cw), lambda b, s: (loc(b, s), 0)),
                     pl.BlockSpec((tl, cw), lambda b, s: (tile(b, s), COL_Y))]
        args += [hf, proj]
    aliases = {}
    if into is not None:
        aliases = {len(args): 0}
        in_specs += [pl.BlockSpec(memory_space=pl.ANY)]
        args += [into]
    return pl.pallas_call(
        functools.partial(_lru_kernel, reverse=reverse, final=final, n_tiles=n_tiles, tl=tl,
                          aliased=into is not None),
        grid=(batch, n_tiles),
        in_specs=in_specs,
        out_specs=[pl.BlockSpec((tl, cw), lambda b, s: (out_blk0 + loc(b, s), 0)),
                   pl.BlockSpec((None, 8, cw), lambda b, s: (b, 0, 0))],
        out_shape=[jax.ShapeDtypeStruct((out_rows, cw), F32),
                   jax.ShapeDtypeStruct((batch, 8, cw), F32)],
        scratch_shapes=[pltpu.VMEM((8, cw), F32)],
        input_output_aliases=aliases,
        compiler_params=_cparams(("parallel", "arbitrary")),
        name="rglru_%s_%s" % ("bwd" if reverse else "fwd", "lat" if row0 == 0 else "ctx"),
    )(*args)


def _merge_kernel(x_ref, oa_ref, ob_ref, oc_ref, g0, g1, g2, g3, g4, g5, wb_ref, wo_ref,
                  gate1_ref, sh2_ref, sc2_ref, lng_ref, lnb_ref, wrh_ref, wrl_ref,
                  xn_ref, h2_ref, sc_ref, *, alpha):
    gates = ((g0, g1), (g2, g3), (g4, g5))
    mix = None
    for r, o_ref in enumerate((oa_ref, ob_ref, oc_ref)):
        proj = _dot(o_ref[...].astype(BF16), wb_ref[r])
        gate = jnp.concatenate([_sigmoid(gates[r][0][...].astype(F32)), _sigmoid(gates[r][1][...].astype(F32))],
                               axis=-1)
        term = gate * proj
        mix = term if mix is None else mix + term
    out = _dot(mix.astype(BF16), wo_ref[...])
    xn = _ln(alpha * x_ref[...] + gate1_ref[...] * out) * lng_ref[...] + lnb_ref[...]
    xn_ref[...] = xn
    h2 = _ln(xn) * (1.0 + sc2_ref[...]) + sh2_ref[...]
    h_hi = h2.astype(BF16)
    h2_ref[...] = h_hi
    h_lo = (h2 - h_hi.astype(F32)).astype(BF16)
    logits = _dot_nt(wrh_ref[...], h_hi) + (_dot_nt(wrh_ref[...], h_lo) + _dot_nt(wrl_ref[...], h_hi))
    sc_ref[...] = _sigmoid(logits)


def _merge(x_all, o_a, o_b, o_c, proj, modp, w_branch, w_out, ln_g, ln_b, wr_hi, wr_lo, seg, *, n_rows, alpha):
    d = x_all.shape[1]
    bw = o_a.shape[1]
    n_exp = wr_hi.shape[0]
    tok = lambda w: pl.BlockSpec((TM, w), lambda i: (i, 0))
    gate = lambda c: pl.BlockSpec((TM, bw), lambda i, c=c: (i, c))
    mod = lambda k: pl.BlockSpec((None, None, 1, d), lambda i, k=k: (k, seg(i), 0, 0))
    full = lambda a: pl.BlockSpec(a.shape, lambda i: (0,) * a.ndim)
    return pl.pallas_call(
        functools.partial(_merge_kernel, alpha=alpha),
        grid=(n_rows // TM,),
        in_specs=[tok(d), tok(bw), tok(bw), tok(bw)] + [gate(COL_G + c) for c in range(6)]
                 + [full(w_branch), full(w_out), mod(2), mod(3), mod(4), full(ln_g), full(ln_b),
                    full(wr_hi), full(wr_lo)],
        out_specs=[tok(d), tok(d), pl.BlockSpec((n_exp, TM), lambda i: (0, i))],
        out_shape=[jax.ShapeDtypeStruct((n_rows, d), F32),
                   jax.ShapeDtypeStruct((n_rows, d), BF16),
                   jax.ShapeDtypeStruct((n_exp, n_rows), F32)],
        compiler_params=_cparams(("parallel",)),
        name="merge_residual_router",
    )(x_all, o_a, o_b, o_c, proj, proj, proj, proj, proj, proj, w_branch, w_out,
      modp, modp, modp, ln_g, ln_b, wr_hi, wr_lo)


def _tile_rows(n_exp):
    return TOP_K * TD + n_exp * UNIT


def _route_kernel(s_ref, bias_ref, tri_ref, ltri_ref, pos_ref, w_ref, rs_ref, rn_ref, unit_ref, tot_ref, carry):
    @pl.when(pl.program_id(0) == 0)
    def _():
        carry[...] = jnp.zeros_like(carry)

    s = s_ref[...]
    n_exp, tn = s.shape
    per = n_exp // N_GROUPS
    neg = -jnp.inf
    biased = s + bias_ref[...]
    sub = lax.broadcasted_iota(I32, (per, tn), 0)
    gs_rows = []
    for g in range(N_GROUPS):
        blk = biased[g * per:(g + 1) * per, :]
        m1 = jnp.max(blk, axis=0, keepdims=True)
        first = jnp.min(jnp.where(blk == m1, sub, per), axis=0, keepdims=True)
        m2 = jnp.max(jnp.where(sub == first, neg, blk), axis=0, keepdims=True)
        gs_rows.append(m1 + m2)
    gs = jnp.concatenate(gs_rows, axis=0)
    gi = lax.broadcasted_iota(I32, gs.shape, 0)
    g_ok = jnp.zeros(gs.shape, F32)
    cur = gs
    for _ in range(TOPK_GROUPS):
        m = jnp.max(cur, axis=0, keepdims=True)
        pick = jnp.min(jnp.where(cur == m, gi, N_GROUPS), axis=0, keepdims=True)
        hit = gi == pick
        g_ok = jnp.where(hit, 1.0, g_ok)
        cur = jnp.where(hit, neg, cur)
    ok_rows = [jnp.broadcast_to(g_ok[g:g + 1, :], (per, tn)) for g in range(N_GROUPS)]
    expert_ok = jnp.concatenate(ok_rows, axis=0)
    masked = jnp.where(expert_ok > 0.0, biased, neg)
    ei = lax.broadcasted_iota(I32, s.shape, 0)
    pick_s, hits = [], []
    sel = jnp.zeros(s.shape, F32)
    for _ in range(TOP_K):
        m = jnp.max(masked, axis=0, keepdims=True)
        pick = jnp.min(jnp.where(masked == m, ei, n_exp), axis=0, keepdims=True)
        hit = ei == pick
        hits.append(hit)
        pick_s.append(jnp.sum(jnp.where(hit, s, 0.0), axis=0, keepdims=True))
        sel = jnp.where(hit, 1.0, sel)
        masked = jnp.where(hit, neg, masked)
    tot = pick_s[0]
    for k in range(1, TOP_K):
        tot = tot + pick_s[k]
    w = jnp.where(sel > 0.0, s, 0.0) / tot * ROUTED_SCALE
    w_hi = w.astype(BF16)
    w_ref[...] = jnp.concatenate([w_hi, (w - w_hi.astype(F32)).astype(BF16)], axis=0)

    incl = _dot(sel.astype(BF16), tri_ref[...])
    count = incl[:, tn - 1:tn]
    run = jnp.floor((count + (UNIT - 1)) * (1.0 / UNIT)) * UNIT
    run_start = _dot(ltri_ref[...], jnp.broadcast_to(run, (n_exp, 128)).astype(BF16))[:, 0:1]
    row_in_tile = incl - sel + run_start
    pos = [jnp.sum(jnp.where(hit, row_in_tile, 0.0), axis=0, keepdims=True) for hit in hits]
    pos_ref[...] = jnp.concatenate(pos, axis=0).astype(I32)
    rs_ref[...] = jnp.broadcast_to(run_start, rs_ref.shape)
    rn_ref[...] = jnp.broadcast_to(run, rn_ref.shape)

    half = UNIT_LANES // 2
    region_used = carry[:, 0:1]
    u0 = run_start * (1.0 / UNIT)
    nu = run * (1.0 / UNIT)
    n_pair = jnp.floor(nu * 0.5)
    n_single = nu - 2.0 * n_pair
    both = jnp.concatenate([jnp.broadcast_to(n_pair, (n_exp, 64)), jnp.broadcast_to(n_single, (n_exp, 64))], axis=1)
    first = _dot(ltri_ref[...], both.astype(BF16))
    pair0, single0 = first[:, 0:1], first[:, 64:65]
    lane = lax.broadcasted_iota(I32, (n_exp, UNIT_LANES), 1)
    is_pair = lane < half
    item = jnp.where(is_pair, lane, lane - half).astype(F32)
    start = jnp.where(is_pair, pair0, single0)
    count = jnp.where(is_pair, n_pair, n_single)
    inside = jnp.logical_and(item >= start, item < start + count)
    off = jnp.where(is_pair, 2.0 * (item - pair0), 2.0 * n_pair)
    eu = lax.broadcasted_iota(I32, (n_exp, UNIT_LANES), 0).astype(F32)
    pick = lambda v: jnp.sum(jnp.where(inside, v, 0.0), axis=0, keepdims=True)
    totals = jnp.where(is_pair[0:1], jnp.sum(n_pair, axis=0, keepdims=True), jnp.sum(n_single, axis=0, keepdims=True))
    pad = jnp.zeros((4, UNIT_LANES), F32)
    unit_ref[...] = jnp.concatenate([pick(off + region_used * (1.0 / UNIT)), pick(eu), pick(off + u0), totals, pad],
                                    axis=0).astype(I32)

    total = region_used + run
    carry[...] = jnp.broadcast_to(total, carry.shape)
    tot_ref[...] = jnp.broadcast_to(total, tot_ref.shape)


def _route(scores_t, bias_col, tri, ltri):
    n_exp, t = scores_t.shape
    nt = t // TD
    return pl.pallas_call(
        _route_kernel,
        grid=(nt,),
        in_specs=[pl.BlockSpec((n_exp, TD), lambda i: (0, i)),
                  pl.BlockSpec((n_exp, 1), lambda i: (0, 0)),
                  pl.BlockSpec((TD, TD), lambda i: (0, 0)),
                  pl.BlockSpec((n_exp, n_exp), lambda i: (0, 0))],
        out_specs=[pl.BlockSpec((TOP_K, TD), lambda i: (0, i)),
                   pl.BlockSpec((2 * n_exp, TD), lambda i: (0, i)),
                   pl.BlockSpec((n_exp, 128), lambda i: (0, i)),
                   pl.BlockSpec((n_exp, 128), lambda i: (0, i)),
                   pl.BlockSpec((8, UNIT_LANES), lambda i: (0, i)),
                   pl.BlockSpec((n_exp, 128), lambda i: (0, 0))],
        out_shape=[jax.ShapeDtypeStruct((TOP_K, t), I32),
                   jax.ShapeDtypeStruct((2 * n_exp, t), BF16),
                   jax.ShapeDtypeStruct((n_exp, nt * 128), F32),
                   jax.ShapeDtypeStruct((n_exp, nt * 128), F32),
                   jax.ShapeDtypeStruct((8, nt * UNIT_LANES), I32),
                   jax.ShapeDtypeStruct((n_exp, 128), F32)],
        scratch_shapes=[pltpu.VMEM((n_exp, 128), F32)],
        compiler_params=_cparams(("arbitrary",)),
        name="route_topk",
    )(scores_t, bias_col, tri, ltri)


def _unit_dest_kernel(start_ref, unit_ref, o_ref, *, n_exp):
    tbl = unit_ref[...]
    exp_of = tbl[1:2, :]
    dst = tbl[0:1, :]
    for e in range(n_exp):
        dst = dst + jnp.where(exp_of == e, start_ref[e], 0)
    o_ref[...] = jnp.concatenate([dst, tbl[2:4, :], jnp.zeros((5, tbl.shape[1]), I32)], axis=0)


def _unit_dest(region_start_units, units):
    nt = units.shape[1] // UNIT_LANES
    per_step = next(k for k in (4, 2, 1) if nt % k == 0)
    blk = lambda: pl.BlockSpec((8, per_step * UNIT_LANES), lambda i, s: (0, i))
    return pl.pallas_call(
        functools.partial(_unit_dest_kernel, n_exp=region_start_units.shape[0]),
        grid_spec=pltpu.PrefetchScalarGridSpec(
            num_scalar_prefetch=1, grid=(nt // per_step,), in_specs=[blk()], out_specs=blk()),
        out_shape=jax.ShapeDtypeStruct(units.shape, I32),
        compiler_params=_cparams(("parallel",)),
        name="route_units",
    )(region_start_units, units)


def _unit_rows(i, n_units=1):
    return pl.ds(pl.multiple_of(i * UNIT, UNIT), n_units * UNIT)


def _copy_items(tbl, make_copy, act):
    half = UNIT_LANES // 2
    _for_each(tbl[2, 0], lambda i: getattr(make_copy(tbl, i, 2), act)())
    _for_each(tbl[2, half], lambda i: getattr(make_copy(tbl, half + i, 1), act)())


def _drain_items(n_pairs, n_singles, make_copy, tbl):
    _for_each(n_pairs, lambda i: make_copy(tbl, 0, 2).wait())
    _for_each(n_singles, lambda i: make_copy(tbl, UNIT_LANES // 2, 1).wait())


def _for_each(count, fn, group=8):
    main = lax.shift_right_logical(count, int(math.log2(group)))

    def many(i, c):
        for r in range(group):
            fn(i * group + r)
        return c

    lax.fori_loop(0, main, many, 0)
    lax.fori_loop(main * group, count, lambda i, c: (fn(i), c)[1], 0)


def _dispatch_kernel(tail_lo_ref, tail_hi_ref, np_ref, ns_ref, unit_ref, pos_ref, w_ref, rs_ref, rn_ref, h_ref,
                     xs_ref, s_scr, zero_scr, sem, zsem, *, n_exp):
    rows = s_scr.shape[1]
    td = h_ref.shape[0]
    step = pl.program_id(0)
    last = pl.num_programs(0) - 1
    slot = step % 2

    def zero_copy(g):
        return pltpu.make_async_copy(zero_scr, xs_ref.at[_unit_rows(g), :], zsem.at[0])

    def item_copy(sl):
        def make(tbl, lane, size):
            return pltpu.make_async_copy(s_scr.at[sl, _unit_rows(tbl[1, lane], size), :],
                                         xs_ref.at[_unit_rows(tbl[0, lane], size), :], sem.at[size - 1, sl])
        return make

    def drain(tile, sl):
        _drain_items(np_ref[tile], ns_ref[tile], item_copy(sl), unit_ref)

    @pl.when(step >= 2)
    def _():
        drain(step - 2, slot)

    @pl.when(step == 0)
    def _():
        zero_scr[...] = jnp.zeros_like(zero_scr)

        def per_expert(e, c):
            lax.fori_loop(tail_lo_ref[e], tail_hi_ref[e], lambda g, cc: (zero_copy(g).start(), cc)[1], 0)
            lax.fori_loop(tail_lo_ref[e], tail_hi_ref[e], lambda g, cc: (zero_copy(g).wait(), cc)[1], 0)
            return c

        lax.fori_loop(0, n_exp, per_expert, 0)

    pos = pos_ref[...]
    h = h_ref[...]
    d = h.shape[1]
    w_dense = w_ref[...]
    run_lo = rs_ref[...]
    run_hi = run_lo + rn_ref[...]
    lane = lax.broadcasted_iota(I32, (SORT_CHUNK, s_scr.shape[2] - d), 1)
    for r0 in range(0, rows, SORT_CHUNK):
        srow = lax.broadcasted_iota(I32, (SORT_CHUNK, td), 0) + r0
        onehot = jnp.zeros((SORT_CHUNK, td), F32)
        for k in range(pos.shape[0]):
            onehot = jnp.where(srow == pos[k:k + 1, :], 1.0, onehot)
        s_e = (lax.broadcasted_iota(I32, (SORT_CHUNK, w_dense.shape[0]), 0) + r0).astype(F32)
        mine = jnp.logical_and(s_e >= run_lo, s_e < run_hi).astype(BF16)
        w_row = jnp.sum(onehot * _dot(mine, w_dense), axis=1, keepdims=True)
        w_hi = w_row.astype(BF16).astype(F32)
        w_lo = w_row - w_hi
        s_scr[slot, r0:r0 + SORT_CHUNK, :d] = _dot(onehot.astype(BF16), h).astype(BF16)
        s_scr[slot, r0:r0 + SORT_CHUNK, d:] = jnp.where(lane == 0, w_hi, jnp.where(lane == 1, w_lo, 0.0)).astype(BF16)

    _copy_items(unit_ref, item_copy(slot), "start")

    @pl.when(step == last)
    def _():
        drain(step, slot)

        @pl.when(step >= 1)
        def _():
            drain(step - 1, 1 - slot)


def _dispatch(tail_lo, tail_hi, n_pairs, n_singles, unit_dst, pos, w_dense, run_start_rows, run_rows, h2, n_slots):
    t, d = h2.shape
    n_exp = tail_lo.shape[0]
    dw = d + W_LANES
    runs = pl.BlockSpec((None, 1, 2 * n_exp), lambda i, a, b, c, e: (i, 0, 0))
    return pl.pallas_call(
        functools.partial(_dispatch_kernel, n_exp=n_exp),
        grid_spec=pltpu.PrefetchScalarGridSpec(
            num_scalar_prefetch=4, grid=(t // TD,),
            in_specs=[pl.BlockSpec((8, UNIT_LANES), lambda i, a, b, c, e: (0, i), memory_space=pltpu.SMEM),
                      pl.BlockSpec((TOP_K, TD), lambda i, a, b, c, e: (0, i)),
                      pl.BlockSpec((2 * n_exp, TD), lambda i, a, b, c, e: (0, i)),
                      runs, runs,
                      pl.BlockSpec((TD, d), lambda i, a, b, c, e: (i, 0))],
            out_specs=pl.BlockSpec(memory_space=pl.ANY),
            scratch_shapes=[pltpu.VMEM((2, _tile_rows(n_exp), dw), BF16), pltpu.VMEM((UNIT, dw), BF16),
                            pltpu.SemaphoreType.DMA((2, 2)), pltpu.SemaphoreType.DMA((1,))]),
        out_shape=jax.ShapeDtypeStruct((n_slots, dw), BF16),
        compiler_params=_cparams(("arbitrary",)),
        name="moe_dispatch",
    )(tail_lo, tail_hi, n_pairs, n_singles, unit_dst, pos, w_dense, run_start_rows, run_rows, h2)


def _expert_kernel(blk0_ref, nblk_ref, tot_ref, wg_ref, wu_ref, wd_ref, xs_ref, ys_ref, xbuf, obuf, wgu_scr, wd_scr,
                   sem_in, sem_out):
    e = pl.program_id(0)
    n = nblk_ref[e]
    b0 = blk0_ref[e]
    de = wg_ref.shape[1]
    wgu_scr[:, :de] = wg_ref[...].astype(BF16)
    wgu_scr[:, de:] = wu_ref[...].astype(BF16)
    wd_scr[...] = wd_ref[...].astype(BF16)

    total = tot_ref[0]

    def rows(g):
        return pl.ds(pl.multiple_of(g * MOE_BLK, MOE_BLK), MOE_BLK)

    def in_copy(g):
        sl = g % EXPERT_BUFS
        return pltpu.make_async_copy(xs_ref.at[rows(g), :], xbuf.at[sl], sem_in.at[sl])

    def out_copy(g):
        sl = g % EXPERT_BUFS
        return pltpu.make_async_copy(obuf.at[sl], ys_ref.at[rows(g), :], sem_out.at[sl])

    @pl.when(e == 0)
    def _():
        for g in range(EXPERT_BUFS - 1):
            @pl.when(g < total)
            def _():
                in_copy(g).start()

    def block(j, c):
        g = b0 + j
        sl = g % EXPERT_BUFS
        in_copy(g).wait()

        @pl.when(g + (EXPERT_BUFS - 1) < total)
        def _():
            in_copy(g + (EXPERT_BUFS - 1)).start()

        @pl.when(g >= EXPERT_BUFS)
        def _():
            out_copy(g - EXPERT_BUFS).wait()

        d = wd_scr.shape[1]
        gu = _dot(xbuf[sl, :, :d], wgu_scr[...])
        act = _silu(gu[:, :de]) * gu[:, de:]
        w_row = xbuf[sl, :, d:d + 1].astype(F32) + xbuf[sl, :, d + 1:d + 2].astype(F32)
        obuf[sl] = (_dot(act.astype(BF16), wd_scr[...]) * w_row).astype(BF16)
        out_copy(g).start()
        return c

    lax.fori_loop(0, n, block, 0)

    @pl.when(e == pl.num_programs(0) - 1)
    def _():
        for back in range(1, EXPERT_BUFS + 1):
            @pl.when(total >= back)
            def _():
                out_copy(total - back).wait()


def _experts(blk0, nblk, n_blocks, xs, w_gate, w_up, w_down, layer):
    n_slots, dw = xs.shape
    _, n_exp, d, de = w_gate.shape
    return pl.pallas_call(
        _expert_kernel,
        grid_spec=pltpu.PrefetchScalarGridSpec(
            num_scalar_prefetch=3, grid=(n_exp,),
            in_specs=[pl.BlockSpec((None, None, d, de), lambda e, a, b, c: (layer, e, 0, 0)),
                      pl.BlockSpec((None, None, d, de), lambda e, a, b, c: (layer, e, 0, 0)),
                      pl.BlockSpec((None, None, de, d), lambda e, a, b, c: (layer, e, 0, 0)),
                      pl.BlockSpec(memory_space=pl.ANY)],
            out_specs=pl.BlockSpec(memory_space=pl.ANY),
            scratch_shapes=[pltpu.VMEM((EXPERT_BUFS, MOE_BLK, dw), BF16), pltpu.VMEM((EXPERT_BUFS, MOE_BLK, d), BF16),
                            pltpu.VMEM((d, 2 * de), BF16), pltpu.VMEM((de, d), BF16),
                            pltpu.SemaphoreType.DMA((EXPERT_BUFS,)), pltpu.SemaphoreType.DMA((EXPERT_BUFS,))]),
        out_shape=jax.ShapeDtypeStruct((n_slots, d), BF16),
        compiler_params=_cparams(("arbitrary",)),
        name="moe_experts",
    )(blk0, nblk, n_blocks, w_gate, w_up, w_down, xs)


def _combine_kernel(unit_ref, next_ref, pos_ref, x_ref, h_ref, gate2_ref,
                    sg_ref, su_ref, sd_ref, lng_ref, lnb_ref, ys_ref, o_ref, s_scr, sem, *, alpha):
    rows = s_scr.shape[1]
    td = x_ref.shape[0]
    step = pl.program_id(0)
    slot = step % 2

    def item_copy(sl):
        def make(tbl, lane, size):
            return pltpu.make_async_copy(ys_ref.at[_unit_rows(tbl[0, lane], size), :],
                                         s_scr.at[sl, _unit_rows(tbl[1, lane], size), :], sem.at[size - 1, sl])
        return make

    def fetch(tbl, sl):
        _copy_items(tbl, item_copy(sl), "start")

    @pl.when(step == 0)
    def _():
        s_scr[...] = jnp.zeros_like(s_scr)
        fetch(unit_ref, 0)

    @pl.when(step + 1 < pl.num_programs(0))
    def _():
        fetch(next_ref, 1 - slot)

    h = h_ref[...]
    acc = _dot((_silu(_dot(h, sg_ref[...])) * _dot(h, su_ref[...])).astype(BF16), sd_ref[...])
    pos = pos_ref[...]
    _drain_items(unit_ref[2, 0], unit_ref[2, UNIT_LANES // 2], item_copy(slot), unit_ref)
    for r0 in range(0, rows, SORT_CHUNK):
        lane = lax.broadcasted_iota(I32, (td, SORT_CHUNK), 1) + r0
        onehot = jnp.zeros((td, SORT_CHUNK), F32)
        for k in range(pos.shape[1]):
            onehot = jnp.where(lane == pos[:, k:k + 1], 1.0, onehot)
        acc = acc + _dot(onehot.astype(BF16), s_scr[slot, r0:r0 + SORT_CHUNK, :])
    o_ref[...] = _ln(alpha * x_ref[...] + gate2_ref[...] * acc) * lng_ref[...] + lnb_ref[...]


def _combine(unit_dst, pos_tok, xn, h2, modp, sh_gate, sh_up, sh_down, ln_g, ln_b, ys, seg_td, *, alpha, n_exp):
    t, d = xn.shape
    nt = t // TD
    tok = pl.BlockSpec((TD, d), lambda i: (i, 0))
    full = lambda a: pl.BlockSpec(a.shape, lambda i: (0,) * a.ndim)
    return pl.pallas_call(
        functools.partial(_combine_kernel, alpha=alpha),
        grid=(nt,),
        in_specs=[pl.BlockSpec((8, UNIT_LANES), lambda i: (0, i), memory_space=pltpu.SMEM),
                  pl.BlockSpec((8, UNIT_LANES), lambda i: (0, jnp.minimum(i + 1, nt - 1)), memory_space=pltpu.SMEM),
                  pl.BlockSpec((TD, pos_tok.shape[1]), lambda i: (i, 0)),
                  tok, tok,
                  pl.BlockSpec((None, None, 1, d), lambda i: (5, seg_td(i), 0, 0)),
                  full(sh_gate), full(sh_up), full(sh_down), full(ln_g), full(ln_b),
                  pl.BlockSpec(memory_space=pl.ANY)],
        out_specs=tok,
        out_shape=jax.ShapeDtypeStruct((t, d), F32),
        scratch_shapes=[pltpu.VMEM((2, _tile_rows(n_exp), d), BF16), pltpu.SemaphoreType.DMA((2, 2))],
        compiler_params=_cparams(("arbitrary",)),
        name="moe_combine",
    )(unit_dst, unit_dst, pos_tok, xn, h2, modp, sh_gate, sh_up, sh_down, ln_g, ln_b, ys)


def _moe(xn, h2, scores_t, modp, seg_td, router_bias, w_gate, w_up, w_down, sh_gate, sh_up, sh_down,
         ln_g, ln_b, tri, ltri, *, alpha, layer):
    t = xn.shape[0]
    n_exp = scores_t.shape[0]
    assert _tile_rows(n_exp) % SORT_CHUNK == 0 and _tile_rows(n_exp) <= UNIT * UNIT_LANES
    pos, w_sel, run_start, run_len, units, tot = _route(scores_t, router_bias.reshape(n_exp, 1), tri, ltri)
    as_rows = lambda a: jnp.tile(a[:, ::128].T, (1, 2))[:, None, :]
    used = tot[:, 0].astype(I32)
    region = (used + MOE_BLK - 1) // MOE_BLK * MOE_BLK
    region_end = jnp.cumsum(region)
    region_start = region_end - region
    n_slots = TOP_K * t + n_exp * UNIT * (t // TD) + n_exp * MOE_BLK
    unit_dst = _unit_dest((region_start // UNIT).astype(I32), units)
    xs = _dispatch(((region_start + used) // UNIT).astype(I32), (region_end // UNIT).astype(I32),
                   unit_dst[2, ::UNIT_LANES], unit_dst[2, UNIT_LANES // 2::UNIT_LANES], unit_dst, pos, w_sel,
                   as_rows(run_start), as_rows(run_len), h2,
                   n_slots)
    ys = _experts((region_start // MOE_BLK).astype(I32), (region // MOE_BLK).astype(I32),
                  (region_end[-1:] // MOE_BLK).astype(I32), xs, w_gate, w_up, w_down, layer)
    return _combine(unit_dst, pos.T, xn, h2, modp, sh_gate, sh_up, sh_down, ln_g, ln_b, ys, seg_td,
                    alpha=alpha, n_exp=n_exp)


def kernel(x, c, ctx, c_ctx, w_mod, b_mod, w_in, b_in, lam_q1, lam_k1, lam_q2, lam_k2, attn_norm_g, sg_ln_g, sg_ln_b, sg_w, sg_b, conv_w, conv_b, lru_wa, lru_ba, lru_wx, lru_bx, lru_lam, w_branch, w_out, ln1_g, ln1_b, w_router, router_bias, moe_w_gate, moe_w_up, moe_w_down, sh_w_gate, sh_w_up, sh_w_down, ln2_g, ln2_b):
    batch, n_lat, d = x.shape
    n_ctx = ctx.shape[1]
    depth = w_mod.shape[0]
    n_exp = w_router.shape[2]
    t_lat = batch * n_lat
    t_ctx = batch * n_ctx
    t_all = t_lat + t_ctx
    assert n_lat % TM == 0 and t_ctx % TM == 0 and n_ctx % B_CHUNK == 0 and t_lat % n_ctx == 0
    assert batch + 1 <= MOD_ROWS and TM % TD == 0
    alpha = (2 * depth) ** 0.25
    cw = conv_w.shape[2]
    tiles_per_batch = n_lat // TM

    seg = lambda i: jnp.minimum(i // tiles_per_batch, batch)
    seg_td = lambda i: jnp.minimum(i // (n_lat // TD), batch)
    tm_in = 2 * TM if (n_lat % (2 * TM) == 0 and t_ctx % (2 * TM) == 0) else TM
    seg_in = lambda i: jnp.minimum(i // (n_lat // tm_in), batch)
    tbl_idx = lambda i: jnp.where(i < t_lat // tm_in, i % (n_lat // tm_in), n_lat // tm_in)

    x_all = jnp.concatenate([x.reshape(t_lat, d), ctx.reshape(t_ctx, d)], axis=0)
    c_all = jnp.zeros((MOD_ROWS, d), F32).at[:batch].set(c).at[batch].set(c_ctx)
    tables = _rope_tables(n_lat, tm_in)
    tri = (jnp.arange(TD)[:, None] <= jnp.arange(TD)[None, :]).astype(BF16)
    ltri = (jnp.arange(n_exp)[None, :] < jnp.arange(n_exp)[:, None]).astype(BF16)
    row = lambda v: v.reshape(1, -1)

    def dense_blocks(w):
        nb, bi, bj = w.shape
        eye = jnp.eye(nb, dtype=w.dtype)
        return (w[:, :, None, :] * eye[:, None, :, None]).reshape(nb * bi, nb * bj)

    for l in range(depth):
        last = l == depth - 1
        lam_init = 0.8 - 0.6 * math.exp(-0.3 * l)
        mod = _mod_rows(c_all, w_mod[l].astype(BF16), row(b_mod[l]))
        modp = mod.reshape(MOD_ROWS, 6, 1, d).transpose(1, 0, 2, 3)

        qr, kr, vb, proj = _in_proj(x_all, modp, w_in[l].astype(BF16), row(b_in[l]), tables, seg_in, tbl_idx, tm_in)

        lam_vecs = jnp.stack([lam_q1[l], lam_k1[l], lam_q2[l], lam_k2[l]])
        gain = attn_norm_g[l].reshape(A_HEADS, A_V_DIM, 1)
        attn = functools.partial(_attention, qr, kr, vb.T, lam_vecs, gain, lam_init=lam_init,
                                 batch=batch, n_lat=n_lat, n_ctx=n_ctx)
        n_rows = t_lat if last else t_all
        o_a = attn(latent_queries=True, out_rows=n_rows)
        if not last:
            o_a = attn(latent_queries=False, out_rows=n_rows, into=o_a)

        o_b = _spatial_gating(proj, row(sg_ln_g[l]), row(sg_ln_b[l]), sg_w[l].astype(BF16),
                              sg_b[l].reshape(B_GROUPS, B_CHUNK, 1))

        lru = functools.partial(_lru_pass, proj, conv_w=conv_w[l], conv_b=row(conv_b[l]), batch=batch)
        zeros_h = jnp.zeros((batch, 8, cw), F32)
        hf = {}
        for direction in range(2):
            wd = jnp.concatenate([dense_blocks(lru_wa[l, direction]), dense_blocks(lru_wx[l, direction])],
                                 axis=1).astype(BF16)
            par = dict(w_dense=wd, ba=row(lru_ba[l, direction]), bx=row(lru_bx[l, direction]),
                       lam=row(lru_lam[l, direction]), reverse=direction == 1, final=direction == 1)
            shared = direction == 1 and not last
            h_ctx, edge = lru(hf.get("ctx"), zeros_h, seq=n_ctx, tl=n_ctx, row0=t_lat,
                              out_rows=t_all if shared else None, out_row0=t_lat if shared else 0, **par)
            h_lat, _ = lru(hf.get("lat"), edge, seq=n_lat, tl=TM, row0=0,
                           out_rows=t_all if shared else None, into=h_ctx if shared else None, **par)
            hf = {"ctx": h_ctx, "lat": h_lat}
        o_c = hf["lat"]
        wr_t = w_router[l].T
        wr_hi = wr_t.astype(BF16)
        wr_lo = (wr_t - wr_hi.astype(F32)).astype(BF16)
        xn, h2, scores_t = _merge(x_all, o_a, o_b, o_c, proj, modp, w_branch[l].astype(BF16),
                                  w_out[l].astype(BF16), row(ln1_g[l]), row(ln1_b[l]), wr_hi, wr_lo, seg,
                                  n_rows=n_rows, alpha=alpha)
        x_all = _moe(xn, h2, scores_t, modp, seg_td, router_bias[l], moe_w_gate,
                     moe_w_up, moe_w_down, sh_w_gate[l].astype(BF16),
                     sh_w_up[l].astype(BF16), sh_w_down[l].astype(BF16), row(ln2_g[l]), row(ln2_b[l]), tri, ltri,
                     alpha=alpha, layer=l)
    return x_all[:t_lat].reshape(batch, n_lat, d)
```

```python
import functools
import math

import jax
import jax.numpy as jnp
from jax import lax
from jax.experimental import pallas as pl
from jax.experimental.pallas import tpu as pltpu

F32 = jnp.float32
BF16 = jnp.bfloat16
I32 = jnp.int32

A_HEADS = 4
A_QK_DIM = 64
A_V_DIM = 2 * A_QK_DIM
GRID_W = 64
ROPE_THETA = 10000.0
B_CHUNK = 128
B_GROUPS = 4
C_BLOCKS = 8
C_POW = 8.0
TOP_K = 8
N_GROUPS = 8
TOPK_GROUPS = 4
ROUTED_SCALE = 2.5
LN_EPS = 1e-6
RMS_EPS = 1e-5

TM = 512
MOE_BLK = 512
TD = 256
UNIT = 16
UNIT_LANES = 256
SORT_CHUNK = 512
W_LANES = 128
EXPERT_BUFS = 4
LRU_BLOCK = 64
HALO = 16
ATTN_HEADS = 4
ATTN_SUB = 1024
MOD_ROWS = 16
VMEM_LIMIT = 56 * 1024 * 1024


def _cparams(sem):
    return pltpu.CompilerParams(dimension_semantics=sem, vmem_limit_bytes=VMEM_LIMIT)


def _ln(x):
    mu = jnp.mean(x, axis=-1, keepdims=True)
    xc = x - mu
    var = jnp.mean(xc * xc, axis=-1, keepdims=True)
    return xc * lax.rsqrt(var + LN_EPS)


def _gelu(x):
    cdf = 0.5 * (1.0 + jnp.tanh(math.sqrt(2.0 / math.pi) * (x + 0.044715 * (x * x * x))))
    return x * cdf


def _sigmoid(x):
    return 0.5 * jnp.tanh(0.5 * x) + 0.5


def _silu(x):
    return x * _sigmoid(x)


def _dot(a, b):
    return jnp.dot(a, b, preferred_element_type=F32)


def _dot_nt(a, b):
    return lax.dot_general(a, b, (((1,), (1,)), ((), ())), preferred_element_type=F32)


def _mod_kernel(c_ref, w_ref, b_ref, o_ref):
    o_ref[...] = _dot(_silu(c_ref[...]).astype(BF16), w_ref[...]) + b_ref[...]


def _mod_rows(c_all, w, b):
    m, d = c_all.shape
    n = w.shape[1]
    tn = 1536
    return pl.pallas_call(
        _mod_kernel,
        grid=(n // tn,),
        in_specs=[pl.BlockSpec((m, d), lambda j: (0, 0)),
                  pl.BlockSpec((d, tn), lambda j: (0, j)),
                  pl.BlockSpec((1, tn), lambda j: (0, j))],
        out_specs=pl.BlockSpec((m, tn), lambda j: (0, j)),
        out_shape=jax.ShapeDtypeStruct((m, n), F32),
        compiler_params=_cparams(("parallel",)),
        name="adaln_rows",
    )(c_all, w, b)


REST_TILE = 1280
QKV_TILES = 3
COL_U, COL_S, COL_X, COL_Y, COL_G = 0, 1, 2, 3, 4


def _in_proj_kernel(x_ref, sh_ref, sc_ref, w_ref, b_ref, cos_ref, sa_ref, sb_ref,
                    q_ref, k_ref, v_ref, h_ref):
    j = pl.program_id(1)

    @pl.when(j == 0)
    def _():
        h = _ln(x_ref[...]) * (1.0 + sc_ref[...]) + sh_ref[...]
        h_ref[...] = h.astype(BF16)

    acc = _dot(h_ref[...], w_ref[...]) + b_ref[...]

    def rope(dst_ref, scale):
        hw = cos_ref.shape[1]
        for c0 in range(0, acc.shape[1], hw):
            x = acc[:, c0:c0 + hw]
            r = x * cos_ref[...] + pltpu.roll(x, hw - 16, 1) * sa_ref[...] + pltpu.roll(x, 16, 1) * sb_ref[...]
            dst_ref[:, c0:c0 + hw] = (r * scale).astype(BF16)

    @pl.when(j == 0)
    def _():
        rope(q_ref, A_QK_DIM ** -0.5 * math.log2(math.e))

    @pl.when(j == 1)
    def _():
        rope(k_ref, 1.0)

    @pl.when(j == 2)
    def _():
        v_ref[...] = acc.astype(BF16)


def _rest_proj_kernel(h_ref, w_ref, b_ref, o_ref):
    o_ref[...] = (_dot(h_ref[...], w_ref[...]) + b_ref[...]).astype(o_ref.dtype)


def _in_proj(x_all, modp, w, b, tables, seg, tbl_idx, tm):
    t, d = x_all.shape
    tn = 2 * A_HEADS * A_QK_DIM
    n_qkv = QKV_TILES * tn
    n_rest = w.shape[1] - n_qkv
    cos, sa, sb = tables
    tbl = pl.BlockSpec((tm, cos.shape[1]), lambda i, j: (tbl_idx(i), 0))
    qkv = pl.BlockSpec((tm, tn), lambda i, j: (i, 0))
    qkv_shape = jax.ShapeDtypeStruct((t, tn), BF16)
    q, k, v, h = pl.pallas_call(
        _in_proj_kernel,
        grid=(t // tm, QKV_TILES),
        in_specs=[pl.BlockSpec((tm, d), lambda i, j: (i, 0)),
                  pl.BlockSpec((None, None, 1, d), lambda i, j: (0, seg(i), 0, 0)),
                  pl.BlockSpec((None, None, 1, d), lambda i, j: (1, seg(i), 0, 0)),
                  pl.BlockSpec((d, tn), lambda i, j: (0, j)),
                  pl.BlockSpec((1, tn), lambda i, j: (0, j)),
                  tbl, tbl, tbl],
        out_specs=[qkv, qkv, qkv, pl.BlockSpec((tm, d), lambda i, j: (i, 0))],
        out_shape=[qkv_shape, qkv_shape, qkv_shape, jax.ShapeDtypeStruct((t, d), BF16)],
        compiler_params=_cparams(("parallel", "arbitrary")),
        name="in_proj_qkv",
    )(x_all, modp, modp, w, b, cos, sa, sb)
    tr = REST_TILE if n_rest % REST_TILE == 0 else tn
    rest = pl.pallas_call(
        _rest_proj_kernel,
        grid=(t // tm, n_rest // tr),
        in_specs=[pl.BlockSpec((tm, d), lambda i, j: (i, 0)),
                  pl.BlockSpec((d, tr), lambda i, j: (0, j)),
                  pl.BlockSpec((1, tr), lambda i, j: (0, j))],
        out_specs=pl.BlockSpec((tm, tr), lambda i, j: (i, j)),
        out_shape=jax.ShapeDtypeStruct((t, n_rest), BF16),
        compiler_params=_cparams(("parallel", "parallel")),
        name="in_proj_rest",
    )(h, w[:, n_qkv:], b[:, n_qkv:])
    return q, k, v, rest


def _rope_tables(n, tm):
    rows = n // GRID_W
    pos_row = jnp.repeat(jnp.arange(rows), GRID_W).astype(F32)
    pos_col = jnp.tile(jnp.arange(GRID_W), rows).astype(F32)
    quarter = A_QK_DIM // 4
    inv = ROPE_THETA ** (-jnp.arange(quarter, dtype=F32) / quarter)
    ang_r = pos_row[:, None] * inv
    ang_c = pos_col[:, None] * inv
    ang = jnp.concatenate([ang_r, ang_r, ang_c, ang_c], axis=-1)
    cos = jnp.cos(ang)
    sin = jnp.sin(ang)
    first_half = (jnp.arange(A_QK_DIM) % (2 * quarter)) < quarter
    sa = jnp.where(first_half, -sin, 0.0)
    sb = jnp.where(first_half, 0.0, sin)

    def widen(tbl, fill):
        tbl = jnp.tile(tbl, (1, 2))
        return jnp.concatenate([tbl, jnp.full((tm, tbl.shape[1]), fill, F32)], axis=0)

    return widen(cos, 1.0), widen(sa, 0.0), widen(sb, 0.0)


def _attn_kernel(*refs, lam_init, with_lat, sub):
    if with_lat:
        lam_ref, gain_ref, q_ref, kc_ref, vc_ref, kl_ref, vl_ref, o_ref = refs
    else:
        lam_ref, gain_ref, q_ref, kc_ref, vc_ref, _, o_ref = refs
    lv = lam_ref[...]
    lam = (jnp.exp(jnp.sum(lv[0:1] * lv[1:2], axis=-1, keepdims=True))
           - jnp.exp(jnp.sum(lv[2:3] * lv[3:4], axis=-1, keepdims=True)) + lam_init)
    chunks = [(kc_ref, vc_ref, 0, kc_ref.shape[0])]
    if with_lat:
        ck = min(sub, kl_ref.shape[0])
        chunks += [(kl_ref, vl_ref, r0, ck) for r0 in range(0, kl_ref.shape[0], ck)]
    for hh in range(gain_ref.shape[0]):
        head = slice(hh * A_V_DIM, (hh + 1) * A_V_DIM)
        probs, dens = [], []
        for c in range(2):
            cols = slice(hh * A_V_DIM + c * A_QK_DIM, hh * A_V_DIM + (c + 1) * A_QK_DIM)
            qc = q_ref[:, cols]
            scores = [_dot_nt(k_ref[r0:r0 + n, cols], qc) for k_ref, _, r0, n in chunks]
            m = None
            for s in scores:
                ms = jnp.max(s, axis=0, keepdims=True)
                m = ms if m is None else jnp.maximum(m, ms)
            p = [jnp.exp2(s - m) for s in scores]
            den = None
            for pj in p:
                ds = jnp.sum(pj, axis=0, keepdims=True)
                den = ds if den is None else den + ds
            probs.append(p)
            dens.append(den)
        ratio = lam * dens[0] / dens[1]
        o = None
        for p1, p2, (_, v_ref, r0, n) in zip(probs[0], probs[1], chunks):
            w = p1 - p2 * ratio
            ps = _dot(v_ref[head, r0:r0 + n], w.astype(BF16))
            o = ps if o is None else o + ps
        o = o / dens[0]
        o = o * lax.rsqrt(jnp.mean(o * o, axis=0, keepdims=True) + RMS_EPS)
        o_ref[:, head] = ((o * gain_ref[hh]) * (1.0 - lam_init)).T


def _attention(qr, kr, vt, lam_vecs, gain, *, lam_init, batch, n_lat, n_ctx, latent_queries, out_rows, into=None):
    hw = ATTN_HEADS * A_V_DIM
    ctx_blk0 = batch * n_lat // n_ctx
    lam_spec = pl.BlockSpec(lam_vecs.shape, lambda b, h, i: (0, 0))
    gain_spec = pl.BlockSpec((ATTN_HEADS, A_V_DIM, 1), lambda b, h, i: (h, 0, 0))
    kv_ctx = pl.BlockSpec((n_ctx, hw), lambda b, h, i: (ctx_blk0 + b, h))
    vt_ctx = pl.BlockSpec((hw, n_ctx), lambda b, h, i: (h, ctx_blk0 + b))
    if latent_queries:
        tq = 256
        nq = n_lat // tq
        q_spec = pl.BlockSpec((tq, hw), lambda b, h, i: (b * nq + i, h))
        kv_lat = pl.BlockSpec((n_lat, hw), lambda b, h, i: (b, h))
        vt_lat = pl.BlockSpec((hw, n_lat), lambda b, h, i: (h, b))
        in_specs = [lam_spec, gain_spec, q_spec, kv_ctx, vt_ctx, kv_lat, vt_lat]
        args = (lam_vecs, gain, qr, kr, vt, kr, vt)
        out_blk0 = 0
        aliases = {}
    else:
        tq = n_ctx
        nq = 1
        q_spec = pl.BlockSpec((tq, hw), lambda b, h, i: (ctx_blk0 + b, h))
        in_specs = [lam_spec, gain_spec, q_spec, kv_ctx, vt_ctx, pl.BlockSpec(memory_space=pl.ANY)]
        args = (lam_vecs, gain, qr, kr, vt, into)
        out_blk0 = ctx_blk0
        aliases = {len(args) - 1: 0}
    return pl.pallas_call(
        functools.partial(_attn_kernel, lam_init=lam_init, with_lat=latent_queries, sub=ATTN_SUB),
        grid=(batch, A_HEADS // ATTN_HEADS, nq),
        in_specs=in_specs,
        out_specs=pl.BlockSpec((tq, hw), lambda b, h, i: (out_blk0 + b * nq + i, h)),
        out_shape=jax.ShapeDtypeStruct((out_rows, A_HEADS * A_V_DIM), F32),
        input_output_aliases=aliases,
        compiler_params=_cparams(("parallel", "parallel", "arbitrary")),
        name="diff_attn_lat" if latent_queries else "diff_attn_ctx",
    )(*args)


def _sg_kernel(u_ref, s_ref, g_ref, b_ref, w_ref, bs_ref, o_ref):
    gw = B_CHUNK
    for ci in range(TM // B_CHUNK):
        rows = slice(ci * B_CHUNK, (ci + 1) * B_CHUNK)
        for g in range(B_GROUPS):
            cols = slice(g * gw, (g + 1) * gw)
            vn = _ln(_gelu(s_ref[rows, cols].astype(F32))) * g_ref[:, cols] + b_ref[:, cols]
            mixed = _dot(w_ref[g], vn.astype(BF16)) + bs_ref[g]
            o_ref[rows, cols] = _gelu(u_ref[rows, cols].astype(F32)) * mixed


def _spatial_gating(proj, ln_g, ln_b, w_s, b_s):
    t = proj.shape[0]
    w = B_GROUPS * B_CHUNK
    full = lambda a: pl.BlockSpec(a.shape, lambda i: (0,) * a.ndim)
    return pl.pallas_call(
        _sg_kernel,
        grid=(t // TM,),
        in_specs=[pl.BlockSpec((TM, w), lambda i: (i, COL_U)),
                  pl.BlockSpec((TM, w), lambda i: (i, COL_S)),
                  full(ln_g), full(ln_b), full(w_s), full(b_s)],
        out_specs=pl.BlockSpec((TM, w), lambda i: (i, 0)),
        out_shape=jax.ShapeDtypeStruct((t, w), F32),
        compiler_params=_cparams(("parallel",)),
        name="spatial_gating",
    )(proj, proj, ln_g, ln_b, w_s, b_s)


def _lru_kernel(*refs, reverse, final, n_tiles, tl, aliased):
    if aliased:
        refs = refs[:12] + refs[13:]
    if final:
        (x_ref, xp_ref, xn_ref, cw_ref, cb_ref, w_ref, ba_ref, bx_ref, lam_ref, h0_ref,
         hf_ref, y_ref, o_ref, hl_ref, carry) = refs
    else:
        (x_ref, xp_ref, xn_ref, cw_ref, cb_ref, w_ref, ba_ref, bx_ref, lam_ref, h0_ref,
         o_ref, hl_ref, carry) = refs
    step = pl.program_id(1)

    @pl.when(step == 0)
    def _():
        carry[...] = h0_ref[...]

    ti = (n_tiles - 1 - step) if reverse else step
    x = x_ref[...].astype(F32)
    cw = x.shape[-1]
    row = lax.broadcasted_iota(I32, x.shape, 0)
    has_prev = ti > 0
    has_next = ti < n_tiles - 1
    p6 = jnp.where(has_prev, xp_ref[HALO - 2:HALO - 1, :].astype(F32), 0.0)
    p7 = jnp.where(has_prev, xp_ref[HALO - 1:HALO, :].astype(F32), 0.0)
    n0 = jnp.where(has_next, xn_ref[0:1, :].astype(F32), 0.0)
    xm1 = jnp.where(row == 0, p7, pltpu.roll(x, 1, 0))
    xm2 = jnp.where(row == 0, p6, jnp.where(row == 1, p7, pltpu.roll(x, 2, 0)))
    xp1 = jnp.where(row == tl - 1, n0, pltpu.roll(x, tl - 1, 0))
    taps = cw_ref[...]
    xs = taps[0:1] * xm2 + taps[1:2] * xm1 + taps[2:3] * x + taps[3:4] * xp1 + cb_ref[...]

    z = _dot(xs.astype(BF16), w_ref[...])
    r = _sigmoid(z[:, :cw] + ba_ref[...])
    ig = _sigmoid(z[:, cw:] + bx_ref[...])
    nl = -lam_ref[...]
    softplus = jnp.maximum(nl, 0.0) + jnp.log(1.0 + jnp.exp(-jnp.abs(nl)))
    log_a = -C_POW * r * softplus
    a = jnp.exp(log_a)
    u = jnp.sqrt(1.0 - jnp.exp(2.0 * log_a)) * (ig * xs)

    sb = min(LRU_BLOCK, tl)
    in_blk = row % sb
    d = 1
    while d < sb:
        if reverse:
            ok = in_blk < sb - d
            shift = tl - d
        else:
            ok = in_blk >= d
            shift = d
        a_sh = jnp.where(ok, pltpu.roll(a, shift, 0), 1.0)
        u_sh = jnp.where(ok, pltpu.roll(u, shift, 0), 0.0)
        u = u + a * u_sh
        a = a * a_sh
        d *= 2
    state = carry[0:1, :]
    blocks = [None] * (tl // sb)
    for b in (reversed(range(tl // sb)) if reverse else range(tl // sb)):
        hb = u[b * sb:(b + 1) * sb] + a[b * sb:(b + 1) * sb] * state
        state = hb[0:1, :] if reverse else hb[sb - 1:sb, :]
        blocks[b] = hb
    h = jnp.concatenate(blocks, axis=0)
    edge = state
    carry[...] = jnp.broadcast_to(edge, carry.shape)
    hl_ref[...] = jnp.broadcast_to(edge, hl_ref.shape)
    if final:
        o_ref[...] = _gelu(y_ref[...].astype(F32)) * (hf_ref[...] + h)
    else:
        o_ref[...] = h


def _lru_pass(proj, hf, h0, conv_w, conv_b, w_dense, ba, bx, lam, *, batch, seq, tl, row0, reverse, final,
              out_rows=None, out_row0=0, into=None):
    t_all = proj.shape[0]
    out_rows = batch * seq if out_rows is None else out_rows
    out_blk0 = out_row0 // tl
    assert into is None or final
    cw = conv_w.shape[1]
    n_tiles = seq // tl
    base = row0 // tl
    sub = tl // HALO
    last_halo = t_all // HALO - 1

    def tile(b, s):
        ti = (n_tiles - 1 - s) if reverse else s
        return base + b * n_tiles + ti

    def loc(b, s):
        ti = (n_tiles - 1 - s) if reverse else s
        return b * n_tiles + ti

    full = lambda a: pl.BlockSpec(a.shape, lambda b, s: (0,) * a.ndim)
    in_specs = [pl.BlockSpec((tl, cw), lambda b, s: (tile(b, s), COL_X)),
                pl.BlockSpec((HALO, cw), lambda b, s: (jnp.maximum(tile(b, s) * sub - 1, 0), COL_X)),
                pl.BlockSpec((HALO, cw), lambda b, s: (jnp.minimum((tile(b, s) + 1) * sub, last_halo), COL_X)),
                full(conv_w), full(conv_b), full(w_dense), full(ba), full(bx), full(lam),
                pl.BlockSpec((None, 8, cw), lambda b, s: (b, 0, 0))]
    args = [proj, proj, proj, conv_w, conv_b, w_dense, ba, bx, lam, h0]
    if final:
        in_specs += [pl.BlockSpec((tl, cw), lambda b, s: (loc(b, s), 0)),
                     pl.BlockSpec((tl, cw), lambda b, s: (tile(b, s), COL_Y))]
        args += [hf, proj]
    aliases = {}
    if into is not None:
        aliases = {len(args): 0}
        in_specs += [pl.BlockSpec(memory_space=pl.ANY)]
        args += [into]
    return pl.pallas_call(
        functools.partial(_lru_kernel, reverse=reverse, final=final, n_tiles=n_tiles, tl=tl,
                          aliased=into is not None),
        grid=(batch, n_tiles),
        in_specs=in_specs,
        out_specs=[pl.BlockSpec((tl, cw), lambda b, s: (out_blk0 + loc(b, s), 0)),
                   pl.BlockSpec((None, 8, cw), lambda b, s: (b, 0, 0))],
        out_shape=[jax.ShapeDtypeStruct((out_rows, cw), F32),
                   jax.ShapeDtypeStruct((batch, 8, cw), F32)],
        scratch_shapes=[pltpu.VMEM((8, cw), F32)],
        input_output_aliases=aliases,
        compiler_params=_cparams(("parallel", "arbitrary")),
        name="rglru_%s_%s" % ("bwd" if reverse else "fwd", "lat" if row0 == 0 else "ctx"),
    )(*args)


def _merge_kernel(x_ref, oa_ref, ob_ref, oc_ref, g0, g1, g2, g3, g4, g5, wb_ref, wo_ref,
                  gate1_ref, sh2_ref, sc2_ref, lng_ref, lnb_ref, wrh_ref, wrl_ref,
                  xn_ref, h2_ref, sc_ref, *, alpha):
    gates = ((g0, g1), (g2, g3), (g4, g5))
    mix = None
    for r, o_ref in enumerate((oa_ref, ob_ref, oc_ref)):
        proj = _dot(o_ref[...].astype(BF16), wb_ref[r])
        gate = jnp.concatenate([_sigmoid(gates[r][0][...].astype(F32)), _sigmoid(gates[r][1][...].astype(F32))],
                               axis=-1)
        term = gate * proj
        mix = term if mix is None else mix + term
    out = _dot(mix.astype(BF16), wo_ref[...])
    xn = _ln(alpha * x_ref[...] + gate1_ref[...] * out) * lng_ref[...] + lnb_ref[...]
    xn_ref[...] = xn
    h2 = _ln(xn) * (1.0 + sc2_ref[...]) + sh2_ref[...]
    h_hi = h2.astype(BF16)
    h2_ref[...] = h_hi
    h_lo = (h2 - h_hi.astype(F32)).astype(BF16)
    logits = _dot_nt(wrh_ref[...], h_hi) + (_dot_nt(wrh_ref[...], h_lo) + _dot_nt(wrl_ref[...], h_hi))
    sc_ref[...] = _sigmoid(logits)


def _merge(x_all, o_a, o_b, o_c, proj, modp, w_branch, w_out, ln_g, ln_b, wr_hi, wr_lo, seg, *, n_rows, alpha):
    d = x_all.shape[1]
    bw = o_a.shape[1]
    n_exp = wr_hi.shape[0]
    tok = lambda w: pl.BlockSpec((TM, w), lambda i: (i, 0))
    gate = lambda c: pl.BlockSpec((TM, bw), lambda i, c=c: (i, c))
    mod = lambda k: pl.BlockSpec((None, None, 1, d), lambda i, k=k: (k, seg(i), 0, 0))
    full = lambda a: pl.BlockSpec(a.shape, lambda i: (0,) * a.ndim)
    return pl.pallas_call(
        functools.partial(_merge_kernel, alpha=alpha),
        grid=(n_rows // TM,),
        in_specs=[tok(d), tok(bw), tok(bw), tok(bw)] + [gate(COL_G + c) for c in range(6)]
                 + [full(w_branch), full(w_out), mod(2), mod(3), mod(4), full(ln_g), full(ln_b),
                    full(wr_hi), full(wr_lo)],
        out_specs=[tok(d), tok(d), pl.BlockSpec((n_exp, TM), lambda i: (0, i))],
        out_shape=[jax.ShapeDtypeStruct((n_rows, d), F32),
                   jax.ShapeDtypeStruct((n_rows, d), BF16),
                   jax.ShapeDtypeStruct((n_exp, n_rows), F32)],
        compiler_params=_cparams(("parallel",)),
        name="merge_residual_router",
    )(x_all, o_a, o_b, o_c, proj, proj, proj, proj, proj, proj, w_branch, w_out,
      modp, modp, modp, ln_g, ln_b, wr_hi, wr_lo)


def _tile_rows(n_exp):
    return TOP_K * TD + n_exp * UNIT


def _route_kernel(s_ref, bias_ref, tri_ref, ltri_ref, pos_ref, w_ref, rs_ref, rn_ref, unit_ref, tot_ref, carry):
    @pl.when(pl.program_id(0) == 0)
    def _():
        carry[...] = jnp.zeros_like(carry)

    s = s_ref[...]
    n_exp, tn = s.shape
    per = n_exp // N_GROUPS
    neg = -jnp.inf
    biased = s + bias_ref[...]
    sub = lax.broadcasted_iota(I32, (per, tn), 0)
    gs_rows = []
    for g in range(N_GROUPS):
        blk = biased[g * per:(g + 1) * per, :]
        m1 = jnp.max(blk, axis=0, keepdims=True)
        first = jnp.min(jnp.where(blk == m1, sub, per), axis=0, keepdims=True)
        m2 = jnp.max(jnp.where(sub == first, neg, blk), axis=0, keepdims=True)
        gs_rows.append(m1 + m2)
    gs = jnp.concatenate(gs_rows, axis=0)
    gi = lax.broadcasted_iota(I32, gs.shape, 0)
    g_ok = jnp.zeros(gs.shape, F32)
    cur = gs
    for _ in range(TOPK_GROUPS):
        m = jnp.max(cur, axis=0, keepdims=True)
        pick = jnp.min(jnp.where(cur == m, gi, N_GROUPS), axis=0, keepdims=True)
        hit = gi == pick
        g_ok = jnp.where(hit, 1.0, g_ok)
        cur = jnp.where(hit, neg, cur)
    ok_rows = [jnp.broadcast_to(g_ok[g:g + 1, :], (per, tn)) for g in range(N_GROUPS)]
    expert_ok = jnp.concatenate(ok_rows, axis=0)
    masked = jnp.where(expert_ok > 0.0, biased, neg)
    ei = lax.broadcasted_iota(I32, s.shape, 0)
    pick_s, hits = [], []
    sel = jnp.zeros(s.shape, F32)
    for _ in range(TOP_K):
        m = jnp.max(masked, axis=0, keepdims=True)
        pick = jnp.min(jnp.where(masked == m, ei, n_exp), axis=0, keepdims=True)
        hit = ei == pick
        hits.append(hit)
        pick_s.append(jnp.sum(jnp.where(hit, s, 0.0), axis=0, keepdims=True))
        sel = jnp.where(hit, 1.0, sel)
        masked = jnp.where(hit, neg, masked)
    tot = pick_s[0]
    for k in range(1, TOP_K):
        tot = tot + pick_s[k]
    w = jnp.where(sel > 0.0, s, 0.0) / tot * ROUTED_SCALE
    w_hi = w.astype(BF16)
    w_ref[...] = jnp.concatenate([w_hi, (w - w_hi.astype(F32)).astype(BF16)], axis=0)

    incl = _dot(sel.astype(BF16), tri_ref[...])
    count = incl[:, tn - 1:tn]
    run = jnp.floor((count + (UNIT - 1)) * (1.0 / UNIT)) * UNIT
    run_start = _dot(ltri_ref[...], jnp.broadcast_to(run, (n_exp, 128)).astype(BF16))[:, 0:1]
    row_in_tile = incl - sel + run_start
    pos = [jnp.sum(jnp.where(hit, row_in_tile, 0.0), axis=0, keepdims=True) for hit in hits]
    pos_ref[...] = jnp.concatenate(pos, axis=0).astype(I32)
    rs_ref[...] = jnp.broadcast_to(run_start, rs_ref.shape)
    rn_ref[...] = jnp.broadcast_to(run, rn_ref.shape)

    half = UNIT_LANES // 2
    region_used = carry[:, 0:1]
    u0 = run_start * (1.0 / UNIT)
    nu = run * (1.0 / UNIT)
    n_pair = jnp.floor(nu * 0.5)
    n_single = nu - 2.0 * n_pair
    both = jnp.concatenate([jnp.broadcast_to(n_pair, (n_exp, 64)), jnp.broadcast_to(n_single, (n_exp, 64))], axis=1)
    first = _dot(ltri_ref[...], both.astype(BF16))
    pair0, single0 = first[:, 0:1], first[:, 64:65]
    lane = lax.broadcasted_iota(I32, (n_exp, UNIT_LANES), 1)
    is_pair = lane < half
    item = jnp.where(is_pair, lane, lane - half).astype(F32)
    start = jnp.where(is_pair, pair0, single0)
    count = jnp.where(is_pair, n_pair, n_single)
    inside = jnp.logical_and(item >= start, item < start + count)
    off = jnp.where(is_pair, 2.0 * (item - pair0), 2.0 * n_pair)
    eu = lax.broadcasted_iota(I32, (n_exp, UNIT_LANES), 0).astype(F32)
    pick = lambda v: jnp.sum(jnp.where(inside, v, 0.0), axis=0, keepdims=True)
    totals = jnp.where(is_pair[0:1], jnp.sum(n_pair, axis=0, keepdims=True), jnp.sum(n_single, axis=0, keepdims=True))
    pad = jnp.zeros((4, UNIT_LANES), F32)
    unit_ref[...] = jnp.concatenate([pick(off + region_used * (1.0 / UNIT)), pick(eu), pick(off + u0), totals, pad],
                                    axis=0).astype(I32)

    total = region_used + run
    carry[...] = jnp.broadcast_to(total, carry.shape)
    tot_ref[...] = jnp.broadcast_to(total, tot_ref.shape)


def _route(scores_t, bias_col, tri, ltri):
    n_exp, t = scores_t.shape
    nt = t // TD
    return pl.pallas_call(
        _route_kernel,
        grid=(nt,),
        in_specs=[pl.BlockSpec((n_exp, TD), lambda i: (0, i)),
                  pl.BlockSpec((n_exp, 1), lambda i: (0, 0)),
                  pl.BlockSpec((TD, TD), lambda i: (0, 0)),
                  pl.BlockSpec((n_exp, n_exp), lambda i: (0, 0))],
        out_specs=[pl.BlockSpec((TOP_K, TD), lambda i: (0, i)),
                   pl.BlockSpec((2 * n_exp, TD), lambda i: (0, i)),
                   pl.BlockSpec((n_exp, 128), lambda i: (0, i)),
                   pl.BlockSpec((n_exp, 128), lambda i: (0, i)),
                   pl.BlockSpec((8, UNIT_LANES), lambda i: (0, i)),
                   pl.BlockSpec((n_exp, 128), lambda i: (0, 0))],
        out_shape=[jax.ShapeDtypeStruct((TOP_K, t), I32),
                   jax.ShapeDtypeStruct((2 * n_exp, t), BF16),
                   jax.ShapeDtypeStruct((n_exp, nt * 128), F32),
                   jax.ShapeDtypeStruct((n_exp, nt * 128), F32),
                   jax.ShapeDtypeStruct((8, nt * UNIT_LANES), I32),
                   jax.ShapeDtypeStruct((n_exp, 128), F32)],
        scratch_shapes=[pltpu.VMEM((n_exp, 128), F32)],
        compiler_params=_cparams(("arbitrary",)),
        name="route_topk",
    )(scores_t, bias_col, tri, ltri)


def _unit_dest_kernel(start_ref, unit_ref, o_ref, *, n_exp):
    tbl = unit_ref[...]
    exp_of = tbl[1:2, :]
    dst = tbl[0:1, :]
    for e in range(n_exp):
        dst = dst + jnp.where(exp_of == e, start_ref[e], 0)
    o_ref[...] = jnp.concatenate([dst, tbl[2:4, :], jnp.zeros((5, tbl.shape[1]), I32)], axis=0)


def _unit_dest(region_start_units, units):
    nt = units.shape[1] // UNIT_LANES
    per_step = next(k for k in (4, 2, 1) if nt % k == 0)
    blk = lambda: pl.BlockSpec((8, per_step * UNIT_LANES), lambda i, s: (0, i))
    return pl.pallas_call(
        functools.partial(_unit_dest_kernel, n_exp=region_start_units.shape[0]),
        grid_spec=pltpu.PrefetchScalarGridSpec(
            num_scalar_prefetch=1, grid=(nt // per_step,), in_specs=[blk()], out_specs=blk()),
        out_shape=jax.ShapeDtypeStruct(units.shape, I32),
        compiler_params=_cparams(("parallel",)),
        name="route_units",
    )(region_start_units, units)


def _unit_rows(i, n_units=1):
    return pl.ds(pl.multiple_of(i * UNIT, UNIT), n_units * UNIT)


def _copy_items(tbl, make_copy, act):
    half = UNIT_LANES // 2
    _for_each(tbl[2, 0], lambda i: getattr(make_copy(tbl, i, 2), act)())
    _for_each(tbl[2, half], lambda i: getattr(make_copy(tbl, half + i, 1), act)())


def _drain_items(n_pairs, n_singles, make_copy, tbl):
    _for_each(n_pairs, lambda i: make_copy(tbl, 0, 2).wait())
    _for_each(n_singles, lambda i: make_copy(tbl, UNIT_LANES // 2, 1).wait())


def _for_each(count, fn, group=8):
    main = lax.shift_right_logical(count, int(math.log2(group)))

    def many(i, c):
        for r in range(group):
            fn(i * group + r)
        return c

    lax.fori_loop(0, main, many, 0)
    lax.fori_loop(main * group, count, lambda i, c: (fn(i), c)[1], 0)


def _dispatch_kernel(tail_lo_ref, tail_hi_ref, np_ref, ns_ref, unit_ref, pos_ref, w_ref, rs_ref, rn_ref, h_ref,
                     xs_ref, s_scr, zero_scr, sem, zsem, *, n_exp):
    rows = s_scr.shape[1]
    td = h_ref.shape[0]
    step = pl.program_id(0)
    last = pl.num_programs(0) - 1
    slot = step % 2

    def zero_copy(g):
        return pltpu.make_async_copy(zero_scr, xs_ref.at[_unit_rows(g), :], zsem.at[0])

    def item_copy(sl):
        def make(tbl, lane, size):
            return pltpu.make_async_copy(s_scr.at[sl, _unit_rows(tbl[1, lane], size), :],
                                         xs_ref.at[_unit_rows(tbl[0, lane], size), :], sem.at[size - 1, sl])
        return make

    def drain(tile, sl):
        _drain_items(np_ref[tile], ns_ref[tile], item_copy(sl), unit_ref)

    @pl.when(step >= 2)
    def _():
        drain(step - 2, slot)

    @pl.when(step == 0)
    def _():
        zero_scr[...] = jnp.zeros_like(zero_scr)

        def per_expert(e, c):
            lax.fori_loop(tail_lo_ref[e], tail_hi_ref[e], lambda g, cc: (zero_copy(g).start(), cc)[1], 0)
            lax.fori_loop(tail_lo_ref[e], tail_hi_ref[e], lambda g, cc: (zero_copy(g).wait(), cc)[1], 0)
            return c

        lax.fori_loop(0, n_exp, per_expert, 0)

    pos = pos_ref[...]
    h = h_ref[...]
    d = h.shape[1]
    w_dense = w_ref[...]
    run_lo = rs_ref[...]
    run_hi = run_lo + rn_ref[...]
    lane = lax.broadcasted_iota(I32, (SORT_CHUNK, s_scr.shape[2] - d), 1)
    for r0 in range(0, rows, SORT_CHUNK):
        srow = (lax.broadcasted_iota(I32, (SORT_CHUNK, td), 0) + r0).astype(jnp.int16)
        onehot = jnp.zeros((SORT_CHUNK, td), BF16)
        for k in range(pos.shape[0]):
            onehot = jnp.where(srow == pos[k:k + 1, :].astype(jnp.int16), jnp.ones((), BF16), onehot)
        s_e = (lax.broadcasted_iota(I32, (SORT_CHUNK, w_dense.shape[0]), 0) + r0).astype(F32)
        mine = jnp.logical_and(s_e >= run_lo, s_e < run_hi).astype(BF16)
        w_row = jnp.sum(onehot.astype(F32) * _dot(mine, w_dense), axis=1, keepdims=True)
        w_hi = w_row.astype(BF16).astype(F32)
        w_lo = w_row - w_hi
        s_scr[slot, r0:r0 + SORT_CHUNK, :d] = _dot(onehot, h).astype(BF16)
        s_scr[slot, r0:r0 + SORT_CHUNK, d:] = jnp.where(lane == 0, w_hi, jnp.where(lane == 1, w_lo, 0.0)).astype(BF16)

    _copy_items(unit_ref, item_copy(slot), "start")

    @pl.when(step == last)
    def _():
        drain(step, slot)

        @pl.when(step >= 1)
        def _():
            drain(step - 1, 1 - slot)


def _dispatch(tail_lo, tail_hi, n_pairs, n_singles, unit_dst, pos, w_dense, run_start_rows, run_rows, h2, n_slots):
    t, d = h2.shape
    n_exp = tail_lo.shape[0]
    dw = d + W_LANES
    runs = pl.BlockSpec((None, 1, 2 * n_exp), lambda i, a, b, c, e: (i, 0, 0))
    return pl.pallas_call(
        functools.partial(_dispatch_kernel, n_exp=n_exp),
        grid_spec=pltpu.PrefetchScalarGridSpec(
            num_scalar_prefetch=4, grid=(t // TD,),
            in_specs=[pl.BlockSpec((8, UNIT_LANES), lambda i, a, b, c, e: (0, i), memory_space=pltpu.SMEM),
                      pl.BlockSpec((TOP_K, TD), lambda i, a, b, c, e: (0, i)),
                      pl.BlockSpec((2 * n_exp, TD), lambda i, a, b, c, e: (0, i)),
                      runs, runs,
                      pl.BlockSpec((TD, d), lambda i, a, b, c, e: (i, 0))],
            out_specs=pl.BlockSpec(memory_space=pl.ANY),
            scratch_shapes=[pltpu.VMEM((2, _tile_rows(n_exp), dw), BF16), pltpu.VMEM((UNIT, dw), BF16),
                            pltpu.SemaphoreType.DMA((2, 2)), pltpu.SemaphoreType.DMA((1,))]),
        out_shape=jax.ShapeDtypeStruct((n_slots, dw), BF16),
        compiler_params=_cparams(("arbitrary",)),
        name="moe_dispatch",
    )(tail_lo, tail_hi, n_pairs, n_singles, unit_dst, pos, w_dense, run_start_rows, run_rows, h2)


def _expert_kernel(blk0_ref, nblk_ref, tot_ref, wg_ref, wu_ref, wd_ref, xs_ref, ys_ref, xbuf, obuf, wgu_scr, wd_scr,
                   sem_in, sem_out):
    e = pl.program_id(0)
    n = nblk_ref[e]
    b0 = blk0_ref[e]
    de = wg_ref.shape[1]
    wgu_scr[:, :de] = wg_ref[...].astype(BF16)
    wgu_scr[:, de:] = wu_ref[...].astype(BF16)
    wd_scr[...] = wd_ref[...].astype(BF16)

    total = tot_ref[0]

    def rows(g):
        return pl.ds(pl.multiple_of(g * MOE_BLK, MOE_BLK), MOE_BLK)

    def in_copy(g):
        sl = g % EXPERT_BUFS
        return pltpu.make_async_copy(xs_ref.at[rows(g), :], xbuf.at[sl], sem_in.at[sl])

    def out_copy(g):
        sl = g % EXPERT_BUFS
        return pltpu.make_async_copy(obuf.at[sl], ys_ref.at[rows(g), :], sem_out.at[sl])

    @pl.when(e == 0)
    def _():
        for g in range(EXPERT_BUFS - 1):
            @pl.when(g < total)
            def _():
                in_copy(g).start()

    def block(j, c):
        g = b0 + j
        sl = g % EXPERT_BUFS
        in_copy(g).wait()

        @pl.when(g + (EXPERT_BUFS - 1) < total)
        def _():
            in_copy(g + (EXPERT_BUFS - 1)).start()

        @pl.when(g >= EXPERT_BUFS)
        def _():
            out_copy(g - EXPERT_BUFS).wait()

        d = wd_scr.shape[1]
        gu = _dot(xbuf[sl, :, :d], wgu_scr[...])
        act = _silu(gu[:, :de]) * gu[:, de:]
        w_row = xbuf[sl, :, d:d + 1].astype(F32) + xbuf[sl, :, d + 1:d + 2].astype(F32)
        obuf[sl] = (_dot(act.astype(BF16), wd_scr[...]) * w_row).astype(BF16)
        out_copy(g).start()
        return c

    lax.fori_loop(0, n, block, 0)

    @pl.when(e == pl.num_programs(0) - 1)
    def _():
        for back in range(1, EXPERT_BUFS + 1):
            @pl.when(total >= back)
            def _():
                out_copy(total - back).wait()


def _experts(blk0, nblk, n_blocks, xs, w_gate, w_up, w_down, layer):
    n_slots, dw = xs.shape
    _, n_exp, d, de = w_gate.shape
    return pl.pallas_call(
        _expert_kernel,
        grid_spec=pltpu.PrefetchScalarGridSpec(
            num_scalar_prefetch=3, grid=(n_exp,),
            in_specs=[pl.BlockSpec((None, None, d, de), lambda e, a, b, c: (layer, e, 0, 0)),
                      pl.BlockSpec((None, None, d, de), lambda e, a, b, c: (layer, e, 0, 0)),
                      pl.BlockSpec((None, None, de, d), lambda e, a, b, c: (layer, e, 0, 0)),
                      pl.BlockSpec(memory_space=pl.ANY)],
            out_specs=pl.BlockSpec(memory_space=pl.ANY),
            scratch_shapes=[pltpu.VMEM((EXPERT_BUFS, MOE_BLK, dw), BF16), pltpu.VMEM((EXPERT_BUFS, MOE_BLK, d), BF16),
                            pltpu.VMEM((d, 2 * de), BF16), pltpu.VMEM((de, d), BF16),
                            pltpu.SemaphoreType.DMA((EXPERT_BUFS,)), pltpu.SemaphoreType.DMA((EXPERT_BUFS,))]),
        out_shape=jax.ShapeDtypeStruct((n_slots, d), BF16),
        compiler_params=_cparams(("arbitrary",)),
        name="moe_experts",
    )(blk0, nblk, n_blocks, w_gate, w_up, w_down, xs)


def _combine_kernel(unit_ref, next_ref, pos_ref, x_ref, h_ref, gate2_ref,
                    sg_ref, su_ref, sd_ref, lng_ref, lnb_ref, ys_ref, o_ref, s_scr, sem, *, alpha):
    rows = s_scr.shape[1]
    td = x_ref.shape[0]
    step = pl.program_id(0)
    slot = step % 2

    def item_copy(sl):
        def make(tbl, lane, size):
            return pltpu.make_async_copy(ys_ref.at[_unit_rows(tbl[0, lane], size), :],
                                         s_scr.at[sl, _unit_rows(tbl[1, lane], size), :], sem.at[size - 1, sl])
        return make

    def fetch(tbl, sl):
        _copy_items(tbl, item_copy(sl), "start")

    @pl.when(step == 0)
    def _():
        s_scr[...] = jnp.zeros_like(s_scr)
        fetch(unit_ref, 0)

    @pl.when(step + 1 < pl.num_programs(0))
    def _():
        fetch(next_ref, 1 - slot)

    h = h_ref[...]
    acc = _dot((_silu(_dot(h, sg_ref[...])) * _dot(h, su_ref[...])).astype(BF16), sd_ref[...])
    pos = pos_ref[...]
    _drain_items(unit_ref[2, 0], unit_ref[2, UNIT_LANES // 2], item_copy(slot), unit_ref)
    for r0 in range(0, rows, SORT_CHUNK):
        lane = (lax.broadcasted_iota(I32, (td, SORT_CHUNK), 1) + r0).astype(jnp.int16)
        onehot = jnp.zeros((td, SORT_CHUNK), BF16)
        for k in range(pos.shape[1]):
            onehot = jnp.where(lane == pos[:, k:k + 1].astype(jnp.int16), jnp.ones((), BF16), onehot)
        acc = acc + _dot(onehot, s_scr[slot, r0:r0 + SORT_CHUNK, :])
    o_ref[...] = _ln(alpha * x_ref[...] + gate2_ref[...] * acc) * lng_ref[...] + lnb_ref[...]


def _combine(unit_dst, pos_tok, xn, h2, modp, sh_gate, sh_up, sh_down, ln_g, ln_b, ys, seg_td, *, alpha, n_exp):
    t, d = xn.shape
    nt = t // TD
    tok = pl.BlockSpec((TD, d), lambda i: (i, 0))
    full = lambda a: pl.BlockSpec(a.shape, lambda i: (0,) * a.ndim)
    return pl.pallas_call(
        functools.partial(_combine_kernel, alpha=alpha),
        grid=(nt,),
        in_specs=[pl.BlockSpec((8, UNIT_LANES), lambda i: (0, i), memory_space=pltpu.SMEM),
                  pl.BlockSpec((8, UNIT_LANES), lambda i: (0, jnp.minimum(i + 1, nt - 1)), memory_space=pltpu.SMEM),
                  pl.BlockSpec((TD, pos_tok.shape[1]), lambda i: (i, 0)),
                  tok, tok,
                  pl.BlockSpec((None, None, 1, d), lambda i: (5, seg_td(i), 0, 0)),
                  full(sh_gate), full(sh_up), full(sh_down), full(ln_g), full(ln_b),
                  pl.BlockSpec(memory_space=pl.ANY)],
        out_specs=tok,
        out_shape=jax.ShapeDtypeStruct((t, d), F32),
        scratch_shapes=[pltpu.VMEM((2, _tile_rows(n_exp), d), BF16), pltpu.SemaphoreType.DMA((2, 2))],
        compiler_params=_cparams(("arbitrary",)),
        name="moe_combine",
    )(unit_dst, unit_dst, pos_tok, xn, h2, modp, sh_gate, sh_up, sh_down, ln_g, ln_b, ys)


def _moe(xn, h2, scores_t, modp, seg_td, router_bias, w_gate, w_up, w_down, sh_gate, sh_up, sh_down,
         ln_g, ln_b, tri, ltri, *, alpha, layer):
    t = xn.shape[0]
    n_exp = scores_t.shape[0]
    assert _tile_rows(n_exp) % SORT_CHUNK == 0 and _tile_rows(n_exp) <= UNIT * UNIT_LANES
    pos, w_sel, run_start, run_len, units, tot = _route(scores_t, router_bias.reshape(n_exp, 1), tri, ltri)
    as_rows = lambda a: jnp.tile(a[:, ::128].T, (1, 2))[:, None, :]
    used = tot[:, 0].astype(I32)
    region = (used + MOE_BLK - 1) // MOE_BLK * MOE_BLK
    region_end = jnp.cumsum(region)
    region_start = region_end - region
    n_slots = TOP_K * t + n_exp * UNIT * (t // TD) + n_exp * MOE_BLK
    unit_dst = _unit_dest((region_start // UNIT).astype(I32), units)
    xs = _dispatch(((region_start + used) // UNIT).astype(I32), (region_end // UNIT).astype(I32),
                   unit_dst[2, ::UNIT_LANES], unit_dst[2, UNIT_LANES // 2::UNIT_LANES], unit_dst, pos, w_sel,
                   as_rows(run_start), as_rows(run_len), h2,
                   n_slots)
    ys = _experts((region_start // MOE_BLK).astype(I32), (region // MOE_BLK).astype(I32),
                  (region_end[-1:] // MOE_BLK).astype(I32), xs, w_gate, w_up, w_down, layer)
    return _combine(unit_dst, pos.T, xn, h2, modp, sh_gate, sh_up, sh_down, ln_g, ln_b, ys, seg_td,
                    alpha=alpha, n_exp=n_exp)


def kernel(x, c, ctx, c_ctx, w_mod, b_mod, w_in, b_in, lam_q1, lam_k1, lam_q2, lam_k2, attn_norm_g, sg_ln_g, sg_ln_b, sg_w, sg_b, conv_w, conv_b, lru_wa, lru_ba, lru_wx, lru_bx, lru_lam, w_branch, w_out, ln1_g, ln1_b, w_router, router_bias, moe_w_gate, moe_w_up, moe_w_down, sh_w_gate, sh_w_up, sh_w_down, ln2_g, ln2_b):
    batch, n_lat, d = x.shape
    n_ctx = ctx.shape[1]
    depth = w_mod.shape[0]
    n_exp = w_router.shape[2]
    t_lat = batch * n_lat
    t_ctx = batch * n_ctx
    t_all = t_lat + t_ctx
    assert n_lat % TM == 0 and t_ctx % TM == 0 and n_ctx % B_CHUNK == 0 and t_lat % n_ctx == 0
    assert batch + 1 <= MOD_ROWS and TM % TD == 0
    alpha = (2 * depth) ** 0.25
    cw = conv_w.shape[2]
    tiles_per_batch = n_lat // TM

    seg = lambda i: jnp.minimum(i // tiles_per_batch, batch)
    seg_td = lambda i: jnp.minimum(i // (n_lat // TD), batch)
    tm_in = 2 * TM if (n_lat % (2 * TM) == 0 and t_ctx % (2 * TM) == 0) else TM
    seg_in = lambda i: jnp.minimum(i // (n_lat // tm_in), batch)
    tbl_idx = lambda i: jnp.where(i < t_lat // tm_in, i % (n_lat // tm_in), n_lat // tm_in)

    x_all = jnp.concatenate([x.reshape(t_lat, d), ctx.reshape(t_ctx, d)], axis=0)
    c_all = jnp.zeros((MOD_ROWS, d), F32).at[:batch].set(c).at[batch].set(c_ctx)
    tables = _rope_tables(n_lat, tm_in)
    tri = (jnp.arange(TD)[:, None] <= jnp.arange(TD)[None, :]).astype(BF16)
    ltri = (jnp.arange(n_exp)[None, :] < jnp.arange(n_exp)[:, None]).astype(BF16)
    row = lambda v: v.reshape(1, -1)

    def dense_blocks(w):
        nb, bi, bj = w.shape
        eye = jnp.eye(nb, dtype=w.dtype)
        return (w[:, :, None, :] * eye[:, None, :, None]).reshape(nb * bi, nb * bj)

    for l in range(depth):
        last = l == depth - 1
        lam_init = 0.8 - 0.6 * math.exp(-0.3 * l)
        mod = _mod_rows(c_all, w_mod[l].astype(BF16), row(b_mod[l]))
        modp = mod.reshape(MOD_ROWS, 6, 1, d).transpose(1, 0, 2, 3)

        qr, kr, vb, proj = _in_proj(x_all, modp, w_in[l].astype(BF16), row(b_in[l]), tables, seg_in, tbl_idx, tm_in)

        lam_vecs = jnp.stack([lam_q1[l], lam_k1[l], lam_q2[l], lam_k2[l]])
        gain = attn_norm_g[l].reshape(A_HEADS, A_V_DIM, 1)
        attn = functools.partial(_attention, qr, kr, vb.T, lam_vecs, gain, lam_init=lam_init,
                                 batch=batch, n_lat=n_lat, n_ctx=n_ctx)
        n_rows = t_lat if last else t_all
        o_a = attn(latent_queries=True, out_rows=n_rows)
        if not last:
            o_a = attn(latent_queries=False, out_rows=n_rows, into=o_a)

        o_b = _spatial_gating(proj, row(sg_ln_g[l]), row(sg_ln_b[l]), sg_w[l].astype(BF16),
                              sg_b[l].reshape(B_GROUPS, B_CHUNK, 1))

        lru = functools.partial(_lru_pass, proj, conv_w=conv_w[l], conv_b=row(conv_b[l]), batch=batch)
        zeros_h = jnp.zeros((batch, 8, cw), F32)
        hf = {}
        for direction in range(2):
            wd = jnp.concatenate([dense_blocks(lru_wa[l, direction]), dense_blocks(lru_wx[l, direction])],
                                 axis=1).astype(BF16)
            par = dict(w_dense=wd, ba=row(lru_ba[l, direction]), bx=row(lru_bx[l, direction]),
                       lam=row(lru_lam[l, direction]), reverse=direction == 1, final=direction == 1)
            shared = direction == 1 and not last
            h_ctx, edge = lru(hf.get("ctx"), zeros_h, seq=n_ctx, tl=n_ctx, row0=t_lat,
                              out_rows=t_all if shared else None, out_row0=t_lat if shared else 0, **par)
            h_lat, _ = lru(hf.get("lat"), edge, seq=n_lat, tl=TM, row0=0,
                           out_rows=t_all if shared else None, into=h_ctx if shared else None, **par)
            hf = {"ctx": h_ctx, "lat": h_lat}
        o_c = hf["lat"]
        wr_t = w_router[l].T
        wr_hi = wr_t.astype(BF16)
        wr_lo = (wr_t - wr_hi.astype(F32)).astype(BF16)
        xn, h2, scores_t = _merge(x_all, o_a, o_b, o_c, proj, modp, w_branch[l].astype(BF16),
                                  w_out[l].astype(BF16), row(ln1_g[l]), row(ln1_b[l]), wr_hi, wr_lo, seg,
                                  n_rows=n_rows, alpha=alpha)
        x_all = _moe(xn, h2, scores_t, modp, seg_td, router_bias[l], moe_w_gate,
                     moe_w_up, moe_w_down, sh_w_gate[l].astype(BF16),
                     sh_w_up[l].astype(BF16), sh_w_down[l].astype(BF16), row(ln2_g[l]), row(ln2_b[l]), tri, ltri,
                     alpha=alpha, layer=l)
    return x_all[:t_lat].reshape(batch, n_lat, d)
```

```python
import functools
import math

import jax
import jax.numpy as jnp
from jax import lax
from jax.experimental import pallas as pl
from jax.experimental.pallas import tpu as pltpu

F32 = jnp.float32
BF16 = jnp.bfloat16
I32 = jnp.int32

A_HEADS = 4
A_QK_DIM = 64
A_V_DIM = 2 * A_QK_DIM
GRID_W = 64
ROPE_THETA = 10000.0
B_CHUNK = 128
B_GROUPS = 4
C_BLOCKS = 8
C_POW = 8.0
TOP_K = 8
N_GROUPS = 8
TOPK_GROUPS = 4
ROUTED_SCALE = 2.5
LN_EPS = 1e-6
RMS_EPS = 1e-5

TM = 512
MOE_BLK = 512
TD = 256
UNIT = 16
UNIT_LANES = 256
SORT_CHUNK = 512
W_LANES = 128
EXPERT_BUFS = 4
LRU_BLOCK = 64
HALO = 16
ATTN_HEADS = 4
ATTN_SUB = 1024
MOD_ROWS = 16
VMEM_LIMIT = 56 * 1024 * 1024


def _cparams(sem):
    return pltpu.CompilerParams(dimension_semantics=sem, vmem_limit_bytes=VMEM_LIMIT)


def _ln(x):
    mu = jnp.mean(x, axis=-1, keepdims=True)
    xc = x - mu
    var = jnp.mean(xc * xc, axis=-1, keepdims=True)
    return xc * lax.rsqrt(var + LN_EPS)


def _gelu(x):
    cdf = 0.5 * (1.0 + jnp.tanh(math.sqrt(2.0 / math.pi) * (x + 0.044715 * (x * x * x))))
    return x * cdf


def _sigmoid(x):
    return 0.5 * jnp.tanh(0.5 * x) + 0.5


def _silu(x):
    return x * _sigmoid(x)


def _dot(a, b):
    return jnp.dot(a, b, preferred_element_type=F32)


def _dot_nt(a, b):
    return lax.dot_general(a, b, (((1,), (1,)), ((), ())), preferred_element_type=F32)


def _mod_kernel(c_ref, w_ref, b_ref, o_ref):
    o_ref[...] = _dot(_silu(c_ref[...]).astype(BF16), w_ref[...]) + b_ref[...]


def _mod_rows(c_all, w, b):
    m, d = c_all.shape
    n = w.shape[1]
    tn = 1536
    return pl.pallas_call(
        _mod_kernel,
        grid=(n // tn,),
        in_specs=[pl.BlockSpec((m, d), lambda j: (0, 0)),
                  pl.BlockSpec((d, tn), lambda j: (0, j)),
                  pl.BlockSpec((1, tn), lambda j: (0, j))],
        out_specs=pl.BlockSpec((m, tn), lambda j: (0, j)),
        out_shape=jax.ShapeDtypeStruct((m, n), F32),
        compiler_params=_cparams(("parallel",)),
        name="adaln_rows",
    )(c_all, w, b)


REST_TILE = 1280
QKV_TILES = 3
COL_U, COL_S, COL_X, COL_Y, COL_G = 0, 1, 2, 3, 4


def _in_proj_kernel(x_ref, sh_ref, sc_ref, w_ref, b_ref, cos_ref, sa_ref, sb_ref,
                    q_ref, k_ref, v_ref, h_ref):
    j = pl.program_id(1)

    @pl.when(j == 0)
    def _():
        h = _ln(x_ref[...]) * (1.0 + sc_ref[...]) + sh_ref[...]
        h_ref[...] = h.astype(BF16)

    acc = _dot(h_ref[...], w_ref[...]) + b_ref[...]

    def rope(dst_ref, scale):
        hw = cos_ref.shape[1]
        for c0 in range(0, acc.shape[1], hw):
            x = acc[:, c0:c0 + hw]
            r = x * cos_ref[...] + pltpu.roll(x, hw - 16, 1) * sa_ref[...] + pltpu.roll(x, 16, 1) * sb_ref[...]
            dst_ref[:, c0:c0 + hw] = (r * scale).astype(BF16)

    @pl.when(j == 0)
    def _():
        rope(q_ref, A_QK_DIM ** -0.5 * math.log2(math.e))

    @pl.when(j == 1)
    def _():
        rope(k_ref, 1.0)

    @pl.when(j == 2)
    def _():
        v_ref[...] = acc.astype(BF16)


def _rest_proj_kernel(h_ref, w_ref, b_ref, o_ref):
    o_ref[...] = (_dot(h_ref[...], w_ref[...]) + b_ref[...]).astype(o_ref.dtype)


def _in_proj(x_all, modp, w, b, tables, seg, tbl_idx, tm):
    t, d = x_all.shape
    tn = 2 * A_HEADS * A_QK_DIM
    n_qkv = QKV_TILES * tn
    n_rest = w.shape[1] - n_qkv
    cos, sa, sb = tables
    tbl = pl.BlockSpec((tm, cos.shape[1]), lambda i, j: (tbl_idx(i), 0))
    qkv = pl.BlockSpec((tm, tn), lambda i, j: (i, 0))
    qkv_shape = jax.ShapeDtypeStruct((t, tn), BF16)
    q, k, v, h = pl.pallas_call(
        _in_proj_kernel,
        grid=(t // tm, QKV_TILES),
        in_specs=[pl.BlockSpec((tm, d), lambda i, j: (i, 0)),
                  pl.BlockSpec((None, None, 1, d), lambda i, j: (0, seg(i), 0, 0)),
                  pl.BlockSpec((None, None, 1, d), lambda i, j: (1, seg(i), 0, 0)),
                  pl.BlockSpec((d, tn), lambda i, j: (0, j)),
                  pl.BlockSpec((1, tn), lambda i, j: (0, j)),
                  tbl, tbl, tbl],
        out_specs=[qkv, qkv, qkv, pl.BlockSpec((tm, d), lambda i, j: (i, 0))],
        out_shape=[qkv_shape, qkv_shape, qkv_shape, jax.ShapeDtypeStruct((t, d), BF16)],
        compiler_params=_cparams(("parallel", "arbitrary")),
        name="in_proj_qkv",
    )(x_all, modp, modp, w, b, cos, sa, sb)
    tr = REST_TILE if n_rest % REST_TILE == 0 else tn
    rest = pl.pallas_call(
        _rest_proj_kernel,
        grid=(t // tm, n_rest // tr),
        in_specs=[pl.BlockSpec((tm, d), lambda i, j: (i, 0)),
                  pl.BlockSpec((d, tr), lambda i, j: (0, j)),
                  pl.BlockSpec((1, tr), lambda i, j: (0, j))],
        out_specs=pl.BlockSpec((tm, tr), lambda i, j: (i, j)),
        out_shape=jax.ShapeDtypeStruct((t, n_rest), BF16),
        compiler_params=_cparams(("parallel", "parallel")),
        name="in_proj_rest",
    )(h, w[:, n_qkv:], b[:, n_qkv:])
    return q, k, v, rest


def _rope_tables(n, tm):
    rows = n // GRID_W
    pos_row = jnp.repeat(jnp.arange(rows), GRID_W).astype(F32)
    pos_col = jnp.tile(jnp.arange(GRID_W), rows).astype(F32)
    quarter = A_QK_DIM // 4
    inv = ROPE_THETA ** (-jnp.arange(quarter, dtype=F32) / quarter)
    ang_r = pos_row[:, None] * inv
    ang_c = pos_col[:, None] * inv
    ang = jnp.concatenate([ang_r, ang_r, ang_c, ang_c], axis=-1)
    cos = jnp.cos(ang)
    sin = jnp.sin(ang)
    first_half = (jnp.arange(A_QK_DIM) % (2 * quarter)) < quarter
    sa = jnp.where(first_half, -sin, 0.0)
    sb = jnp.where(first_half, 0.0, sin)

    def widen(tbl, fill):
        tbl = jnp.tile(tbl, (1, 2))
        return jnp.concatenate([tbl, jnp.full((tm, tbl.shape[1]), fill, F32)], axis=0)

    return widen(cos, 1.0), widen(sa, 0.0), widen(sb, 0.0)


def _attn_kernel(*refs, lam_init, with_lat, sub):
    if with_lat:
        lam_ref, gain_ref, q_ref, kc_ref, vc_ref, kl_ref, vl_ref, o_ref = refs
    else:
        lam_ref, gain_ref, q_ref, kc_ref, vc_ref, _, o_ref = refs
    lv = lam_ref[...]
    lam = (jnp.exp(jnp.sum(lv[0:1] * lv[1:2], axis=-1, keepdims=True))
           - jnp.exp(jnp.sum(lv[2:3] * lv[3:4], axis=-1, keepdims=True)) + lam_init)
    chunks = [(kc_ref, vc_ref, 0, kc_ref.shape[0])]
    if with_lat:
        ck = min(sub, kl_ref.shape[0])
        chunks += [(kl_ref, vl_ref, r0, ck) for r0 in range(0, kl_ref.shape[0], ck)]
    for hh in range(gain_ref.shape[0]):
        head = slice(hh * A_V_DIM, (hh + 1) * A_V_DIM)
        probs, dens = [], []
        for c in range(2):
            cols = slice(hh * A_V_DIM + c * A_QK_DIM, hh * A_V_DIM + (c + 1) * A_QK_DIM)
            qc = q_ref[:, cols]
            scores = [_dot_nt(k_ref[r0:r0 + n, cols], qc) for k_ref, _, r0, n in chunks]
            m = None
            for s in scores:
                ms = jnp.max(s, axis=0, keepdims=True)
                m = ms if m is None else jnp.maximum(m, ms)
            p = [jnp.exp2(s - m) for s in scores]
            den = None
            for pj in p:
                ds = jnp.sum(pj, axis=0, keepdims=True)
                den = ds if den is None else den + ds
            probs.append(p)
            dens.append(den)
        ratio = lam * dens[0] / dens[1]
        o = None
        for p1, p2, (_, v_ref, r0, n) in zip(probs[0], probs[1], chunks):
            w = p1 - p2 * ratio
            ps = _dot(v_ref[head, r0:r0 + n], w.astype(BF16))
            o = ps if o is None else o + ps
        o = o / dens[0]
        o = o * lax.rsqrt(jnp.mean(o * o, axis=0, keepdims=True) + RMS_EPS)
        o_ref[:, head] = ((o * gain_ref[hh]) * (1.0 - lam_init)).T


def _attention(qr, kr, vt, lam_vecs, gain, *, lam_init, batch, n_lat, n_ctx, latent_queries, out_rows, into=None):
    hw = ATTN_HEADS * A_V_DIM
    ctx_blk0 = batch * n_lat // n_ctx
    lam_spec = pl.BlockSpec(lam_vecs.shape, lambda b, h, i: (0, 0))
    gain_spec = pl.BlockSpec((ATTN_HEADS, A_V_DIM, 1), lambda b, h, i: (h, 0, 0))
    kv_ctx = pl.BlockSpec((n_ctx, hw), lambda b, h, i: (ctx_blk0 + b, h))
    vt_ctx = pl.BlockSpec((hw, n_ctx), lambda b, h, i: (h, ctx_blk0 + b))
    if latent_queries:
        tq = 256
        nq = n_lat // tq
        q_spec = pl.BlockSpec((tq, hw), lambda b, h, i: (b * nq + i, h))
        kv_lat = pl.BlockSpec((n_lat, hw), lambda b, h, i: (b, h))
        vt_lat = pl.BlockSpec((hw, n_lat), lambda b, h, i: (h, b))
        in_specs = [lam_spec, gain_spec, q_spec, kv_ctx, vt_ctx, kv_lat, vt_lat]
        args = (lam_vecs, gain, qr, kr, vt, kr, vt)
        out_blk0 = 0
        aliases = {}
    else:
        tq = n_ctx
        nq = 1
        q_spec = pl.BlockSpec((tq, hw), lambda b, h, i: (ctx_blk0 + b, h))
        in_specs = [lam_spec, gain_spec, q_spec, kv_ctx, vt_ctx, pl.BlockSpec(memory_space=pl.ANY)]
        args = (lam_vecs, gain, qr, kr, vt, into)
        out_blk0 = ctx_blk0
        aliases = {len(args) - 1: 0}
    return pl.pallas_call(
        functools.partial(_attn_kernel, lam_init=lam_init, with_lat=latent_queries, sub=ATTN_SUB),
        grid=(batch, A_HEADS // ATTN_HEADS, nq),
        in_specs=in_specs,
        out_specs=pl.BlockSpec((tq, hw), lambda b, h, i: (out_blk0 + b * nq + i, h)),
        out_shape=jax.ShapeDtypeStruct((out_rows, A_HEADS * A_V_DIM), F32),
        input_output_aliases=aliases,
        compiler_params=_cparams(("parallel", "parallel", "arbitrary")),
        name="diff_attn_lat" if latent_queries else "diff_attn_ctx",
    )(*args)


def _sg_kernel(u_ref, s_ref, g_ref, b_ref, w_ref, bs_ref, o_ref):
    gw = B_CHUNK
    for ci in range(TM // B_CHUNK):
        rows = slice(ci * B_CHUNK, (ci + 1) * B_CHUNK)
        for g in range(B_GROUPS):
            cols = slice(g * gw, (g + 1) * gw)
            vn = _ln(_gelu(s_ref[rows, cols].astype(F32))) * g_ref[:, cols] + b_ref[:, cols]
            mixed = _dot(w_ref[g], vn.astype(BF16)) + bs_ref[g]
            o_ref[rows, cols] = _gelu(u_ref[rows, cols].astype(F32)) * mixed


def _spatial_gating(proj, ln_g, ln_b, w_s, b_s):
    t = proj.shape[0]
    w = B_GROUPS * B_CHUNK
    full = lambda a: pl.BlockSpec(a.shape, lambda i: (0,) * a.ndim)
    return pl.pallas_call(
        _sg_kernel,
        grid=(t // TM,),
        in_specs=[pl.BlockSpec((TM, w), lambda i: (i, COL_U)),
                  pl.BlockSpec((TM, w), lambda i: (i, COL_S)),
                  full(ln_g), full(ln_b), full(w_s), full(b_s)],
        out_specs=pl.BlockSpec((TM, w), lambda i: (i, 0)),
        out_shape=jax.ShapeDtypeStruct((t, w), F32),
        compiler_params=_cparams(("parallel",)),
        name="spatial_gating",
    )(proj, proj, ln_g, ln_b, w_s, b_s)


def _lru_kernel(*refs, reverse, final, n_tiles, tl, aliased):
    if aliased:
        refs = refs[:12] + refs[13:]
    if final:
        (x_ref, xp_ref, xn_ref, cw_ref, cb_ref, w_ref, ba_ref, bx_ref, lam_ref, h0_ref,
         hf_ref, y_ref, o_ref, hl_ref, carry) = refs
    else:
        (x_ref, xp_ref, xn_ref, cw_ref, cb_ref, w_ref, ba_ref, bx_ref, lam_ref, h0_ref,
         o_ref, hl_ref, carry) = refs
    step = pl.program_id(1)

    @pl.when(step == 0)
    def _():
        carry[...] = h0_ref[...]

    ti = (n_tiles - 1 - step) if reverse else step
    x = x_ref[...].astype(F32)
    cw = x.shape[-1]
    row = lax.broadcasted_iota(I32, x.shape, 0)
    has_prev = ti > 0
    has_next = ti < n_tiles - 1
    p6 = jnp.where(has_prev, xp_ref[HALO - 2:HALO - 1, :].astype(F32), 0.0)
    p7 = jnp.where(has_prev, xp_ref[HALO - 1:HALO, :].astype(F32), 0.0)
    n0 = jnp.where(has_next, xn_ref[0:1, :].astype(F32), 0.0)
    xm1 = jnp.where(row == 0, p7, pltpu.roll(x, 1, 0))
    xm2 = jnp.where(row == 0, p6, jnp.where(row == 1, p7, pltpu.roll(x, 2, 0)))
    xp1 = jnp.where(row == tl - 1, n0, pltpu.roll(x, tl - 1, 0))
    taps = cw_ref[...]
    xs = taps[0:1] * xm2 + taps[1:2] * xm1 + taps[2:3] * x + taps[3:4] * xp1 + cb_ref[...]

    z = _dot(xs.astype(BF16), w_ref[...])
    r = _sigmoid(z[:, :cw] + ba_ref[...])
    ig = _sigmoid(z[:, cw:] + bx_ref[...])
    nl = -lam_ref[...]
    softplus = jnp.maximum(nl, 0.0) + jnp.log(1.0 + jnp.exp(-jnp.abs(nl)))
    log_a = -C_POW * r * softplus
    a = jnp.exp(log_a)
    u = jnp.sqrt(1.0 - jnp.exp(2.0 * log_a)) * (ig * xs)

    sb = min(LRU_BLOCK, tl)
    in_blk = row % sb
    d = 1
    while d < sb:
        if reverse:
            ok = in_blk < sb - d
            shift = tl - d
        else:
            ok = in_blk >= d
            shift = d
        a_sh = jnp.where(ok, pltpu.roll(a, shift, 0), 1.0)
        u_sh = jnp.where(ok, pltpu.roll(u, shift, 0), 0.0)
        u = u + a * u_sh
        a = a * a_sh
        d *= 2
    state = carry[0:1, :]
    blocks = [None] * (tl // sb)
    for b in (reversed(range(tl // sb)) if reverse else range(tl // sb)):
        hb = u[b * sb:(b + 1) * sb] + a[b * sb:(b + 1) * sb] * state
        state = hb[0:1, :] if reverse else hb[sb - 1:sb, :]
        blocks[b] = hb
    h = jnp.concatenate(blocks, axis=0)
    edge = state
    carry[...] = jnp.broadcast_to(edge, carry.shape)
    hl_ref[...] = jnp.broadcast_to(edge, hl_ref.shape)
    if final:
        o_ref[...] = _gelu(y_ref[...].astype(F32)) * (hf_ref[...] + h)
    else:
        o_ref[...] = h


def _lru_pass(proj, hf, h0, conv_w, conv_b, w_dense, ba, bx, lam, *, batch, seq, tl, row0, reverse, final,
              out_rows=None, out_row0=0, into=None):
    t_all = proj.shape[0]
    out_rows = batch * seq if out_rows is None else out_rows
    out_blk0 = out_row0 // tl
    assert into is None or final
    cw = conv_w.shape[1]
    n_tiles = seq // tl
    base = row0 // tl
    sub = tl // HALO
    last_halo = t_all // HALO - 1

    def tile(b, s):
        ti = (n_tiles - 1 - s) if reverse else s
        return base + b * n_tiles + ti

    def loc(b, s):
        ti = (n_tiles - 1 - s) if reverse else s
        return b * n_tiles + ti

    full = lambda a: pl.BlockSpec(a.shape, lambda b, s: (0,) * a.ndim)
    in_specs = [pl.BlockSpec((tl, cw), lambda b, s: (tile(b, s), COL_X)),
                pl.BlockSpec((HALO, cw), lambda b, s: (jnp.maximum(tile(b, s) * sub - 1, 0), COL_X)),
                pl.BlockSpec((HALO, cw), lambda b, s: (jnp.minimum((tile(b, s) + 1) * sub, last_halo), COL_X)),
                full(conv_w), full(conv_b), full(w_dense), full(ba), full(bx), full(lam),
                pl.BlockSpec((None, 8, cw), lambda b, s: (b, 0, 0))]
    args = [proj, proj, proj, conv_w, conv_b, w_dense, ba, bx, lam, h0]
    if final:
        in_specs += [pl.BlockSpec((tl, cw), lambda b, s: (loc(b, s), 0)),
                     pl.BlockSpec((tl, cw), lambda b, s: (tile(b, s), COL_Y))]
        args += [hf, proj]
    aliases = {}
    if into is not None:
        aliases = {len(args): 0}
        in_specs += [pl.BlockSpec(memory_space=pl.ANY)]
        args += [into]
    return pl.pallas_call(
        functools.partial(_lru_kernel, reverse=reverse, final=final, n_tiles=n_tiles, tl=tl,
                          aliased=into is not None),
        grid=(batch, n_tiles),
        in_specs=in_specs,
        out_specs=[pl.BlockSpec((tl, cw), lambda b, s: (out_blk0 + loc(b, s), 0)),
                   pl.BlockSpec((None, 8, cw), lambda b, s: (b, 0, 0))],
        out_shape=[jax.ShapeDtypeStruct((out_rows, cw), F32),
                   jax.ShapeDtypeStruct((batch, 8, cw), F32)],
        scratch_shapes=[pltpu.VMEM((8, cw), F32)],
        input_output_aliases=aliases,
        compiler_params=_cparams(("parallel", "arbitrary")),
        name="rglru_%s_%s" % ("bwd" if reverse else "fwd", "lat" if row0 == 0 else "ctx"),
    )(*args)


def _merge_kernel(x_ref, oa_ref, ob_ref, oc_ref, g0, g1, g2, g3, g4, g5, wb_ref, wo_ref,
                  gate1_ref, sh2_ref, sc2_ref, lng_ref, lnb_ref, wrh_ref, wrl_ref,
                  xn_ref, h2_ref, sc_ref, *, alpha):
    gates = ((g0, g1), (g2, g3), (g4, g5))
    mix = None
    for r, o_ref in enumerate((oa_ref, ob_ref, oc_ref)):
        proj = _dot(o_ref[...].astype(BF16), wb_ref[r])
        gate = jnp.concatenate([_sigmoid(gates[r][0][...].astype(F32)), _sigmoid(gates[r][1][...].astype(F32))],
                               axis=-1)
        term = gate * proj
        mix = term if mix is None else mix + term
    out = _dot(mix.astype(BF16), wo_ref[...])
    xn = _ln(alpha * x_ref[...] + gate1_ref[...] * out) * lng_ref[...] + lnb_ref[...]
    xn_ref[...] = xn
    h2 = _ln(xn) * (1.0 + sc2_ref[...]) + sh2_ref[...]
    h_hi = h2.astype(BF16)
    h2_ref[...] = h_hi
    h_lo = (h2 - h_hi.astype(F32)).astype(BF16)
    logits = _dot_nt(wrh_ref[...], h_hi) + (_dot_nt(wrh_ref[...], h_lo) + _dot_nt(wrl_ref[...], h_hi))
    sc_ref[...] = _sigmoid(logits)


def _merge(x_all, o_a, o_b, o_c, proj, modp, w_branch, w_out, ln_g, ln_b, wr_hi, wr_lo, seg, *, n_rows, alpha):
    d = x_all.shape[1]
    bw = o_a.shape[1]
    n_exp = wr_hi.shape[0]
    tok = lambda w: pl.BlockSpec((TM, w), lambda i: (i, 0))
    gate = lambda c: pl.BlockSpec((TM, bw), lambda i, c=c: (i, c))
    mod = lambda k: pl.BlockSpec((None, None, 1, d), lambda i, k=k: (k, seg(i), 0, 0))
    full = lambda a: pl.BlockSpec(a.shape, lambda i: (0,) * a.ndim)
    return pl.pallas_call(
        functools.partial(_merge_kernel, alpha=alpha),
        grid=(n_rows // TM,),
        in_specs=[tok(d), tok(bw), tok(bw), tok(bw)] + [gate(COL_G + c) for c in range(6)]
                 + [full(w_branch), full(w_out), mod(2), mod(3), mod(4), full(ln_g), full(ln_b),
                    full(wr_hi), full(wr_lo)],
        out_specs=[tok(d), tok(d), pl.BlockSpec((n_exp, TM), lambda i: (0, i))],
        out_shape=[jax.ShapeDtypeStruct((n_rows, d), F32),
                   jax.ShapeDtypeStruct((n_rows, d), BF16),
                   jax.ShapeDtypeStruct((n_exp, n_rows), F32)],
        compiler_params=_cparams(("parallel",)),
        name="merge_residual_router",
    )(x_all, o_a, o_b, o_c, proj, proj, proj, proj, proj, proj, w_branch, w_out,
      modp, modp, modp, ln_g, ln_b, wr_hi, wr_lo)


def _tile_rows(n_exp):
    return TOP_K * TD + n_exp * UNIT


def _route_kernel(s_ref, bias_ref, tri_ref, ltri_ref, pos_ref, w_ref, rs_ref, rn_ref, unit_ref, tot_ref, carry):
    @pl.when(pl.program_id(0) == 0)
    def _():
        carry[...] = jnp.zeros_like(carry)

    s = s_ref[...]
    n_exp, tn = s.shape
    per = n_exp // N_GROUPS
    neg = -jnp.inf
    biased = s + bias_ref[...]
    sub = lax.broadcasted_iota(I32, (per, tn), 0)
    gs_rows = []
    for g in range(N_GROUPS):
        blk = biased[g * per:(g + 1) * per, :]
        m1 = jnp.max(blk, axis=0, keepdims=True)
        first = jnp.min(jnp.where(blk == m1, sub, per), axis=0, keepdims=True)
        m2 = jnp.max(jnp.where(sub == first, neg, blk), axis=0, keepdims=True)
        gs_rows.append(m1 + m2)
    gs = jnp.concatenate(gs_rows, axis=0)
    gi = lax.broadcasted_iota(I32, gs.shape, 0)
    g_ok = jnp.zeros(gs.shape, F32)
    cur = gs
    for _ in range(TOPK_GROUPS):
        m = jnp.max(cur, axis=0, keepdims=True)
        pick = jnp.min(jnp.where(cur == m, gi, N_GROUPS), axis=0, keepdims=True)
        hit = gi == pick
        g_ok = jnp.where(hit, 1.0, g_ok)
        cur = jnp.where(hit, neg, cur)
    ok_rows = [jnp.broadcast_to(g_ok[g:g + 1, :], (per, tn)) for g in range(N_GROUPS)]
    expert_ok = jnp.concatenate(ok_rows, axis=0)
    masked = jnp.where(expert_ok > 0.0, biased, neg)
    ei = lax.broadcasted_iota(I32, s.shape, 0)
    pick_s, hits = [], []
    sel = jnp.zeros(s.shape, F32)
    for _ in range(TOP_K):
        m = jnp.max(masked, axis=0, keepdims=True)
        pick = jnp.min(jnp.where(masked == m, ei, n_exp), axis=0, keepdims=True)
        hit = ei == pick
        hits.append(hit)
        pick_s.append(jnp.sum(jnp.where(hit, s, 0.0), axis=0, keepdims=True))
        sel = jnp.where(hit, 1.0, sel)
        masked = jnp.where(hit, neg, masked)
    tot = pick_s[0]
    for k in range(1, TOP_K):
        tot = tot + pick_s[k]
    w = jnp.where(sel > 0.0, s, 0.0) / tot * ROUTED_SCALE
    w_hi = w.astype(BF16)
    w_ref[...] = jnp.concatenate([w_hi, (w - w_hi.astype(F32)).astype(BF16)], axis=0)

    incl = _dot(sel.astype(BF16), tri_ref[...])
    count = incl[:, tn - 1:tn]
    run = jnp.floor((count + (UNIT - 1)) * (1.0 / UNIT)) * UNIT
    run_start = _dot(ltri_ref[...], jnp.broadcast_to(run, (n_exp, 128)).astype(BF16))[:, 0:1]
    row_in_tile = incl - sel + run_start
    pos = [jnp.sum(jnp.where(hit, row_in_tile, 0.0), axis=0, keepdims=True) for hit in hits]
    pos_ref[...] = jnp.concatenate(pos, axis=0).astype(I32)
    rs_ref[...] = jnp.broadcast_to(run_start, rs_ref.shape)
    rn_ref[...] = jnp.broadcast_to(run, rn_ref.shape)

    half = UNIT_LANES // 2
    region_used = carry[:, 0:1]
    u0 = run_start * (1.0 / UNIT)
    nu = run * (1.0 / UNIT)
    n_pair = jnp.floor(nu * 0.5)
    n_single = nu - 2.0 * n_pair
    both = jnp.concatenate([jnp.broadcast_to(n_pair, (n_exp, 64)), jnp.broadcast_to(n_single, (n_exp, 64))], axis=1)
    first = _dot(ltri_ref[...], both.astype(BF16))
    pair0, single0 = first[:, 0:1], first[:, 64:65]
    lane = lax.broadcasted_iota(I32, (n_exp, UNIT_LANES), 1)
    is_pair = lane < half
    item = jnp.where(is_pair, lane, lane - half).astype(F32)
    start = jnp.where(is_pair, pair0, single0)
    count = jnp.where(is_pair, n_pair, n_single)
    inside = jnp.logical_and(item >= start, item < start + count)
    off = jnp.where(is_pair, 2.0 * (item - pair0), 2.0 * n_pair)
    eu = lax.broadcasted_iota(I32, (n_exp, UNIT_LANES), 0).astype(F32)
    pick = lambda v: jnp.sum(jnp.where(inside, v, 0.0), axis=0, keepdims=True)
    totals = jnp.where(is_pair[0:1], jnp.sum(n_pair, axis=0, keepdims=True), jnp.sum(n_single, axis=0, keepdims=True))
    pad = jnp.zeros((4, UNIT_LANES), F32)
    unit_ref[...] = jnp.concatenate([pick(off + region_used * (1.0 / UNIT)), pick(eu), pick(off + u0), totals, pad],
                                    axis=0).astype(I32)

    total = region_used + run
    carry[...] = jnp.broadcast_to(total, carry.shape)
    tot_ref[...] = jnp.broadcast_to(total, tot_ref.shape)


def _route(scores_t, bias_col, tri, ltri):
    n_exp, t = scores_t.shape
    nt = t // TD
    return pl.pallas_call(
        _route_kernel,
        grid=(nt,),
        in_specs=[pl.BlockSpec((n_exp, TD), lambda i: (0, i)),
                  pl.BlockSpec((n_exp, 1), lambda i: (0, 0)),
                  pl.BlockSpec((TD, TD), lambda i: (0, 0)),
                  pl.BlockSpec((n_exp, n_exp), lambda i: (0, 0))],
        out_specs=[pl.BlockSpec((TOP_K, TD), lambda i: (0, i)),
                   pl.BlockSpec((2 * n_exp, TD), lambda i: (0, i)),
                   pl.BlockSpec((n_exp, 128), lambda i: (0, i)),
                   pl.BlockSpec((n_exp, 128), lambda i: (0, i)),
                   pl.BlockSpec((8, UNIT_LANES), lambda i: (0, i)),
                   pl.BlockSpec((n_exp, 128), lambda i: (0, 0))],
        out_shape=[jax.ShapeDtypeStruct((TOP_K, t), I32),
                   jax.ShapeDtypeStruct((2 * n_exp, t), BF16),
                   jax.ShapeDtypeStruct((n_exp, nt * 128), F32),
                   jax.ShapeDtypeStruct((n_exp, nt * 128), F32),
                   jax.ShapeDtypeStruct((8, nt * UNIT_LANES), I32),
                   jax.ShapeDtypeStruct((n_exp, 128), F32)],
        scratch_shapes=[pltpu.VMEM((n_exp, 128), F32)],
        compiler_params=_cparams(("arbitrary",)),
        name="route_topk",
    )(scores_t, bias_col, tri, ltri)


def _unit_dest_kernel(start_ref, unit_ref, o_ref, *, n_exp):
    tbl = unit_ref[...]
    exp_of = tbl[1:2, :]
    dst = tbl[0:1, :]
    for e in range(n_exp):
        dst = dst + jnp.where(exp_of == e, start_ref[e], 0)
    o_ref[...] = jnp.concatenate([dst, tbl[2:4, :], jnp.zeros((5, tbl.shape[1]), I32)], axis=0)


def _unit_dest(region_start_units, units):
    nt = units.shape[1] // UNIT_LANES
    per_step = next(k for k in (4, 2, 1) if nt % k == 0)
    blk = lambda: pl.BlockSpec((8, per_step * UNIT_LANES), lambda i, s: (0, i))
    return pl.pallas_call(
        functools.partial(_unit_dest_kernel, n_exp=region_start_units.shape[0]),
        grid_spec=pltpu.PrefetchScalarGridSpec(
            num_scalar_prefetch=1, grid=(nt // per_step,), in_specs=[blk()], out_specs=blk()),
        out_shape=jax.ShapeDtypeStruct(units.shape, I32),
        compiler_params=_cparams(("parallel",)),
        name="route_units",
    )(region_start_units, units)


def _unit_rows(i, n_units=1):
    return pl.ds(pl.multiple_of(i * UNIT, UNIT), n_units * UNIT)


def _copy_items(tbl, make_copy, act):
    half = UNIT_LANES // 2
    _for_each(tbl[2, 0], lambda i: getattr(make_copy(tbl, i, 2), act)())
    _for_each(tbl[2, half], lambda i: getattr(make_copy(tbl, half + i, 1), act)())


def _drain_items(n_pairs, n_singles, make_copy, tbl):
    _for_each(n_pairs, lambda i: make_copy(tbl, 0, 2).wait())
    _for_each(n_singles, lambda i: make_copy(tbl, UNIT_LANES // 2, 1).wait())


def _for_each(count, fn, group=8):
    main = lax.shift_right_logical(count, int(math.log2(group)))

    def many(i, c):
        for r in range(group):
            fn(i * group + r)
        return c

    lax.fori_loop(0, main, many, 0)
    lax.fori_loop(main * group, count, lambda i, c: (fn(i), c)[1], 0)


def _dispatch_kernel(tail_lo_ref, tail_hi_ref, np_ref, ns_ref, unit_ref, pos_ref, w_ref, rs_ref, rn_ref, h_ref,
                     xs_ref, s_scr, zero_scr, sem, zsem, *, n_exp):
    rows = s_scr.shape[1]
    td = h_ref.shape[0]
    step = pl.program_id(0)
    last = pl.num_programs(0) - 1
    slot = step % 2

    def zero_copy(g):
        return pltpu.make_async_copy(zero_scr, xs_ref.at[_unit_rows(g), :], zsem.at[0])

    def item_copy(sl):
        def make(tbl, lane, size):
            return pltpu.make_async_copy(s_scr.at[sl, _unit_rows(tbl[1, lane], size), :],
                                         xs_ref.at[_unit_rows(tbl[0, lane], size), :], sem.at[size - 1, sl])
        return make

    def drain(tile, sl):
        _drain_items(np_ref[tile], ns_ref[tile], item_copy(sl), unit_ref)

    @pl.when(step >= 2)
    def _():
        drain(step - 2, slot)

    @pl.when(step == 0)
    def _():
        zero_scr[...] = jnp.zeros_like(zero_scr)

        def per_expert(e, c):
            lax.fori_loop(tail_lo_ref[e], tail_hi_ref[e], lambda g, cc: (zero_copy(g).start(), cc)[1], 0)
            lax.fori_loop(tail_lo_ref[e], tail_hi_ref[e], lambda g, cc: (zero_copy(g).wait(), cc)[1], 0)
            return c

        lax.fori_loop(0, n_exp, per_expert, 0)

    pos = pos_ref[...]
    h = h_ref[...]
    d = h.shape[1]
    w_dense = w_ref[...]
    run_lo = rs_ref[...]
    run_hi = run_lo + rn_ref[...]
    lane = lax.broadcasted_iota(I32, (SORT_CHUNK, s_scr.shape[2] - d), 1)
    for r0 in range(0, rows, SORT_CHUNK):
        srow = (lax.broadcasted_iota(I32, (SORT_CHUNK, td), 0) + r0).astype(jnp.int16)
        onehot = jnp.zeros((SORT_CHUNK, td), BF16)
        for k in range(pos.shape[0]):
            onehot = jnp.where(srow == pos[k:k + 1, :].astype(jnp.int16), jnp.ones((), BF16), onehot)
        s_e = (lax.broadcasted_iota(I32, (SORT_CHUNK, w_dense.shape[0]), 0) + r0).astype(F32)
        mine = jnp.logical_and(s_e >= run_lo, s_e < run_hi).astype(BF16)
        w_row = jnp.sum(onehot.astype(F32) * _dot(mine, w_dense), axis=1, keepdims=True)
        w_hi = w_row.astype(BF16).astype(F32)
        w_lo = w_row - w_hi
        s_scr[slot, r0:r0 + SORT_CHUNK, :d] = _dot(onehot, h).astype(BF16)
        s_scr[slot, r0:r0 + SORT_CHUNK, d:] = jnp.where(lane == 0, w_hi, jnp.where(lane == 1, w_lo, 0.0)).astype(BF16)

    _copy_items(unit_ref, item_copy(slot), "start")

    @pl.when(step == last)
    def _():
        drain(step, slot)

        @pl.when(step >= 1)
        def _():
            drain(step - 1, 1 - slot)


def _dispatch(tail_lo, tail_hi, n_pairs, n_singles, unit_dst, pos, w_dense, run_start_rows, run_rows, h2, n_slots):
    t, d = h2.shape
    n_exp = tail_lo.shape[0]
    dw = d + W_LANES
    runs = pl.BlockSpec((None, 1, 2 * n_exp), lambda i, a, b, c, e: (i, 0, 0))
    return pl.pallas_call(
        functools.partial(_dispatch_kernel, n_exp=n_exp),
        grid_spec=pltpu.PrefetchScalarGridSpec(
            num_scalar_prefetch=4, grid=(t // TD,),
            in_specs=[pl.BlockSpec((8, UNIT_LANES), lambda i, a, b, c, e: (0, i), memory_space=pltpu.SMEM),
                      pl.BlockSpec((TOP_K, TD), lambda i, a, b, c, e: (0, i)),
                      pl.BlockSpec((2 * n_exp, TD), lambda i, a, b, c, e: (0, i)),
                      runs, runs,
                      pl.BlockSpec((TD, d), lambda i, a, b, c, e: (i, 0))],
            out_specs=pl.BlockSpec(memory_space=pl.ANY),
            scratch_shapes=[pltpu.VMEM((2, _tile_rows(n_exp), dw), BF16), pltpu.VMEM((UNIT, dw), BF16),
                            pltpu.SemaphoreType.DMA((2, 2)), pltpu.SemaphoreType.DMA((1,))]),
        out_shape=jax.ShapeDtypeStruct((n_slots, dw), BF16),
        compiler_params=_cparams(("arbitrary",)),
        name="moe_dispatch",
    )(tail_lo, tail_hi, n_pairs, n_singles, unit_dst, pos, w_dense, run_start_rows, run_rows, h2)


def _expert_kernel(blk0_ref, nblk_ref, tot_ref, wg_ref, wu_ref, wd_ref, xs_ref, ys_ref, xbuf, obuf, wgu_scr, wd_scr,
                   sem_in, sem_out):
    e = pl.program_id(0)
    n = nblk_ref[e]
    b0 = blk0_ref[e]
    de = wg_ref.shape[1]
    wgu_scr[:, :de] = wg_ref[...].astype(BF16)
    wgu_scr[:, de:] = wu_ref[...].astype(BF16)
    wd_scr[...] = wd_ref[...].astype(BF16)

    total = tot_ref[0]

    def rows(g):
        return pl.ds(pl.multiple_of(g * MOE_BLK, MOE_BLK), MOE_BLK)

    def in_copy(g):
        sl = g % EXPERT_BUFS
        return pltpu.make_async_copy(xs_ref.at[rows(g), :], xbuf.at[sl], sem_in.at[sl])

    def out_copy(g):
        sl = g % EXPERT_BUFS
        return pltpu.make_async_copy(obuf.at[sl], ys_ref.at[rows(g), :], sem_out.at[sl])

    @pl.when(e == 0)
    def _():
        for g in range(EXPERT_BUFS - 1):
            @pl.when(g < total)
            def _():
                in_copy(g).start()

    def block(j, c):
        g = b0 + j
        sl = g % EXPERT_BUFS
        in_copy(g).wait()

        @pl.when(g + (EXPERT_BUFS - 1) < total)
        def _():
            in_copy(g + (EXPERT_BUFS - 1)).start()

        @pl.when(g >= EXPERT_BUFS)
        def _():
            out_copy(g - EXPERT_BUFS).wait()

        d = wd_scr.shape[1]
        gu = _dot(xbuf[sl, :, :d], wgu_scr[...])
        act = _silu(gu[:, :de]) * gu[:, de:]
        w_row = xbuf[sl, :, d:d + 1].astype(F32) + xbuf[sl, :, d + 1:d + 2].astype(F32)
        obuf[sl] = (_dot(act.astype(BF16), wd_scr[...]) * w_row).astype(BF16)
        out_copy(g).start()
        return c

    lax.fori_loop(0, n, block, 0)

    @pl.when(e == pl.num_programs(0) - 1)
    def _():
        for back in range(1, EXPERT_BUFS + 1):
            @pl.when(total >= back)
            def _():
                out_copy(total - back).wait()


def _experts(blk0, nblk, n_blocks, xs, w_gate, w_up, w_down, layer):
    n_slots, dw = xs.shape
    _, n_exp, d, de = w_gate.shape
    return pl.pallas_call(
        _expert_kernel,
        grid_spec=pltpu.PrefetchScalarGridSpec(
            num_scalar_prefetch=3, grid=(n_exp,),
            in_specs=[pl.BlockSpec((None, None, d, de), lambda e, a, b, c: (layer, e, 0, 0)),
                      pl.BlockSpec((None, None, d, de), lambda e, a, b, c: (layer, e, 0, 0)),
                      pl.BlockSpec((None, None, de, d), lambda e, a, b, c: (layer, e, 0, 0)),
                      pl.BlockSpec(memory_space=pl.ANY)],
            out_specs=pl.BlockSpec(memory_space=pl.ANY),
            scratch_shapes=[pltpu.VMEM((EXPERT_BUFS, MOE_BLK, dw), BF16), pltpu.VMEM((EXPERT_BUFS, MOE_BLK, d), BF16),
                            pltpu.VMEM((d, 2 * de), BF16), pltpu.VMEM((de, d), BF16),
                            pltpu.SemaphoreType.DMA((EXPERT_BUFS,)), pltpu.SemaphoreType.DMA((EXPERT_BUFS,))]),
        out_shape=jax.ShapeDtypeStruct((n_slots, d), BF16),
        compiler_params=_cparams(("arbitrary",)),
        name="moe_experts",
    )(blk0, nblk, n_blocks, w_gate, w_up, w_down, xs)


def _combine_kernel(unit_ref, next_ref, pos_ref, x_ref, h_ref, gate2_ref,
                    sg_ref, su_ref, sd_ref, lng_ref, lnb_ref, ys_ref, o_ref, s_scr, sem, *, alpha):
    rows = s_scr.shape[1]
    td = x_ref.shape[0]
    step = pl.program_id(0)
    slot = step % 2

    def item_copy(sl):
        def make(tbl, lane, size):
            return pltpu.make_async_copy(ys_ref.at[_unit_rows(tbl[0, lane], size), :],
                                         s_scr.at[sl, _unit_rows(tbl[1, lane], size), :], sem.at[size - 1, sl])
        return make

    def fetch(tbl, sl):
        _copy_items(tbl, item_copy(sl), "start")

    @pl.when(step == 0)
    def _():
        s_scr[...] = jnp.zeros_like(s_scr)
        fetch(unit_ref, 0)

    @pl.when(step + 1 < pl.num_programs(0))
    def _():
        fetch(next_ref, 1 - slot)

    h = h_ref[...]
    acc = _dot((_silu(_dot(h, sg_ref[...])) * _dot(h, su_ref[...])).astype(BF16), sd_ref[...])
    pos = pos_ref[...]
    _drain_items(unit_ref[2, 0], unit_ref[2, UNIT_LANES // 2], item_copy(slot), unit_ref)
    for r0 in range(0, rows, SORT_CHUNK):
        lane = (lax.broadcasted_iota(I32, (td, SORT_CHUNK), 1) + r0).astype(jnp.int16)
        onehot = jnp.zeros((td, SORT_CHUNK), BF16)
        for k in range(pos.shape[1]):
            onehot = jnp.where(lane == pos[:, k:k + 1].astype(jnp.int16), jnp.ones((), BF16), onehot)
        acc = acc + _dot(onehot, s_scr[slot, r0:r0 + SORT_CHUNK, :])
    o_ref[...] = _ln(alpha * x_ref[...] + gate2_ref[...] * acc) * lng_ref[...] + lnb_ref[...]


def _combine(unit_dst, pos_tok, xn, h2, modp, sh_gate, sh_up, sh_down, ln_g, ln_b, ys, seg_td, *, alpha, n_exp):
    t, d = xn.shape
    nt = t // TD
    tok = pl.BlockSpec((TD, d), lambda i: (i, 0))
    full = lambda a: pl.BlockSpec(a.shape, lambda i: (0,) * a.ndim)
    return pl.pallas_call(
        functools.partial(_combine_kernel, alpha=alpha),
        grid=(nt,),
        in_specs=[pl.BlockSpec((8, UNIT_LANES), lambda i: (0, i), memory_space=pltpu.SMEM),
                  pl.BlockSpec((8, UNIT_LANES), lambda i: (0, jnp.minimum(i + 1, nt - 1)), memory_space=pltpu.SMEM),
                  pl.BlockSpec((TD, pos_tok.shape[1]), lambda i: (i, 0)),
                  tok, tok,
                  pl.BlockSpec((None, None, 1, d), lambda i: (5, seg_td(i), 0, 0)),
                  full(sh_gate), full(sh_up), full(sh_down), full(ln_g), full(ln_b),
                  pl.BlockSpec(memory_space=pl.ANY)],
        out_specs=tok,
        out_shape=jax.ShapeDtypeStruct((t, d), F32),
        scratch_shapes=[pltpu.VMEM((2, _tile_rows(n_exp), d), BF16), pltpu.SemaphoreType.DMA((2, 2))],
        compiler_params=_cparams(("arbitrary",)),
        name="moe_combine",
    )(unit_dst, unit_dst, pos_tok, xn, h2, modp, sh_gate, sh_up, sh_down, ln_g, ln_b, ys)


def _moe(xn, h2, scores_t, modp, seg_td, router_bias, w_gate, w_up, w_down, sh_gate, sh_up, sh_down,
         ln_g, ln_b, tri, ltri, *, alpha, layer):
    t = xn.shape[0]
    n_exp = scores_t.shape[0]
    assert _tile_rows(n_exp) % SORT_CHUNK == 0
    assert _tile_rows(n_exp) <= UNIT * UNIT_LANES and n_exp <= UNIT_LANES // 2
    assert _tile_rows(n_exp) < 2 ** 15
    pos, w_sel, run_start, run_len, units, tot = _route(scores_t, router_bias.reshape(n_exp, 1), tri, ltri)
    as_rows = lambda a: jnp.tile(a[:, ::128].T, (1, 2))[:, None, :]
    used = tot[:, 0].astype(I32)
    region = (used + MOE_BLK - 1) // MOE_BLK * MOE_BLK
    region_end = jnp.cumsum(region)
    region_start = region_end - region
    n_slots = TOP_K * t + n_exp * UNIT * (t // TD) + n_exp * MOE_BLK
    unit_dst = _unit_dest((region_start // UNIT).astype(I32), units)
    xs = _dispatch(((region_start + used) // UNIT).astype(I32), (region_end // UNIT).astype(I32),
                   unit_dst[2, ::UNIT_LANES], unit_dst[2, UNIT_LANES // 2::UNIT_LANES], unit_dst, pos, w_sel,
                   as_rows(run_start), as_rows(run_len), h2,
                   n_slots)
    ys = _experts((region_start // MOE_BLK).astype(I32), (region // MOE_BLK).astype(I32),
                  (region_end[-1:] // MOE_BLK).astype(I32), xs, w_gate, w_up, w_down, layer)
    return _combine(unit_dst, pos.T, xn, h2, modp, sh_gate, sh_up, sh_down, ln_g, ln_b, ys, seg_td,
                    alpha=alpha, n_exp=n_exp)


def kernel(x, c, ctx, c_ctx, w_mod, b_mod, w_in, b_in, lam_q1, lam_k1, lam_q2, lam_k2, attn_norm_g, sg_ln_g, sg_ln_b, sg_w, sg_b, conv_w, conv_b, lru_wa, lru_ba, lru_wx, lru_bx, lru_lam, w_branch, w_out, ln1_g, ln1_b, w_router, router_bias, moe_w_gate, moe_w_up, moe_w_down, sh_w_gate, sh_w_up, sh_w_down, ln2_g, ln2_b):
    batch, n_lat, d = x.shape
    n_ctx = ctx.shape[1]
    depth = w_mod.shape[0]
    n_exp = w_router.shape[2]
    t_lat = batch * n_lat
    t_ctx = batch * n_ctx
    t_all = t_lat + t_ctx
    assert n_lat % TM == 0 and t_ctx % TM == 0 and n_ctx % B_CHUNK == 0 and t_lat % n_ctx == 0
    assert batch + 1 <= MOD_ROWS and TM % TD == 0
    alpha = (2 * depth) ** 0.25
    cw = conv_w.shape[2]
    tiles_per_batch = n_lat // TM

    seg = lambda i: jnp.minimum(i // tiles_per_batch, batch)
    seg_td = lambda i: jnp.minimum(i // (n_lat // TD), batch)
    tm_in = 2 * TM if (n_lat % (2 * TM) == 0 and t_ctx % (2 * TM) == 0) else TM
    seg_in = lambda i: jnp.minimum(i // (n_lat // tm_in), batch)
    tbl_idx = lambda i: jnp.where(i < t_lat // tm_in, i % (n_lat // tm_in), n_lat // tm_in)

    x_all = jnp.concatenate([x.reshape(t_lat, d), ctx.reshape(t_ctx, d)], axis=0)
    c_all = jnp.zeros((MOD_ROWS, d), F32).at[:batch].set(c).at[batch].set(c_ctx)
    tables = _rope_tables(n_lat, tm_in)
    tri = (jnp.arange(TD)[:, None] <= jnp.arange(TD)[None, :]).astype(BF16)
    ltri = (jnp.arange(n_exp)[None, :] < jnp.arange(n_exp)[:, None]).astype(BF16)
    row = lambda v: v.reshape(1, -1)

    def dense_blocks(w):
        nb, bi, bj = w.shape
        eye = jnp.eye(nb, dtype=w.dtype)
        return (w[:, :, None, :] * eye[:, None, :, None]).reshape(nb * bi, nb * bj)

    for l in range(depth):
        last = l == depth - 1
        lam_init = 0.8 - 0.6 * math.exp(-0.3 * l)
        mod = _mod_rows(c_all, w_mod[l].astype(BF16), row(b_mod[l]))
        modp = mod.reshape(MOD_ROWS, 6, 1, d).transpose(1, 0, 2, 3)

        qr, kr, vb, proj = _in_proj(x_all, modp, w_in[l].astype(BF16), row(b_in[l]), tables, seg_in, tbl_idx, tm_in)

        lam_vecs = jnp.stack([lam_q1[l], lam_k1[l], lam_q2[l], lam_k2[l]])
        gain = attn_norm_g[l].reshape(A_HEADS, A_V_DIM, 1)
        attn = functools.partial(_attention, qr, kr, vb.T, lam_vecs, gain, lam_init=lam_init,
                                 batch=batch, n_lat=n_lat, n_ctx=n_ctx)
        n_rows = t_lat if last else t_all
        o_a = attn(latent_queries=True, out_rows=n_rows)
        if not last:
            o_a = attn(latent_queries=False, out_rows=n_rows, into=o_a)

        o_b = _spatial_gating(proj, row(sg_ln_g[l]), row(sg_ln_b[l]), sg_w[l].astype(BF16),
                              sg_b[l].reshape(B_GROUPS, B_CHUNK, 1))

        lru = functools.partial(_lru_pass, proj, conv_w=conv_w[l], conv_b=row(conv_b[l]), batch=batch)
        zeros_h = jnp.zeros((batch, 8, cw), F32)
        hf = {}
        for direction in range(2):
            wd = jnp.concatenate([dense_blocks(lru_wa[l, direction]), dense_blocks(lru_wx[l, direction])],
                                 axis=1).astype(BF16)
            par = dict(w_dense=wd, ba=row(lru_ba[l, direction]), bx=row(lru_bx[l, direction]),
                       lam=row(lru_lam[l, direction]), reverse=direction == 1, final=direction == 1)
            shared = direction == 1 and not last
            h_ctx, edge = lru(hf.get("ctx"), zeros_h, seq=n_ctx, tl=n_ctx, row0=t_lat,
                              out_rows=t_all if shared else None, out_row0=t_lat if shared else 0, **par)
            h_lat, _ = lru(hf.get("lat"), edge, seq=n_lat, tl=TM, row0=0,
                           out_rows=t_all if shared else None, into=h_ctx if shared else None, **par)
            hf = {"ctx": h_ctx, "lat": h_lat}
        o_c = hf["lat"]
        wr_t = w_router[l].T
        wr_hi = wr_t.astype(BF16)
        wr_lo = (wr_t - wr_hi.astype(F32)).astype(BF16)
        xn, h2, scores_t = _merge(x_all, o_a, o_b, o_c, proj, modp, w_branch[l].astype(BF16),
                                  w_out[l].astype(BF16), row(ln1_g[l]), row(ln1_b[l]), wr_hi, wr_lo, seg,
                                  n_rows=n_rows, alpha=alpha)
        x_all = _moe(xn, h2, scores_t, modp, seg_td, router_bias[l], moe_w_gate,
                     moe_w_up, moe_w_down, sh_w_gate[l].astype(BF16),
                     sh_w_up[l].astype(BF16), sh_w_down[l].astype(BF16), row(ln2_g[l]), row(ln2_b[l]), tri, ltri,
                     alpha=alpha, layer=l)
    return x_all[:t_lat].reshape(batch, n_lat, d)
```

```python
import functools
import math

import jax
import jax.numpy as jnp
from jax import lax
from jax.experimental import pallas as pl
from jax.experimental.pallas import tpu as pltpu

F32 = jnp.float32
BF16 = jnp.bfloat16
I32 = jnp.int32

A_HEADS = 4
A_QK_DIM = 64
A_V_DIM = 2 * A_QK_DIM
GRID_W = 64
ROPE_THETA = 10000.0
B_CHUNK = 128
B_GROUPS = 4
C_BLOCKS = 8
C_POW = 8.0
TOP_K = 8
N_GROUPS = 8
TOPK_GROUPS = 4
ROUTED_SCALE = 2.5
LN_EPS = 1e-6
RMS_EPS = 1e-5

TM = 512
MOE_BLK = 512
TD = 256
UNIT = 16
UNIT_LANES = 256
SORT_CHUNK = 512
W_LANES = 128
EXPERT_BUFS = 4
LRU_BLOCK = 64
HALO = 16
ATTN_HEADS = 4
ATTN_SUB = 1024
MOD_ROWS = 16
VMEM_LIMIT = 56 * 1024 * 1024


def _cparams(sem):
    return pltpu.CompilerParams(dimension_semantics=sem, vmem_limit_bytes=VMEM_LIMIT)


def _ln(x):
    mu = jnp.mean(x, axis=-1, keepdims=True)
    xc = x - mu
    var = jnp.mean(xc * xc, axis=-1, keepdims=True)
    return xc * lax.rsqrt(var + LN_EPS)


def _gelu(x):
    cdf = 0.5 * (1.0 + jnp.tanh(math.sqrt(2.0 / math.pi) * (x + 0.044715 * (x * x * x))))
    return x * cdf


def _sigmoid(x):
    return 0.5 * jnp.tanh(0.5 * x) + 0.5


def _sigmoid_rel(x):
    return 1.0 / (1.0 + jnp.exp(-x))


def _silu(x):
    return x * _sigmoid(x)


def _dot(a, b):
    return jnp.dot(a, b, preferred_element_type=F32)


def _dot_nt(a, b):
    return lax.dot_general(a, b, (((1,), (1,)), ((), ())), preferred_element_type=F32)


def _mod_kernel(c_ref, w_ref, b_ref, o_ref):
    o_ref[...] = _dot(_silu(c_ref[...]).astype(BF16), w_ref[...]) + b_ref[...]


def _mod_rows(c_all, w, b):
    m, d = c_all.shape
    n = w.shape[1]
    tn = 1536
    return pl.pallas_call(
        _mod_kernel,
        grid=(n // tn,),
        in_specs=[pl.BlockSpec((m, d), lambda j: (0, 0)),
                  pl.BlockSpec((d, tn), lambda j: (0, j)),
                  pl.BlockSpec((1, tn), lambda j: (0, j))],
        out_specs=pl.BlockSpec((m, tn), lambda j: (0, j)),
        out_shape=jax.ShapeDtypeStruct((m, n), F32),
        compiler_params=_cparams(("parallel",)),
        name="adaln_rows",
    )(c_all, w, b)


REST_TILE = 1280
QKV_TILES = 3
COL_U, COL_S, COL_X, COL_Y, COL_G = 0, 1, 2, 3, 4


def _in_proj_kernel(x_ref, sh_ref, sc_ref, w_ref, b_ref, cos_ref, sa_ref, sb_ref,
                    q_ref, k_ref, v_ref, h_ref):
    j = pl.program_id(1)

    @pl.when(j == 0)
    def _():
        h = _ln(x_ref[...]) * (1.0 + sc_ref[...]) + sh_ref[...]
        h_ref[...] = h.astype(BF16)

    acc = _dot(h_ref[...], w_ref[...]) + b_ref[...]

    def rope(dst_ref, scale):
        hw = cos_ref.shape[1]
        for c0 in range(0, acc.shape[1], hw):
            x = acc[:, c0:c0 + hw]
            r = x * cos_ref[...] + pltpu.roll(x, hw - 16, 1) * sa_ref[...] + pltpu.roll(x, 16, 1) * sb_ref[...]
            dst_ref[:, c0:c0 + hw] = (r * scale).astype(BF16)

    @pl.when(j == 0)
    def _():
        rope(q_ref, A_QK_DIM ** -0.5 * math.log2(math.e))

    @pl.when(j == 1)
    def _():
        rope(k_ref, 1.0)

    @pl.when(j == 2)
    def _():
        v_ref[...] = acc.astype(BF16)


def _rest_proj_kernel(h_ref, w_ref, b_ref, o_ref):
    o_ref[...] = (_dot(h_ref[...], w_ref[...]) + b_ref[...]).astype(o_ref.dtype)


def _in_proj(x_all, modp, w, b, tables, seg, tbl_idx, tm):
    t, d = x_all.shape
    tn = 2 * A_HEADS * A_QK_DIM
    n_qkv = QKV_TILES * tn
    n_rest = w.shape[1] - n_qkv
    cos, sa, sb = tables
    tbl = pl.BlockSpec((tm, cos.shape[1]), lambda i, j: (tbl_idx(i), 0))
    qkv = pl.BlockSpec((tm, tn), lambda i, j: (i, 0))
    qkv_shape = jax.ShapeDtypeStruct((t, tn), BF16)
    q, k, v, h = pl.pallas_call(
        _in_proj_kernel,
        grid=(t // tm, QKV_TILES),
        in_specs=[pl.BlockSpec((tm, d), lambda i, j: (i, 0)),
                  pl.BlockSpec((None, None, 1, d), lambda i, j: (0, seg(i), 0, 0)),
                  pl.BlockSpec((None, None, 1, d), lambda i, j: (1, seg(i), 0, 0)),
                  pl.BlockSpec((d, tn), lambda i, j: (0, j)),
                  pl.BlockSpec((1, tn), lambda i, j: (0, j)),
                  tbl, tbl, tbl],
        out_specs=[qkv, qkv, qkv, pl.BlockSpec((tm, d), lambda i, j: (i, 0))],
        out_shape=[qkv_shape, qkv_shape, qkv_shape, jax.ShapeDtypeStruct((t, d), BF16)],
        compiler_params=_cparams(("parallel", "arbitrary")),
        name="in_proj_qkv",
    )(x_all, modp, modp, w, b, cos, sa, sb)
    tr = REST_TILE if n_rest % REST_TILE == 0 else tn
    rest = pl.pallas_call(
        _rest_proj_kernel,
        grid=(t // tm, n_rest // tr),
        in_specs=[pl.BlockSpec((tm, d), lambda i, j: (i, 0)),
                  pl.BlockSpec((d, tr), lambda i, j: (0, j)),
                  pl.BlockSpec((1, tr), lambda i, j: (0, j))],
        out_specs=pl.BlockSpec((tm, tr), lambda i, j: (i, j)),
        out_shape=jax.ShapeDtypeStruct((t, n_rest), BF16),
        compiler_params=_cparams(("parallel", "parallel")),
        name="in_proj_rest",
    )(h, w[:, n_qkv:], b[:, n_qkv:])
    return q, k, v, rest


def _rope_tables(n, tm):
    rows = n // GRID_W
    pos_row = jnp.repeat(jnp.arange(rows), GRID_W).astype(F32)
    pos_col = jnp.tile(jnp.arange(GRID_W), rows).astype(F32)
    quarter = A_QK_DIM // 4
    inv = ROPE_THETA ** (-jnp.arange(quarter, dtype=F32) / quarter)
    ang_r = pos_row[:, None] * inv
    ang_c = pos_col[:, None] * inv
    ang = jnp.concatenate([ang_r, ang_r, ang_c, ang_c], axis=-1)
    cos = jnp.cos(ang)
    sin = jnp.sin(ang)
    first_half = (jnp.arange(A_QK_DIM) % (2 * quarter)) < quarter
    sa = jnp.where(first_half, -sin, 0.0)
    sb = jnp.where(first_half, 0.0, sin)

    def widen(tbl, fill):
        tbl = jnp.tile(tbl, (1, 2))
        return jnp.concatenate([tbl, jnp.full((tm, tbl.shape[1]), fill, F32)], axis=0)

    return widen(cos, 1.0), widen(sa, 0.0), widen(sb, 0.0)


def _attn_kernel(*refs, lam_init, with_lat, sub):
    if with_lat:
        lam_ref, gain_ref, q_ref, kc_ref, vc_ref, kl_ref, vl_ref, o_ref = refs
    else:
        lam_ref, gain_ref, q_ref, kc_ref, vc_ref, _, o_ref = refs
    lv = lam_ref[...]
    lam = (jnp.exp(jnp.sum(lv[0:1] * lv[1:2], axis=-1, keepdims=True))
           - jnp.exp(jnp.sum(lv[2:3] * lv[3:4], axis=-1, keepdims=True)) + lam_init)
    chunks = [(kc_ref, vc_ref, 0, kc_ref.shape[0])]
    if with_lat:
        ck = min(sub, kl_ref.shape[0])
        chunks += [(kl_ref, vl_ref, r0, ck) for r0 in range(0, kl_ref.shape[0], ck)]
    for hh in range(gain_ref.shape[0]):
        head = slice(hh * A_V_DIM, (hh + 1) * A_V_DIM)
        probs, dens = [], []
        for c in range(2):
            cols = slice(hh * A_V_DIM + c * A_QK_DIM, hh * A_V_DIM + (c + 1) * A_QK_DIM)
            qc = q_ref[:, cols]
            scores = [_dot_nt(k_ref[r0:r0 + n, cols], qc) for k_ref, _, r0, n in chunks]
            m = None
            for s in scores:
                ms = jnp.max(s, axis=0, keepdims=True)
                m = ms if m is None else jnp.maximum(m, ms)
            p = [jnp.exp2(s - m) for s in scores]
            den = None
            for pj in p:
                ds = jnp.sum(pj, axis=0, keepdims=True)
                den = ds if den is None else den + ds
            probs.append(p)
            dens.append(den)
        ratio = lam * dens[0] / dens[1]
        o = None
        for p1, p2, (_, v_ref, r0, n) in zip(probs[0], probs[1], chunks):
            w = p1 - p2 * ratio
            ps = _dot(v_ref[head, r0:r0 + n], w.astype(BF16))
            o = ps if o is None else o + ps
        o = o / dens[0]
        o = o * lax.rsqrt(jnp.mean(o * o, axis=0, keepdims=True) + RMS_EPS)
        o_ref[:, head] = ((o * gain_ref[hh]) * (1.0 - lam_init)).T


def _attention(qr, kr, vt, lam_vecs, gain, *, lam_init, batch, n_lat, n_ctx, latent_queries, out_rows, into=None):
    hw = ATTN_HEADS * A_V_DIM
    ctx_blk0 = batch * n_lat // n_ctx
    lam_spec = pl.BlockSpec(lam_vecs.shape, lambda b, h, i: (0, 0))
    gain_spec = pl.BlockSpec((ATTN_HEADS, A_V_DIM, 1), lambda b, h, i: (h, 0, 0))
    kv_ctx = pl.BlockSpec((n_ctx, hw), lambda b, h, i: (ctx_blk0 + b, h))
    vt_ctx = pl.BlockSpec((hw, n_ctx), lambda b, h, i: (h, ctx_blk0 + b))
    if latent_queries:
        tq = 256
        nq = n_lat // tq
        q_spec = pl.BlockSpec((tq, hw), lambda b, h, i: (b * nq + i, h))
        kv_lat = pl.BlockSpec((n_lat, hw), lambda b, h, i: (b, h))
        vt_lat = pl.BlockSpec((hw, n_lat), lambda b, h, i: (h, b))
        in_specs = [lam_spec, gain_spec, q_spec, kv_ctx, vt_ctx, kv_lat, vt_lat]
        args = (lam_vecs, gain, qr, kr, vt, kr, vt)
        out_blk0 = 0
        aliases = {}
    else:
        tq = n_ctx
        nq = 1
        q_spec = pl.BlockSpec((tq, hw), lambda b, h, i: (ctx_blk0 + b, h))
        in_specs = [lam_spec, gain_spec, q_spec, kv_ctx, vt_ctx, pl.BlockSpec(memory_space=pl.ANY)]
        args = (lam_vecs, gain, qr, kr, vt, into)
        out_blk0 = ctx_blk0
        aliases = {len(args) - 1: 0}
    return pl.pallas_call(
        functools.partial(_attn_kernel, lam_init=lam_init, with_lat=latent_queries, sub=ATTN_SUB),
        grid=(batch, A_HEADS // ATTN_HEADS, nq),
        in_specs=in_specs,
        out_specs=pl.BlockSpec((tq, hw), lambda b, h, i: (out_blk0 + b * nq + i, h)),
        out_shape=jax.ShapeDtypeStruct((out_rows, A_HEADS * A_V_DIM), F32),
        input_output_aliases=aliases,
        compiler_params=_cparams(("parallel", "parallel", "arbitrary")),
        name="diff_attn_lat" if latent_queries else "diff_attn_ctx",
    )(*args)


def _sg_kernel(u_ref, s_ref, g_ref, b_ref, w_ref, bs_ref, o_ref):
    gw = B_CHUNK
    for ci in range(TM // B_CHUNK):
        rows = slice(ci * B_CHUNK, (ci + 1) * B_CHUNK)
        for g in range(B_GROUPS):
            cols = slice(g * gw, (g + 1) * gw)
            vn = _ln(_gelu(s_ref[rows, cols].astype(F32))) * g_ref[:, cols] + b_ref[:, cols]
            mixed = _dot(w_ref[g], vn.astype(BF16)) + bs_ref[g]
            o_ref[rows, cols] = _gelu(u_ref[rows, cols].astype(F32)) * mixed


def _spatial_gating(proj, ln_g, ln_b, w_s, b_s):
    t = proj.shape[0]
    w = B_GROUPS * B_CHUNK
    full = lambda a: pl.BlockSpec(a.shape, lambda i: (0,) * a.ndim)
    return pl.pallas_call(
        _sg_kernel,
        grid=(t // TM,),
        in_specs=[pl.BlockSpec((TM, w), lambda i: (i, COL_U)),
                  pl.BlockSpec((TM, w), lambda i: (i, COL_S)),
                  full(ln_g), full(ln_b), full(w_s), full(b_s)],
        out_specs=pl.BlockSpec((TM, w), lambda i: (i, 0)),
        out_shape=jax.ShapeDtypeStruct((t, w), F32),
        compiler_params=_cparams(("parallel",)),
        name="spatial_gating",
    )(proj, proj, ln_g, ln_b, w_s, b_s)


def _lru_kernel(*refs, reverse, final, n_tiles, tl, aliased):
    if aliased:
        refs = refs[:12] + refs[13:]
    if final:
        (x_ref, xp_ref, xn_ref, cw_ref, cb_ref, w_ref, ba_ref, bx_ref, lam_ref, h0_ref,
         hf_ref, y_ref, o_ref, hl_ref, carry) = refs
    else:
        (x_ref, xp_ref, xn_ref, cw_ref, cb_ref, w_ref, ba_ref, bx_ref, lam_ref, h0_ref,
         o_ref, hl_ref, carry) = refs
    step = pl.program_id(1)

    @pl.when(step == 0)
    def _():
        carry[...] = h0_ref[...]

    ti = (n_tiles - 1 - step) if reverse else step
    x = x_ref[...].astype(F32)
    cw = x.shape[-1]
    row = lax.broadcasted_iota(I32, x.shape, 0)
    has_prev = ti > 0
    has_next = ti < n_tiles - 1
    p6 = jnp.where(has_prev, xp_ref[HALO - 2:HALO - 1, :].astype(F32), 0.0)
    p7 = jnp.where(has_prev, xp_ref[HALO - 1:HALO, :].astype(F32), 0.0)
    n0 = jnp.where(has_next, xn_ref[0:1, :].astype(F32), 0.0)
    xm1 = jnp.where(row == 0, p7, pltpu.roll(x, 1, 0))
    xm2 = jnp.where(row == 0, p6, jnp.where(row == 1, p7, pltpu.roll(x, 2, 0)))
    xp1 = jnp.where(row == tl - 1, n0, pltpu.roll(x, tl - 1, 0))
    taps = cw_ref[...]
    xs = taps[0:1] * xm2 + taps[1:2] * xm1 + taps[2:3] * x + taps[3:4] * xp1 + cb_ref[...]

    z = _dot(xs.astype(BF16), w_ref[...])
    r = _sigmoid_rel(z[:, :cw] + ba_ref[...])
    ig = _sigmoid_rel(z[:, cw:] + bx_ref[...])
    nl = -lam_ref[...]
    softplus = jnp.maximum(nl, 0.0) + jnp.log(1.0 + jnp.exp(-jnp.abs(nl)))
    log_a = -C_POW * r * softplus
    a = jnp.exp(log_a)
    u = jnp.sqrt(-jnp.tanh(log_a) * (a * a + 1.0)) * (ig * xs)

    sb = min(LRU_BLOCK, tl)
    in_blk = row % sb
    d = 1
    while d < sb:
        if reverse:
            ok = in_blk < sb - d
            shift = tl - d
        else:
            ok = in_blk >= d
            shift = d
        a_sh = jnp.where(ok, pltpu.roll(a, shift, 0), 1.0)
        u_sh = jnp.where(ok, pltpu.roll(u, shift, 0), 0.0)
        u = u + a * u_sh
        a = a * a_sh
        d *= 2
    state = carry[0:1, :]
    blocks = [None] * (tl // sb)
    for b in (reversed(range(tl // sb)) if reverse else range(tl // sb)):
        hb = u[b * sb:(b + 1) * sb] + a[b * sb:(b + 1) * sb] * state
        state = hb[0:1, :] if reverse else hb[sb - 1:sb, :]
        blocks[b] = hb
    h = jnp.concatenate(blocks, axis=0)
    edge = state
    carry[...] = jnp.broadcast_to(edge, carry.shape)
    hl_ref[...] = jnp.broadcast_to(edge, hl_ref.shape)
    if final:
        o_ref[...] = _gelu(y_ref[...].astype(F32)) * (hf_ref[...] + h)
    else:
        o_ref[...] = h


def _lru_pass(proj, hf, h0, conv_w, conv_b, w_dense, ba, bx, lam, *, batch, seq, tl, row0, reverse, final,
              out_rows=None, out_row0=0, into=None):
    t_all = proj.shape[0]
    out_rows = batch * seq if out_rows is None else out_rows
    out_blk0 = out_row0 // tl
    assert into is None or final
    cw = conv_w.shape[1]
    n_tiles = seq // tl
    base = row0 // tl
    sub = tl // HALO
    last_halo = t_all // HALO - 1

    def tile(b, s):
        ti = (n_tiles - 1 - s) if reverse else s
        return base + b * n_tiles + ti

    def loc(b, s):
        ti = (n_tiles - 1 - s) if reverse else s
        return b * n_tiles + ti

    full = lambda a: pl.BlockSpec(a.shape, lambda b, s: (0,) * a.ndim)
    in_specs = [pl.BlockSpec((tl, cw), lambda b, s: (tile(b, s), COL_X)),
                pl.BlockSpec((HALO, cw), lambda b, s: (jnp.maximum(tile(b, s) * sub - 1, 0), COL_X)),
                pl.BlockSpec((HALO, cw), lambda b, s: (jnp.minimum((tile(b, s) + 1) * sub, last_halo), COL_X)),
                full(conv_w), full(conv_b), full(w_dense), full(ba), full(bx), full(lam),
                pl.BlockSpec((None, 8, cw), lambda b, s: (b, 0, 0))]
    args = [proj, proj, proj, conv_w, conv_b, w_dense, ba, bx, lam, h0]
    if final:
        in_specs += [pl.BlockSpec((tl, cw), lambda b, s: (loc(b, s), 0)),
                     pl.BlockSpec((tl, cw), lambda b, s: (tile(b, s), COL_Y))]
        args += [hf, proj]
    aliases = {}
    if into is not None:
        aliases = {len(args): 0}
        in_specs += [pl.BlockSpec(memory_space=pl.ANY)]
        args += [into]
    return pl.pallas_call(
        functools.partial(_lru_kernel, reverse=reverse, final=final, n_tiles=n_tiles, tl=tl,
                          aliased=into is not None),
        grid=(batch, n_tiles),
        in_specs=in_specs,
        out_specs=[pl.BlockSpec((tl, cw), lambda b, s: (out_blk0 + loc(b, s), 0)),
                   pl.BlockSpec((None, 8, cw), lambda b, s: (b, 0, 0))],
        out_shape=[jax.ShapeDtypeStruct((out_rows, cw), F32),
                   jax.ShapeDtypeStruct((batch, 8, cw), F32)],
        scratch_shapes=[pltpu.VMEM((8, cw), F32)],
        input_output_aliases=aliases,
        compiler_params=_cparams(("parallel", "arbitrary")),
        name="rglru_%s_%s" % ("bwd" if reverse else "fwd", "lat" if row0 == 0 else "ctx"),
    )(*args)


def _merge_kernel(x_ref, oa_ref, ob_ref, oc_ref, g0, g1, g2, g3, g4, g5, wb_ref, wo_ref,
                  gate1_ref, sh2_ref, sc2_ref, lng_ref, lnb_ref, wrh_ref, wrl_ref,
                  xn_ref, h2_ref, sc_ref, *, alpha):
    gates = ((g0, g1), (g2, g3), (g4, g5))
    mix = None
    for r, o_ref in enumerate((oa_ref, ob_ref, oc_ref)):
        proj = _dot(o_ref[...].astype(BF16), wb_ref[r])
        gate = jnp.concatenate([_sigmoid(gates[r][0][...].astype(F32)), _sigmoid(gates[r][1][...].astype(F32))],
                               axis=-1)
        term = gate * proj
        mix = term if mix is None else mix + term
    out = _dot(mix.astype(BF16), wo_ref[...])
    xn = _ln(alpha * x_ref[...] + gate1_ref[...] * out) * lng_ref[...] + lnb_ref[...]
    xn_ref[...] = xn
    h2 = _ln(xn) * (1.0 + sc2_ref[...]) + sh2_ref[...]
    h_hi = h2.astype(BF16)
    h2_ref[...] = h_hi
    h_lo = (h2 - h_hi.astype(F32)).astype(BF16)
    logits = _dot_nt(wrh_ref[...], h_hi) + (_dot_nt(wrh_ref[...], h_lo) + _dot_nt(wrl_ref[...], h_hi))
    sc_ref[...] = _sigmoid_rel(logits)


def _merge(x_all, o_a, o_b, o_c, proj, modp, w_branch, w_out, ln_g, ln_b, wr_hi, wr_lo, seg, *, n_rows, alpha):
    d = x_all.shape[1]
    bw = o_a.shape[1]
    n_exp = wr_hi.shape[0]
    tok = lambda w: pl.BlockSpec((TM, w), lambda i: (i, 0))
    gate = lambda c: pl.BlockSpec((TM, bw), lambda i, c=c: (i, c))
    mod = lambda k: pl.BlockSpec((None, None, 1, d), lambda i, k=k: (k, seg(i), 0, 0))
    full = lambda a: pl.BlockSpec(a.shape, lambda i: (0,) * a.ndim)
    return pl.pallas_call(
        functools.partial(_merge_kernel, alpha=alpha),
        grid=(n_rows // TM,),
        in_specs=[tok(d), tok(bw), tok(bw), tok(bw)] + [gate(COL_G + c) for c in range(6)]
                 + [full(w_branch), full(w_out), mod(2), mod(3), mod(4), full(ln_g), full(ln_b),
                    full(wr_hi), full(wr_lo)],
        out_specs=[tok(d), tok(d), pl.BlockSpec((n_exp, TM), lambda i: (0, i))],
        out_shape=[jax.ShapeDtypeStruct((n_rows, d), F32),
                   jax.ShapeDtypeStruct((n_rows, d), BF16),
                   jax.ShapeDtypeStruct((n_exp, n_rows), F32)],
        compiler_params=_cparams(("parallel",)),
        name="merge_residual_router",
    )(x_all, o_a, o_b, o_c, proj, proj, proj, proj, proj, proj, w_branch, w_out,
      modp, modp, modp, ln_g, ln_b, wr_hi, wr_lo)


def _tile_rows(n_exp):
    return TOP_K * TD + n_exp * UNIT


def _route_kernel(s_ref, bias_ref, tri_ref, ltri_ref, pos_ref, w_ref, rs_ref, rn_ref, unit_ref, tot_ref, carry):
    @pl.when(pl.program_id(0) == 0)
    def _():
        carry[...] = jnp.zeros_like(carry)

    s = s_ref[...]
    n_exp, tn = s.shape
    per = n_exp // N_GROUPS
    neg = -jnp.inf
    biased = s + bias_ref[...]
    sub = lax.broadcasted_iota(I32, (per, tn), 0)
    gs_rows = []
    for g in range(N_GROUPS):
        blk = biased[g * per:(g + 1) * per, :]
        m1 = jnp.max(blk, axis=0, keepdims=True)
        first = jnp.min(jnp.where(blk == m1, sub, per), axis=0, keepdims=True)
        m2 = jnp.max(jnp.where(sub == first, neg, blk), axis=0, keepdims=True)
        gs_rows.append(m1 + m2)
    gs = jnp.concatenate(gs_rows, axis=0)
    gi = lax.broadcasted_iota(I32, gs.shape, 0)
    g_ok = jnp.zeros(gs.shape, F32)
    cur = gs
    for _ in range(TOPK_GROUPS):
        m = jnp.max(cur, axis=0, keepdims=True)
        pick = jnp.min(jnp.where(cur == m, gi, N_GROUPS), axis=0, keepdims=True)
        hit = gi == pick
        g_ok = jnp.where(hit, 1.0, g_ok)
        cur = jnp.where(hit, neg, cur)
    ok_rows = [jnp.broadcast_to(g_ok[g:g + 1, :], (per, tn)) for g in range(N_GROUPS)]
    expert_ok = jnp.concatenate(ok_rows, axis=0)
    masked = jnp.where(expert_ok > 0.0, biased, neg)
    ei = lax.broadcasted_iota(I32, s.shape, 0)
    pick_s, hits = [], []
    sel = jnp.zeros(s.shape, F32)
    for _ in range(TOP_K):
        m = jnp.max(masked, axis=0, keepdims=True)
        pick = jnp.min(jnp.where(masked == m, ei, n_exp), axis=0, keepdims=True)
        hit = ei == pick
        hits.append(hit)
        pick_s.append(jnp.sum(jnp.where(hit, s, 0.0), axis=0, keepdims=True))
        sel = jnp.where(hit, 1.0, sel)
        masked = jnp.where(hit, neg, masked)
    tot = pick_s[0]
    for k in range(1, TOP_K):
        tot = tot + pick_s[k]
    w = jnp.where(sel > 0.0, s, 0.0) / tot * ROUTED_SCALE
    w_hi = w.astype(BF16)
    w_ref[...] = jnp.concatenate([w_hi, (w - w_hi.astype(F32)).astype(BF16)], axis=0)

    incl = _dot(sel.astype(BF16), tri_ref[...])
    count = incl[:, tn - 1:tn]
    run = jnp.floor((count + (UNIT - 1)) * (1.0 / UNIT)) * UNIT
    run_start = _dot(ltri_ref[...], jnp.broadcast_to(run, (n_exp, 128)).astype(BF16))[:, 0:1]
    row_in_tile = incl - sel + run_start
    pos = [jnp.sum(jnp.where(hit, row_in_tile, 0.0), axis=0, keepdims=True) for hit in hits]
    pos_ref[...] = jnp.concatenate(pos, axis=0).astype(I32)
    rs_ref[...] = jnp.broadcast_to(run_start, rs_ref.shape)
    rn_ref[...] = jnp.broadcast_to(run, rn_ref.shape)

    half = UNIT_LANES // 2
    region_used = carry[:, 0:1]
    u0 = run_start * (1.0 / UNIT)
    nu = run * (1.0 / UNIT)
    n_pair = jnp.floor(nu * 0.5)
    n_single = nu - 2.0 * n_pair
    both = jnp.concatenate([jnp.broadcast_to(n_pair, (n_exp, 64)), jnp.broadcast_to(n_single, (n_exp, 64))], axis=1)
    first = _dot(ltri_ref[...], both.astype(BF16))
    pair0, single0 = first[:, 0:1], first[:, 64:65]
    lane = lax.broadcasted_iota(I32, (n_exp, UNIT_LANES), 1)
    is_pair = lane < half
    item = jnp.where(is_pair, lane, lane - half).astype(F32)
    start = jnp.where(is_pair, pair0, single0)
    count = jnp.where(is_pair, n_pair, n_single)
    inside = jnp.logical_and(item >= start, item < start + count)
    off = jnp.where(is_pair, 2.0 * (item - pair0), 2.0 * n_pair)
    eu = lax.broadcasted_iota(I32, (n_exp, UNIT_LANES), 0).astype(F32)
    pick = lambda v: jnp.sum(jnp.where(inside, v, 0.0), axis=0, keepdims=True)
    totals = jnp.where(is_pair[0:1], jnp.sum(n_pair, axis=0, keepdims=True), jnp.sum(n_single, axis=0, keepdims=True))
    pad = jnp.zeros((4, UNIT_LANES), F32)
    unit_ref[...] = jnp.concatenate([pick(off + region_used * (1.0 / UNIT)), pick(eu), pick(off + u0), totals, pad],
                                    axis=0).astype(I32)

    total = region_used + run
    carry[...] = jnp.broadcast_to(total, carry.shape)
    tot_ref[...] = jnp.broadcast_to(total, tot_ref.shape)


def _route(scores_t, bias_col, tri, ltri):
    n_exp, t = scores_t.shape
    nt = t // TD
    return pl.pallas_call(
        _route_kernel,
        grid=(nt,),
        in_specs=[pl.BlockSpec((n_exp, TD), lambda i: (0, i)),
                  pl.BlockSpec((n_exp, 1), lambda i: (0, 0)),
                  pl.BlockSpec((TD, TD), lambda i: (0, 0)),
                  pl.BlockSpec((n_exp, n_exp), lambda i: (0, 0))],
        out_specs=[pl.BlockSpec((TOP_K, TD), lambda i: (0, i)),
                   pl.BlockSpec((2 * n_exp, TD), lambda i: (0, i)),
                   pl.BlockSpec((n_exp, 128), lambda i: (0, i)),
                   pl.BlockSpec((n_exp, 128), lambda i: (0, i)),
                   pl.BlockSpec((8, UNIT_LANES), lambda i: (0, i)),
                   pl.BlockSpec((n_exp, 128), lambda i: (0, 0))],
        out_shape=[jax.ShapeDtypeStruct((TOP_K, t), I32),
                   jax.ShapeDtypeStruct((2 * n_exp, t), BF16),
                   jax.ShapeDtypeStruct((n_exp, nt * 128), F32),
                   jax.ShapeDtypeStruct((n_exp, nt * 128), F32),
                   jax.ShapeDtypeStruct((8, nt * UNIT_LANES), I32),
                   jax.ShapeDtypeStruct((n_exp, 128), F32)],
        scratch_shapes=[pltpu.VMEM((n_exp, 128), F32)],
        compiler_params=_cparams(("arbitrary",)),
        name="route_topk",
    )(scores_t, bias_col, tri, ltri)


def _unit_dest_kernel(start_ref, unit_ref, o_ref, *, n_exp):
    tbl = unit_ref[...]
    exp_of = tbl[1:2, :]
    dst = tbl[0:1, :]
    for e in range(n_exp):
        dst = dst + jnp.where(exp_of == e, start_ref[e], 0)
    o_ref[...] = jnp.concatenate([dst, tbl[2:4, :], jnp.zeros((5, tbl.shape[1]), I32)], axis=0)


def _unit_dest(region_start_units, units):
    nt = units.shape[1] // UNIT_LANES
    per_step = next(k for k in (4, 2, 1) if nt % k == 0)
    blk = lambda: pl.BlockSpec((8, per_step * UNIT_LANES), lambda i, s: (0, i))
    return pl.pallas_call(
        functools.partial(_unit_dest_kernel, n_exp=region_start_units.shape[0]),
        grid_spec=pltpu.PrefetchScalarGridSpec(
            num_scalar_prefetch=1, grid=(nt // per_step,), in_specs=[blk()], out_specs=blk()),
        out_shape=jax.ShapeDtypeStruct(units.shape, I32),
        compiler_params=_cparams(("parallel",)),
        name="route_units",
    )(region_start_units, units)


def _unit_rows(i, n_units=1):
    return pl.ds(pl.multiple_of(i * UNIT, UNIT), n_units * UNIT)


def _copy_items(tbl, make_copy, act):
    half = UNIT_LANES // 2
    _for_each(tbl[2, 0], lambda i: getattr(make_copy(tbl, i, 2), act)())
    _for_each(tbl[2, half], lambda i: getattr(make_copy(tbl, half + i, 1), act)())


def _drain_items(n_pairs, n_singles, make_copy, tbl):
    _for_each(n_pairs, lambda i: make_copy(tbl, 0, 2).wait())
    _for_each(n_singles, lambda i: make_copy(tbl, UNIT_LANES // 2, 1).wait())


def _for_each(count, fn, group=8):
    main = lax.shift_right_logical(count, int(math.log2(group)))

    def many(i, c):
        for r in range(group):
            fn(i * group + r)
        return c

    lax.fori_loop(0, main, many, 0)
    lax.fori_loop(main * group, count, lambda i, c: (fn(i), c)[1], 0)


def _dispatch_kernel(tail_lo_ref, tail_hi_ref, np_ref, ns_ref, unit_ref, pos_ref, w_ref, rs_ref, rn_ref, h_ref,
                     xs_ref, s_scr, zero_scr, sem, zsem, *, n_exp):
    rows = s_scr.shape[1]
    td = h_ref.shape[0]
    step = pl.program_id(0)
    last = pl.num_programs(0) - 1
    slot = step % 2

    def zero_copy(g):
        return pltpu.make_async_copy(zero_scr, xs_ref.at[_unit_rows(g), :], zsem.at[0])

    def item_copy(sl):
        def make(tbl, lane, size):
            return pltpu.make_async_copy(s_scr.at[sl, _unit_rows(tbl[1, lane], size), :],
                                         xs_ref.at[_unit_rows(tbl[0, lane], size), :], sem.at[size - 1, sl])
        return make

    def drain(tile, sl):
        _drain_items(np_ref[tile], ns_ref[tile], item_copy(sl), unit_ref)

    @pl.when(step >= 2)
    def _():
        drain(step - 2, slot)

    @pl.when(step == 0)
    def _():
        zero_scr[...] = jnp.zeros_like(zero_scr)

        def per_expert(e, c):
            lax.fori_loop(tail_lo_ref[e], tail_hi_ref[e], lambda g, cc: (zero_copy(g).start(), cc)[1], 0)
            lax.fori_loop(tail_lo_ref[e], tail_hi_ref[e], lambda g, cc: (zero_copy(g).wait(), cc)[1], 0)
            return c

        lax.fori_loop(0, n_exp, per_expert, 0)

    pos = pos_ref[...]
    h = h_ref[...]
    d = h.shape[1]
    w_dense = w_ref[...]
    run_lo = rs_ref[...]
    run_hi = run_lo + rn_ref[...]
    lane = lax.broadcasted_iota(I32, (SORT_CHUNK, s_scr.shape[2] - d), 1)
    for r0 in range(0, rows, SORT_CHUNK):
        srow = (lax.broadcasted_iota(I32, (SORT_CHUNK, td), 0) + r0).astype(jnp.int16)
        onehot = jnp.zeros((SORT_CHUNK, td), BF16)
        for k in range(pos.shape[0]):
            onehot = jnp.where(srow == pos[k:k + 1, :].astype(jnp.int16), jnp.ones((), BF16), onehot)
        s_e = (lax.broadcasted_iota(I32, (SORT_CHUNK, w_dense.shape[0]), 0) + r0).astype(F32)
        mine = jnp.logical_and(s_e >= run_lo, s_e < run_hi).astype(BF16)
        w_row = jnp.sum(onehot.astype(F32) * _dot(mine, w_dense), axis=1, keepdims=True)
        w_hi = w_row.astype(BF16).astype(F32)
        w_lo = w_row - w_hi
        s_scr[slot, r0:r0 + SORT_CHUNK, :d] = _dot(onehot, h).astype(BF16)
        s_scr[slot, r0:r0 + SORT_CHUNK, d:] = jnp.where(lane == 0, w_hi, jnp.where(lane == 1, w_lo, 0.0)).astype(BF16)

    _copy_items(unit_ref, item_copy(slot), "start")

    @pl.when(step == last)
    def _():
        drain(step, slot)

        @pl.when(step >= 1)
        def _():
            drain(step - 1, 1 - slot)


def _dispatch(tail_lo, tail_hi, n_pairs, n_singles, unit_dst, pos, w_dense, run_start_rows, run_rows, h2, n_slots):
    t, d = h2.shape
    n_exp = tail_lo.shape[0]
    dw = d + W_LANES
    runs = pl.BlockSpec((None, 1, 2 * n_exp), lambda i, a, b, c, e: (i, 0, 0))
    return pl.pallas_call(
        functools.partial(_dispatch_kernel, n_exp=n_exp),
        grid_spec=pltpu.PrefetchScalarGridSpec(
            num_scalar_prefetch=4, grid=(t // TD,),
            in_specs=[pl.BlockSpec((8, UNIT_LANES), lambda i, a, b, c, e: (0, i), memory_space=pltpu.SMEM),
                      pl.BlockSpec((TOP_K, TD), lambda i, a, b, c, e: (0, i)),
                      pl.BlockSpec((2 * n_exp, TD), lambda i, a, b, c, e: (0, i)),
                      runs, runs,
                      pl.BlockSpec((TD, d), lambda i, a, b, c, e: (i, 0))],
            out_specs=pl.BlockSpec(memory_space=pl.ANY),
            scratch_shapes=[pltpu.VMEM((2, _tile_rows(n_exp), dw), BF16), pltpu.VMEM((UNIT, dw), BF16),
                            pltpu.SemaphoreType.DMA((2, 2)), pltpu.SemaphoreType.DMA((1,))]),
        out_shape=jax.ShapeDtypeStruct((n_slots, dw), BF16),
        compiler_params=_cparams(("arbitrary",)),
        name="moe_dispatch",
    )(tail_lo, tail_hi, n_pairs, n_singles, unit_dst, pos, w_dense, run_start_rows, run_rows, h2)


def _expert_kernel(blk0_ref, nblk_ref, tot_ref, wg_ref, wu_ref, wd_ref, xs_ref, ys_ref, xbuf, obuf, wgu_scr, wd_scr,
                   sem_in, sem_out):
    e = pl.program_id(0)
    n = nblk_ref[e]
    b0 = blk0_ref[e]
    de = wg_ref.shape[1]
    wgu_scr[:, :de] = wg_ref[...].astype(BF16)
    wgu_scr[:, de:] = wu_ref[...].astype(BF16)
    wd_scr[...] = wd_ref[...].astype(BF16)

    total = tot_ref[0]

    def rows(g):
        return pl.ds(pl.multiple_of(g * MOE_BLK, MOE_BLK), MOE_BLK)

    def in_copy(g):
        sl = g % EXPERT_BUFS
        return pltpu.make_async_copy(xs_ref.at[rows(g), :], xbuf.at[sl], sem_in.at[sl])

    def out_copy(g):
        sl = g % EXPERT_BUFS
        return pltpu.make_async_copy(obuf.at[sl], ys_ref.at[rows(g), :], sem_out.at[sl])

    @pl.when(e == 0)
    def _():
        for g in range(EXPERT_BUFS - 1):
            @pl.when(g < total)
            def _():
                in_copy(g).start()

    def block(j, c):
        g = b0 + j
        sl = g % EXPERT_BUFS
        in_copy(g).wait()

        @pl.when(g + (EXPERT_BUFS - 1) < total)
        def _():
            in_copy(g + (EXPERT_BUFS - 1)).start()

        @pl.when(g >= EXPERT_BUFS)
        def _():
            out_copy(g - EXPERT_BUFS).wait()

        d = wd_scr.shape[1]
        gu = _dot(xbuf[sl, :, :d], wgu_scr[...])
        act = _silu(gu[:, :de]) * gu[:, de:]
        w_row = xbuf[sl, :, d:d + 1].astype(F32) + xbuf[sl, :, d + 1:d + 2].astype(F32)
        obuf[sl] = (_dot(act.astype(BF16), wd_scr[...]) * w_row).astype(BF16)
        out_copy(g).start()
        return c

    lax.fori_loop(0, n, block, 0)

    @pl.when(e == pl.num_programs(0) - 1)
    def _():
        for back in range(1, EXPERT_BUFS + 1):
            @pl.when(total >= back)
            def _():
                out_copy(total - back).wait()


def _experts(blk0, nblk, n_blocks, xs, w_gate, w_up, w_down, layer):
    n_slots, dw = xs.shape
    _, n_exp, d, de = w_gate.shape
    return pl.pallas_call(
        _expert_kernel,
        grid_spec=pltpu.PrefetchScalarGridSpec(
            num_scalar_prefetch=3, grid=(n_exp,),
            in_specs=[pl.BlockSpec((None, None, d, de), lambda e, a, b, c: (layer, e, 0, 0)),
                      pl.BlockSpec((None, None, d, de), lambda e, a, b, c: (layer, e, 0, 0)),
                      pl.BlockSpec((None, None, de, d), lambda e, a, b, c: (layer, e, 0, 0)),
                      pl.BlockSpec(memory_space=pl.ANY)],
            out_specs=pl.BlockSpec(memory_space=pl.ANY),
            scratch_shapes=[pltpu.VMEM((EXPERT_BUFS, MOE_BLK, dw), BF16), pltpu.VMEM((EXPERT_BUFS, MOE_BLK, d), BF16),
                            pltpu.VMEM((d, 2 * de), BF16), pltpu.VMEM((de, d), BF16),
                            pltpu.SemaphoreType.DMA((EXPERT_BUFS,)), pltpu.SemaphoreType.DMA((EXPERT_BUFS,))]),
        out_shape=jax.ShapeDtypeStruct((n_slots, d), BF16),
        compiler_params=_cparams(("arbitrary",)),
        name="moe_experts",
    )(blk0, nblk, n_blocks, w_gate, w_up, w_down, xs)


def _combine_kernel(unit_ref, next_ref, pos_ref, x_ref, h_ref, gate2_ref,
                    sg_ref, su_ref, sd_ref, lng_ref, lnb_ref, ys_ref, o_ref, s_scr, sem, *, alpha):
    rows = s_scr.shape[1]
    td = x_ref.shape[0]
    step = pl.program_id(0)
    slot = step % 2

    def item_copy(sl):
        def make(tbl, lane, size):
            return pltpu.make_async_copy(ys_ref.at[_unit_rows(tbl[0, lane], size), :],
                                         s_scr.at[sl, _unit_rows(tbl[1, lane], size), :], sem.at[size - 1, sl])
        return make

    def fetch(tbl, sl):
        _copy_items(tbl, item_copy(sl), "start")

    @pl.when(step == 0)
    def _():
        s_scr[...] = jnp.zeros_like(s_scr)
        fetch(unit_ref, 0)

    @pl.when(step + 1 < pl.num_programs(0))
    def _():
        fetch(next_ref, 1 - slot)

    h = h_ref[...]
    acc = _dot((_silu(_dot(h, sg_ref[...])) * _dot(h, su_ref[...])).astype(BF16), sd_ref[...])
    pos = pos_ref[...]
    _drain_items(unit_ref[2, 0], unit_ref[2, UNIT_LANES // 2], item_copy(slot), unit_ref)
    for r0 in range(0, rows, SORT_CHUNK):
        lane = (lax.broadcasted_iota(I32, (td, SORT_CHUNK), 1) + r0).astype(jnp.int16)
        onehot = jnp.zeros((td, SORT_CHUNK), BF16)
        for k in range(pos.shape[1]):
            onehot = jnp.where(lane == pos[:, k:k + 1].astype(jnp.int16), jnp.ones((), BF16), onehot)
        acc = acc + _dot(onehot, s_scr[slot, r0:r0 + SORT_CHUNK, :])
    o_ref[...] = _ln(alpha * x_ref[...] + gate2_ref[...] * acc) * lng_ref[...] + lnb_ref[...]


def _combine(unit_dst, pos_tok, xn, h2, modp, sh_gate, sh_up, sh_down, ln_g, ln_b, ys, seg_td, *, alpha, n_exp):
    t, d = xn.shape
    nt = t // TD
    tok = pl.BlockSpec((TD, d), lambda i: (i, 0))
    full = lambda a: pl.BlockSpec(a.shape, lambda i: (0,) * a.ndim)
    return pl.pallas_call(
        functools.partial(_combine_kernel, alpha=alpha),
        grid=(nt,),
        in_specs=[pl.BlockSpec((8, UNIT_LANES), lambda i: (0, i), memory_space=pltpu.SMEM),
                  pl.BlockSpec((8, UNIT_LANES), lambda i: (0, jnp.minimum(i + 1, nt - 1)), memory_space=pltpu.SMEM),
                  pl.BlockSpec((TD, pos_tok.shape[1]), lambda i: (i, 0)),
                  tok, tok,
                  pl.BlockSpec((None, None, 1, d), lambda i: (5, seg_td(i), 0, 0)),
                  full(sh_gate), full(sh_up), full(sh_down), full(ln_g), full(ln_b),
                  pl.BlockSpec(memory_space=pl.ANY)],
        out_specs=tok,
        out_shape=jax.ShapeDtypeStruct((t, d), F32),
        scratch_shapes=[pltpu.VMEM((2, _tile_rows(n_exp), d), BF16), pltpu.SemaphoreType.DMA((2, 2))],
        compiler_params=_cparams(("arbitrary",)),
        name="moe_combine",
    )(unit_dst, unit_dst, pos_tok, xn, h2, modp, sh_gate, sh_up, sh_down, ln_g, ln_b, ys)


def _moe(xn, h2, scores_t, modp, seg_td, router_bias, w_gate, w_up, w_down, sh_gate, sh_up, sh_down,
         ln_g, ln_b, tri, ltri, *, alpha, layer):
    t = xn.shape[0]
    n_exp = scores_t.shape[0]
    assert _tile_rows(n_exp) % SORT_CHUNK == 0
    assert _tile_rows(n_exp) <= UNIT * UNIT_LANES and n_exp <= UNIT_LANES // 2
    assert _tile_rows(n_exp) < 2 ** 15
    pos, w_sel, run_start, run_len, units, tot = _route(scores_t, router_bias.reshape(n_exp, 1), tri, ltri)
    as_rows = lambda a: jnp.tile(a[:, ::128].T, (1, 2))[:, None, :]
    used = tot[:, 0].astype(I32)
    region = (used + MOE_BLK - 1) // MOE_BLK * MOE_BLK
    region_end = jnp.cumsum(region)
    region_start = region_end - region
    n_slots = TOP_K * t + n_exp * UNIT * (t // TD) + n_exp * MOE_BLK
    unit_dst = _unit_dest((region_start // UNIT).astype(I32), units)
    xs = _dispatch(((region_start + used) // UNIT).astype(I32), (region_end // UNIT).astype(I32),
                   unit_dst[2, ::UNIT_LANES], unit_dst[2, UNIT_LANES // 2::UNIT_LANES], unit_dst, pos, w_sel,
                   as_rows(run_start), as_rows(run_len), h2,
                   n_slots)
    ys = _experts((region_start // MOE_BLK).astype(I32), (region // MOE_BLK).astype(I32),
                  (region_end[-1:] // MOE_BLK).astype(I32), xs, w_gate, w_up, w_down, layer)
    return _combine(unit_dst, pos.T, xn, h2, modp, sh_gate, sh_up, sh_down, ln_g, ln_b, ys, seg_td,
                    alpha=alpha, n_exp=n_exp)


def kernel(x, c, ctx, c_ctx, w_mod, b_mod, w_in, b_in, lam_q1, lam_k1, lam_q2, lam_k2, attn_norm_g, sg_ln_g, sg_ln_b, sg_w, sg_b, conv_w, conv_b, lru_wa, lru_ba, lru_wx, lru_bx, lru_lam, w_branch, w_out, ln1_g, ln1_b, w_router, router_bias, moe_w_gate, moe_w_up, moe_w_down, sh_w_gate, sh_w_up, sh_w_down, ln2_g, ln2_b):
    batch, n_lat, d = x.shape
    n_ctx = ctx.shape[1]
    depth = w_mod.shape[0]
    n_exp = w_router.shape[2]
    t_lat = batch * n_lat
    t_ctx = batch * n_ctx
    t_all = t_lat + t_ctx
    assert n_lat % TM == 0 and t_ctx % TM == 0 and n_ctx % B_CHUNK == 0 and t_lat % n_ctx == 0
    assert batch + 1 <= MOD_ROWS and TM % TD == 0
    alpha = (2 * depth) ** 0.25
    cw = conv_w.shape[2]
    tiles_per_batch = n_lat // TM

    seg = lambda i: jnp.minimum(i // tiles_per_batch, batch)
    seg_td = lambda i: jnp.minimum(i // (n_lat // TD), batch)
    tm_in = 2 * TM if (n_lat % (2 * TM) == 0 and t_ctx % (2 * TM) == 0) else TM
    seg_in = lambda i: jnp.minimum(i // (n_lat // tm_in), batch)
    tbl_idx = lambda i: jnp.where(i < t_lat // tm_in, i % (n_lat // tm_in), n_lat // tm_in)

    x_all = jnp.concatenate([x.reshape(t_lat, d), ctx.reshape(t_ctx, d)], axis=0)
    c_all = jnp.zeros((MOD_ROWS, d), F32).at[:batch].set(c).at[batch].set(c_ctx)
    tables = _rope_tables(n_lat, tm_in)
    tri = (jnp.arange(TD)[:, None] <= jnp.arange(TD)[None, :]).astype(BF16)
    ltri = (jnp.arange(n_exp)[None, :] < jnp.arange(n_exp)[:, None]).astype(BF16)
    row = lambda v: v.reshape(1, -1)

    def dense_blocks(w):
        nb, bi, bj = w.shape
        eye = jnp.eye(nb, dtype=w.dtype)
        return (w[:, :, None, :] * eye[:, None, :, None]).reshape(nb * bi, nb * bj)

    for l in range(depth):
        last = l == depth - 1
        lam_init = 0.8 - 0.6 * math.exp(-0.3 * l)
        mod = _mod_rows(c_all, w_mod[l].astype(BF16), row(b_mod[l]))
        modp = mod.reshape(MOD_ROWS, 6, 1, d).transpose(1, 0, 2, 3)

        qr, kr, vb, proj = _in_proj(x_all, modp, w_in[l].astype(BF16), row(b_in[l]), tables, seg_in, tbl_idx, tm_in)

        lam_vecs = jnp.stack([lam_q1[l], lam_k1[l], lam_q2[l], lam_k2[l]])
        gain = attn_norm_g[l].reshape(A_HEADS, A_V_DIM, 1)
        attn = functools.partial(_attention, qr, kr, vb.T, lam_vecs, gain, lam_init=lam_init,
                                 batch=batch, n_lat=n_lat, n_ctx=n_ctx)
        n_rows = t_lat if last else t_all
        o_a = attn(latent_queries=True, out_rows=n_rows)
        if not last:
            o_a = attn(latent_queries=False, out_rows=n_rows, into=o_a)

        o_b = _spatial_gating(proj, row(sg_ln_g[l]), row(sg_ln_b[l]), sg_w[l].astype(BF16),
                              sg_b[l].reshape(B_GROUPS, B_CHUNK, 1))

        lru = functools.partial(_lru_pass, proj, conv_w=conv_w[l], conv_b=row(conv_b[l]), batch=batch)
        zeros_h = jnp.zeros((batch, 8, cw), F32)
        hf = {}
        for direction in range(2):
            wd = jnp.concatenate([dense_blocks(lru_wa[l, direction]), dense_blocks(lru_wx[l, direction])],
                                 axis=1).astype(BF16)
            par = dict(w_dense=wd, ba=row(lru_ba[l, direction]), bx=row(lru_bx[l, direction]),
                       lam=row(lru_lam[l, direction]), reverse=direction == 1, final=direction == 1)
            shared = direction == 1 and not last
            h_ctx, edge = lru(hf.get("ctx"), zeros_h, seq=n_ctx, tl=n_ctx, row0=t_lat,
                              out_rows=t_all if shared else None, out_row0=t_lat if shared else 0, **par)
            h_lat, _ = lru(hf.get("lat"), edge, seq=n_lat, tl=TM, row0=0,
                           out_rows=t_all if shared else None, into=h_ctx if shared else None, **par)
            hf = {"ctx": h_ctx, "lat": h_lat}
        o_c = hf["lat"]
        wr_t = w_router[l].T
        wr_hi = wr_t.astype(BF16)
        wr_lo = (wr_t - wr_hi.astype(F32)).astype(BF16)
        xn, h2, scores_t = _merge(x_all, o_a, o_b, o_c, proj, modp, w_branch[l].astype(BF16),
                                  w_out[l].astype(BF16), row(ln1_g[l]), row(ln1_b[l]), wr_hi, wr_lo, seg,
                                  n_rows=n_rows, alpha=alpha)
        x_all = _moe(xn, h2, scores_t, modp, seg_td, router_bias[l], moe_w_gate,
                     moe_w_up, moe_w_down, sh_w_gate[l].astype(BF16),
                     sh_w_up[l].astype(BF16), sh_w_down[l].astype(BF16), row(ln2_g[l]), row(ln2_b[l]), tri, ltri,
                     alpha=alpha, layer=l)
    return x_all[:t_lat].reshape(batch, n_lat, d)
```

```python
import functools
import math

import jax
import jax.numpy as jnp
from jax import lax
from jax.experimental import pallas as pl
from jax.experimental.pallas import tpu as pltpu

F32 = jnp.float32
BF16 = jnp.bfloat16
I32 = jnp.int32

A_HEADS = 4
A_QK_DIM = 64
A_V_DIM = 2 * A_QK_DIM
GRID_W = 64
ROPE_THETA = 10000.0
B_CHUNK = 128
B_GROUPS = 4
C_BLOCKS = 8
C_POW = 8.0
TOP_K = 8
N_GROUPS = 8
TOPK_GROUPS = 4
ROUTED_SCALE = 2.5
LN_EPS = 1e-6
RMS_EPS = 1e-5

TM = 512
MOE_BLK = 512
TD = 256
UNIT = 16
UNIT_LANES = 256
SORT_CHUNK = 512
W_LANES = 128
EXPERT_BUFS = 4
LRU_BLOCK = 64
HALO = 16
ATTN_HEADS = 4
ATTN_SUB = 1024
MOD_ROWS = 16
VMEM_LIMIT = 56 * 1024 * 1024


def _cparams(sem):
    return pltpu.CompilerParams(dimension_semantics=sem, vmem_limit_bytes=VMEM_LIMIT)


def _ln(x):
    mu = jnp.mean(x, axis=-1, keepdims=True)
    xc = x - mu
    var = jnp.mean(xc * xc, axis=-1, keepdims=True)
    return xc * lax.rsqrt(var + LN_EPS)


def _gelu(x):
    cdf = 0.5 * (1.0 + jnp.tanh(math.sqrt(2.0 / math.pi) * (x + 0.044715 * (x * x * x))))
    return x * cdf


def _sigmoid(x):
    return 0.5 * jnp.tanh(0.5 * x) + 0.5


def _sigmoid_rel(x):
    return 1.0 / (1.0 + jnp.exp(-x))


def _silu(x):
    return x * _sigmoid(x)


def _dot(a, b):
    return jnp.dot(a, b, preferred_element_type=F32)


def _dot_nt(a, b):
    return lax.dot_general(a, b, (((1,), (1,)), ((), ())), preferred_element_type=F32)


def _mod_kernel(c_ref, w_ref, b_ref, o_ref):
    o_ref[...] = _dot(_silu(c_ref[...]).astype(BF16), w_ref[...]) + b_ref[...]


def _mod_rows(c_all, w, b):
    m, d = c_all.shape
    n = w.shape[1]
    tn = 1536
    return pl.pallas_call(
        _mod_kernel,
        grid=(n // tn,),
        in_specs=[pl.BlockSpec((m, d), lambda j: (0, 0)),
                  pl.BlockSpec((d, tn), lambda j: (0, j)),
                  pl.BlockSpec((1, tn), lambda j: (0, j))],
        out_specs=pl.BlockSpec((m, tn), lambda j: (0, j)),
        out_shape=jax.ShapeDtypeStruct((m, n), F32),
        compiler_params=_cparams(("parallel",)),
        name="adaln_rows",
    )(c_all, w, b)


REST_TILE = 1280
QKV_TILES = 3
COL_U, COL_S, COL_X, COL_Y, COL_G = 0, 1, 2, 3, 4


def _in_proj_kernel(x_ref, sh_ref, sc_ref, w_ref, b_ref, cos_ref, sa_ref, sb_ref,
                    q_ref, k_ref, v_ref, h_ref):
    j = pl.program_id(1)

    @pl.when(j == 0)
    def _():
        h = _ln(x_ref[...]) * (1.0 + sc_ref[...]) + sh_ref[...]
        h_ref[...] = h.astype(BF16)

    acc = _dot(h_ref[...], w_ref[...]) + b_ref[...]

    def rope(dst_ref, scale):
        hw = cos_ref.shape[1]
        for c0 in range(0, acc.shape[1], hw):
            x = acc[:, c0:c0 + hw]
            r = x * cos_ref[...] + pltpu.roll(x, hw - 16, 1) * sa_ref[...] + pltpu.roll(x, 16, 1) * sb_ref[...]
            dst_ref[:, c0:c0 + hw] = (r * scale).astype(BF16)

    @pl.when(j == 0)
    def _():
        rope(q_ref, A_QK_DIM ** -0.5 * math.log2(math.e))

    @pl.when(j == 1)
    def _():
        rope(k_ref, 1.0)

    @pl.when(j == 2)
    def _():
        v_ref[...] = acc.astype(BF16)


def _rest_proj_kernel(h_ref, w_ref, b_ref, o_ref):
    o_ref[...] = (_dot(h_ref[...], w_ref[...]) + b_ref[...]).astype(o_ref.dtype)


def _in_proj(x_all, modp, w, b, tables, seg, tbl_idx, tm):
    t, d = x_all.shape
    tn = 2 * A_HEADS * A_QK_DIM
    n_qkv = QKV_TILES * tn
    n_rest = w.shape[1] - n_qkv
    cos, sa, sb = tables
    tbl = pl.BlockSpec((tm, cos.shape[1]), lambda i, j: (tbl_idx(i), 0))
    qkv = pl.BlockSpec((tm, tn), lambda i, j: (i, 0))
    qkv_shape = jax.ShapeDtypeStruct((t, tn), BF16)
    q, k, v, h = pl.pallas_call(
        _in_proj_kernel,
        grid=(t // tm, QKV_TILES),
        in_specs=[pl.BlockSpec((tm, d), lambda i, j: (i, 0)),
                  pl.BlockSpec((None, None, 1, d), lambda i, j: (0, seg(i), 0, 0)),
                  pl.BlockSpec((None, None, 1, d), lambda i, j: (1, seg(i), 0, 0)),
                  pl.BlockSpec((d, tn), lambda i, j: (0, j)),
                  pl.BlockSpec((1, tn), lambda i, j: (0, j)),
                  tbl, tbl, tbl],
        out_specs=[qkv, qkv, qkv, pl.BlockSpec((tm, d), lambda i, j: (i, 0))],
        out_shape=[qkv_shape, qkv_shape, qkv_shape, jax.ShapeDtypeStruct((t, d), BF16)],
        compiler_params=_cparams(("parallel", "arbitrary")),
        name="in_proj_qkv",
    )(x_all, modp, modp, w, b, cos, sa, sb)
    tr = REST_TILE if n_rest % REST_TILE == 0 else tn
    rest = pl.pallas_call(
        _rest_proj_kernel,
        grid=(t // tm, n_rest // tr),
        in_specs=[pl.BlockSpec((tm, d), lambda i, j: (i, 0)),
                  pl.BlockSpec((d, tr), lambda i, j: (0, j)),
                  pl.BlockSpec((1, tr), lambda i, j: (0, j))],
        out_specs=pl.BlockSpec((tm, tr), lambda i, j: (i, j)),
        out_shape=jax.ShapeDtypeStruct((t, n_rest), BF16),
        compiler_params=_cparams(("parallel", "parallel")),
        name="in_proj_rest",
    )(h, w[:, n_qkv:], b[:, n_qkv:])
    return q, k, v, rest


def _rope_tables(n, tm):
    rows = n // GRID_W
    pos_row = jnp.repeat(jnp.arange(rows), GRID_W).astype(F32)
    pos_col = jnp.tile(jnp.arange(GRID_W), rows).astype(F32)
    quarter = A_QK_DIM // 4
    inv = ROPE_THETA ** (-jnp.arange(quarter, dtype=F32) / quarter)
    ang_r = pos_row[:, None] * inv
    ang_c = pos_col[:, None] * inv
    ang = jnp.concatenate([ang_r, ang_r, ang_c, ang_c], axis=-1)
    cos = jnp.cos(ang)
    sin = jnp.sin(ang)
    first_half = (jnp.arange(A_QK_DIM) % (2 * quarter)) < quarter
    sa = jnp.where(first_half, -sin, 0.0)
    sb = jnp.where(first_half, 0.0, sin)

    def widen(tbl, fill):
        tbl = jnp.tile(tbl, (1, 2))
        return jnp.concatenate([tbl, jnp.full((tm, tbl.shape[1]), fill, F32)], axis=0)

    return widen(cos, 1.0), widen(sa, 0.0), widen(sb, 0.0)


def _attn_kernel(*refs, lam_init, with_lat, sub):
    if with_lat:
        lam_ref, gain_ref, q_ref, kc_ref, vc_ref, kl_ref, vl_ref, o_ref = refs
    else:
        lam_ref, gain_ref, q_ref, kc_ref, vc_ref, _, o_ref = refs
    lv = lam_ref[...]
    lam = (jnp.exp(jnp.sum(lv[0:1] * lv[1:2], axis=-1, keepdims=True))
           - jnp.exp(jnp.sum(lv[2:3] * lv[3:4], axis=-1, keepdims=True)) + lam_init)
    chunks = [(kc_ref, vc_ref, 0, kc_ref.shape[0])]
    if with_lat:
        ck = min(sub, kl_ref.shape[0])
        chunks += [(kl_ref, vl_ref, r0, ck) for r0 in range(0, kl_ref.shape[0], ck)]
    for hh in range(gain_ref.shape[0]):
        head = slice(hh * A_V_DIM, (hh + 1) * A_V_DIM)
        probs, dens = [], []
        for c in range(2):
            cols = slice(hh * A_V_DIM + c * A_QK_DIM, hh * A_V_DIM + (c + 1) * A_QK_DIM)
            qc = q_ref[:, cols]
            scores = [_dot_nt(k_ref[r0:r0 + n, cols], qc) for k_ref, _, r0, n in chunks]
            m = None
            for s in scores:
                ms = jnp.max(s, axis=0, keepdims=True)
                m = ms if m is None else jnp.maximum(m, ms)
            p = [jnp.exp2(s - m) for s in scores]
            den = None
            for pj in p:
                ds = jnp.sum(pj, axis=0, keepdims=True)
                den = ds if den is None else den + ds
            probs.append(p)
            dens.append(den)
        ratio = lam * dens[0] / dens[1]
        o = None
        for p1, p2, (_, v_ref, r0, n) in zip(probs[0], probs[1], chunks):
            w = p1 - p2 * ratio
            ps = _dot(v_ref[head, r0:r0 + n], w.astype(BF16))
            o = ps if o is None else o + ps
        o = o / dens[0]
        o = o * lax.rsqrt(jnp.mean(o * o, axis=0, keepdims=True) + RMS_EPS)
        o_ref[:, head] = ((o * gain_ref[hh]) * (1.0 - lam_init)).T


def _attention(qr, kr, vt, lam_vecs, gain, *, lam_init, batch, n_lat, n_ctx, latent_queries, out_rows, into=None):
    hw = ATTN_HEADS * A_V_DIM
    ctx_blk0 = batch * n_lat // n_ctx
    lam_spec = pl.BlockSpec(lam_vecs.shape, lambda b, h, i: (0, 0))
    gain_spec = pl.BlockSpec((ATTN_HEADS, A_V_DIM, 1), lambda b, h, i: (h, 0, 0))
    kv_ctx = pl.BlockSpec((n_ctx, hw), lambda b, h, i: (ctx_blk0 + b, h))
    vt_ctx = pl.BlockSpec((hw, n_ctx), lambda b, h, i: (h, ctx_blk0 + b))
    if latent_queries:
        tq = 256
        nq = n_lat // tq
        q_spec = pl.BlockSpec((tq, hw), lambda b, h, i: (b * nq + i, h))
        kv_lat = pl.BlockSpec((n_lat, hw), lambda b, h, i: (b, h))
        vt_lat = pl.BlockSpec((hw, n_lat), lambda b, h, i: (h, b))
        in_specs = [lam_spec, gain_spec, q_spec, kv_ctx, vt_ctx, kv_lat, vt_lat]
        args = (lam_vecs, gain, qr, kr, vt, kr, vt)
        out_blk0 = 0
        aliases = {}
    else:
        tq = n_ctx
        nq = 1
        q_spec = pl.BlockSpec((tq, hw), lambda b, h, i: (ctx_blk0 + b, h))
        in_specs = [lam_spec, gain_spec, q_spec, kv_ctx, vt_ctx, pl.BlockSpec(memory_space=pl.ANY)]
        args = (lam_vecs, gain, qr, kr, vt, into)
        out_blk0 = ctx_blk0
        aliases = {len(args) - 1: 0}
    return pl.pallas_call(
        functools.partial(_attn_kernel, lam_init=lam_init, with_lat=latent_queries, sub=ATTN_SUB),
        grid=(batch, A_HEADS // ATTN_HEADS, nq),
        in_specs=in_specs,
        out_specs=pl.BlockSpec((tq, hw), lambda b, h, i: (out_blk0 + b * nq + i, h)),
        out_shape=jax.ShapeDtypeStruct((out_rows, A_HEADS * A_V_DIM), F32),
        input_output_aliases=aliases,
        compiler_params=_cparams(("parallel", "parallel", "arbitrary")),
        name="diff_attn_lat" if latent_queries else "diff_attn_ctx",
    )(*args)


def _sg_kernel(u_ref, s_ref, g_ref, b_ref, w_ref, bs_ref, o_ref):
    gw = B_CHUNK
    for ci in range(TM // B_CHUNK):
        rows = slice(ci * B_CHUNK, (ci + 1) * B_CHUNK)
        for g in range(B_GROUPS):
            cols = slice(g * gw, (g + 1) * gw)
            vn = _ln(_gelu(s_ref[rows, cols].astype(F32))) * g_ref[:, cols] + b_ref[:, cols]
            mixed = _dot(w_ref[g], vn.astype(BF16)) + bs_ref[g]
            o_ref[rows, cols] = _gelu(u_ref[rows, cols].astype(F32)) * mixed


def _spatial_gating(proj, ln_g, ln_b, w_s, b_s):
    t = proj.shape[0]
    w = B_GROUPS * B_CHUNK
    full = lambda a: pl.BlockSpec(a.shape, lambda i: (0,) * a.ndim)
    return pl.pallas_call(
        _sg_kernel,
        grid=(t // TM,),
        in_specs=[pl.BlockSpec((TM, w), lambda i: (i, COL_U)),
                  pl.BlockSpec((TM, w), lambda i: (i, COL_S)),
                  full(ln_g), full(ln_b), full(w_s), full(b_s)],
        out_specs=pl.BlockSpec((TM, w), lambda i: (i, 0)),
        out_shape=jax.ShapeDtypeStruct((t, w), F32),
        compiler_params=_cparams(("parallel",)),
        name="spatial_gating",
    )(proj, proj, ln_g, ln_b, w_s, b_s)


def _lru_kernel(*refs, reverse, final, n_tiles, tl, aliased):
    if aliased:
        refs = refs[:12] + refs[13:]
    if final:
        (x_ref, xp_ref, xn_ref, cw_ref, cb_ref, w_ref, ba_ref, bx_ref, lam_ref, h0_ref,
         hf_ref, y_ref, o_ref, hl_ref, carry) = refs
    else:
        (x_ref, xp_ref, xn_ref, cw_ref, cb_ref, w_ref, ba_ref, bx_ref, lam_ref, h0_ref,
         o_ref, hl_ref, carry) = refs
    step = pl.program_id(1)

    @pl.when(step == 0)
    def _():
        carry[...] = h0_ref[...]

    ti = (n_tiles - 1 - step) if reverse else step
    x = x_ref[...].astype(F32)
    cw = x.shape[-1]
    row = lax.broadcasted_iota(I32, x.shape, 0)
    has_prev = ti > 0
    has_next = ti < n_tiles - 1
    p6 = jnp.where(has_prev, xp_ref[HALO - 2:HALO - 1, :].astype(F32), 0.0)
    p7 = jnp.where(has_prev, xp_ref[HALO - 1:HALO, :].astype(F32), 0.0)
    n0 = jnp.where(has_next, xn_ref[0:1, :].astype(F32), 0.0)
    xm1 = jnp.where(row == 0, p7, pltpu.roll(x, 1, 0))
    xm2 = jnp.where(row == 0, p6, jnp.where(row == 1, p7, pltpu.roll(x, 2, 0)))
    xp1 = jnp.where(row == tl - 1, n0, pltpu.roll(x, tl - 1, 0))
    taps = cw_ref[...]
    xs = taps[0:1] * xm2 + taps[1:2] * xm1 + taps[2:3] * x + taps[3:4] * xp1 + cb_ref[...]

    z = _dot(xs.astype(BF16), w_ref[...])
    r = _sigmoid_rel(z[:, :cw] + ba_ref[...])
    ig = _sigmoid_rel(z[:, cw:] + bx_ref[...])
    nl = -lam_ref[...]
    softplus = jnp.maximum(nl, 0.0) + jnp.log(1.0 + jnp.exp(-jnp.abs(nl)))
    log_a = -C_POW * r * softplus
    a = jnp.exp(log_a)
    u = jnp.sqrt(-jnp.tanh(log_a) * (a * a + 1.0)) * (ig * xs)

    sb = min(LRU_BLOCK, tl)
    in_blk = row % sb
    d = 1
    while d < sb:
        if reverse:
            ok = in_blk < sb - d
            shift = tl - d
        else:
            ok = in_blk >= d
            shift = d
        a_sh = jnp.where(ok, pltpu.roll(a, shift, 0), 1.0)
        u_sh = jnp.where(ok, pltpu.roll(u, shift, 0), 0.0)
        u = u + a * u_sh
        a = a * a_sh
        d *= 2
    state = carry[0:1, :]
    blocks = [None] * (tl // sb)
    for b in (reversed(range(tl // sb)) if reverse else range(tl // sb)):
        hb = u[b * sb:(b + 1) * sb] + a[b * sb:(b + 1) * sb] * state
        state = hb[0:1, :] if reverse else hb[sb - 1:sb, :]
        blocks[b] = hb
    h = jnp.concatenate(blocks, axis=0)
    edge = state
    carry[...] = jnp.broadcast_to(edge, carry.shape)
    hl_ref[...] = jnp.broadcast_to(edge, hl_ref.shape)
    if final:
        o_ref[...] = _gelu(y_ref[...].astype(F32)) * (hf_ref[...] + h)
    else:
        o_ref[...] = h


def _lru_pass(proj, hf, h0, conv_w, conv_b, w_dense, ba, bx, lam, *, batch, seq, tl, row0, reverse, final,
              out_rows=None, out_row0=0, into=None):
    t_all = proj.shape[0]
    out_rows = batch * seq if out_rows is None else out_rows
    out_blk0 = out_row0 // tl
    assert into is None or final
    cw = conv_w.shape[1]
    n_tiles = seq // tl
    base = row0 // tl
    sub = tl // HALO
    last_halo = t_all // HALO - 1

    def tile(b, s):
        ti = (n_tiles - 1 - s) if reverse else s
        return base + b * n_tiles + ti

    def loc(b, s):
        ti = (n_tiles - 1 - s) if reverse else s
        return b * n_tiles + ti

    full = lambda a: pl.BlockSpec(a.shape, lambda b, s: (0,) * a.ndim)
    in_specs = [pl.BlockSpec((tl, cw), lambda b, s: (tile(b, s), COL_X)),
                pl.BlockSpec((HALO, cw), lambda b, s: (jnp.maximum(tile(b, s) * sub - 1, 0), COL_X)),
                pl.BlockSpec((HALO, cw), lambda b, s: (jnp.minimum((tile(b, s) + 1) * sub, last_halo), COL_X)),
                full(conv_w), full(conv_b), full(w_dense), full(ba), full(bx), full(lam),
                pl.BlockSpec((None, 8, cw), lambda b, s: (b, 0, 0))]
    args = [proj, proj, proj, conv_w, conv_b, w_dense, ba, bx, lam, h0]
    if final:
        in_specs += [pl.BlockSpec((tl, cw), lambda b, s: (loc(b, s), 0)),
                     pl.BlockSpec((tl, cw), lambda b, s: (tile(b, s), COL_Y))]
        args += [hf, proj]
    aliases = {}
    if into is not None:
        aliases = {len(args): 0}
        in_specs += [pl.BlockSpec(memory_space=pl.ANY)]
        args += [into]
    return pl.pallas_call(
        functools.partial(_lru_kernel, reverse=reverse, final=final, n_tiles=n_tiles, tl=tl,
                          aliased=into is not None),
        grid=(batch, n_tiles),
        in_specs=in_specs,
        out_specs=[pl.BlockSpec((tl, cw), lambda b, s: (out_blk0 + loc(b, s), 0)),
                   pl.BlockSpec((None, 8, cw), lambda b, s: (b, 0, 0))],
        out_shape=[jax.ShapeDtypeStruct((out_rows, cw), F32),
                   jax.ShapeDtypeStruct((batch, 8, cw), F32)],
        scratch_shapes=[pltpu.VMEM((8, cw), F32)],
        input_output_aliases=aliases,
        compiler_params=_cparams(("parallel", "arbitrary")),
        name="rglru_%s_%s" % ("bwd" if reverse else "fwd", "lat" if row0 == 0 else "ctx"),
    )(*args)


def _merge_kernel(x_ref, oa_ref, ob_ref, oc_ref, g0, g1, g2, g3, g4, g5, wb_ref, wo_ref,
                  gate1_ref, sh2_ref, sc2_ref, lng_ref, lnb_ref, wrh_ref, wrl_ref,
                  xn_ref, h2_ref, sc_ref, *, alpha):
    gates = ((g0, g1), (g2, g3), (g4, g5))
    mix = None
    for r, o_ref in enumerate((oa_ref, ob_ref, oc_ref)):
        proj = _dot(o_ref[...].astype(BF16), wb_ref[r])
        gate = jnp.concatenate([_sigmoid(gates[r][0][...].astype(F32)), _sigmoid(gates[r][1][...].astype(F32))],
                               axis=-1)
        term = gate * proj
        mix = term if mix is None else mix + term
    out = _dot(mix.astype(BF16), wo_ref[...])
    xn = _ln(alpha * x_ref[...] + gate1_ref[...] * out) * lng_ref[...] + lnb_ref[...]
    xn_ref[...] = xn
    h2 = _ln(xn) * (1.0 + sc2_ref[...]) + sh2_ref[...]
    h_hi = h2.astype(BF16)
    h2_ref[...] = h_hi
    h_lo = (h2 - h_hi.astype(F32)).astype(BF16)
    logits = _dot_nt(wrh_ref[...], h_hi) + (_dot_nt(wrh_ref[...], h_lo) + _dot_nt(wrl_ref[...], h_hi))
    sc_ref[...] = _sigmoid_rel(logits)


def _merge(x_all, o_a, o_b, o_c, proj, modp, w_branch, w_out, ln_g, ln_b, wr_hi, wr_lo, seg, *, n_rows, alpha):
    d = x_all.shape[1]
    bw = o_a.shape[1]
    n_exp = wr_hi.shape[0]
    tok = lambda w: pl.BlockSpec((TM, w), lambda i: (i, 0))
    gate = lambda c: pl.BlockSpec((TM, bw), lambda i, c=c: (i, c))
    mod = lambda k: pl.BlockSpec((None, None, 1, d), lambda i, k=k: (k, seg(i), 0, 0))
    full = lambda a: pl.BlockSpec(a.shape, lambda i: (0,) * a.ndim)
    return pl.pallas_call(
        functools.partial(_merge_kernel, alpha=alpha),
        grid=(n_rows // TM,),
        in_specs=[tok(d), tok(bw), tok(bw), tok(bw)] + [gate(COL_G + c) for c in range(6)]
                 + [full(w_branch), full(w_out), mod(2), mod(3), mod(4), full(ln_g), full(ln_b),
                    full(wr_hi), full(wr_lo)],
        out_specs=[tok(d), tok(d), pl.BlockSpec((n_exp, TM), lambda i: (0, i))],
        out_shape=[jax.ShapeDtypeStruct((n_rows, d), F32),
                   jax.ShapeDtypeStruct((n_rows, d), BF16),
                   jax.ShapeDtypeStruct((n_exp, n_rows), F32)],
        compiler_params=_cparams(("parallel",)),
        name="merge_residual_router",
    )(x_all, o_a, o_b, o_c, proj, proj, proj, proj, proj, proj, w_branch, w_out,
      modp, modp, modp, ln_g, ln_b, wr_hi, wr_lo)


def _tile_rows(n_exp):
    return TOP_K * TD + n_exp * UNIT


def _route_kernel(s_ref, bias_ref, tri_ref, ltri_ref, pos_ref, w_ref, rs_ref, rn_ref, unit_ref, tot_ref, carry):
    @pl.when(pl.program_id(0) == 0)
    def _():
        carry[...] = jnp.zeros_like(carry)

    s = s_ref[...]
    n_exp, tn = s.shape
    per = n_exp // N_GROUPS
    neg = -jnp.inf
    biased = s + bias_ref[...]
    sub = lax.broadcasted_iota(I32, (per, tn), 0)
    gs_rows = []
    for g in range(N_GROUPS):
        blk = biased[g * per:(g + 1) * per, :]
        m1 = jnp.max(blk, axis=0, keepdims=True)
        first = jnp.min(jnp.where(blk == m1, sub, per), axis=0, keepdims=True)
        m2 = jnp.max(jnp.where(sub == first, neg, blk), axis=0, keepdims=True)
        gs_rows.append(m1 + m2)
    gs = jnp.concatenate(gs_rows, axis=0)
    gi = lax.broadcasted_iota(I32, gs.shape, 0)
    g_ok = jnp.zeros(gs.shape, F32)
    cur = gs
    for _ in range(TOPK_GROUPS):
        m = jnp.max(cur, axis=0, keepdims=True)
        pick = jnp.min(jnp.where(cur == m, gi, N_GROUPS), axis=0, keepdims=True)
        hit = gi == pick
        g_ok = jnp.where(hit, 1.0, g_ok)
        cur = jnp.where(hit, neg, cur)
    ok_rows = [jnp.broadcast_to(g_ok[g:g + 1, :], (per, tn)) for g in range(N_GROUPS)]
    expert_ok = jnp.concatenate(ok_rows, axis=0)
    masked = jnp.where(expert_ok > 0.0, biased, neg)
    ei = lax.broadcasted_iota(I32, s.shape, 0)
    pick_s, hits = [], []
    sel = jnp.zeros(s.shape, F32)
    for _ in range(TOP_K):
        m = jnp.max(masked, axis=0, keepdims=True)
        pick = jnp.min(jnp.where(masked == m, ei, n_exp), axis=0, keepdims=True)
        hit = ei == pick
        hits.append(hit)
        pick_s.append(jnp.sum(jnp.where(hit, s, 0.0), axis=0, keepdims=True))
        sel = jnp.where(hit, 1.0, sel)
        masked = jnp.where(hit, neg, masked)
    tot = pick_s[0]
    for k in range(1, TOP_K):
        tot = tot + pick_s[k]
    w = jnp.where(sel > 0.0, s, 0.0) / tot * ROUTED_SCALE
    w_hi = w.astype(BF16)
    w_ref[...] = jnp.concatenate([w_hi, (w - w_hi.astype(F32)).astype(BF16)], axis=0)

    incl = _dot(sel.astype(BF16), tri_ref[...])
    count = incl[:, tn - 1:tn]
    run = jnp.floor((count + (UNIT - 1)) * (1.0 / UNIT)) * UNIT
    run_start = _dot(ltri_ref[...], jnp.broadcast_to(run, (n_exp, 128)).astype(BF16))[:, 0:1]
    row_in_tile = incl - sel + run_start
    pos = [jnp.sum(jnp.where(hit, row_in_tile, 0.0), axis=0, keepdims=True) for hit in hits]
    pos_ref[...] = jnp.concatenate(pos, axis=0).astype(I32)
    rs_ref[...] = jnp.broadcast_to(run_start, rs_ref.shape)
    rn_ref[...] = jnp.broadcast_to(run, rn_ref.shape)

    half = UNIT_LANES // 2
    region_used = carry[:, 0:1]
    u0 = run_start * (1.0 / UNIT)
    nu = run * (1.0 / UNIT)
    n_pair = jnp.floor(nu * 0.5)
    n_single = nu - 2.0 * n_pair
    both = jnp.concatenate([jnp.broadcast_to(n_pair, (n_exp, 64)), jnp.broadcast_to(n_single, (n_exp, 64))], axis=1)
    first = _dot(ltri_ref[...], both.astype(BF16))
    pair0, single0 = first[:, 0:1], first[:, 64:65]
    lane = lax.broadcasted_iota(I32, (n_exp, UNIT_LANES), 1)
    is_pair = lane < half
    item = jnp.where(is_pair, lane, lane - half).astype(F32)
    start = jnp.where(is_pair, pair0, single0)
    count = jnp.where(is_pair, n_pair, n_single)
    inside = jnp.logical_and(item >= start, item < start + count)
    off = jnp.where(is_pair, 2.0 * (item - pair0), 2.0 * n_pair)
    eu = lax.broadcasted_iota(I32, (n_exp, UNIT_LANES), 0).astype(F32)
    pick = lambda v: jnp.sum(jnp.where(inside, v, 0.0), axis=0, keepdims=True)
    totals = jnp.where(is_pair[0:1], jnp.sum(n_pair, axis=0, keepdims=True), jnp.sum(n_single, axis=0, keepdims=True))
    pad = jnp.zeros((4, UNIT_LANES), F32)
    unit_ref[...] = jnp.concatenate([pick(off + region_used * (1.0 / UNIT)), pick(eu), pick(off + u0), totals, pad],
                                    axis=0).astype(I32)

    total = region_used + run
    carry[...] = jnp.broadcast_to(total, carry.shape)
    tot_ref[...] = jnp.broadcast_to(total, tot_ref.shape)


def _route(scores_t, bias_col, tri, ltri):
    n_exp, t = scores_t.shape
    nt = t // TD
    return pl.pallas_call(
        _route_kernel,
        grid=(nt,),
        in_specs=[pl.BlockSpec((n_exp, TD), lambda i: (0, i)),
                  pl.BlockSpec((n_exp, 1), lambda i: (0, 0)),
                  pl.BlockSpec((TD, TD), lambda i: (0, 0)),
                  pl.BlockSpec((n_exp, n_exp), lambda i: (0, 0))],
        out_specs=[pl.BlockSpec((TOP_K, TD), lambda i: (0, i)),
                   pl.BlockSpec((2 * n_exp, TD), lambda i: (0, i)),
                   pl.BlockSpec((n_exp, 128), lambda i: (0, i)),
                   pl.BlockSpec((n_exp, 128), lambda i: (0, i)),
                   pl.BlockSpec((8, UNIT_LANES), lambda i: (0, i)),
                   pl.BlockSpec((n_exp, 128), lambda i: (0, 0))],
        out_shape=[jax.ShapeDtypeStruct((TOP_K, t), I32),
                   jax.ShapeDtypeStruct((2 * n_exp, t), BF16),
                   jax.ShapeDtypeStruct((n_exp, nt * 128), F32),
                   jax.ShapeDtypeStruct((n_exp, nt * 128), F32),
                   jax.ShapeDtypeStruct((8, nt * UNIT_LANES), I32),
                   jax.ShapeDtypeStruct((n_exp, 128), F32)],
        scratch_shapes=[pltpu.VMEM((n_exp, 128), F32)],
        compiler_params=_cparams(("arbitrary",)),
        name="route_topk",
    )(scores_t, bias_col, tri, ltri)


def _unit_dest_kernel(start_ref, unit_ref, o_ref, *, n_exp):
    tbl = unit_ref[...]
    exp_of = tbl[1:2, :]
    dst = tbl[0:1, :]
    for e in range(n_exp):
        dst = dst + jnp.where(exp_of == e, start_ref[e], 0)
    o_ref[...] = jnp.concatenate([dst, tbl[2:4, :], jnp.zeros((5, tbl.shape[1]), I32)], axis=0)


def _unit_dest(region_start_units, units):
    nt = units.shape[1] // UNIT_LANES
    per_step = next(k for k in (4, 2, 1) if nt % k == 0)
    blk = lambda: pl.BlockSpec((8, per_step * UNIT_LANES), lambda i, s: (0, i))
    return pl.pallas_call(
        functools.partial(_unit_dest_kernel, n_exp=region_start_units.shape[0]),
        grid_spec=pltpu.PrefetchScalarGridSpec(
            num_scalar_prefetch=1, grid=(nt // per_step,), in_specs=[blk()], out_specs=blk()),
        out_shape=jax.ShapeDtypeStruct(units.shape, I32),
        compiler_params=_cparams(("parallel",)),
        name="route_units",
    )(region_start_units, units)


def _unit_rows(i, n_units=1):
    return pl.ds(pl.multiple_of(i * UNIT, UNIT), n_units * UNIT)


def _copy_items(tbl, make_copy, act):
    assert act == "start"
    half = UNIT_LANES // 2
    _for_each(tbl[2, 0], lambda i, r=0: make_copy(tbl, i, 2).start(priority=r % 2), with_lane=True)
    _for_each(tbl[2, half], lambda i, r=0: make_copy(tbl, half + i, 1).start(priority=r % 2), with_lane=True)


def _drain_items(n_pairs, n_singles, make_copy, tbl):
    _for_each(n_pairs, lambda i: make_copy(tbl, 0, 2).wait())
    _for_each(n_singles, lambda i: make_copy(tbl, UNIT_LANES // 2, 1).wait())


def _for_each(count, fn, group=8, with_lane=False):
    main = lax.shift_right_logical(count, int(math.log2(group)))

    def many(i, c):
        for r in range(group):
            fn(i * group + r, r) if with_lane else fn(i * group + r)
        return c

    lax.fori_loop(0, main, many, 0)
    lax.fori_loop(main * group, count, lambda i, c: (fn(i), c)[1], 0)


def _dispatch_kernel(tail_lo_ref, tail_hi_ref, np_ref, ns_ref, unit_ref, pos_ref, w_ref, rs_ref, rn_ref, h_ref,
                     xs_ref, s_scr, zero_scr, sem, zsem, *, n_exp):
    rows = s_scr.shape[1]
    td = h_ref.shape[0]
    step = pl.program_id(0)
    last = pl.num_programs(0) - 1
    slot = step % 2

    def zero_copy(g):
        return pltpu.make_async_copy(zero_scr, xs_ref.at[_unit_rows(g), :], zsem.at[0])

    def item_copy(sl):
        def make(tbl, lane, size):
            return pltpu.make_async_copy(s_scr.at[sl, _unit_rows(tbl[1, lane], size), :],
                                         xs_ref.at[_unit_rows(tbl[0, lane], size), :], sem.at[size - 1, sl])
        return make

    def drain(tile, sl):
        _drain_items(np_ref[tile], ns_ref[tile], item_copy(sl), unit_ref)

    @pl.when(step >= 2)
    def _():
        drain(step - 2, slot)

    @pl.when(step == 0)
    def _():
        zero_scr[...] = jnp.zeros_like(zero_scr)

        def per_expert(e, c):
            lax.fori_loop(tail_lo_ref[e], tail_hi_ref[e], lambda g, cc: (zero_copy(g).start(), cc)[1], 0)
            lax.fori_loop(tail_lo_ref[e], tail_hi_ref[e], lambda g, cc: (zero_copy(g).wait(), cc)[1], 0)
            return c

        lax.fori_loop(0, n_exp, per_expert, 0)

    pos = pos_ref[...]
    h = h_ref[...]
    d = h.shape[1]
    w_dense = w_ref[...]
    run_lo = rs_ref[...]
    run_hi = run_lo + rn_ref[...]
    lane = lax.broadcasted_iota(I32, (SORT_CHUNK, s_scr.shape[2] - d), 1)
    for r0 in range(0, rows, SORT_CHUNK):
        srow = (lax.broadcasted_iota(I32, (SORT_CHUNK, td), 0) + r0).astype(jnp.int16)
        onehot = jnp.zeros((SORT_CHUNK, td), BF16)
        for k in range(pos.shape[0]):
            onehot = jnp.where(srow == pos[k:k + 1, :].astype(jnp.int16), jnp.ones((), BF16), onehot)
        s_e = (lax.broadcasted_iota(I32, (SORT_CHUNK, w_dense.shape[0]), 0) + r0).astype(F32)
        mine = jnp.logical_and(s_e >= run_lo, s_e < run_hi).astype(BF16)
        w_row = jnp.sum(onehot.astype(F32) * _dot(mine, w_dense), axis=1, keepdims=True)
        w_hi = w_row.astype(BF16).astype(F32)
        w_lo = w_row - w_hi
        s_scr[slot, r0:r0 + SORT_CHUNK, :d] = _dot(onehot, h).astype(BF16)
        s_scr[slot, r0:r0 + SORT_CHUNK, d:] = jnp.where(lane == 0, w_hi, jnp.where(lane == 1, w_lo, 0.0)).astype(BF16)

    _copy_items(unit_ref, item_copy(slot), "start")

    @pl.when(step == last)
    def _():
        drain(step, slot)

        @pl.when(step >= 1)
        def _():
            drain(step - 1, 1 - slot)


def _dispatch(tail_lo, tail_hi, n_pairs, n_singles, unit_dst, pos, w_dense, run_start_rows, run_rows, h2, n_slots):
    t, d = h2.shape
    n_exp = tail_lo.shape[0]
    dw = d + W_LANES
    runs = pl.BlockSpec((None, 1, 2 * n_exp), lambda i, a, b, c, e: (i, 0, 0))
    return pl.pallas_call(
        functools.partial(_dispatch_kernel, n_exp=n_exp),
        grid_spec=pltpu.PrefetchScalarGridSpec(
            num_scalar_prefetch=4, grid=(t // TD,),
            in_specs=[pl.BlockSpec((8, UNIT_LANES), lambda i, a, b, c, e: (0, i), memory_space=pltpu.SMEM),
                      pl.BlockSpec((TOP_K, TD), lambda i, a, b, c, e: (0, i)),
                      pl.BlockSpec((2 * n_exp, TD), lambda i, a, b, c, e: (0, i)),
                      runs, runs,
                      pl.BlockSpec((TD, d), lambda i, a, b, c, e: (i, 0))],
            out_specs=pl.BlockSpec(memory_space=pl.ANY),
            scratch_shapes=[pltpu.VMEM((2, _tile_rows(n_exp), dw), BF16), pltpu.VMEM((UNIT, dw), BF16),
                            pltpu.SemaphoreType.DMA((2, 2)), pltpu.SemaphoreType.DMA((1,))]),
        out_shape=jax.ShapeDtypeStruct((n_slots, dw), BF16),
        compiler_params=_cparams(("arbitrary",)),
        name="moe_dispatch",
    )(tail_lo, tail_hi, n_pairs, n_singles, unit_dst, pos, w_dense, run_start_rows, run_rows, h2)


def _expert_kernel(blk0_ref, nblk_ref, tot_ref, wg_ref, wu_ref, wd_ref, xs_ref, ys_ref, xbuf, obuf, wgu_scr, wd_scr,
                   sem_in, sem_out):
    e = pl.program_id(0)
    n = nblk_ref[e]
    b0 = blk0_ref[e]
    de = wg_ref.shape[1]
    wgu_scr[:, :de] = wg_ref[...].astype(BF16)
    wgu_scr[:, de:] = wu_ref[...].astype(BF16)
    wd_scr[...] = wd_ref[...].astype(BF16)

    total = tot_ref[0]

    def rows(g):
        return pl.ds(pl.multiple_of(g * MOE_BLK, MOE_BLK), MOE_BLK)

    def in_copy(g):
        sl = g % EXPERT_BUFS
        return pltpu.make_async_copy(xs_ref.at[rows(g), :], xbuf.at[sl], sem_in.at[sl])

    def out_copy(g):
        sl = g % EXPERT_BUFS
        return pltpu.make_async_copy(obuf.at[sl], ys_ref.at[rows(g), :], sem_out.at[sl])

    @pl.when(e == 0)
    def _():
        for g in range(EXPERT_BUFS - 1):
            @pl.when(g < total)
            def _():
                in_copy(g).start()

    def block(j, c):
        g = b0 + j
        sl = g % EXPERT_BUFS
        in_copy(g).wait()

        @pl.when(g + (EXPERT_BUFS - 1) < total)
        def _():
            in_copy(g + (EXPERT_BUFS - 1)).start()

        @pl.when(g >= EXPERT_BUFS)
        def _():
            out_copy(g - EXPERT_BUFS).wait()

        d = wd_scr.shape[1]
        gu = _dot(xbuf[sl, :, :d], wgu_scr[...])
        act = _silu(gu[:, :de]) * gu[:, de:]
        w_row = xbuf[sl, :, d:d + 1].astype(F32) + xbuf[sl, :, d + 1:d + 2].astype(F32)
        obuf[sl] = (_dot(act.astype(BF16), wd_scr[...]) * w_row).astype(BF16)
        out_copy(g).start()
        return c

    lax.fori_loop(0, n, block, 0)

    @pl.when(e == pl.num_programs(0) - 1)
    def _():
        for back in range(1, EXPERT_BUFS + 1):
            @pl.when(total >= back)
            def _():
                out_copy(total - back).wait()


def _experts(blk0, nblk, n_blocks, xs, w_gate, w_up, w_down, layer):
    n_slots, dw = xs.shape
    _, n_exp, d, de = w_gate.shape
    return pl.pallas_call(
        _expert_kernel,
        grid_spec=pltpu.PrefetchScalarGridSpec(
            num_scalar_prefetch=3, grid=(n_exp,),
            in_specs=[pl.BlockSpec((None, None, d, de), lambda e, a, b, c: (layer, e, 0, 0)),
                      pl.BlockSpec((None, None, d, de), lambda e, a, b, c: (layer, e, 0, 0)),
                      pl.BlockSpec((None, None, de, d), lambda e, a, b, c: (layer, e, 0, 0)),
                      pl.BlockSpec(memory_space=pl.ANY)],
            out_specs=pl.BlockSpec(memory_space=pl.ANY),
            scratch_shapes=[pltpu.VMEM((EXPERT_BUFS, MOE_BLK, dw), BF16), pltpu.VMEM((EXPERT_BUFS, MOE_BLK, d), BF16),
                            pltpu.VMEM((d, 2 * de), BF16), pltpu.VMEM((de, d), BF16),
                            pltpu.SemaphoreType.DMA((EXPERT_BUFS,)), pltpu.SemaphoreType.DMA((EXPERT_BUFS,))]),
        out_shape=jax.ShapeDtypeStruct((n_slots, d), BF16),
        compiler_params=_cparams(("arbitrary",)),
        name="moe_experts",
    )(blk0, nblk, n_blocks, w_gate, w_up, w_down, xs)


def _combine_kernel(unit_ref, next_ref, pos_ref, x_ref, h_ref, gate2_ref,
                    sg_ref, su_ref, sd_ref, lng_ref, lnb_ref, ys_ref, o_ref, s_scr, sem, *, alpha):
    rows = s_scr.shape[1]
    td = x_ref.shape[0]
    step = pl.program_id(0)
    slot = step % 2

    def item_copy(sl):
        def make(tbl, lane, size):
            return pltpu.make_async_copy(ys_ref.at[_unit_rows(tbl[0, lane], size), :],
                                         s_scr.at[sl, _unit_rows(tbl[1, lane], size), :], sem.at[size - 1, sl])
        return make

    def fetch(tbl, sl):
        _copy_items(tbl, item_copy(sl), "start")

    @pl.when(step == 0)
    def _():
        s_scr[...] = jnp.zeros_like(s_scr)
        fetch(unit_ref, 0)

    @pl.when(step + 1 < pl.num_programs(0))
    def _():
        fetch(next_ref, 1 - slot)

    h = h_ref[...]
    acc = _dot((_silu(_dot(h, sg_ref[...])) * _dot(h, su_ref[...])).astype(BF16), sd_ref[...])
    pos = pos_ref[...]
    _drain_items(unit_ref[2, 0], unit_ref[2, UNIT_LANES // 2], item_copy(slot), unit_ref)
    for r0 in range(0, rows, SORT_CHUNK):
        lane = (lax.broadcasted_iota(I32, (td, SORT_CHUNK), 1) + r0).astype(jnp.int16)
        onehot = jnp.zeros((td, SORT_CHUNK), BF16)
        for k in range(pos.shape[1]):
            onehot = jnp.where(lane == pos[:, k:k + 1].astype(jnp.int16), jnp.ones((), BF16), onehot)
        acc = acc + _dot(onehot, s_scr[slot, r0:r0 + SORT_CHUNK, :])
    o_ref[...] = _ln(alpha * x_ref[...] + gate2_ref[...] * acc) * lng_ref[...] + lnb_ref[...]


def _combine(unit_dst, pos_tok, xn, h2, modp, sh_gate, sh_up, sh_down, ln_g, ln_b, ys, seg_td, *, alpha, n_exp):
    t, d = xn.shape
    nt = t // TD
    tok = pl.BlockSpec((TD, d), lambda i: (i, 0))
    full = lambda a: pl.BlockSpec(a.shape, lambda i: (0,) * a.ndim)
    return pl.pallas_call(
        functools.partial(_combine_kernel, alpha=alpha),
        grid=(nt,),
        in_specs=[pl.BlockSpec((8, UNIT_LANES), lambda i: (0, i), memory_space=pltpu.SMEM),
                  pl.BlockSpec((8, UNIT_LANES), lambda i: (0, jnp.minimum(i + 1, nt - 1)), memory_space=pltpu.SMEM),
                  pl.BlockSpec((TD, pos_tok.shape[1]), lambda i: (i, 0)),
                  tok, tok,
                  pl.BlockSpec((None, None, 1, d), lambda i: (5, seg_td(i), 0, 0)),
                  full(sh_gate), full(sh_up), full(sh_down), full(ln_g), full(ln_b),
                  pl.BlockSpec(memory_space=pl.ANY)],
        out_specs=tok,
        out_shape=jax.ShapeDtypeStruct((t, d), F32),
        scratch_shapes=[pltpu.VMEM((2, _tile_rows(n_exp), d), BF16), pltpu.SemaphoreType.DMA((2, 2))],
        compiler_params=_cparams(("arbitrary",)),
        name="moe_combine",
    )(unit_dst, unit_dst, pos_tok, xn, h2, modp, sh_gate, sh_up, sh_down, ln_g, ln_b, ys)


def _moe(xn, h2, scores_t, modp, seg_td, router_bias, w_gate, w_up, w_down, sh_gate, sh_up, sh_down,
         ln_g, ln_b, tri, ltri, *, alpha, layer):
    t = xn.shape[0]
    n_exp = scores_t.shape[0]
    assert _tile_rows(n_exp) % SORT_CHUNK == 0
    assert _tile_rows(n_exp) <= UNIT * UNIT_LANES and n_exp <= UNIT_LANES // 2
    assert _tile_rows(n_exp) < 2 ** 15
    pos, w_sel, run_start, run_len, units, tot = _route(scores_t, router_bias.reshape(n_exp, 1), tri, ltri)
    as_rows = lambda a: jnp.tile(a[:, ::128].T, (1, 2))[:, None, :]
    used = tot[:, 0].astype(I32)
    region = (used + MOE_BLK - 1) // MOE_BLK * MOE_BLK
    region_end = jnp.cumsum(region)
    region_start = region_end - region
    n_slots = TOP_K * t + n_exp * UNIT * (t // TD) + n_exp * MOE_BLK
    unit_dst = _unit_dest((region_start // UNIT).astype(I32), units)
    xs = _dispatch(((region_start + used) // UNIT).astype(I32), (region_end // UNIT).astype(I32),
                   unit_dst[2, ::UNIT_LANES], unit_dst[2, UNIT_LANES // 2::UNIT_LANES], unit_dst, pos, w_sel,
                   as_rows(run_start), as_rows(run_len), h2,
                   n_slots)
    ys = _experts((region_start // MOE_BLK).astype(I32), (region // MOE_BLK).astype(I32),
                  (region_end[-1:] // MOE_BLK).astype(I32), xs, w_gate, w_up, w_down, layer)
    return _combine(unit_dst, pos.T, xn, h2, modp, sh_gate, sh_up, sh_down, ln_g, ln_b, ys, seg_td,
                    alpha=alpha, n_exp=n_exp)


def kernel(x, c, ctx, c_ctx, w_mod, b_mod, w_in, b_in, lam_q1, lam_k1, lam_q2, lam_k2, attn_norm_g, sg_ln_g, sg_ln_b, sg_w, sg_b, conv_w, conv_b, lru_wa, lru_ba, lru_wx, lru_bx, lru_lam, w_branch, w_out, ln1_g, ln1_b, w_router, router_bias, moe_w_gate, moe_w_up, moe_w_down, sh_w_gate, sh_w_up, sh_w_down, ln2_g, ln2_b):
    batch, n_lat, d = x.shape
    n_ctx = ctx.shape[1]
    depth = w_mod.shape[0]
    n_exp = w_router.shape[2]
    t_lat = batch * n_lat
    t_ctx = batch * n_ctx
    t_all = t_lat + t_ctx
    assert n_lat % TM == 0 and t_ctx % TM == 0 and n_ctx % B_CHUNK == 0 and t_lat % n_ctx == 0
    assert batch + 1 <= MOD_ROWS and TM % TD == 0
    alpha = (2 * depth) ** 0.25
    cw = conv_w.shape[2]
    tiles_per_batch = n_lat // TM

    seg = lambda i: jnp.minimum(i // tiles_per_batch, batch)
    seg_td = lambda i: jnp.minimum(i // (n_lat // TD), batch)
    tm_in = 2 * TM if (n_lat % (2 * TM) == 0 and t_ctx % (2 * TM) == 0) else TM
    seg_in = lambda i: jnp.minimum(i // (n_lat // tm_in), batch)
    tbl_idx = lambda i: jnp.where(i < t_lat // tm_in, i % (n_lat // tm_in), n_lat // tm_in)

    x_all = jnp.concatenate([x.reshape(t_lat, d), ctx.reshape(t_ctx, d)], axis=0)
    c_all = jnp.zeros((MOD_ROWS, d), F32).at[:batch].set(c).at[batch].set(c_ctx)
    tables = _rope_tables(n_lat, tm_in)
    tri = (jnp.arange(TD)[:, None] <= jnp.arange(TD)[None, :]).astype(BF16)
    ltri = (jnp.arange(n_exp)[None, :] < jnp.arange(n_exp)[:, None]).astype(BF16)
    row = lambda v: v.reshape(1, -1)

    def dense_blocks(w):
        nb, bi, bj = w.shape
        eye = jnp.eye(nb, dtype=w.dtype)
        return (w[:, :, None, :] * eye[:, None, :, None]).reshape(nb * bi, nb * bj)

    for l in range(depth):
        last = l == depth - 1
        lam_init = 0.8 - 0.6 * math.exp(-0.3 * l)
        mod = _mod_rows(c_all, w_mod[l].astype(BF16), row(b_mod[l]))
        modp = mod.reshape(MOD_ROWS, 6, 1, d).transpose(1, 0, 2, 3)

        qr, kr, vb, proj = _in_proj(x_all, modp, w_in[l].astype(BF16), row(b_in[l]), tables, seg_in, tbl_idx, tm_in)

        lam_vecs = jnp.stack([lam_q1[l], lam_k1[l], lam_q2[l], lam_k2[l]])
        gain = attn_norm_g[l].reshape(A_HEADS, A_V_DIM, 1)
        attn = functools.partial(_attention, qr, kr, vb.T, lam_vecs, gain, lam_init=lam_init,
                                 batch=batch, n_lat=n_lat, n_ctx=n_ctx)
        n_rows = t_lat if last else t_all
        o_a = attn(latent_queries=True, out_rows=n_rows)
        if not last:
            o_a = attn(latent_queries=False, out_rows=n_rows, into=o_a)

        o_b = _spatial_gating(proj, row(sg_ln_g[l]), row(sg_ln_b[l]), sg_w[l].astype(BF16),
                              sg_b[l].reshape(B_GROUPS, B_CHUNK, 1))

        lru = functools.partial(_lru_pass, proj, conv_w=conv_w[l], conv_b=row(conv_b[l]), batch=batch)
        zeros_h = jnp.zeros((batch, 8, cw), F32)
        hf = {}
        for direction in range(2):
            wd = jnp.concatenate([dense_blocks(lru_wa[l, direction]), dense_blocks(lru_wx[l, direction])],
                                 axis=1).astype(BF16)
            par = dict(w_dense=wd, ba=row(lru_ba[l, direction]), bx=row(lru_bx[l, direction]),
                       lam=row(lru_lam[l, direction]), reverse=direction == 1, final=direction == 1)
            shared = direction == 1 and not last
            h_ctx, edge = lru(hf.get("ctx"), zeros_h, seq=n_ctx, tl=n_ctx, row0=t_lat,
                              out_rows=t_all if shared else None, out_row0=t_lat if shared else 0, **par)
            h_lat, _ = lru(hf.get("lat"), edge, seq=n_lat, tl=TM, row0=0,
                           out_rows=t_all if shared else None, into=h_ctx if shared else None, **par)
            hf = {"ctx": h_ctx, "lat": h_lat}
        o_c = hf["lat"]
        wr_t = w_router[l].T
        wr_hi = wr_t.astype(BF16)
        wr_lo = (wr_t - wr_hi.astype(F32)).astype(BF16)
        xn, h2, scores_t = _merge(x_all, o_a, o_b, o_c, proj, modp, w_branch[l].astype(BF16),
                                  w_out[l].astype(BF16), row(ln1_g[l]), row(ln1_b[l]), wr_hi, wr_lo, seg,
                                  n_rows=n_rows, alpha=alpha)
        x_all = _moe(xn, h2, scores_t, modp, seg_td, router_bias[l], moe_w_gate,
                     moe_w_up, moe_w_down, sh_w_gate[l].astype(BF16),
                     sh_w_up[l].astype(BF16), sh_w_down[l].astype(BF16), row(ln2_g[l]), row(ln2_b[l]), tri, ltri,
                     alpha=alpha, layer=l)
    return x_all[:t_lat].reshape(batch, n_lat, d)
```
